```python
import functools
import jax, jax.numpy as jnp
from jax import lax
import numpy as np

D_MODEL = 1024
BATCH = 8
SEQ = 4096
DEPTH = 1
DEC_BATCH = 8
DEC_SEQ = 64
PAST_LEN = 2048

CHUNK = 64
LEFT_CHUNKS = 8
ATT_WINDOW = LEFT_CHUNKS * CHUNK
D_MIX = D_MODEL
N_HEADS_ATT = 8
HEAD_DIM_ATT = 64
N_HEADS_RET = 4
HEAD_DIM_RET = 128
GROUP_W = N_HEADS_ATT * HEAD_DIM_ATT
N_PROJ_SLOTS = 7
MAX_REL = 256
RET_DECAY_OFFSET = 5.0
ROPE_BASE = 10000.0
N_EXPERTS = 32
TOP_K = 4
D_FF = D_MODEL
SWIGLU_LIMIT = 7.0
SWIGLU_ALPHA = 1.702
MOE_BLOCK = 128
N_ADA = 6
NORM_EPS = 1e-6
NEG_INF = -1e30

kernel_name = "hybrid_chunk_attn_retention_moe_stream_step"


def rms_norm(x, g):
    xf = x.astype(jnp.float32)
    y = xf * lax.rsqrt(jnp.mean(xf * xf, axis=-1, keepdims=True) + NORM_EPS)
    return (y * g.astype(jnp.float32)).astype(x.dtype)


def head_group_norm(o, g):
    mu = jnp.mean(o, axis=-1, keepdims=True)
    var = jnp.mean(jnp.square(o - mu), axis=-1, keepdims=True)
    y = (o - mu) * lax.rsqrt(var + NORM_EPS)
    return y.reshape(o.shape[0], o.shape[1], -1) * g.astype(jnp.float32)


def rotary(x, pos):
    half = x.shape[-1] // 2
    inv = ROPE_BASE ** (-jnp.arange(half, dtype=jnp.float32) / half)
    ang = pos.astype(jnp.float32)[:, None] * inv[None, :]
    cos = jnp.cos(ang)[None, :, None, :]
    sin = jnp.sin(ang)[None, :, None, :]
    xf = x.astype(jnp.float32)
    x1, x2 = xf[..., :half], xf[..., half:]
    return jnp.concatenate([x1 * cos - x2 * sin, x2 * cos + x1 * sin], axis=-1).astype(x.dtype)


def ada_modulation(c, w_ada, b_ada):
    m = jax.nn.silu(c) @ w_ada + b_ada
    return jnp.split(m[:, None, :], N_ADA, axis=-1)


def modulate(h, shift, scale):
    return h * (1.0 + scale) + shift


def mix_projections(h, pos, w_in, g_q, g_k):
    B, S, _ = h.shape
    z = h @ w_in
    qa, ka, va, qb, kb, vb, gb = jnp.split(z, N_PROJ_SLOTS, axis=-1)
    qa = rms_norm(qa.reshape(B, S, N_HEADS_ATT, HEAD_DIM_ATT), g_q)
    ka = rms_norm(ka.reshape(B, S, N_HEADS_ATT, HEAD_DIM_ATT), g_k)
    va = va.reshape(B, S, N_HEADS_ATT, HEAD_DIM_ATT)
    qb = rotary(qb.reshape(B, S, N_HEADS_RET, HEAD_DIM_RET), pos)
    kb = rotary(kb.reshape(B, S, N_HEADS_RET, HEAD_DIM_RET), pos) * (HEAD_DIM_RET ** -0.5)
    vb = vb.reshape(B, S, N_HEADS_RET, HEAD_DIM_RET)
    return qa, ka, va, qb, kb, vb, gb


def attend(q, k, v, q_pos, k_pos, k_valid, rel_bias):
    s = jnp.einsum('bqhd,bkhd->bhqk', q, k).astype(jnp.float32) * (HEAD_DIM_ATT ** -0.5)
    rel = jnp.clip(q_pos[:, None] - k_pos[None, :], -MAX_REL, MAX_REL) + MAX_REL
    s = s + rel_bias.astype(jnp.float32)[:, rel][None]
    s = jnp.where(k_valid[None, None, None, :], s, NEG_INF)
    p = jax.nn.softmax(s, axis=-1).astype(v.dtype)
    return jnp.einsum('bhqk,bkhd->bqhd', p, v)


def chunk_band_attention_prompt(q, k, v, rel_bias):
    B, S, H, Dh = q.shape
    n_chunks = S // CHUNK
    band = ATT_WINDOW + CHUNK
    pad = ((0, 0), (ATT_WINDOW, 0), (0, 0), (0, 0))
    kp = jnp.pad(k, pad)
    vp = jnp.pad(v, pad)

    def one_chunk(ci):
        start = ci * CHUNK
        qc = lax.dynamic_slice_in_dim(q, start, CHUNK, axis=1)
        kc = lax.dynamic_slice_in_dim(kp, start, band, axis=1)
        vc = lax.dynamic_slice_in_dim(vp, start, band, axis=1)
        q_pos = start + jnp.arange(CHUNK, dtype=jnp.int32)
        k_pos = start - ATT_WINDOW + jnp.arange(band, dtype=jnp.int32)
        return attend(qc, kc, vc, q_pos, k_pos, k_pos >= 0, rel_bias)

    out = lax.map(one_chunk, jnp.arange(n_chunks, dtype=jnp.int32))
    return jnp.moveaxis(out, 0, 1).reshape(B, S, H, Dh)


def chunk_band_attention_sample(q, k, v, rel_bias, k_cache, v_cache):
    L = k_cache.shape[1]
    T = q.shape[1]
    keys = jnp.concatenate([k_cache.astype(k.dtype), k], axis=1)
    vals = jnp.concatenate([v_cache.astype(v.dtype), v], axis=1)
    q_pos = PAST_LEN + jnp.arange(T, dtype=jnp.int32)
    k_pos = jnp.concatenate([PAST_LEN - L + jnp.arange(L, dtype=jnp.int32), q_pos])
    return attend(q, keys, vals, q_pos, k_pos, jnp.ones((L + T,), dtype=bool), rel_bias)


def retention_decay_logs():
    return jnp.log1p(-jnp.exp2(-RET_DECAY_OFFSET - jnp.arange(N_HEADS_RET, dtype=jnp.float32)))


def retention_chunk(state, q, k, v, log_g):
    C = q.shape[1]
    n = jnp.arange(C, dtype=jnp.float32)
    diff = n[:, None] - n[None, :]
    decay = jnp.where(diff[None] >= 0, jnp.exp(jnp.maximum(diff, 0.0)[None] * log_g[:, None, None]), 0.0)
    qf, kf, vf = q.astype(jnp.float32), k.astype(jnp.float32), v.astype(jnp.float32)
    scores = jnp.einsum('bnhd,bmhd->bhnm', qf, kf) * decay[None]
    inner = jnp.einsum('bhnm,bmhe->bnhe', scores, vf)
    xi = jnp.exp((n + 1.0)[:, None] * log_g[None, :])
    cross = jnp.einsum('bnhd,bhde->bnhe', qf, state) * xi[None, :, :, None]
    zeta = jnp.exp((C - 1.0 - n)[:, None] * log_g[None, :])
    new_state = jnp.exp(C * log_g)[None, :, None, None] * state + jnp.einsum(
        'bmhd,bmhe->bhde', kf * zeta[None, :, :, None], vf)
    return new_state, inner + cross


def retention_prompt(q, k, v):
    B, S, H, Dk = q.shape
    Dv = v.shape[-1]
    n_chunks = S // CHUNK
    log_g = retention_decay_logs()

    def to_chunks(t):
        return jnp.moveaxis(t.reshape(B, n_chunks, CHUNK, H, t.shape[-1]), 1, 0)

    def step(s, qkv):
        return retention_chunk(s, qkv[0], qkv[1], qkv[2], log_g)

    s0 = jnp.zeros((B, H, Dk, Dv), jnp.float32)
    s_final, o = lax.scan(step, s0, (to_chunks(q), to_chunks(k), to_chunks(v)))
    return jnp.moveaxis(o, 0, 1).reshape(B, S, H, Dv), s_final


def retention_sample(q, k, v, state):
    s_new, o = retention_chunk(state.astype(jnp.float32), q, k, v, retention_decay_logs())
    return o, s_new


def mix_output(att, ret, gb, g_ret_out, w_out):
    B, S = att.shape[:2]
    ret_out = jax.nn.silu(gb) * head_group_norm(ret, g_ret_out).astype(gb.dtype)
    merged = jnp.concatenate([att.reshape(B, S, GROUP_W), ret_out], axis=-1)
    return merged @ w_out


def moe_ffn(h, w_router, b_router, w_up, b_up, w_down, b_down):
    B, S, D = h.shape
    x = h.reshape(-1, D)
    T = x.shape[0]
    logits = (x @ w_router).astype(jnp.float32) + b_router.astype(jnp.float32)
    top_logit, top_idx = lax.top_k(logits, TOP_K)
    gates = jax.nn.softmax(top_logit, axis=-1)
    flat_e = top_idx.reshape(-1)
    order = jnp.argsort(flat_e)
    sorted_e = flat_e[order]
    sorted_tok = (order // TOP_K).astype(jnp.int32)
    sorted_gate = gates.reshape(-1)[order]
    counts = jnp.bincount(flat_e, length=N_EXPERTS)
    padded = (counts + MOE_BLOCK - 1) // MOE_BLOCK * MOE_BLOCK
    start = jnp.cumsum(counts) - counts
    padded_end = jnp.cumsum(padded)
    padded_start = padded_end - padded
    dest = padded_start[sorted_e] + jnp.arange(T * TOP_K) - start[sorted_e]
    n_blocks = -(-(T * TOP_K + N_EXPERTS * (MOE_BLOCK - 1)) // MOE_BLOCK)
    n_rows = n_blocks * MOE_BLOCK
    row_tok = jnp.zeros((n_rows,), jnp.int32).at[dest].set(sorted_tok)
    row_gate = jnp.zeros((n_rows,), jnp.float32).at[dest].set(sorted_gate)
    block_start = jnp.arange(n_blocks) * MOE_BLOCK
    block_e = jnp.minimum(jnp.searchsorted(padded_end, block_start, side='right'), N_EXPERTS - 1)
    xb = x[row_tok].reshape(n_blocks, MOE_BLOCK, D)

    def expert_block(args):
        xe, e = args
        u = xe @ w_up[e] + b_up[e]
        glu, lin = jnp.split(u, 2, axis=-1)
        glu = jnp.minimum(glu, SWIGLU_LIMIT)
        lin = jnp.clip(lin, -SWIGLU_LIMIT, SWIGLU_LIMIT)
        act = glu * jax.nn.sigmoid(SWIGLU_ALPHA * glu) * (lin + 1.0)
        return act @ w_down[e] + b_down[e]

    yb = lax.map(expert_block, (xb, block_e)).reshape(n_rows, D)
    y = jax.ops.segment_sum(yb * row_gate[:, None].astype(yb.dtype), row_tok, num_segments=T)
    return y.reshape(B, S, D)


def trunk_layer(x, c, pos, attention_fn, retention_fn, w_ada, b_ada, g_norm_mix, g_norm_ffn,
                w_in, g_q, g_k, rel_bias, g_ret_out, w_out, w_router, b_router,
                w_up, b_up, w_down, b_down):
    shift_m, scale_m, gate_m, shift_f, scale_f, gate_f = ada_modulation(c, w_ada, b_ada)
    h = modulate(rms_norm(x, g_norm_mix), shift_m, scale_m)
    qa, ka, va, qb, kb, vb, gb = mix_projections(h, pos, w_in, g_q, g_k)
    att = attention_fn(qa, ka, va, rel_bias)
    ret, ret_state = retention_fn(qb, kb, vb)
    x = x + gate_m * mix_output(att, ret, gb, g_ret_out, w_out)
    h = modulate(rms_norm(x, g_norm_ffn), shift_f, scale_f)
    x = x + gate_f * moe_ffn(h, w_router, b_router, w_up, b_up, w_down, b_down)
    return x, ka, va, ret_state


def setup_inputs(seed: int = 0) -> dict:
    key = jax.random.key(seed)
    ks = jax.random.split(key, 24)

    def nrm(k, shape, scale):
        return jax.random.normal(k, shape, jnp.float32) * scale

    cache_len = min(ATT_WINDOW, PAST_LEN)
    return {
        "x_prompt": nrm(ks[0], (BATCH, SEQ, D_MODEL), 1.0),
        "x_sample": nrm(ks[1], (DEC_BATCH, DEC_SEQ, D_MODEL), 1.0),
        "c_prompt": nrm(ks[2], (BATCH, D_MODEL), 1.0),
        "c_sample": nrm(ks[3], (DEC_BATCH, D_MODEL), 1.0),
        "cache_att_k": nrm(ks[4], (DEPTH, DEC_BATCH, cache_len, N_HEADS_ATT, HEAD_DIM_ATT), 1.0),
        "cache_att_v": nrm(ks[5], (DEPTH, DEC_BATCH, cache_len, N_HEADS_ATT, HEAD_DIM_ATT), 1.0),
        "state_ret": nrm(ks[6], (DEPTH, DEC_BATCH, N_HEADS_RET, HEAD_DIM_RET, HEAD_DIM_RET), 0.5),
        "w_ada": nrm(ks[7], (DEPTH, D_MODEL, N_ADA * D_MODEL), 0.2 * D_MODEL ** -0.5),
        "b_ada": nrm(ks[8], (DEPTH, N_ADA * D_MODEL), 0.1),
        "g_norm_mix": 1.0 + nrm(ks[9], (DEPTH, D_MODEL), 0.05),
        "g_norm_ffn": 1.0 + nrm(ks[10], (DEPTH, D_MODEL), 0.05),
        "w_in": nrm(ks[11], (DEPTH, D_MODEL, N_PROJ_SLOTS * GROUP_W), D_MODEL ** -0.5),
        "g_q": 1.0 + nrm(ks[12], (DEPTH, HEAD_DIM_ATT), 0.05),
        "g_k": 1.0 + nrm(ks[13], (DEPTH, HEAD_DIM_ATT), 0.05),
        "rel_bias": nrm(ks[14], (DEPTH, N_HEADS_ATT, 2 * MAX_REL + 1), 0.1),
        "g_ret_out": 1.0 + nrm(ks[15], (DEPTH, GROUP_W), 0.05),
        "w_out": nrm(ks[16], (DEPTH, D_MIX, D_MODEL), D_MIX ** -0.5),
        "w_router": nrm(ks[17], (DEPTH, D_MODEL, N_EXPERTS), D_MODEL ** -0.5),
        "b_router": nrm(ks[18], (DEPTH, N_EXPERTS), 0.01),
        "w_up": nrm(ks[19], (DEPTH, N_EXPERTS, D_MODEL, 2 * D_FF), D_MODEL ** -0.5),
        "b_up": nrm(ks[20], (DEPTH, N_EXPERTS, 2 * D_FF), 0.01),
        "w_down": nrm(ks[21], (DEPTH, N_EXPERTS, D_FF, D_MODEL), D_FF ** -0.5),
        "b_down": nrm(ks[22], (DEPTH, N_EXPERTS, D_MODEL), 0.01),
    }


def reference(x_prompt, x_sample, c_prompt, c_sample, cache_att_k, cache_att_v, state_ret,
              w_ada, b_ada, g_norm_mix, g_norm_ffn, w_in, g_q, g_k, rel_bias, g_ret_out, w_out,
              w_router, b_router, w_up, b_up, w_down, b_down):
    S = x_prompt.shape[1]
    T = x_sample.shape[1]
    pos_prompt = jnp.arange(S, dtype=jnp.int32)
    pos_sample = PAST_LEN + jnp.arange(T, dtype=jnp.int32)
    keep = min(ATT_WINDOW, S)
    xp, xs = x_prompt, x_sample
    kp_rows, vp_rows, sp_states, ks_rows, vs_rows, ss_states = [], [], [], [], [], []
    for l in range(DEPTH):
        lp = (w_ada[l], b_ada[l], g_norm_mix[l], g_norm_ffn[l], w_in[l], g_q[l], g_k[l],
              rel_bias[l], g_ret_out[l], w_out[l], w_router[l], b_router[l],
              w_up[l], b_up[l], w_down[l], b_down[l])
        xp, ka_p, va_p, s_p = trunk_layer(xp, c_prompt, pos_prompt, chunk_band_attention_prompt,
                                          retention_prompt, *lp)
        att_s = functools.partial(chunk_band_attention_sample, k_cache=cache_att_k[l], v_cache=cache_att_v[l])
        ret_s = functools.partial(retention_sample, state=state_ret[l])
        xs, ka_s, va_s, s_s = trunk_layer(xs, c_sample, pos_sample, att_s, ret_s, *lp)
        kp_rows.append(ka_p[:, S - keep:])
        vp_rows.append(va_p[:, S - keep:])
        sp_states.append(s_p.astype(x_prompt.dtype))
        ks_rows.append(ka_s)
        vs_rows.append(va_s)
        ss_states.append(s_s.astype(x_sample.dtype))
    return (xp, xs, jnp.stack(kp_rows), jnp.stack(vp_rows), jnp.stack(sp_states),
            jnp.stack(ks_rows), jnp.stack(vs_rows), jnp.stack(ss_states))
```

```python
import functools

import numpy as np
import jax
import jax.numpy as jnp
from jax import lax
from jax.experimental import pallas as pl
from jax.experimental.pallas import tpu as pltpu

F32 = jnp.float32
BF16 = jnp.bfloat16
I32 = jnp.int32

D_MODEL = 1024
GROUP_W = 512
N_SLOTS = 7
N_HEADS_ATT = 8
HEAD_DIM_ATT = 64
N_HEADS_RET = 4
HEAD_DIM_RET = 128
CHUNK = 64
ATT_WINDOW = 512
MAX_REL = 256
PAST_LEN = 2048
RET_DECAY_OFFSET = 5.0
ROPE_BASE = 10000.0
N_EXPERTS = 32
TOP_K = 4
D_FF = 1024
SWIGLU_LIMIT = 7.0
SWIGLU_ALPHA = 1.702
N_ADA = 6
NORM_EPS = 1e-6
NEG_INF = -1e30

TM = 512
GROUPS_PER_TILE = TM // CHUNK
ATT_QB = 256
RET_CB = 256
SEG_ALIGN = 16
SEG_BITS = 6
CAP = TOP_K * TM + N_EXPERTS * SEG_ALIGN
BM = 512
VMEM_LIMIT = 56 * 1024 * 1024


def _cparams(sem, vmem=None):
    return pltpu.CompilerParams(dimension_semantics=sem, vmem_limit_bytes=vmem)


def _ada_kernel(c_ref, w_ref, b_ref, o_ref):
    c = c_ref[...]
    s = c * jax.nn.sigmoid(c)
    o_ref[...] = jnp.dot(s.astype(BF16), w_ref[...].astype(BF16),
                         preferred_element_type=F32) + b_ref[...]


def _ada(c_all, w_ada, b_ada):
    n, d = c_all.shape
    cols = w_ada.shape[1]
    tn = 1536
    return pl.pallas_call(
        _ada_kernel,
        grid=(cols // tn,),
        in_specs=[pl.BlockSpec((n, d), lambda j: (0, 0)),
                  pl.BlockSpec((d, tn), lambda j: (0, j)),
                  pl.BlockSpec((1, tn), lambda j: (0, j))],
        out_specs=pl.BlockSpec((n, tn), lambda j: (0, j)),
        out_shape=jax.ShapeDtypeStruct((n, cols), F32),
        compiler_params=_cparams(("arbitrary",), VMEM_LIMIT),
        name="ada",
    )(c_all, w_ada, b_ada.reshape(1, cols))


def _rms_rows(x, g):
    ms = jnp.mean(x * x, axis=-1, keepdims=True)
    return x * lax.rsqrt(ms + NORM_EPS) * g


def _per_group(x, fn, *mods):
    x3 = x.reshape(GROUPS_PER_TILE, CHUNK, x.shape[-1])
    y3 = fn(x3, *[m[:, None, :] for m in mods])
    return y3.reshape(x.shape)


def _inproj_kernel(ntp, xp_ref, xs_ref, sh_ref, sc_ref, gn_ref, w_ref, bd_ref, gq_ref, gk_ref,
                   cos_ref, sin_ref,
                   qa_ref, ka_ref, va_ref, qb_ref, kb_ref, vb_ref, gb_ref,
                   kpt_ref, vpt_ref, kst_ref, vst_ref):
    i = pl.program_id(0)
    is_p = i < ntp
    x = jnp.where(is_p, xp_ref[...], xs_ref[...])
    y = _rms_rows(x, gn_ref[...])
    h = _per_group(y, lambda a, sh, sc: a * (1.0 + sc) + sh, sh_ref[...], sc_ref[...])
    hb = h.astype(BF16)

    def proj(s):
        return jnp.dot(hb, w_ref[:, s * GROUP_W:(s + 1) * GROUP_W], preferred_element_type=F32)

    def head_rms(z, g):
        ss = jnp.dot((z * z).astype(BF16), bd_ref[...], preferred_element_type=F32)
        return z * lax.rsqrt(ss * (1.0 / HEAD_DIM_ATT) + NORM_EPS) * g

    cos = cos_ref[...]
    sin = sin_ref[...]

    def rot(z):
        outs = []
        for hh in range(N_HEADS_RET):
            zh = z[:, hh * HEAD_DIM_RET:(hh + 1) * HEAD_DIM_RET]
            outs.append(zh * cos + pltpu.roll(zh, HEAD_DIM_RET // 2, axis=1) * sin)
        return jnp.concatenate(outs, axis=1)

    qa_ref[...] = head_rms(proj(0), gq_ref[...]).astype(BF16)
    ka = head_rms(proj(1), gk_ref[...])
    ka_ref[...] = ka.astype(BF16)
    va = proj(2)
    va_ref[...] = va.astype(BF16)

    @pl.when(is_p)
    def _():
        kpt_ref[...] = ka
        vpt_ref[...] = va

    @pl.when(jnp.logical_not(is_p))
    def _():
        kst_ref[...] = ka
        vst_ref[...] = va

    qb_ref[...] = rot(proj(3)).astype(BF16)
    kb_ref[...] = (rot(proj(4)) * (HEAD_DIM_RET ** -0.5)).astype(BF16)
    vb_ref[...] = proj(5).astype(BF16)
    gb_ref[...] = proj(6).astype(BF16)


def _inproj(xp, xs, shift, scale, g_norm, w_in_b, bd, gq8, gk8, cos_t, sin_t, nb, tps):
    rp = xp.shape[0]
    ntp = rp // TM
    nt = ntp + 1
    r = rp + TM
    row = lambda i: (i, 0)
    full = lambda i: (0, 0)
    tab = lambda i: (jnp.where(i < ntp, i % tps, tps), 0)
    act = jax.ShapeDtypeStruct((r, GROUP_W), BF16)
    return pl.pallas_call(
        functools.partial(_inproj_kernel, ntp),
        grid=(nt,),
        in_specs=[pl.BlockSpec((TM, D_MODEL), lambda i: (jnp.minimum(i, ntp - 1), 0)),
                  pl.BlockSpec((TM, D_MODEL), full),
                  pl.BlockSpec((GROUPS_PER_TILE, D_MODEL), row),
                  pl.BlockSpec((GROUPS_PER_TILE, D_MODEL), row),
                  pl.BlockSpec((1, D_MODEL), full),
                  pl.BlockSpec((D_MODEL, N_SLOTS * GROUP_W), full),
                  pl.BlockSpec((GROUP_W, GROUP_W), full),
                  pl.BlockSpec((1, GROUP_W), full),
                  pl.BlockSpec((1, GROUP_W), full),
                  pl.BlockSpec((TM, HEAD_DIM_RET), tab),
                  pl.BlockSpec((TM, HEAD_DIM_RET), tab)],
        out_specs=[pl.BlockSpec((TM, GROUP_W), row)] * 7 + [
            pl.BlockSpec((TM, GROUP_W), lambda i: (jnp.minimum(i // tps, nb - 1), 0)),
            pl.BlockSpec((TM, GROUP_W), lambda i: (jnp.minimum(i // tps, nb - 1), 0)),
            pl.BlockSpec((TM, GROUP_W), full),
            pl.BlockSpec((TM, GROUP_W), full)],
        out_shape=[act] * 7 + [jax.ShapeDtypeStruct((nb * TM, GROUP_W), F32)] * 2
        + [jax.ShapeDtypeStruct((TM, GROUP_W), F32)] * 2,
        compiler_params=_cparams(("arbitrary",), VMEM_LIMIT),
        name="inproj",
    )(xp, xs, shift, scale, g_norm, w_in_b, bd, gq8, gk8, cos_t, sin_t)


def _attn_heads(q, k, v, bias_ref, first_valid_col):
    qb_rows, kb_rows = q.shape[0], k.shape[0]
    col = lax.broadcasted_iota(I32, (qb_rows, kb_rows), 1)
    valid = col >= first_valid_col
    outs = []
    for hh in range(N_HEADS_ATT):
        hs = slice(hh * HEAD_DIM_ATT, (hh + 1) * HEAD_DIM_ATT)
        s = lax.dot_general(q[:, hs], k[:, hs], (((1,), (1,)), ((), ())),
                            preferred_element_type=F32)
        s = s * (HEAD_DIM_ATT ** -0.5) + bias_ref[hh]
        s = jnp.where(valid, s, NEG_INF)
        m = jnp.max(s, axis=-1, keepdims=True)
        e = jnp.exp(s - m)
        l = jnp.sum(e, axis=-1, keepdims=True)
        o = jnp.dot(e.astype(BF16), v[:, hs], preferred_element_type=F32)
        outs.append(o / l)
    return jnp.concatenate(outs, axis=1)


def _attn_prompt_kernel(q_ref, k0_ref, k1_ref, k2_ref, v0_ref, v1_ref, v2_ref, bias_ref, o_ref):
    j = pl.program_id(1)
    k = jnp.concatenate([k0_ref[...], k1_ref[...], k2_ref[...]], axis=0)
    v = jnp.concatenate([v0_ref[...], v1_ref[...], v2_ref[...]], axis=0)
    first_valid = (2 - j) * ATT_QB
    o_ref[...] = _attn_heads(q_ref[...], k, v, bias_ref, first_valid).astype(BF16)


def _attn_prompt(qa, ka, va, bias, nb, seq):
    r = nb * seq
    nq = seq // ATT_QB
    blk = lambda back: (lambda b, j: (b * nq + jnp.maximum(j - back, 0), 0))
    spec = lambda back: pl.BlockSpec((ATT_QB, GROUP_W), blk(back))
    return pl.pallas_call(
        _attn_prompt_kernel,
        grid=(nb, nq),
        in_specs=[spec(0), spec(2), spec(1), spec(0), spec(2), spec(1), spec(0),
                  pl.BlockSpec(bias.shape, lambda b, j: (0, 0, 0))],
        out_specs=spec(0),
        out_shape=jax.ShapeDtypeStruct((r, GROUP_W), BF16),
        compiler_params=_cparams(("arbitrary", "arbitrary"), VMEM_LIMIT),
        name="attn_prompt",
    )(qa, ka, ka, ka, va, va, va, bias)


def _attn_sample_kernel(q_ref, kn_ref, vn_ref, kc_ref, vc_ref, bias_ref, o_ref):
    k = jnp.concatenate([kc_ref[0], kn_ref[...]], axis=0)
    v = jnp.concatenate([vc_ref[0], vn_ref[...]], axis=0)
    o_ref[...] = _attn_heads(q_ref[...], k, v, bias_ref, 0).astype(BF16)


def _attn_sample(qa, ka, va, kc, vc, bias, rp):
    ndb = kc.shape[0]
    base = rp // CHUNK
    spec = pl.BlockSpec((CHUNK, GROUP_W), lambda b: (base + b, 0))
    cspec = pl.BlockSpec((1, ATT_WINDOW, GROUP_W), lambda b: (b, 0, 0))
    return pl.pallas_call(
        _attn_sample_kernel,
        grid=(ndb,),
        in_specs=[spec, spec, spec, cspec, cspec,
                  pl.BlockSpec(bias.shape, lambda b: (0, 0, 0))],
        out_specs=pl.BlockSpec((CHUNK, GROUP_W), lambda b: (b, 0)),
        out_shape=jax.ShapeDtypeStruct((ndb * CHUNK, GROUP_W), BF16),
        compiler_params=_cparams(("arbitrary",), VMEM_LIMIT),
        name="attn_sample",
    )(qa, ka, va, kc, vc, bias)


def _ret_kernel(state_decay, q_ref, k_ref, v_ref, g_ref, s0_ref, dm_ref, xi_ref, zeta_ref,
                gro_ref, o_ref, sn_ref, s_scr):
    j = pl.program_id(1)

    @pl.when(j == 0)
    def _():
        s_scr[...] = s0_ref[0]

    outs = []
    for hh in range(N_HEADS_RET):
        hs = slice(hh * HEAD_DIM_RET, (hh + 1) * HEAD_DIM_RET)
        q = q_ref[:, hs]
        k = k_ref[:, hs]
        v = v_ref[:, hs]
        st = s_scr[hh]
        sc = lax.dot_general(q, k, (((1,), (1,)), ((), ())), preferred_element_type=F32) * dm_ref[hh]
        inner = jnp.dot(sc.astype(BF16), v, preferred_element_type=F32)
        cross = jnp.dot(q, st.astype(BF16), preferred_element_type=F32) * xi_ref[:, hs]
        o = inner + cross
        kz = k.astype(F32) * zeta_ref[:, hs]
        s_scr[hh] = state_decay[hh] * st + jnp.dot(kz.T.astype(BF16), v, preferred_element_type=F32)
        mu = jnp.mean(o, axis=-1, keepdims=True)
        oc = o - mu
        var = jnp.mean(oc * oc, axis=-1, keepdims=True)
        outs.append(oc * lax.rsqrt(var + NORM_EPS))
    y = jnp.concatenate(outs, axis=1) * gro_ref[...]
    g = g_ref[...].astype(F32)
    o_ref[...] = (g * jax.nn.sigmoid(g) * y).astype(BF16)

    @pl.when(j == pl.num_programs(1) - 1)
    def _():
        sn_ref[0] = s_scr[...]


def _ret_consts(cb):
    log_g = np.log1p(-np.exp2(-RET_DECAY_OFFSET - np.arange(N_HEADS_RET, dtype=np.float64)))
    n = np.arange(cb, dtype=np.float64)
    diff = n[:, None] - n[None, :]
    dm = np.where(diff[None] >= 0, np.exp(np.maximum(diff, 0.0)[None] * log_g[:, None, None]), 0.0)
    xi = np.exp((n + 1.0)[:, None] * log_g[None, :])
    zeta = np.exp((cb - 1.0 - n)[:, None] * log_g[None, :])
    rep = lambda a: np.repeat(a, HEAD_DIM_RET, axis=1)
    state_decay = tuple(float(v) for v in np.exp(cb * log_g))
    return (jnp.asarray(dm, F32), jnp.asarray(rep(xi), F32), jnp.asarray(rep(zeta), F32), state_decay)


def _ret(qb, kb, vb, gb, s0, g_ro, cb, row0, nb, nc, name):
    dm, xi, zeta, state_decay = _ret_consts(cb)
    base = row0 // cb
    spec = pl.BlockSpec((cb, GROUP_W), lambda b, j: (base + b * nc + j, 0))
    sspec = pl.BlockSpec((1, N_HEADS_RET, HEAD_DIM_RET, HEAD_DIM_RET), lambda b, j: (b, 0, 0, 0))
    full2 = lambda b, j: (0, 0)
    return pl.pallas_call(
        functools.partial(_ret_kernel, state_decay),
        grid=(nb, nc),
        in_specs=[spec, spec, spec, spec, sspec,
                  pl.BlockSpec(dm.shape, lambda b, j: (0, 0, 0)),
                  pl.BlockSpec(xi.shape, full2), pl.BlockSpec(zeta.shape, full2),
                  pl.BlockSpec((1, GROUP_W), full2)],
        out_specs=[pl.BlockSpec((cb, GROUP_W), lambda b, j: (b * nc + j, 0)), sspec],
        out_shape=[jax.ShapeDtypeStruct((nb * nc * cb, GROUP_W), BF16),
                   jax.ShapeDtypeStruct(s0.shape, F32)],
        scratch_shapes=[pltpu.VMEM((N_HEADS_RET, HEAD_DIM_RET, HEAD_DIM_RET), F32)],
        compiler_params=_cparams(("arbitrary", "arbitrary"), VMEM_LIMIT),
        name=name,
    )(qb, kb, vb, gb, s0, dm, xi, zeta, g_ro)


def _outproj_kernel(ntp, attp_ref, atts_ref, retp_ref, rets_ref, xp_ref, xs_ref, gm_ref, shf_ref, scf_ref,
                    gn_ref, wo_ref, wr_ref, br_ref, upper_ref, lower_ref,
                    x1_ref, h2_ref, slot_ref, cols_ref, cnt_ref):
    i = pl.program_id(0)
    is_p = i < ntp
    x = jnp.where(is_p, xp_ref[...], xs_ref[...])
    att = jnp.where(is_p, attp_ref[...], atts_ref[...])
    ret = jnp.where(is_p, retp_ref[...], rets_ref[...])
    mix = (jnp.dot(att, wo_ref[:GROUP_W, :], preferred_element_type=F32)
           + jnp.dot(ret, wo_ref[GROUP_W:, :], preferred_element_type=F32))
    x1 = _per_group(mix, lambda a, gm: a * gm, gm_ref[...]) + x
    x1_ref[...] = x1
    y = _rms_rows(x1, gn_ref[...])
    h2 = _per_group(y, lambda a, sh, sc: a * (1.0 + sc) + sh, shf_ref[...], scf_ref[...])
    h2b = h2.astype(BF16)
    h2_ref[...] = h2b

    logits = lax.dot_general(wr_ref[...], h2b, (((1,), (1,)), ((), ())),
                             preferred_element_type=F32) + br_ref[...]
    eidx = lax.broadcasted_iota(I32, logits.shape, 0).astype(F32)
    work = logits
    sel, top = [], []
    for _ in range(TOP_K):
        m = jnp.max(work, axis=0, keepdims=True)
        idx = jnp.min(jnp.where(work == m, eidx, float(N_EXPERTS)), axis=0, keepdims=True)
        hit = eidx == idx
        sel.append(hit)
        top.append(m)
        work = jnp.where(hit, -jnp.inf, work)
    ex = [jnp.exp(t - top[0]) for t in top]
    den = ex[0] + ex[1] + ex[2] + ex[3]
    gates = [e / den for e in ex]

    multi = (sel[0] | sel[1] | sel[2] | sel[3])
    multi_f = jnp.where(multi, 1.0, 0.0)
    rank = jnp.dot(multi_f.astype(BF16), upper_ref[...], preferred_element_type=F32)
    cnt = jnp.sum(multi_f, axis=1, keepdims=True)
    cnt_pad = jnp.floor((cnt + (SEG_ALIGN - 1.0)) * (1.0 / SEG_ALIGN)) * SEG_ALIGN
    cnt_pad_b = jnp.broadcast_to(cnt_pad, (N_EXPERTS, 128))
    seg_off = jnp.dot(lower_ref[...], cnt_pad_b.astype(BF16), preferred_element_type=F32)[:, :1]
    pos = seg_off + rank
    slots = [jnp.sum(jnp.where(s, pos, 0.0), axis=0, keepdims=True) for s in sel]
    slot_rows = jnp.concatenate(slots, axis=0)
    gate_rows = jnp.concatenate(gates, axis=0)
    slot_ref[0] = slot_rows.astype(I32)
    cnt_ref[0] = cnt_pad_b.astype(I32)
    both = jnp.concatenate([slot_rows, gate_rows, jnp.zeros((128 - 2 * TOP_K, TM), F32)], axis=0)
    cols_ref[0] = both.T


def _outproj(att_p, att_s, ret_p, ret_s, xp, xs, gate_m, shift_f, scale_f, g_norm, w_out_b, wr_t, br,
             upper, lower):
    rp = xp.shape[0]
    ntp = rp // TM
    nt = ntp + 1
    r = rp + TM
    row = lambda i: (i, 0)
    row3 = lambda i: (i, 0, 0)
    full = lambda i: (0, 0)
    prow = lambda i: (jnp.minimum(i, ntp - 1), 0)
    mod = pl.BlockSpec((GROUPS_PER_TILE, D_MODEL), row)
    return pl.pallas_call(
        functools.partial(_outproj_kernel, ntp),
        grid=(nt,),
        in_specs=[pl.BlockSpec((TM, GROUP_W), prow), pl.BlockSpec((TM, GROUP_W), full),
                  pl.BlockSpec((TM, GROUP_W), prow), pl.BlockSpec((TM, GROUP_W), full),
                  pl.BlockSpec((TM, D_MODEL), prow),
                  pl.BlockSpec((TM, D_MODEL), full),
                  mod, mod, mod,
                  pl.BlockSpec((1, D_MODEL), full),
                  pl.BlockSpec((D_MODEL, D_MODEL), full),
                  pl.BlockSpec((N_EXPERTS, D_MODEL), full),
                  pl.BlockSpec((N_EXPERTS, 1), full),
                  pl.BlockSpec((TM, TM), full),
                  pl.BlockSpec((N_EXPERTS, N_EXPERTS), full)],
        out_specs=[pl.BlockSpec((TM, D_MODEL), row), pl.BlockSpec((TM, D_MODEL), row),
                   pl.BlockSpec((1, TOP_K, TM), row3),
                   pl.BlockSpec((1, TM, 128), row3), pl.BlockSpec((1, N_EXPERTS, 128), row3)],
        out_shape=[jax.ShapeDtypeStruct((r, D_MODEL), F32), jax.ShapeDtypeStruct((r, D_MODEL), BF16),
                   jax.ShapeDtypeStruct((nt, TOP_K, TM), I32),
                   jax.ShapeDtypeStruct((nt, TM, 128), F32), jax.ShapeDtypeStruct((nt, N_EXPERTS, 128), I32)],
        compiler_params=_cparams(("arbitrary",), VMEM_LIMIT),
        name="outproj",
    )(att_p, att_s, ret_p, ret_s, xp, xs, gate_m, shift_f, scale_f, g_norm, w_out_b, wr_t, br, upper, lower)


def _segment_copies(n, src_at, dst_at, sem, bits, wait):
    for b in range(bits):
        size = SEG_ALIGN << b
        start = pl.multiple_of((n >> (b + 5)) << (b + 5), SEG_ALIGN)

        @pl.when(((n >> (b + 4)) & 1) == 1)
        def _(size=size, start=start):
            cp = pltpu.make_async_copy(src_at(start, size), dst_at(start, size), sem)
            if wait:
                cp.wait()
            else:
                cp.start()


def _dispatch_kernel(nt, n_blocks, off_ref, cnt_ref, base_ref, tail0_ref, tailn_ref, na_ref,
                     h2_ref, slot_ref, xb_ref, xs_scr, zero_scr, sems, tail_sem):
    i = pl.program_id(0)
    cur = i % 2

    def tile_copies(t, buf, wait):
        def body(e, c):
            n = cnt_ref[t * N_EXPERTS + e]
            off = pl.multiple_of(off_ref[t * N_EXPERTS + e], SEG_ALIGN)
            base = pl.multiple_of(base_ref[t * N_EXPERTS + e], SEG_ALIGN)
            _segment_copies(
                n,
                lambda s, z: xs_scr.at[buf, pl.ds(off + s, z), :],
                lambda s, z: xb_ref.at[pl.ds(base + s, z), :],
                sems.at[buf], SEG_BITS, wait)
            return c
        lax.fori_loop(0, N_EXPERTS, body, 0)

    def tail_copies(wait):
        def body(e, c):
            n = tailn_ref[e]
            base = pl.multiple_of(tail0_ref[e], SEG_ALIGN)
            _segment_copies(
                n,
                lambda s, z: zero_scr.at[pl.ds(s, z), :],
                lambda s, z: xb_ref.at[pl.ds(base + s, z), :],
                tail_sem, SEG_BITS - 1, wait)
            return c
        lax.fori_loop(0, N_EXPERTS, body, 0)

        def unused(j, c):
            cp = pltpu.make_async_copy(zero_scr, xb_ref.at[pl.ds(pl.multiple_of(j * BM, BM), BM), :],
                                       tail_sem)
            if wait:
                cp.wait()
            else:
                cp.start()
            return c
        lax.fori_loop(na_ref[0], n_blocks, unused, 0)

    @pl.when(i >= 2)
    def _():
        tile_copies(i - 2, cur, True)

    slot = slot_ref[0]
    srow = lax.broadcasted_iota(I32, (CAP, TM), 0)
    hit = (srow == slot[0:1]) | (srow == slot[1:2]) | (srow == slot[2:3]) | (srow == slot[3:4])
    onehot = jnp.where(hit, 1.0, 0.0).astype(BF16)
    xs_scr[cur] = jnp.dot(onehot, h2_ref[...], preferred_element_type=F32).astype(BF16)
    tile_copies(i, cur, False)

    @pl.when(i == nt - 1)
    def _():
        zero_scr[...] = jnp.zeros_like(zero_scr)
        tail_copies(False)
        if nt >= 2:
            tile_copies(i - 1, 1 - cur, True)
        tile_copies(i, cur, True)
        tail_copies(True)


def _dispatch(h2, slot, off, cnt, base, tail0, tailn, n_act, n_blocks):
    nt = slot.shape[0]
    n_rows = n_blocks * BM
    grid_spec = pltpu.PrefetchScalarGridSpec(
        num_scalar_prefetch=6,
        grid=(nt,),
        in_specs=[pl.BlockSpec((TM, D_MODEL), lambda i, *_: (i, 0)),
                  pl.BlockSpec((1, TOP_K, TM), lambda i, *_: (i, 0, 0))],
        out_specs=pl.BlockSpec(memory_space=pl.ANY),
        scratch_shapes=[pltpu.VMEM((2, CAP, D_MODEL), BF16),
                        pltpu.VMEM((BM, D_MODEL), BF16),
                        pltpu.SemaphoreType.DMA((2,)),
                        pltpu.SemaphoreType.DMA(())],
    )
    return pl.pallas_call(
        functools.partial(_dispatch_kernel, nt, n_blocks),
        grid_spec=grid_spec,
        out_shape=jax.ShapeDtypeStruct((n_rows, D_MODEL), BF16),
        compiler_params=_cparams(("arbitrary",), VMEM_LIMIT),
        name="dispatch",
    )(off, cnt, base, tail0, tailn, n_act, h2, slot)


def _experts_kernel(be_ref, bi_ref, na_ref, x_ref, wu_ref, bu_ref, wd_ref, bd_ref, y_ref, wu_scr, wd_scr):
    j = pl.program_id(0)

    @pl.when(j < na_ref[0])
    def _():
        prev = be_ref[jnp.maximum(j - 1, 0)]

        @pl.when((j == 0) | (be_ref[j] != prev))
        def _():
            wu_scr[...] = wu_ref[0].astype(BF16)
            wd_scr[...] = wd_ref[0].astype(BF16)

        u = jnp.dot(x_ref[...], wu_scr[...], preferred_element_type=F32) + bu_ref[0]
        glu = jnp.minimum(u[:, :D_FF], SWIGLU_LIMIT)
        lin = jnp.clip(u[:, D_FF:], -SWIGLU_LIMIT, SWIGLU_LIMIT)
        act = glu * jax.nn.sigmoid(SWIGLU_ALPHA * glu) * (lin + 1.0)
        y = jnp.dot(act.astype(BF16), wd_scr[...], preferred_element_type=F32) + bd_ref[0]
        y_ref[...] = y.astype(BF16)

    @pl.when(j >= na_ref[0])
    def _():
        y_ref[...] = jnp.zeros_like(y_ref)


def _experts(xb, blk_e, blk_i, n_act, w_up, b_up, w_down, b_down):
    n_rows = xb.shape[0]
    nblk = n_rows // BM
    grid_spec = pltpu.PrefetchScalarGridSpec(
        num_scalar_prefetch=3,
        grid=(nblk,),
        in_specs=[pl.BlockSpec((BM, D_MODEL), lambda j, be, bi, na: (bi[j], 0)),
                  pl.BlockSpec((1, D_MODEL, 2 * D_FF), lambda j, be, bi, na: (be[j], 0, 0)),
                  pl.BlockSpec((1, 1, 2 * D_FF), lambda j, be, bi, na: (be[j], 0, 0)),
                  pl.BlockSpec((1, D_FF, D_MODEL), lambda j, be, bi, na: (be[j], 0, 0)),
                  pl.BlockSpec((1, 1, D_MODEL), lambda j, be, bi, na: (be[j], 0, 0))],
        out_specs=pl.BlockSpec((BM, D_MODEL), lambda j, be, bi, na: (j, 0)),
        scratch_shapes=[pltpu.VMEM((D_MODEL, 2 * D_FF), BF16), pltpu.VMEM((D_FF, D_MODEL), BF16)],
    )
    return pl.pallas_call(
        _experts_kernel,
        grid_spec=grid_spec,
        out_shape=jax.ShapeDtypeStruct((n_rows, D_MODEL), BF16),
        compiler_params=_cparams(("arbitrary",), VMEM_LIMIT),
        name="experts",
    )(blk_e, blk_i, n_act, xb, w_up, b_up.reshape(N_EXPERTS, 1, 2 * D_FF), w_down,
      b_down.reshape(N_EXPERTS, 1, D_MODEL))


def _combine_kernel(nt, ntp, off_ref, cnt_ref, base_ref, yb_ref, cols_ref, x1_ref, gf_ref,
                    op_ref, os_ref, ys_scr, sems):
    i = pl.program_id(0)
    cur = i % 2

    def tile_copies(t, buf, wait):
        def body(e, c):
            n = cnt_ref[t * N_EXPERTS + e]
            off = pl.multiple_of(off_ref[t * N_EXPERTS + e], SEG_ALIGN)
            base = pl.multiple_of(base_ref[t * N_EXPERTS + e], SEG_ALIGN)
            _segment_copies(
                n,
                lambda s, z: yb_ref.at[pl.ds(base + s, z), :],
                lambda s, z: ys_scr.at[buf, pl.ds(off + s, z), :],
                sems.at[buf], SEG_BITS, wait)
            return c
        lax.fori_loop(0, N_EXPERTS, body, 0)

    @pl.when(i == 0)
    def _():
        ys_scr[...] = jnp.zeros_like(ys_scr)
        tile_copies(0, 0, False)

    @pl.when(i + 1 < nt)
    def _():
        tile_copies(i + 1, 1 - cur, False)

    tile_copies(i, cur, True)

    cols = cols_ref[0]
    lane = lax.broadcasted_iota(I32, (TM, CAP), 1)
    w = jnp.zeros((TM, CAP), F32)
    for k in range(TOP_K):
        sk = cols[:, k:k + 1].astype(I32)
        gk = cols[:, TOP_K + k:TOP_K + k + 1]
        w = jnp.where(lane == sk, gk, w)
    y = jnp.dot(w.astype(BF16), ys_scr[cur], preferred_element_type=F32)
    out = x1_ref[...] + _per_group(y, lambda a, gf: a * gf, gf_ref[...])

    @pl.when(i < ntp)
    def _():
        op_ref[...] = out

    @pl.when(i >= ntp)
    def _():
        os_ref[...] = out


def _combine(yb, cols, x1, gate_f, off, cnt, base, ntp):
    nt = cols.shape[0]
    grid_spec = pltpu.PrefetchScalarGridSpec(
        num_scalar_prefetch=3,
        grid=(nt,),
        in_specs=[pl.BlockSpec(memory_space=pl.ANY),
                  pl.BlockSpec((1, TM, 128), lambda i, *_: (i, 0, 0)),
                  pl.BlockSpec((TM, D_MODEL), lambda i, *_: (i, 0)),
                  pl.BlockSpec((GROUPS_PER_TILE, D_MODEL), lambda i, *_: (i, 0))],
        out_specs=[pl.BlockSpec((TM, D_MODEL), lambda i, *_: (jnp.minimum(i, ntp - 1), 0)),
                   pl.BlockSpec((TM, D_MODEL), lambda i, *_: (0, 0))],
        scratch_shapes=[pltpu.VMEM((2, CAP, D_MODEL), BF16), pltpu.SemaphoreType.DMA((2,))],
    )
    return pl.pallas_call(
        functools.partial(_combine_kernel, nt, ntp),
        grid_spec=grid_spec,
        out_shape=[jax.ShapeDtypeStruct((ntp * TM, D_MODEL), F32),
                   jax.ShapeDtypeStruct((TM, D_MODEL), F32)],
        compiler_params=_cparams(("arbitrary",), VMEM_LIMIT),
        name="combine",
    )(off, cnt, base, yb, cols, x1, gate_f)


def _rotary_tables(seq, dec_batch, dec_seq):
    half = HEAD_DIM_RET // 2
    inv = ROPE_BASE ** (-np.arange(half, dtype=np.float64) / half)
    pos = np.concatenate([np.arange(seq), np.tile(PAST_LEN + np.arange(dec_seq), dec_batch)])
    ang = pos.astype(np.float64)[:, None] * inv[None, :]
    cos = np.concatenate([np.cos(ang), np.cos(ang)], axis=1)
    sin = np.concatenate([-np.sin(ang), np.sin(ang)], axis=1)
    return jnp.asarray(cos, F32), jnp.asarray(sin, F32)


def _band_bias(rel_bias, qb):
    kb = ATT_WINDOW + qb
    q = np.arange(qb)[:, None]
    k = np.arange(kb)[None, :]
    rel = np.clip(q - (k - ATT_WINDOW), -MAX_REL, MAX_REL) + MAX_REL
    qc = q // CHUNK
    kc = (k - ATT_WINDOW) // CHUNK
    band = (kc >= qc - ATT_WINDOW // CHUNK) & (kc <= qc)
    bias = rel_bias.astype(F32)[:, rel]
    return jnp.where(band[None], bias, NEG_INF)


def _routing_tables(cnt, n_blocks):
    nt = cnt.shape[0]
    off = jnp.cumsum(cnt, axis=1) - cnt
    rows_e = jnp.sum(cnt, axis=0)
    nblk_e = (rows_e + BM - 1) // BM
    blk_end = jnp.cumsum(nblk_e)
    start_e = (blk_end - nblk_e) * BM
    base = start_e[None, :] + jnp.cumsum(cnt, axis=0) - cnt
    n_act = blk_end[-1]
    j = jnp.minimum(jnp.arange(n_blocks), n_act - 1)
    blk_e = jnp.minimum(jnp.searchsorted(blk_end, j, side='right'), N_EXPERTS - 1)
    tail0 = start_e + rows_e
    tailn = nblk_e * BM - rows_e
    i32 = lambda a: a.astype(I32)
    return (i32(off.reshape(nt * N_EXPERTS)), i32(cnt.reshape(nt * N_EXPERTS)),
            i32(base.reshape(nt * N_EXPERTS)), i32(tail0), i32(tailn),
            i32(blk_e), i32(j), i32(n_act.reshape(1)))


def kernel(x_prompt, x_sample, c_prompt, c_sample, cache_att_k, cache_att_v, state_ret, w_ada, b_ada,
           g_norm_mix, g_norm_ffn, w_in, g_q, g_k, rel_bias, g_ret_out, w_out, w_router, b_router,
           w_up, b_up, w_down, b_down):
    nb, seq, d = x_prompt.shape
    ndb, dseq, _ = x_sample.shape
    assert d == D_MODEL and ndb * dseq == TM and dseq == CHUNK
    assert seq % TM == 0 and seq >= ATT_WINDOW and cache_att_k.shape[2] == ATT_WINDOW
    assert w_ada.shape[0] == 1
    rp = nb * seq
    ntp = rp // TM
    nt = ntp + 1
    tps = seq // TM

    xp = x_prompt.reshape(rp, d)
    xs = x_sample.reshape(TM, d)

    m = _ada(jnp.concatenate([c_prompt, c_sample], axis=0), w_ada[0], b_ada[0])
    m = m.reshape(nb + ndb, N_ADA, d)
    group_seq = np.concatenate([np.repeat(np.arange(nb), seq // CHUNK), nb + np.arange(ndb)])
    mods = [m[group_seq, a, :] for a in range(N_ADA)]
    shift_m, scale_m, gate_m, shift_f, scale_f, gate_f = mods

    cos_t, sin_t = _rotary_tables(seq, ndb, dseq)
    bd = jnp.asarray(np.kron(np.eye(N_HEADS_ATT), np.ones((HEAD_DIM_ATT, HEAD_DIM_ATT))), BF16)
    tile8 = lambda g: jnp.tile(g.astype(F32), N_HEADS_ATT).reshape(1, GROUP_W)
    (qa, ka, va, qb, kb, vb, gb, kp_tail, vp_tail, ks_new, vs_new) = _inproj(
        xp, xs, shift_m, scale_m, g_norm_mix[0].reshape(1, d), w_in[0].astype(BF16), bd,
        tile8(g_q[0]), tile8(g_k[0]), cos_t, sin_t, nb, tps)

    att_p = _attn_prompt(qa, ka, va, _band_bias(rel_bias[0], ATT_QB), nb, seq)
    att_s = _attn_sample(qa, ka, va,
                         cache_att_k[0].reshape(ndb, ATT_WINDOW, GROUP_W).astype(BF16),
                         cache_att_v[0].reshape(ndb, ATT_WINDOW, GROUP_W).astype(BF16),
                         _band_bias(rel_bias[0], CHUNK), rp)

    g_ro = g_ret_out[0].astype(F32).reshape(1, GROUP_W)
    zero_state = jnp.zeros((nb, N_HEADS_RET, HEAD_DIM_RET, HEAD_DIM_RET), F32)
    ret_p, state_p = _ret(qb, kb, vb, gb, zero_state, g_ro, RET_CB, 0, nb, seq // RET_CB, "ret_prompt")
    ret_s, state_s = _ret(qb, kb, vb, gb, state_ret[0].astype(F32), g_ro, CHUNK, rp, ndb, 1, "ret_sample")

    upper = jnp.asarray(np.triu(np.ones((TM, TM)), 1), BF16)
    lower = jnp.asarray(np.tril(np.ones((N_EXPERTS, N_EXPERTS)), -1), BF16)
    x1, h2, slot, cols, cnt = _outproj(
        att_p, att_s, ret_p, ret_s, xp, xs, gate_m, shift_f, scale_f, g_norm_ffn[0].reshape(1, d),
        w_out[0].astype(BF16), w_router[0].T.astype(BF16), b_router[0].astype(F32).reshape(N_EXPERTS, 1),
        upper, lower)

    n_blocks = (TOP_K * (rp + TM) + nt * N_EXPERTS * (SEG_ALIGN - 1)) // BM + 1 + N_EXPERTS
    off, cntf, base, tail0, tailn, blk_e, blk_i, n_act = _routing_tables(cnt[:, :, 0], n_blocks)
    xb = _dispatch(h2, slot, off, cntf, base, tail0, tailn, n_act, n_blocks)
    yb = _experts(xb, blk_e, blk_i, n_act, w_up[0], b_up[0], w_down[0], b_down[0])
    out_p, out_s = _combine(yb, cols, x1, gate_f, off, cntf, base, ntp)

    heads = (N_HEADS_ATT, HEAD_DIM_ATT)
    return (out_p.reshape(nb, seq, d), out_s.reshape(ndb, dseq, d),
            kp_tail.reshape(1, nb, ATT_WINDOW, *heads), vp_tail.reshape(1, nb, ATT_WINDOW, *heads),
            state_p[None],
            ks_new.reshape(1, ndb, dseq, *heads), vs_new.reshape(1, ndb, dseq, *heads),
            state_s[None])
```

```python
import functools

import numpy as np
import jax
import jax.numpy as jnp
from jax import lax
from jax.experimental import pallas as pl
from jax.experimental.pallas import tpu as pltpu

F32 = jnp.float32
BF16 = jnp.bfloat16
I32 = jnp.int32

D_MODEL = 1024
GROUP_W = 512
N_SLOTS = 7
N_HEADS_ATT = 8
HEAD_DIM_ATT = 64
N_HEADS_RET = 4
HEAD_DIM_RET = 128
CHUNK = 64
ATT_WINDOW = 512
MAX_REL = 256
PAST_LEN = 2048
RET_DECAY_OFFSET = 5.0
ROPE_BASE = 10000.0
N_EXPERTS = 32
TOP_K = 4
D_FF = 1024
SWIGLU_LIMIT = 7.0
SWIGLU_ALPHA = 1.702
N_ADA = 6
NORM_EPS = 1e-6
NEG_INF = -1e30

TM = 512
GROUPS_PER_TILE = TM // CHUNK
ATT_QB = 256
RET_CB = 256
SEG_ALIGN = 16
SEG_BITS = 6
CAP = TOP_K * TM + N_EXPERTS * SEG_ALIGN
BM = 512
VMEM_LIMIT = 56 * 1024 * 1024


def _cparams(sem, vmem=None):
    return pltpu.CompilerParams(dimension_semantics=sem, vmem_limit_bytes=vmem)


def _ada_kernel(c_ref, w_ref, b_ref, o_ref):
    c = c_ref[...]
    s = c * jax.nn.sigmoid(c)
    o_ref[...] = jnp.dot(s.astype(BF16), w_ref[...].astype(BF16),
                         preferred_element_type=F32) + b_ref[...]


def _ada(c_all, w_ada, b_ada):
    n, d = c_all.shape
    cols = w_ada.shape[1]
    tn = 1536
    return pl.pallas_call(
        _ada_kernel,
        grid=(cols // tn,),
        in_specs=[pl.BlockSpec((n, d), lambda j: (0, 0)),
                  pl.BlockSpec((d, tn), lambda j: (0, j)),
                  pl.BlockSpec((1, tn), lambda j: (0, j))],
        out_specs=pl.BlockSpec((n, tn), lambda j: (0, j)),
        out_shape=jax.ShapeDtypeStruct((n, cols), F32),
        compiler_params=_cparams(("arbitrary",), VMEM_LIMIT),
        name="ada",
    )(c_all, w_ada, b_ada.reshape(1, cols))


def _rms_rows(x, g):
    ms = jnp.mean(x * x, axis=-1, keepdims=True)
    return x * lax.rsqrt(ms + NORM_EPS) * g


def _mod_row(ntp, tps, nb):
    return lambda i, *_: (jnp.where(i < ntp, i // tps, nb), 0)


def _per_group(x, fn, *mods):
    x3 = x.reshape(GROUPS_PER_TILE, CHUNK, x.shape[-1])
    y3 = fn(x3, *[m[:, None, :] for m in mods])
    return y3.reshape(x.shape)


def _inproj_kernel(ntp, xp_ref, xs_ref, sh_ref, sc_ref, gn_ref, w_ref, bd_ref, gq_ref, gk_ref,
                   cos_ref, sin_ref,
                   qa_ref, ka_ref, va_ref, qb_ref, kb_ref, vb_ref, gb_ref,
                   kpt_ref, vpt_ref, kst_ref, vst_ref):
    i = pl.program_id(0)
    is_p = i < ntp
    x = jnp.where(is_p, xp_ref[...], xs_ref[...])
    y = _rms_rows(x, gn_ref[...])
    h = _per_group(y, lambda a, sh, sc: a * (1.0 + sc) + sh, sh_ref[...], sc_ref[...])
    hb = h.astype(BF16)

    def proj(s):
        return jnp.dot(hb, w_ref[:, s * GROUP_W:(s + 1) * GROUP_W], preferred_element_type=F32)

    def head_rms(z, g):
        ss = jnp.dot((z * z).astype(BF16), bd_ref[...], preferred_element_type=F32)
        return z * lax.rsqrt(ss * (1.0 / HEAD_DIM_ATT) + NORM_EPS) * g

    cos = cos_ref[...]
    sin = sin_ref[...]

    def rot(z):
        outs = []
        for hh in range(N_HEADS_RET):
            zh = z[:, hh * HEAD_DIM_RET:(hh + 1) * HEAD_DIM_RET]
            outs.append(zh * cos + pltpu.roll(zh, HEAD_DIM_RET // 2, axis=1) * sin)
        return jnp.concatenate(outs, axis=1)

    qa_ref[...] = head_rms(proj(0), gq_ref[...]).astype(BF16)
    ka = head_rms(proj(1), gk_ref[...])
    ka_ref[...] = ka.astype(BF16)
    va = proj(2)
    va_ref[...] = va.astype(BF16)

    @pl.when(is_p)
    def _():
        kpt_ref[...] = ka
        vpt_ref[...] = va

    @pl.when(jnp.logical_not(is_p))
    def _():
        kst_ref[...] = ka
        vst_ref[...] = va

    qb_ref[...] = rot(proj(3)).astype(BF16)
    kb_ref[...] = (rot(proj(4)) * (HEAD_DIM_RET ** -0.5)).astype(BF16)
    vb_ref[...] = proj(5).astype(BF16)
    gb_ref[...] = proj(6).astype(BF16)


def _inproj(xp, xs, shift, scale, g_norm, w_in_b, bd, gq8, gk8, cos_t, sin_t, nb, tps):
    rp = xp.shape[0]
    ntp = rp // TM
    nt = ntp + 1
    r = rp + TM
    row = lambda i: (i, 0)
    full = lambda i: (0, 0)
    tab = lambda i: (jnp.where(i < ntp, i % tps, tps), 0)
    act = jax.ShapeDtypeStruct((r, GROUP_W), BF16)
    return pl.pallas_call(
        functools.partial(_inproj_kernel, ntp),
        grid=(nt,),
        in_specs=[pl.BlockSpec((TM, D_MODEL), lambda i: (jnp.minimum(i, ntp - 1), 0)),
                  pl.BlockSpec((TM, D_MODEL), full),
                  pl.BlockSpec((GROUPS_PER_TILE, D_MODEL), _mod_row(ntp, tps, nb)),
                  pl.BlockSpec((GROUPS_PER_TILE, D_MODEL), _mod_row(ntp, tps, nb)),
                  pl.BlockSpec((1, D_MODEL), full),
                  pl.BlockSpec((D_MODEL, N_SLOTS * GROUP_W), full),
                  pl.BlockSpec((GROUP_W, GROUP_W), full),
                  pl.BlockSpec((1, GROUP_W), full),
                  pl.BlockSpec((1, GROUP_W), full),
                  pl.BlockSpec((TM, HEAD_DIM_RET), tab),
                  pl.BlockSpec((TM, HEAD_DIM_RET), tab)],
        out_specs=[pl.BlockSpec((TM, GROUP_W), row)] * 7 + [
            pl.BlockSpec((TM, GROUP_W), lambda i: (jnp.minimum(i // tps, nb - 1), 0)),
            pl.BlockSpec((TM, GROUP_W), lambda i: (jnp.minimum(i // tps, nb - 1), 0)),
            pl.BlockSpec((TM, GROUP_W), full),
            pl.BlockSpec((TM, GROUP_W), full)],
        out_shape=[act] * 7 + [jax.ShapeDtypeStruct((nb * TM, GROUP_W), F32)] * 2
        + [jax.ShapeDtypeStruct((TM, GROUP_W), F32)] * 2,
        compiler_params=_cparams(("arbitrary",), VMEM_LIMIT),
        name="inproj",
    )(xp, xs, shift, scale, g_norm, w_in_b, bd, gq8, gk8, cos_t, sin_t)


def _attn_heads(q, k, v, bias_ref, first_valid_col):
    qb_rows, kb_rows = q.shape[0], k.shape[0]
    col = lax.broadcasted_iota(I32, (qb_rows, kb_rows), 1)
    valid = col >= first_valid_col
    outs = []
    for hh in range(N_HEADS_ATT):
        hs = slice(hh * HEAD_DIM_ATT, (hh + 1) * HEAD_DIM_ATT)
        s = lax.dot_general(q[:, hs], k[:, hs], (((1,), (1,)), ((), ())),
                            preferred_element_type=F32)
        s = s * (HEAD_DIM_ATT ** -0.5) + bias_ref[hh]
        s = jnp.where(valid, s, NEG_INF)
        m = jnp.max(s, axis=-1, keepdims=True)
        e = jnp.exp(s - m)
        l = jnp.sum(e, axis=-1, keepdims=True)
        o = jnp.dot(e.astype(BF16), v[:, hs], preferred_element_type=F32)
        outs.append(o / l)
    return jnp.concatenate(outs, axis=1)


def _fill_band_bias(rev_ref, bias_scr):
    _, qb_rows, kb_rows = bias_scr.shape
    width = rev_ref.shape[1]
    q = lax.broadcasted_iota(I32, (qb_rows, kb_rows), 0)
    k = lax.broadcasted_iota(I32, (qb_rows, kb_rows), 1)
    qc = q >> 6
    kc = (k - ATT_WINDOW) >> 6
    band = (kc >= qc - ATT_WINDOW // CHUNK) & (kc <= qc)
    for hh in range(N_HEADS_ATT):
        rows = jnp.broadcast_to(rev_ref[hh:hh + 1, :], (qb_rows, width))
        toep = pltpu.roll(rows, width - MAX_REL, 1, stride=1, stride_axis=0)
        bias_scr[hh] = jnp.where(band, toep[:, :kb_rows], NEG_INF)


def _attn_prompt_kernel(q_ref, k0_ref, k1_ref, k2_ref, v0_ref, v1_ref, v2_ref, rev_ref, o_ref, bias_scr):
    j = pl.program_id(1)

    @pl.when((pl.program_id(0) == 0) & (j == 0))
    def _():
        _fill_band_bias(rev_ref, bias_scr)

    k = jnp.concatenate([k0_ref[...], k1_ref[...], k2_ref[...]], axis=0)
    v = jnp.concatenate([v0_ref[...], v1_ref[...], v2_ref[...]], axis=0)
    first_valid = (2 - j) * ATT_QB
    o_ref[...] = _attn_heads(q_ref[...], k, v, bias_scr, first_valid).astype(BF16)


def _attn_prompt(qa, ka, va, rev, nb, seq):
    r = nb * seq
    nq = seq // ATT_QB
    blk = lambda back: (lambda b, j: (b * nq + jnp.maximum(j - back, 0), 0))
    spec = lambda back: pl.BlockSpec((ATT_QB, GROUP_W), blk(back))
    return pl.pallas_call(
        _attn_prompt_kernel,
        grid=(nb, nq),
        in_specs=[spec(0), spec(2), spec(1), spec(0), spec(2), spec(1), spec(0),
                  pl.BlockSpec(rev.shape, lambda b, j: (0, 0))],
        out_specs=spec(0),
        out_shape=jax.ShapeDtypeStruct((r, GROUP_W), BF16),
        scratch_shapes=[pltpu.VMEM((N_HEADS_ATT, ATT_QB, ATT_WINDOW + ATT_QB), F32)],
        compiler_params=_cparams(("arbitrary", "arbitrary"), VMEM_LIMIT),
        name="attn_prompt",
    )(qa, ka, ka, ka, va, va, va, rev)


def _attn_sample_kernel(q_ref, kn_ref, vn_ref, kc_ref, vc_ref, rev_ref, o_ref, bias_scr):
    @pl.when(pl.program_id(0) == 0)
    def _():
        _fill_band_bias(rev_ref, bias_scr)

    k = jnp.concatenate([kc_ref[0], kn_ref[...]], axis=0)
    v = jnp.concatenate([vc_ref[0], vn_ref[...]], axis=0)
    o_ref[...] = _attn_heads(q_ref[...], k, v, bias_scr, 0).astype(BF16)


def _attn_sample(qa, ka, va, kc, vc, rev, rp):
    ndb = kc.shape[0]
    base = rp // CHUNK
    spec = pl.BlockSpec((CHUNK, GROUP_W), lambda b: (base + b, 0))
    cspec = pl.BlockSpec((1, ATT_WINDOW, GROUP_W), lambda b: (b, 0, 0))
    return pl.pallas_call(
        _attn_sample_kernel,
        grid=(ndb,),
        in_specs=[spec, spec, spec, cspec, cspec,
                  pl.BlockSpec(rev.shape, lambda b: (0, 0))],
        out_specs=pl.BlockSpec((CHUNK, GROUP_W), lambda b: (b, 0)),
        out_shape=jax.ShapeDtypeStruct((ndb * CHUNK, GROUP_W), BF16),
        scratch_shapes=[pltpu.VMEM((N_HEADS_ATT, CHUNK, ATT_WINDOW + CHUNK), F32)],
        compiler_params=_cparams(("arbitrary",), VMEM_LIMIT),
        name="attn_sample",
    )(qa, ka, va, kc, vc, rev)


def _ret_kernel(state_decay, q_ref, k_ref, v_ref, g_ref, s0_ref, dm_ref, xi_ref, zeta_ref,
                gro_ref, o_ref, sn_ref, s_scr):
    j = pl.program_id(1)

    @pl.when(j == 0)
    def _():
        s_scr[...] = s0_ref[0]

    outs = []
    for hh in range(N_HEADS_RET):
        hs = slice(hh * HEAD_DIM_RET, (hh + 1) * HEAD_DIM_RET)
        q = q_ref[:, hs]
        k = k_ref[:, hs]
        v = v_ref[:, hs]
        st = s_scr[hh]
        sc = lax.dot_general(q, k, (((1,), (1,)), ((), ())), preferred_element_type=F32) * dm_ref[hh]
        inner = jnp.dot(sc.astype(BF16), v, preferred_element_type=F32)
        cross = jnp.dot(q, st.astype(BF16), preferred_element_type=F32) * xi_ref[:, hs]
        o = inner + cross
        kz = k.astype(F32) * zeta_ref[:, hs]
        s_scr[hh] = state_decay[hh] * st + jnp.dot(kz.T.astype(BF16), v, preferred_element_type=F32)
        mu = jnp.mean(o, axis=-1, keepdims=True)
        oc = o - mu
        var = jnp.mean(oc * oc, axis=-1, keepdims=True)
        outs.append(oc * lax.rsqrt(var + NORM_EPS))
    y = jnp.concatenate(outs, axis=1) * gro_ref[...]
    g = g_ref[...].astype(F32)
    o_ref[...] = (g * jax.nn.sigmoid(g) * y).astype(BF16)

    @pl.when(j == pl.num_programs(1) - 1)
    def _():
        sn_ref[0] = s_scr[...]


def _ret_consts(cb):
    log_g = np.log1p(-np.exp2(-RET_DECAY_OFFSET - np.arange(N_HEADS_RET, dtype=np.float64)))
    n = np.arange(cb, dtype=np.float64)
    diff = n[:, None] - n[None, :]
    dm = np.where(diff[None] >= 0, np.exp(np.maximum(diff, 0.0)[None] * log_g[:, None, None]), 0.0)
    xi = np.exp((n + 1.0)[:, None] * log_g[None, :])
    zeta = np.exp((cb - 1.0 - n)[:, None] * log_g[None, :])
    rep = lambda a: np.repeat(a, HEAD_DIM_RET, axis=1)
    state_decay = tuple(float(v) for v in np.exp(cb * log_g))
    return (jnp.asarray(dm, F32), jnp.asarray(rep(xi), F32), jnp.asarray(rep(zeta), F32), state_decay)


def _ret(qb, kb, vb, gb, s0, g_ro, cb, row0, nb, nc, name):
    dm, xi, zeta, state_decay = _ret_consts(cb)
    base = row0 // cb
    spec = pl.BlockSpec((cb, GROUP_W), lambda b, j: (base + b * nc + j, 0))
    sspec = pl.BlockSpec((1, N_HEADS_RET, HEAD_DIM_RET, HEAD_DIM_RET), lambda b, j: (b, 0, 0, 0))
    full2 = lambda b, j: (0, 0)
    return pl.pallas_call(
        functools.partial(_ret_kernel, state_decay),
        grid=(nb, nc),
        in_specs=[spec, spec, spec, spec, sspec,
                  pl.BlockSpec(dm.shape, lambda b, j: (0, 0, 0)),
                  pl.BlockSpec(xi.shape, full2), pl.BlockSpec(zeta.shape, full2),
                  pl.BlockSpec((1, GROUP_W), full2)],
        out_specs=[pl.BlockSpec((cb, GROUP_W), lambda b, j: (b * nc + j, 0)), sspec],
        out_shape=[jax.ShapeDtypeStruct((nb * nc * cb, GROUP_W), BF16),
                   jax.ShapeDtypeStruct(s0.shape, F32)],
        scratch_shapes=[pltpu.VMEM((N_HEADS_RET, HEAD_DIM_RET, HEAD_DIM_RET), F32)],
        compiler_params=_cparams(("arbitrary", "arbitrary"), VMEM_LIMIT),
        name=name,
    )(qb, kb, vb, gb, s0, dm, xi, zeta, g_ro)


def _outproj_kernel(ntp, attp_ref, atts_ref, retp_ref, rets_ref, xp_ref, xs_ref, gm_ref, shf_ref, scf_ref,
                    gn_ref, wo_ref, wr_ref, br_ref, upper_ref, lower_ref,
                    x1_ref, h2_ref, slot_ref, cols_ref, cnt_ref):
    i = pl.program_id(0)
    is_p = i < ntp
    x = jnp.where(is_p, xp_ref[...], xs_ref[...])
    att = jnp.where(is_p, attp_ref[...], atts_ref[...])
    ret = jnp.where(is_p, retp_ref[...], rets_ref[...])
    mix = (jnp.dot(att, wo_ref[:GROUP_W, :], preferred_element_type=F32)
           + jnp.dot(ret, wo_ref[GROUP_W:, :], preferred_element_type=F32))
    x1 = _per_group(mix, lambda a, gm: a * gm, gm_ref[...]) + x
    x1_ref[...] = x1
    y = _rms_rows(x1, gn_ref[...])
    h2 = _per_group(y, lambda a, sh, sc: a * (1.0 + sc) + sh, shf_ref[...], scf_ref[...])
    h2b = h2.astype(BF16)
    h2_ref[...] = h2b

    logits = lax.dot_general(wr_ref[...], h2b, (((1,), (1,)), ((), ())),
                             preferred_element_type=F32) + br_ref[...]
    eidx = lax.broadcasted_iota(I32, logits.shape, 0).astype(F32)
    work = logits
    sel, top = [], []
    for _ in range(TOP_K):
        m = jnp.max(work, axis=0, keepdims=True)
        idx = jnp.min(jnp.where(work == m, eidx, float(N_EXPERTS)), axis=0, keepdims=True)
        hit = eidx == idx
        sel.append(hit)
        top.append(m)
        work = jnp.where(hit, -jnp.inf, work)
    ex = [jnp.exp(t - top[0]) for t in top]
    den = ex[0] + ex[1] + ex[2] + ex[3]
    gates = [e / den for e in ex]

    multi = (sel[0] | sel[1] | sel[2] | sel[3])
    multi_f = jnp.where(multi, 1.0, 0.0)
    rank = jnp.dot(multi_f.astype(BF16), upper_ref[...], preferred_element_type=F32)
    cnt = jnp.sum(multi_f, axis=1, keepdims=True)
    cnt_pad = jnp.floor((cnt + (SEG_ALIGN - 1.0)) * (1.0 / SEG_ALIGN)) * SEG_ALIGN
    cnt_pad_b = jnp.broadcast_to(cnt_pad, (N_EXPERTS, 128))
    seg_off = jnp.dot(lower_ref[...], cnt_pad_b.astype(BF16), preferred_element_type=F32)[:, :1]
    pos = seg_off + rank
    slots = [jnp.sum(jnp.where(s, pos, 0.0), axis=0, keepdims=True) for s in sel]
    slot_rows = jnp.concatenate(slots, axis=0)
    gate_rows = jnp.concatenate(gates, axis=0)
    slot_ref[0] = slot_rows.astype(I32)
    cnt_ref[0] = cnt_pad_b.astype(I32)
    both = jnp.concatenate([slot_rows, gate_rows, jnp.zeros((128 - 2 * TOP_K, TM), F32)], axis=0)
    cols_ref[0] = both.T


def _outproj(att_p, att_s, ret_p, ret_s, xp, xs, gate_m, shift_f, scale_f, g_norm, w_out_b, wr_t, br,
             upper, lower, nb, tps):
    rp = xp.shape[0]
    ntp = rp // TM
    nt = ntp + 1
    r = rp + TM
    row = lambda i: (i, 0)
    row3 = lambda i: (i, 0, 0)
    full = lambda i: (0, 0)
    prow = lambda i: (jnp.minimum(i, ntp - 1), 0)
    mod = pl.BlockSpec((GROUPS_PER_TILE, D_MODEL), _mod_row(ntp, tps, nb))
    return pl.pallas_call(
        functools.partial(_outproj_kernel, ntp),
        grid=(nt,),
        in_specs=[pl.BlockSpec((TM, GROUP_W), prow), pl.BlockSpec((TM, GROUP_W), full),
                  pl.BlockSpec((TM, GROUP_W), prow), pl.BlockSpec((TM, GROUP_W), full),
                  pl.BlockSpec((TM, D_MODEL), prow),
                  pl.BlockSpec((TM, D_MODEL), full),
                  mod, mod, mod,
                  pl.BlockSpec((1, D_MODEL), full),
                  pl.BlockSpec((D_MODEL, D_MODEL), full),
                  pl.BlockSpec((N_EXPERTS, D_MODEL), full),
                  pl.BlockSpec((N_EXPERTS, 1), full),
                  pl.BlockSpec((TM, TM), full),
                  pl.BlockSpec((N_EXPERTS, N_EXPERTS), full)],
        out_specs=[pl.BlockSpec((TM, D_MODEL), row), pl.BlockSpec((TM, D_MODEL), row),
                   pl.BlockSpec((1, TOP_K, TM), row3),
                   pl.BlockSpec((1, TM, 128), row3), pl.BlockSpec((1, N_EXPERTS, 128), row3)],
        out_shape=[jax.ShapeDtypeStruct((r, D_MODEL), F32), jax.ShapeDtypeStruct((r, D_MODEL), BF16),
                   jax.ShapeDtypeStruct((nt, TOP_K, TM), I32),
                   jax.ShapeDtypeStruct((nt, TM, 128), F32), jax.ShapeDtypeStruct((nt, N_EXPERTS, 128), I32)],
        compiler_params=_cparams(("arbitrary",), VMEM_LIMIT),
        name="outproj",
    )(att_p, att_s, ret_p, ret_s, xp, xs, gate_m, shift_f, scale_f, g_norm, w_out_b, wr_t, br, upper, lower)


def _segment_copies(n, src_at, dst_at, sem, bits, wait):
    for b in range(bits):
        size = SEG_ALIGN << b
        start = pl.multiple_of((n >> (b + 5)) << (b + 5), SEG_ALIGN)

        @pl.when(((n >> (b + 4)) & 1) == 1)
        def _(size=size, start=start):
            cp = pltpu.make_async_copy(src_at(start, size), dst_at(start, size), sem)
            if wait:
                cp.wait()
            else:
                cp.start()


def _dispatch_kernel(nt, n_blocks, off_ref, cnt_ref, base_ref, tail0_ref, tailn_ref, na_ref,
                     h2_ref, slot_ref, xb_ref, xs_scr, zero_scr, sems, tail_sem):
    i = pl.program_id(0)
    cur = i % 2

    def tile_copies(t, buf, wait):
        def body(e, c):
            n = cnt_ref[t * N_EXPERTS + e]
            off = pl.multiple_of(off_ref[t * N_EXPERTS + e], SEG_ALIGN)
            base = pl.multiple_of(base_ref[t * N_EXPERTS + e], SEG_ALIGN)
            _segment_copies(
                n,
                lambda s, z: xs_scr.at[buf, pl.ds(off + s, z), :],
                lambda s, z: xb_ref.at[pl.ds(base + s, z), :],
                sems.at[buf], SEG_BITS, wait)
            return c
        lax.fori_loop(0, N_EXPERTS, body, 0)

    def tail_copies(wait):
        def body(e, c):
            n = tailn_ref[e]
            base = pl.multiple_of(tail0_ref[e], SEG_ALIGN)
            _segment_copies(
                n,
                lambda s, z: zero_scr.at[pl.ds(s, z), :],
                lambda s, z: xb_ref.at[pl.ds(base + s, z), :],
                tail_sem, SEG_BITS - 1, wait)
            return c
        lax.fori_loop(0, N_EXPERTS, body, 0)

        def unused(j, c):
            cp = pltpu.make_async_copy(zero_scr, xb_ref.at[pl.ds(pl.multiple_of(j * BM, BM), BM), :],
                                       tail_sem)
            if wait:
                cp.wait()
            else:
                cp.start()
            return c
        lax.fori_loop(na_ref[0], n_blocks, unused, 0)

    @pl.when(i >= 2)
    def _():
        tile_copies(i - 2, cur, True)

    slot = slot_ref[0]
    srow = lax.broadcasted_iota(I32, (CAP, TM), 0)
    hit = (srow == slot[0:1]) | (srow == slot[1:2]) | (srow == slot[2:3]) | (srow == slot[3:4])
    onehot = jnp.where(hit, 1.0, 0.0).astype(BF16)
    xs_scr[cur] = jnp.dot(onehot, h2_ref[...], preferred_element_type=F32).astype(BF16)
    tile_copies(i, cur, False)

    @pl.when(i == nt - 1)
    def _():
        zero_scr[...] = jnp.zeros_like(zero_scr)
        tail_copies(False)
        if nt >= 2:
            tile_copies(i - 1, 1 - cur, True)
        tile_copies(i, cur, True)
        tail_copies(True)


def _dispatch(h2, slot, off, cnt, base, tail0, tailn, n_act, n_blocks):
    nt = slot.shape[0]
    n_rows = n_blocks * BM
    grid_spec = pltpu.PrefetchScalarGridSpec(
        num_scalar_prefetch=6,
        grid=(nt,),
        in_specs=[pl.BlockSpec((TM, D_MODEL), lambda i, *_: (i, 0)),
                  pl.BlockSpec((1, TOP_K, TM), lambda i, *_: (i, 0, 0))],
        out_specs=pl.BlockSpec(memory_space=pl.ANY),
        scratch_shapes=[pltpu.VMEM((2, CAP, D_MODEL), BF16),
                        pltpu.VMEM((BM, D_MODEL), BF16),
                        pltpu.SemaphoreType.DMA((2,)),
                        pltpu.SemaphoreType.DMA(())],
    )
    return pl.pallas_call(
        functools.partial(_dispatch_kernel, nt, n_blocks),
        grid_spec=grid_spec,
        out_shape=jax.ShapeDtypeStruct((n_rows, D_MODEL), BF16),
        compiler_params=_cparams(("arbitrary",), VMEM_LIMIT),
        name="dispatch",
    )(off, cnt, base, tail0, tailn, n_act, h2, slot)


def _experts_kernel(be_ref, bi_ref, na_ref, x_ref, wu_ref, bu_ref, wd_ref, bd_ref, y_ref, wu_scr, wd_scr):
    j = pl.program_id(0)

    @pl.when(j < na_ref[0])
    def _():
        prev = be_ref[jnp.maximum(j - 1, 0)]

        @pl.when((j == 0) | (be_ref[j] != prev))
        def _():
            wu_scr[...] = wu_ref[0].astype(BF16)
            wd_scr[...] = wd_ref[0].astype(BF16)

        u = jnp.dot(x_ref[...], wu_scr[...], preferred_element_type=F32) + bu_ref[0]
        glu = jnp.minimum(u[:, :D_FF], SWIGLU_LIMIT)
        lin = jnp.clip(u[:, D_FF:], -SWIGLU_LIMIT, SWIGLU_LIMIT)
        act = glu * jax.nn.sigmoid(SWIGLU_ALPHA * glu) * (lin + 1.0)
        y = jnp.dot(act.astype(BF16), wd_scr[...], preferred_element_type=F32) + bd_ref[0]
        y_ref[...] = y.astype(BF16)

    @pl.when(j >= na_ref[0])
    def _():
        y_ref[...] = jnp.zeros_like(y_ref)


def _experts(xb, blk_e, blk_i, n_act, w_up, b_up, w_down, b_down):
    n_rows = xb.shape[0]
    nblk = n_rows // BM
    grid_spec = pltpu.PrefetchScalarGridSpec(
        num_scalar_prefetch=3,
        grid=(nblk,),
        in_specs=[pl.BlockSpec((BM, D_MODEL), lambda j, be, bi, na: (bi[j], 0)),
                  pl.BlockSpec((1, D_MODEL, 2 * D_FF), lambda j, be, bi, na: (be[j], 0, 0)),
                  pl.BlockSpec((1, 1, 2 * D_FF), lambda j, be, bi, na: (be[j], 0, 0)),
                  pl.BlockSpec((1, D_FF, D_MODEL), lambda j, be, bi, na: (be[j], 0, 0)),
                  pl.BlockSpec((1, 1, D_MODEL), lambda j, be, bi, na: (be[j], 0, 0))],
        out_specs=pl.BlockSpec((BM, D_MODEL), lambda j, be, bi, na: (j, 0)),
        scratch_shapes=[pltpu.VMEM((D_MODEL, 2 * D_FF), BF16), pltpu.VMEM((D_FF, D_MODEL), BF16)],
    )
    return pl.pallas_call(
        _experts_kernel,
        grid_spec=grid_spec,
        out_shape=jax.ShapeDtypeStruct((n_rows, D_MODEL), BF16),
        compiler_params=_cparams(("arbitrary",), VMEM_LIMIT),
        name="experts",
    )(blk_e, blk_i, n_act, xb, w_up, b_up.reshape(N_EXPERTS, 1, 2 * D_FF), w_down,
      b_down.reshape(N_EXPERTS, 1, D_MODEL))


def _combine_kernel(nt, ntp, off_ref, cnt_ref, base_ref, yb_ref, cols_ref, x1_ref, gf_ref,
                    op_ref, os_ref, ys_scr, sems):
    i = pl.program_id(0)
    cur = i % 2

    def tile_copies(t, buf, wait):
        def body(e, c):
            n = cnt_ref[t * N_EXPERTS + e]
            off = pl.multiple_of(off_ref[t * N_EXPERTS + e], SEG_ALIGN)
            base = pl.multiple_of(base_ref[t * N_EXPERTS + e], SEG_ALIGN)
            _segment_copies(
                n,
                lambda s, z: yb_ref.at[pl.ds(base + s, z), :],
                lambda s, z: ys_scr.at[buf, pl.ds(off + s, z), :],
                sems.at[buf], SEG_BITS, wait)
            return c
        lax.fori_loop(0, N_EXPERTS, body, 0)

    @pl.when(i == 0)
    def _():
        ys_scr[...] = jnp.zeros_like(ys_scr)
        tile_copies(0, 0, False)

    @pl.when(i + 1 < nt)
    def _():
        tile_copies(i + 1, 1 - cur, False)

    tile_copies(i, cur, True)

    cols = cols_ref[0]
    lane = lax.broadcasted_iota(I32, (TM, CAP), 1)
    w = jnp.zeros((TM, CAP), F32)
    for k in range(TOP_K):
        sk = cols[:, k:k + 1].astype(I32)
        gk = cols[:, TOP_K + k:TOP_K + k + 1]
        w = jnp.where(lane == sk, gk, w)
    y = jnp.dot(w.astype(BF16), ys_scr[cur], preferred_element_type=F32)
    out = x1_ref[...] + _per_group(y, lambda a, gf: a * gf, gf_ref[...])

    @pl.when(i < ntp)
    def _():
        op_ref[...] = out

    @pl.when(i >= ntp)
    def _():
        os_ref[...] = out


def _combine(yb, cols, x1, gate_f, off, cnt, base, ntp, nb, tps):
    nt = cols.shape[0]
    grid_spec = pltpu.PrefetchScalarGridSpec(
        num_scalar_prefetch=3,
        grid=(nt,),
        in_specs=[pl.BlockSpec(memory_space=pl.ANY),
                  pl.BlockSpec((1, TM, 128), lambda i, *_: (i, 0, 0)),
                  pl.BlockSpec((TM, D_MODEL), lambda i, *_: (i, 0)),
                  pl.BlockSpec((GROUPS_PER_TILE, D_MODEL), _mod_row(ntp, tps, nb))],
        out_specs=[pl.BlockSpec((TM, D_MODEL), lambda i, *_: (jnp.minimum(i, ntp - 1), 0)),
                   pl.BlockSpec((TM, D_MODEL), lambda i, *_: (0, 0))],
        scratch_shapes=[pltpu.VMEM((2, CAP, D_MODEL), BF16), pltpu.SemaphoreType.DMA((2,))],
    )
    return pl.pallas_call(
        functools.partial(_combine_kernel, nt, ntp),
        grid_spec=grid_spec,
        out_shape=[jax.ShapeDtypeStruct((ntp * TM, D_MODEL), F32),
                   jax.ShapeDtypeStruct((TM, D_MODEL), F32)],
        compiler_params=_cparams(("arbitrary",), VMEM_LIMIT),
        name="combine",
    )(off, cnt, base, yb, cols, x1, gate_f)


def _rotary_tables(seq, dec_batch, dec_seq):
    half = HEAD_DIM_RET // 2
    inv = ROPE_BASE ** (-np.arange(half, dtype=np.float64) / half)
    pos = np.concatenate([np.arange(seq), np.tile(PAST_LEN + np.arange(dec_seq), dec_batch)])
    ang = pos.astype(np.float64)[:, None] * inv[None, :]
    cos = np.concatenate([np.cos(ang), np.cos(ang)], axis=1)
    sin = np.concatenate([-np.sin(ang), np.sin(ang)], axis=1)
    return jnp.asarray(cos, F32), jnp.asarray(sin, F32)


def _rel_bias_reversed(rel_bias):
    heads = rel_bias.shape[0]
    ext = jnp.concatenate([rel_bias[:, 1:], jnp.broadcast_to(rel_bias[:, -1:], (heads, 2 * MAX_REL))], axis=1)
    return ext[:, ::-1].astype(F32)


def _group_mods(m, nb, ndb):
    assert ndb == GROUPS_PER_TILE
    mp = jnp.broadcast_to(m[:nb, None], (nb, GROUPS_PER_TILE) + m.shape[1:])
    allm = jnp.concatenate([mp.reshape((nb * GROUPS_PER_TILE,) + m.shape[1:]), m[nb:]], axis=0)
    return jnp.transpose(allm, (1, 0, 2))


def _routing_tables(cnt, n_blocks):
    nt = cnt.shape[0]
    off = jnp.cumsum(cnt, axis=1) - cnt
    rows_e = jnp.sum(cnt, axis=0)
    nblk_e = (rows_e + BM - 1) // BM
    blk_end = jnp.cumsum(nblk_e)
    start_e = (blk_end - nblk_e) * BM
    base = start_e[None, :] + jnp.cumsum(cnt, axis=0) - cnt
    n_act = blk_end[-1]
    j = jnp.minimum(jnp.arange(n_blocks), n_act - 1)
    blk_e = jnp.minimum(jnp.sum(blk_end[None, :] <= j[:, None], axis=1), N_EXPERTS - 1)
    tail0 = start_e + rows_e
    tailn = nblk_e * BM - rows_e
    i32 = lambda a: a.astype(I32)
    return (i32(off.reshape(nt * N_EXPERTS)), i32(cnt.reshape(nt * N_EXPERTS)),
            i32(base.reshape(nt * N_EXPERTS)), i32(tail0), i32(tailn),
            i32(blk_e), i32(j), i32(n_act.reshape(1)))


def kernel(x_prompt, x_sample, c_prompt, c_sample, cache_att_k, cache_att_v, state_ret, w_ada, b_ada,
           g_norm_mix, g_norm_ffn, w_in, g_q, g_k, rel_bias, g_ret_out, w_out, w_router, b_router,
           w_up, b_up, w_down, b_down):
    nb, seq, d = x_prompt.shape
    ndb, dseq, _ = x_sample.shape
    assert d == D_MODEL and ndb * dseq == TM and dseq == CHUNK
    assert seq % TM == 0 and seq >= ATT_WINDOW and cache_att_k.shape[2] == ATT_WINDOW
    assert w_ada.shape[0] == 1
    rp = nb * seq
    ntp = rp // TM
    nt = ntp + 1
    tps = seq // TM

    xp = x_prompt.reshape(rp, d)
    xs = x_sample.reshape(TM, d)

    m = _ada(jnp.concatenate([c_prompt, c_sample], axis=0), w_ada[0], b_ada[0])
    mods = _group_mods(m.reshape(nb + ndb, N_ADA, d), nb, ndb)
    shift_m, scale_m, gate_m, shift_f, scale_f, gate_f = [mods[a] for a in range(N_ADA)]

    cos_t, sin_t = _rotary_tables(seq, ndb, dseq)
    bd = jnp.asarray(np.kron(np.eye(N_HEADS_ATT), np.ones((HEAD_DIM_ATT, HEAD_DIM_ATT))), BF16)
    tile8 = lambda g: jnp.tile(g.astype(F32), N_HEADS_ATT).reshape(1, GROUP_W)
    (qa, ka, va, qb, kb, vb, gb, kp_tail, vp_tail, ks_new, vs_new) = _inproj(
        xp, xs, shift_m, scale_m, g_norm_mix[0].reshape(1, d), w_in[0].astype(BF16), bd,
        tile8(g_q[0]), tile8(g_k[0]), cos_t, sin_t, nb, tps)

    rev = _rel_bias_reversed(rel_bias[0])
    att_p = _attn_prompt(qa, ka, va, rev, nb, seq)
    att_s = _attn_sample(qa, ka, va,
                         cache_att_k[0].reshape(ndb, ATT_WINDOW, GROUP_W).astype(BF16),
                         cache_att_v[0].reshape(ndb, ATT_WINDOW, GROUP_W).astype(BF16),
                         rev, rp)

    g_ro = g_ret_out[0].astype(F32).reshape(1, GROUP_W)
    zero_state = jnp.zeros((nb, N_HEADS_RET, HEAD_DIM_RET, HEAD_DIM_RET), F32)
    ret_p, state_p = _ret(qb, kb, vb, gb, zero_state, g_ro, RET_CB, 0, nb, seq // RET_CB, "ret_prompt")
    ret_s, state_s = _ret(qb, kb, vb, gb, state_ret[0].astype(F32), g_ro, CHUNK, rp, ndb, 1, "ret_sample")

    upper = jnp.asarray(np.triu(np.ones((TM, TM)), 1), BF16)
    lower = jnp.asarray(np.tril(np.ones((N_EXPERTS, N_EXPERTS)), -1), BF16)
    x1, h2, slot, cols, cnt = _outproj(
        att_p, att_s, ret_p, ret_s, xp, xs, gate_m, shift_f, scale_f, g_norm_ffn[0].reshape(1, d),
        w_out[0].astype(BF16), w_router[0].T.astype(BF16), b_router[0].astype(F32).reshape(N_EXPERTS, 1),
        upper, lower, nb, tps)

    n_blocks = (TOP_K * (rp + TM) + nt * N_EXPERTS * (SEG_ALIGN - 1)) // BM + 1 + N_EXPERTS
    off, cntf, base, tail0, tailn, blk_e, blk_i, n_act = _routing_tables(cnt[:, :, 0], n_blocks)
    xb = _dispatch(h2, slot, off, cntf, base, tail0, tailn, n_act, n_blocks)
    yb = _experts(xb, blk_e, blk_i, n_act, w_up[0], b_up[0], w_down[0], b_down[0])
    out_p, out_s = _combine(yb, cols, x1, gate_f, off, cntf, base, ntp, nb, tps)

    heads = (N_HEADS_ATT, HEAD_DIM_ATT)
    return (out_p.reshape(nb, seq, d), out_s.reshape(ndb, dseq, d),
            kp_tail.reshape(1, nb, ATT_WINDOW, *heads), vp_tail.reshape(1, nb, ATT_WINDOW, *heads),
            state_p[None],
            ks_new.reshape(1, ndb, dseq, *heads), vs_new.reshape(1, ndb, dseq, *heads),
            state_s[None])
```

```python
import functools

import numpy as np
import jax
import jax.numpy as jnp
from jax import lax
from jax.experimental import pallas as pl
from jax.experimental.pallas import tpu as pltpu

F32 = jnp.float32
BF16 = jnp.bfloat16
I32 = jnp.int32

D_MODEL = 1024
GROUP_W = 512
N_SLOTS = 7
N_HEADS_ATT = 8
HEAD_DIM_ATT = 64
N_HEADS_RET = 4
HEAD_DIM_RET = 128
CHUNK = 64
ATT_WINDOW = 512
MAX_REL = 256
PAST_LEN = 2048
RET_DECAY_OFFSET = 5.0
ROPE_BASE = 10000.0
N_EXPERTS = 32
TOP_K = 4
D_FF = 1024
SWIGLU_LIMIT = 7.0
SWIGLU_ALPHA = 1.702
N_ADA = 6
NORM_EPS = 1e-6
NEG_INF = -1e30

TM = 512
GROUPS_PER_TILE = TM // CHUNK
ATT_QB = 256
RET_CB = 256
SEG_ALIGN = 16
CAP = TOP_K * TM + N_EXPERTS * SEG_ALIGN
BM = 512
VMEM_LIMIT = 56 * 1024 * 1024


def _cparams(sem, vmem=None):
    return pltpu.CompilerParams(dimension_semantics=sem, vmem_limit_bytes=vmem)


def _ada_kernel(c_ref, w_ref, b_ref, o_ref):
    c = c_ref[...]
    s = c * jax.nn.sigmoid(c)
    o_ref[...] = jnp.dot(s.astype(BF16), w_ref[...].astype(BF16),
                         preferred_element_type=F32) + b_ref[...]


def _ada(c_all, w_ada, b_ada):
    n, d = c_all.shape
    cols = w_ada.shape[1]
    tn = 1536
    return pl.pallas_call(
        _ada_kernel,
        grid=(cols // tn,),
        in_specs=[pl.BlockSpec((n, d), lambda j: (0, 0)),
                  pl.BlockSpec((d, tn), lambda j: (0, j)),
                  pl.BlockSpec((1, tn), lambda j: (0, j))],
        out_specs=pl.BlockSpec((n, tn), lambda j: (0, j)),
        out_shape=jax.ShapeDtypeStruct((n, cols), F32),
        compiler_params=_cparams(("arbitrary",), VMEM_LIMIT),
        name="ada",
    )(c_all, w_ada, b_ada.reshape(1, cols))


def _rms_rows(x, g):
    ms = jnp.mean(x * x, axis=-1, keepdims=True)
    return x * lax.rsqrt(ms + NORM_EPS) * g


def _mod_row(ntp, tps, nb):
    return lambda i, *_: (jnp.where(i < ntp, i // tps, nb), 0)


def _per_group(x, fn, *mods):
    x3 = x.reshape(GROUPS_PER_TILE, CHUNK, x.shape[-1])
    y3 = fn(x3, *[m[:, None, :] for m in mods])
    return y3.reshape(x.shape)


def _inproj_kernel(ntp, xp_ref, xs_ref, sh_ref, sc_ref, gn_ref, w_ref, bd_ref, gq_ref, gk_ref,
                   cos_ref, sin_ref,
                   qa_ref, ka_ref, va_ref, qb_ref, kb_ref, vb_ref, gb_ref,
                   kpt_ref, vpt_ref, kst_ref, vst_ref):
    i = pl.program_id(0)
    is_p = i < ntp
    x = jnp.where(is_p, xp_ref[...], xs_ref[...])
    y = _rms_rows(x, gn_ref[...])
    h = _per_group(y, lambda a, sh, sc: a * (1.0 + sc) + sh, sh_ref[...], sc_ref[...])
    hb = h.astype(BF16)

    def proj(s):
        return jnp.dot(hb, w_ref[:, s * GROUP_W:(s + 1) * GROUP_W], preferred_element_type=F32)

    def head_rms(z, g):
        zz = (z * z).astype(BF16)
        half = GROUP_W // 2
        ss = jnp.concatenate(
            [jnp.dot(zz[:, :half], bd_ref[...], preferred_element_type=F32),
             jnp.dot(zz[:, half:], bd_ref[...], preferred_element_type=F32)], axis=1)
        return z * lax.rsqrt(ss * (1.0 / HEAD_DIM_ATT) + NORM_EPS) * g

    cos = cos_ref[...]
    sin = sin_ref[...]

    def rot(z):
        outs = []
        for hh in range(N_HEADS_RET):
            zh = z[:, hh * HEAD_DIM_RET:(hh + 1) * HEAD_DIM_RET]
            outs.append(zh * cos + pltpu.roll(zh, HEAD_DIM_RET // 2, axis=1) * sin)
        return jnp.concatenate(outs, axis=1)

    qa_ref[...] = head_rms(proj(0), gq_ref[...]).astype(BF16)
    ka = head_rms(proj(1), gk_ref[...])
    ka_ref[...] = ka.T.astype(BF16)
    va = proj(2)
    va_ref[...] = va.astype(BF16)

    @pl.when(is_p)
    def _():
        kpt_ref[...] = ka
        vpt_ref[...] = va

    @pl.when(jnp.logical_not(is_p))
    def _():
        kst_ref[...] = ka
        vst_ref[...] = va

    qb_ref[...] = rot(proj(3)).astype(BF16)
    kb_ref[...] = (rot(proj(4)) * (HEAD_DIM_RET ** -0.5)).astype(BF16)
    vb_ref[...] = proj(5).astype(BF16)
    gb_ref[...] = proj(6).astype(BF16)


def _inproj(xp, xs, shift, scale, g_norm, w_in_b, bd, gq8, gk8, cos_t, sin_t, nb, tps):
    rp = xp.shape[0]
    ntp = rp // TM
    nt = ntp + 1
    r = rp + TM
    row = lambda i: (i, 0)
    full = lambda i: (0, 0)
    tab = lambda i: (jnp.where(i < ntp, i % tps, tps), 0)
    act = jax.ShapeDtypeStruct((r, GROUP_W), BF16)
    return pl.pallas_call(
        functools.partial(_inproj_kernel, ntp),
        grid=(nt,),
        in_specs=[pl.BlockSpec((TM, D_MODEL), lambda i: (jnp.minimum(i, ntp - 1), 0)),
                  pl.BlockSpec((TM, D_MODEL), full),
                  pl.BlockSpec((GROUPS_PER_TILE, D_MODEL), _mod_row(ntp, tps, nb)),
                  pl.BlockSpec((GROUPS_PER_TILE, D_MODEL), _mod_row(ntp, tps, nb)),
                  pl.BlockSpec((1, D_MODEL), full),
                  pl.BlockSpec((D_MODEL, N_SLOTS * GROUP_W), full),
                  pl.BlockSpec((GROUP_W // 2, GROUP_W // 2), full),
                  pl.BlockSpec((1, GROUP_W), full),
                  pl.BlockSpec((1, GROUP_W), full),
                  pl.BlockSpec((TM, HEAD_DIM_RET), tab),
                  pl.BlockSpec((TM, HEAD_DIM_RET), tab)],
        out_specs=[pl.BlockSpec((TM, GROUP_W), row), pl.BlockSpec((GROUP_W, TM), lambda i: (0, i))]
        + [pl.BlockSpec((TM, GROUP_W), row)] * 5 + [
            pl.BlockSpec((TM, GROUP_W), lambda i: (jnp.minimum(i // tps, nb - 1), 0)),
            pl.BlockSpec((TM, GROUP_W), lambda i: (jnp.minimum(i // tps, nb - 1), 0)),
            pl.BlockSpec((TM, GROUP_W), full),
            pl.BlockSpec((TM, GROUP_W), full)],
        out_shape=[act, jax.ShapeDtypeStruct((GROUP_W, r), BF16)] + [act] * 5
        + [jax.ShapeDtypeStruct((nb * TM, GROUP_W), F32)] * 2
        + [jax.ShapeDtypeStruct((TM, GROUP_W), F32)] * 2,
        compiler_params=_cparams(("arbitrary",), VMEM_LIMIT),
        name="inproj",
    )(xp, xs, shift, scale, g_norm, w_in_b, bd, gq8, gk8, cos_t, sin_t)


def _attn_heads(q, k, v, bias_ref, k_feature_major, first_valid_col=None):
    qb_rows, kb_rows = q.shape[0], v.shape[0]
    if first_valid_col is not None:
        valid = lax.broadcasted_iota(I32, (qb_rows, kb_rows), 1) >= first_valid_col
    pair_w = 2 * HEAD_DIM_ATT
    low = lax.broadcasted_iota(I32, (1, pair_w), 1) < HEAD_DIM_ATT
    outs = []
    for pp in range(N_HEADS_ATT // 2):
        ps = slice(pp * pair_w, (pp + 1) * pair_w)
        q2, v2 = q[:, ps], v[:, ps]
        k2 = k[ps, :] if k_feature_major else k[:, ps]
        halves = []
        for half in range(2):
            qm = jnp.where(low if half == 0 else jnp.logical_not(low), q2, jnp.zeros_like(q2))
            if k_feature_major:
                s = jnp.dot(qm, k2, preferred_element_type=F32)
            else:
                s = lax.dot_general(qm, k2, (((1,), (1,)), ((), ())), preferred_element_type=F32)
            s = s + bias_ref[2 * pp + half]
            if first_valid_col is not None:
                s = jnp.where(valid, s, NEG_INF)
            m = jnp.max(s, axis=-1, keepdims=True)
            e = jnp.exp(s - m)
            l = jnp.sum(e, axis=-1, keepdims=True)
            halves.append(jnp.dot(e.astype(BF16), v2, preferred_element_type=F32) / l)
        outs.append(jnp.where(low, halves[0], halves[1]))
    return jnp.concatenate(outs, axis=1)


def _fill_band_bias(rev_ref, bias_scr):
    _, qb_rows, kb_rows = bias_scr.shape
    width = rev_ref.shape[1]
    q = lax.broadcasted_iota(I32, (qb_rows, kb_rows), 0)
    k = lax.broadcasted_iota(I32, (qb_rows, kb_rows), 1)
    qc = q >> 6
    kc = (k - ATT_WINDOW) >> 6
    band = (kc >= qc - ATT_WINDOW // CHUNK) & (kc <= qc)
    for hh in range(N_HEADS_ATT):
        rows = jnp.broadcast_to(rev_ref[hh:hh + 1, :], (qb_rows, width))
        toep = pltpu.roll(rows, width - MAX_REL, 1, stride=1, stride_axis=0)
        bias_scr[hh] = jnp.where(band, toep[:, :kb_rows], NEG_INF)


def _attn_prompt_kernel(q_ref, k0_ref, k1_ref, k2_ref, v0_ref, v1_ref, v2_ref, rev_ref, o_ref, bias_scr):
    j = pl.program_id(1)

    @pl.when((pl.program_id(0) == 0) & (j == 0))
    def _():
        _fill_band_bias(rev_ref, bias_scr)

    k = jnp.concatenate([k0_ref[...], k1_ref[...], k2_ref[...]], axis=1)
    v = jnp.concatenate([v0_ref[...], v1_ref[...], v2_ref[...]], axis=0)

    @pl.when(j >= 2)
    def _():
        o_ref[...] = _attn_heads(q_ref[...], k, v, bias_scr, True).astype(BF16)

    @pl.when(j < 2)
    def _():
        o_ref[...] = _attn_heads(q_ref[...], k, v, bias_scr, True, (2 - j) * ATT_QB).astype(BF16)


def _attn_prompt(qa, ka_t, va, rev, nb, seq):
    r = nb * seq
    nq = seq // ATT_QB
    blk = lambda back: (lambda b, j: (b * nq + jnp.maximum(j - back, 0), 0))
    spec = lambda back: pl.BlockSpec((ATT_QB, GROUP_W), blk(back))
    tspec = lambda back: pl.BlockSpec((GROUP_W, ATT_QB), lambda b, j: (0, b * nq + jnp.maximum(j - back, 0)))
    return pl.pallas_call(
        _attn_prompt_kernel,
        grid=(nb, nq),
        in_specs=[spec(0), tspec(2), tspec(1), tspec(0), spec(2), spec(1), spec(0),
                  pl.BlockSpec(rev.shape, lambda b, j: (0, 0))],
        out_specs=spec(0),
        out_shape=jax.ShapeDtypeStruct((r, GROUP_W), BF16),
        scratch_shapes=[pltpu.VMEM((N_HEADS_ATT, ATT_QB, ATT_WINDOW + ATT_QB), F32)],
        compiler_params=_cparams(("arbitrary", "arbitrary"), VMEM_LIMIT),
        name="attn_prompt",
    )(qa, ka_t, ka_t, ka_t, va, va, va, rev)


def _attn_sample_kernel(q_ref, kn_ref, vn_ref, kc_ref, vc_ref, rev_ref, o_ref, bias_scr):
    @pl.when(pl.program_id(0) == 0)
    def _():
        _fill_band_bias(rev_ref, bias_scr)

    k = jnp.concatenate([kc_ref[0], kn_ref[...].astype(BF16)], axis=0)
    v = jnp.concatenate([vc_ref[0], vn_ref[...]], axis=0)
    o_ref[...] = _attn_heads(q_ref[...], k, v, bias_scr, False).astype(BF16)


def _attn_sample(qa, ks_new, va, kc, vc, rev, rp):
    ndb = kc.shape[0]
    base = rp // CHUNK
    spec = pl.BlockSpec((CHUNK, GROUP_W), lambda b: (base + b, 0))
    cspec = pl.BlockSpec((1, ATT_WINDOW, GROUP_W), lambda b: (b, 0, 0))
    return pl.pallas_call(
        _attn_sample_kernel,
        grid=(ndb,),
        in_specs=[spec, pl.BlockSpec((CHUNK, GROUP_W), lambda b: (b, 0)), spec, cspec, cspec,
                  pl.BlockSpec(rev.shape, lambda b: (0, 0))],
        out_specs=pl.BlockSpec((CHUNK, GROUP_W), lambda b: (b, 0)),
        out_shape=jax.ShapeDtypeStruct((ndb * CHUNK, GROUP_W), BF16),
        scratch_shapes=[pltpu.VMEM((N_HEADS_ATT, CHUNK, ATT_WINDOW + CHUNK), F32)],
        compiler_params=_cparams(("arbitrary",), VMEM_LIMIT),
        name="attn_sample",
    )(qa, ks_new, va, kc, vc, rev)


def _ret_kernel(state_decay, q_ref, k_ref, v_ref, g_ref, s0_ref, dm_ref, xi_ref, zeta_ref,
                gro_ref, o_ref, sn_ref, s_scr):
    j = pl.program_id(1)

    @pl.when(j == 0)
    def _():
        s_scr[...] = s0_ref[0]

    outs = []
    for hh in range(N_HEADS_RET):
        hs = slice(hh * HEAD_DIM_RET, (hh + 1) * HEAD_DIM_RET)
        q = q_ref[:, hs]
        k = k_ref[:, hs]
        v = v_ref[:, hs]
        st = s_scr[hh]
        sc = lax.dot_general(q, k, (((1,), (1,)), ((), ())), preferred_element_type=F32) * dm_ref[hh]
        inner = jnp.dot(sc.astype(BF16), v, preferred_element_type=F32)
        cross = jnp.dot(q, st.astype(BF16), preferred_element_type=F32) * xi_ref[:, hs]
        o = inner + cross
        kz = k.astype(F32) * zeta_ref[:, hs]
        s_scr[hh] = state_decay[hh] * st + jnp.dot(kz.T.astype(BF16), v, preferred_element_type=F32)
        mu = jnp.mean(o, axis=-1, keepdims=True)
        oc = o - mu
        var = jnp.mean(oc * oc, axis=-1, keepdims=True)
        outs.append(oc * lax.rsqrt(var + NORM_EPS))
    y = jnp.concatenate(outs, axis=1) * gro_ref[...]
    g = g_ref[...].astype(F32)
    o_ref[...] = (g * jax.nn.sigmoid(g) * y).astype(BF16)

    @pl.when(j == pl.num_programs(1) - 1)
    def _():
        sn_ref[0] = s_scr[...]


def _ret_consts(cb):
    log_g = np.log1p(-np.exp2(-RET_DECAY_OFFSET - np.arange(N_HEADS_RET, dtype=np.float64)))
    n = np.arange(cb, dtype=np.float64)
    diff = n[:, None] - n[None, :]
    dm = np.where(diff[None] >= 0, np.exp(np.maximum(diff, 0.0)[None] * log_g[:, None, None]), 0.0)
    xi = np.exp((n + 1.0)[:, None] * log_g[None, :])
    zeta = np.exp((cb - 1.0 - n)[:, None] * log_g[None, :])
    rep = lambda a: np.repeat(a, HEAD_DIM_RET, axis=1)
    state_decay = tuple(float(v) for v in np.exp(cb * log_g))
    return (jnp.asarray(dm, F32), jnp.asarray(rep(xi), F32), jnp.asarray(rep(zeta), F32), state_decay)


def _ret(qb, kb, vb, gb, s0, g_ro, cb, row0, nb, nc, name):
    dm, xi, zeta, state_decay = _ret_consts(cb)
    base = row0 // cb
    spec = pl.BlockSpec((cb, GROUP_W), lambda b, j: (base + b * nc + j, 0))
    sspec = pl.BlockSpec((1, N_HEADS_RET, HEAD_DIM_RET, HEAD_DIM_RET), lambda b, j: (b, 0, 0, 0))
    full2 = lambda b, j: (0, 0)
    return pl.pallas_call(
        functools.partial(_ret_kernel, state_decay),
        grid=(nb, nc),
        in_specs=[spec, spec, spec, spec, sspec,
                  pl.BlockSpec(dm.shape, lambda b, j: (0, 0, 0)),
                  pl.BlockSpec(xi.shape, full2), pl.BlockSpec(zeta.shape, full2),
                  pl.BlockSpec((1, GROUP_W), full2)],
        out_specs=[pl.BlockSpec((cb, GROUP_W), lambda b, j: (b * nc + j, 0)), sspec],
        out_shape=[jax.ShapeDtypeStruct((nb * nc * cb, GROUP_W), BF16),
                   jax.ShapeDtypeStruct(s0.shape, F32)],
        scratch_shapes=[pltpu.VMEM((N_HEADS_RET, HEAD_DIM_RET, HEAD_DIM_RET), F32)],
        compiler_params=_cparams(("arbitrary", "arbitrary"), VMEM_LIMIT),
        name=name,
    )(qb, kb, vb, gb, s0, dm, xi, zeta, g_ro)


def _outproj_kernel(ntp, attp_ref, atts_ref, retp_ref, rets_ref, xp_ref, xs_ref, gm_ref, shf_ref, scf_ref,
                    gn_ref, wo_ref, wr_ref, br_ref, upper_ref, lower_ref,
                    x1_ref, h2_ref, slot_ref, cols_ref, cnt_ref):
    i = pl.program_id(0)
    is_p = i < ntp
    x = jnp.where(is_p, xp_ref[...], xs_ref[...])
    att = jnp.where(is_p, attp_ref[...], atts_ref[...])
    ret = jnp.where(is_p, retp_ref[...], rets_ref[...])
    mix = (jnp.dot(att, wo_ref[:GROUP_W, :], preferred_element_type=F32)
           + jnp.dot(ret, wo_ref[GROUP_W:, :], preferred_element_type=F32))
    x1 = _per_group(mix, lambda a, gm: a * gm, gm_ref[...]) + x
    x1_ref[...] = x1
    y = _rms_rows(x1, gn_ref[...])
    h2 = _per_group(y, lambda a, sh, sc: a * (1.0 + sc) + sh, shf_ref[...], scf_ref[...])
    h2b = h2.astype(BF16)
    h2_ref[...] = h2b

    logits = lax.dot_general(wr_ref[...], h2b, (((1,), (1,)), ((), ())),
                             preferred_element_type=F32) + br_ref[...]
    eidx = lax.broadcasted_iota(I32, logits.shape, 0).astype(F32)
    work = logits
    sel, top = [], []
    for _ in range(TOP_K):
        m = jnp.max(work, axis=0, keepdims=True)
        idx = jnp.min(jnp.where(work == m, eidx, float(N_EXPERTS)), axis=0, keepdims=True)
        hit = eidx == idx
        sel.append(hit)
        top.append(m)
        work = jnp.where(hit, -jnp.inf, work)
    ex = [jnp.exp(t - top[0]) for t in top]
    den = ex[0] + ex[1] + ex[2] + ex[3]
    gates = [e / den for e in ex]

    multi = (sel[0] | sel[1] | sel[2] | sel[3])
    multi_f = jnp.where(multi, 1.0, 0.0)
    rank = jnp.dot(multi_f.astype(BF16), upper_ref[...], preferred_element_type=F32)
    cnt = jnp.sum(multi_f, axis=1, keepdims=True)
    cnt_pad = jnp.floor((cnt + (SEG_ALIGN - 1.0)) * (1.0 / SEG_ALIGN)) * SEG_ALIGN
    cnt_pad_b = jnp.broadcast_to(cnt_pad, (N_EXPERTS, 128))
    seg_off = jnp.dot(lower_ref[...], cnt_pad_b.astype(BF16), preferred_element_type=F32)[:, :1]
    pos = seg_off + rank
    slots = [jnp.sum(jnp.where(s, pos, 0.0), axis=0, keepdims=True) for s in sel]
    slot_rows = jnp.concatenate(slots, axis=0)
    gate_rows = jnp.concatenate(gates, axis=0)
    slot_ref[0] = slot_rows.astype(I32)
    cnt_ref[0] = cnt_pad_b.astype(I32)
    both = jnp.concatenate([slot_rows, gate_rows, jnp.zeros((128 - 2 * TOP_K, TM), F32)], axis=0)
    cols_ref[0] = both.T


def _outproj(att_p, att_s, ret_p, ret_s, xp, xs, gate_m, shift_f, scale_f, g_norm, w_out_b, wr_t, br,
             upper, lower, nb, tps):
    rp = xp.shape[0]
    ntp = rp // TM
    nt = ntp + 1
    r = rp + TM
    row = lambda i: (i, 0)
    row3 = lambda i: (i, 0, 0)
    full = lambda i: (0, 0)
    prow = lambda i: (jnp.minimum(i, ntp - 1), 0)
    mod = pl.BlockSpec((GROUPS_PER_TILE, D_MODEL), _mod_row(ntp, tps, nb))
    return pl.pallas_call(
        functools.partial(_outproj_kernel, ntp),
        grid=(nt,),
        in_specs=[pl.BlockSpec((TM, GROUP_W), prow), pl.BlockSpec((TM, GROUP_W), full),
                  pl.BlockSpec((TM, GROUP_W), prow), pl.BlockSpec((TM, GROUP_W), full),
                  pl.BlockSpec((TM, D_MODEL), prow),
                  pl.BlockSpec((TM, D_MODEL), full),
                  mod, mod, mod,
                  pl.BlockSpec((1, D_MODEL), full),
                  pl.BlockSpec((D_MODEL, D_MODEL), full),
                  pl.BlockSpec((N_EXPERTS, D_MODEL), full),
                  pl.BlockSpec((N_EXPERTS, 1), full),
                  pl.BlockSpec((TM, TM), full),
                  pl.BlockSpec((N_EXPERTS, N_EXPERTS), full)],
        out_specs=[pl.BlockSpec((TM, D_MODEL), row), pl.BlockSpec((TM, D_MODEL), row),
                   pl.BlockSpec((1, TOP_K, TM), row3),
                   pl.BlockSpec((1, TM, 128), row3), pl.BlockSpec((1, N_EXPERTS, 128), row3)],
        out_shape=[jax.ShapeDtypeStruct((r, D_MODEL), F32), jax.ShapeDtypeStruct((r, D_MODEL), BF16),
                   jax.ShapeDtypeStruct((nt, TOP_K, TM), I32),
                   jax.ShapeDtypeStruct((nt, TM, 128), F32), jax.ShapeDtypeStruct((nt, N_EXPERTS, 128), I32)],
        compiler_params=_cparams(("arbitrary",), VMEM_LIMIT),
        name="outproj",
    )(att_p, att_s, ret_p, ret_s, xp, xs, gate_m, shift_f, scale_f, g_norm, w_out_b, wr_t, br, upper, lower)


def _rows_copy(n, src_rows, dst_rows, sem):
    size = pl.multiple_of(n, SEG_ALIGN)
    return pltpu.make_async_copy(src_rows(size), dst_rows(size), sem)


def _start_segments(t, cnt_ref, off_ref, base_ref, local_rows, sorted_rows, sem, to_sorted):
    def body(e, c):
        n = cnt_ref[t * N_EXPERTS + e]
        off = pl.multiple_of(off_ref[t * N_EXPERTS + e], SEG_ALIGN)
        base = pl.multiple_of(base_ref[t * N_EXPERTS + e], SEG_ALIGN)
        local = lambda z: local_rows(off, z)
        remote = lambda z: sorted_rows(base, z)

        @pl.when(n > 0)
        def _():
            (_rows_copy(n, local, remote, sem) if to_sorted else _rows_copy(n, remote, local, sem)).start()
        return c
    lax.fori_loop(0, N_EXPERTS, body, 0)


def _dispatch_kernel(nt, n_blocks, off_ref, cnt_ref, base_ref, tot_ref, tail0_ref, tailn_ref, na_ref,
                     h2_ref, slot_ref, xb_ref, xs_scr, zero_scr, sems, tail_sem):
    i = pl.program_id(0)
    cur = i % 2
    sorted_rows = lambda r, z: xb_ref.at[pl.ds(r, z), :]

    def start_tile(t, buf):
        _start_segments(t, cnt_ref, off_ref, base_ref, lambda r, z: xs_scr.at[buf, pl.ds(r, z), :],
                        sorted_rows, sems.at[buf], True)

    def wait_tile(t, buf):
        _rows_copy(tot_ref[t], lambda z: xs_scr.at[buf, pl.ds(0, z), :], lambda z: sorted_rows(0, z),
                   sems.at[buf]).wait()

    def tail_copies(wait):
        def body(e, c):
            base = pl.multiple_of(tail0_ref[e], SEG_ALIGN)

            @pl.when(tailn_ref[e] > 0)
            def _():
                cp = _rows_copy(tailn_ref[e], lambda z: zero_scr.at[pl.ds(0, z), :],
                                lambda z: sorted_rows(base, z), tail_sem)
                cp.wait() if wait else cp.start()
            return c
        lax.fori_loop(0, N_EXPERTS, body, 0)

        def unused(j, c):
            cp = pltpu.make_async_copy(zero_scr, sorted_rows(pl.multiple_of(j * BM, BM), BM), tail_sem)
            cp.wait() if wait else cp.start()
            return c
        lax.fori_loop(na_ref[0], n_blocks, unused, 0)

    @pl.when(i >= 2)
    def _():
        wait_tile(i - 2, cur)

    slot = slot_ref[0]
    srow = lax.broadcasted_iota(I32, (CAP, TM), 0)
    hit = (srow == slot[0:1]) | (srow == slot[1:2]) | (srow == slot[2:3]) | (srow == slot[3:4])
    onehot = jnp.where(hit, 1.0, 0.0).astype(BF16)
    xs_scr[cur] = jnp.dot(onehot, h2_ref[...], preferred_element_type=F32).astype(BF16)
    start_tile(i, cur)

    @pl.when(i == nt - 1)
    def _():
        zero_scr[...] = jnp.zeros_like(zero_scr)
        tail_copies(False)
        if nt >= 2:
            wait_tile(i - 1, 1 - cur)
        wait_tile(i, cur)
        tail_copies(True)


def _dispatch(h2, slot, off, cnt, base, tot, tail0, tailn, n_act, n_blocks):
    nt = slot.shape[0]
    n_rows = n_blocks * BM
    grid_spec = pltpu.PrefetchScalarGridSpec(
        num_scalar_prefetch=7,
        grid=(nt,),
        in_specs=[pl.BlockSpec((TM, D_MODEL), lambda i, *_: (i, 0)),
                  pl.BlockSpec((1, TOP_K, TM), lambda i, *_: (i, 0, 0))],
        out_specs=pl.BlockSpec(memory_space=pl.ANY),
        scratch_shapes=[pltpu.VMEM((2, CAP, D_MODEL), BF16),
                        pltpu.VMEM((BM, D_MODEL), BF16),
                        pltpu.SemaphoreType.DMA((2,)),
                        pltpu.SemaphoreType.DMA(())],
    )
    return pl.pallas_call(
        functools.partial(_dispatch_kernel, nt, n_blocks),
        grid_spec=grid_spec,
        out_shape=jax.ShapeDtypeStruct((n_rows, D_MODEL), BF16),
        compiler_params=_cparams(("arbitrary",), VMEM_LIMIT),
        name="dispatch",
    )(off, cnt, base, tot, tail0, tailn, n_act, h2, slot)


def _experts_kernel(be_ref, bi_ref, na_ref, x_ref, wu_ref, bu_ref, wd_ref, bd_ref, y_ref, wu_scr, wd_scr):
    j = pl.program_id(0)

    @pl.when(j < na_ref[0])
    def _():
        prev = be_ref[jnp.maximum(j - 1, 0)]

        @pl.when((j == 0) | (be_ref[j] != prev))
        def _():
            wu_scr[...] = wu_ref[0].astype(BF16)
            wd_scr[...] = wd_ref[0].astype(BF16)

        u = jnp.dot(x_ref[...], wu_scr[...], preferred_element_type=F32) + bu_ref[0]
        glu = jnp.minimum(u[:, :D_FF], SWIGLU_LIMIT)
        lin = jnp.clip(u[:, D_FF:], -SWIGLU_LIMIT, SWIGLU_LIMIT)
        act = glu * jax.nn.sigmoid(SWIGLU_ALPHA * glu) * (lin + 1.0)
        y = jnp.dot(act.astype(BF16), wd_scr[...], preferred_element_type=F32) + bd_ref[0]
        y_ref[...] = y.astype(BF16)

    @pl.when(j >= na_ref[0])
    def _():
        y_ref[...] = jnp.zeros_like(y_ref)


def _experts(xb, blk_e, blk_i, n_act, w_up, b_up, w_down, b_down):
    n_rows = xb.shape[0]
    nblk = n_rows // BM
    grid_spec = pltpu.PrefetchScalarGridSpec(
        num_scalar_prefetch=3,
        grid=(nblk,),
        in_specs=[pl.BlockSpec((BM, D_MODEL), lambda j, be, bi, na: (bi[j], 0)),
                  pl.BlockSpec((1, D_MODEL, 2 * D_FF), lambda j, be, bi, na: (be[j], 0, 0)),
                  pl.BlockSpec((1, 1, 2 * D_FF), lambda j, be, bi, na: (be[j], 0, 0)),
                  pl.BlockSpec((1, D_FF, D_MODEL), lambda j, be, bi, na: (be[j], 0, 0)),
                  pl.BlockSpec((1, 1, D_MODEL), lambda j, be, bi, na: (be[j], 0, 0))],
        out_specs=pl.BlockSpec((BM, D_MODEL), lambda j, be, bi, na: (j, 0)),
        scratch_shapes=[pltpu.VMEM((D_MODEL, 2 * D_FF), BF16), pltpu.VMEM((D_FF, D_MODEL), BF16)],
    )
    return pl.pallas_call(
        _experts_kernel,
        grid_spec=grid_spec,
        out_shape=jax.ShapeDtypeStruct((n_rows, D_MODEL), BF16),
        compiler_params=_cparams(("arbitrary",), VMEM_LIMIT),
        name="experts",
    )(blk_e, blk_i, n_act, xb, w_up, b_up.reshape(N_EXPERTS, 1, 2 * D_FF), w_down,
      b_down.reshape(N_EXPERTS, 1, D_MODEL))


def _combine_kernel(nt, ntp, off_ref, cnt_ref, base_ref, tot_ref, yb_ref, cols_ref, x1_ref, gf_ref,
                    op_ref, os_ref, ys_scr, sems):
    i = pl.program_id(0)
    cur = i % 2

    sorted_rows = lambda r, z: yb_ref.at[pl.ds(r, z), :]

    def start_tile(t, buf):
        _start_segments(t, cnt_ref, off_ref, base_ref, lambda r, z: ys_scr.at[buf, pl.ds(r, z), :],
                        sorted_rows, sems.at[buf], False)

    @pl.when(i == 0)
    def _():
        ys_scr[...] = jnp.zeros_like(ys_scr)
        start_tile(0, 0)

    @pl.when(i + 1 < nt)
    def _():
        start_tile(i + 1, 1 - cur)

    _rows_copy(tot_ref[i], lambda z: sorted_rows(0, z), lambda z: ys_scr.at[cur, pl.ds(0, z), :],
               sems.at[cur]).wait()

    cols = cols_ref[0]
    lane = lax.broadcasted_iota(I32, (TM, CAP), 1)
    w = jnp.zeros((TM, CAP), F32)
    for k in range(TOP_K):
        sk = cols[:, k:k + 1].astype(I32)
        gk = cols[:, TOP_K + k:TOP_K + k + 1]
        w = jnp.where(lane == sk, gk, w)
    y = jnp.dot(w.astype(BF16), ys_scr[cur], preferred_element_type=F32)
    out = x1_ref[...] + _per_group(y, lambda a, gf: a * gf, gf_ref[...])

    @pl.when(i < ntp)
    def _():
        op_ref[...] = out

    @pl.when(i >= ntp)
    def _():
        os_ref[...] = out


def _combine(yb, cols, x1, gate_f, off, cnt, base, tot, ntp, nb, tps):
    nt = cols.shape[0]
    grid_spec = pltpu.PrefetchScalarGridSpec(
        num_scalar_prefetch=4,
        grid=(nt,),
        in_specs=[pl.BlockSpec(memory_space=pl.ANY),
                  pl.BlockSpec((1, TM, 128), lambda i, *_: (i, 0, 0)),
                  pl.BlockSpec((TM, D_MODEL), lambda i, *_: (i, 0)),
                  pl.BlockSpec((GROUPS_PER_TILE, D_MODEL), _mod_row(ntp, tps, nb))],
        out_specs=[pl.BlockSpec((TM, D_MODEL), lambda i, *_: (jnp.minimum(i, ntp - 1), 0)),
                   pl.BlockSpec((TM, D_MODEL), lambda i, *_: (0, 0))],
        scratch_shapes=[pltpu.VMEM((2, CAP, D_MODEL), BF16), pltpu.SemaphoreType.DMA((2,))],
    )
    return pl.pallas_call(
        functools.partial(_combine_kernel, nt, ntp),
        grid_spec=grid_spec,
        out_shape=[jax.ShapeDtypeStruct((ntp * TM, D_MODEL), F32),
                   jax.ShapeDtypeStruct((TM, D_MODEL), F32)],
        compiler_params=_cparams(("arbitrary",), VMEM_LIMIT),
        name="combine",
    )(off, cnt, base, tot, yb, cols, x1, gate_f)


def _rotary_tables(seq, dec_batch, dec_seq):
    half = HEAD_DIM_RET // 2
    inv = ROPE_BASE ** (-np.arange(half, dtype=np.float64) / half)
    pos = np.concatenate([np.arange(seq), np.tile(PAST_LEN + np.arange(dec_seq), dec_batch)])
    ang = pos.astype(np.float64)[:, None] * inv[None, :]
    cos = np.concatenate([np.cos(ang), np.cos(ang)], axis=1)
    sin = np.concatenate([-np.sin(ang), np.sin(ang)], axis=1)
    return jnp.asarray(cos, F32), jnp.asarray(sin, F32)


def _rel_bias_reversed(rel_bias):
    heads = rel_bias.shape[0]
    ext = jnp.concatenate([rel_bias[:, 1:], jnp.broadcast_to(rel_bias[:, -1:], (heads, 2 * MAX_REL))], axis=1)
    return ext[:, ::-1].astype(F32)


def _group_mods(m, nb, ndb):
    assert ndb == GROUPS_PER_TILE
    mp = jnp.broadcast_to(m[:nb, None], (nb, GROUPS_PER_TILE) + m.shape[1:])
    allm = jnp.concatenate([mp.reshape((nb * GROUPS_PER_TILE,) + m.shape[1:]), m[nb:]], axis=0)
    return jnp.transpose(allm, (1, 0, 2))


def _routing_tables(cnt, n_blocks):
    nt = cnt.shape[0]
    off = jnp.cumsum(cnt, axis=1) - cnt
    rows_e = jnp.sum(cnt, axis=0)
    nblk_e = (rows_e + BM - 1) // BM
    blk_end = jnp.cumsum(nblk_e)
    start_e = (blk_end - nblk_e) * BM
    base = start_e[None, :] + jnp.cumsum(cnt, axis=0) - cnt
    n_act = blk_end[-1]
    j = jnp.minimum(jnp.arange(n_blocks), n_act - 1)
    blk_e = jnp.minimum(jnp.sum(blk_end[None, :] <= j[:, None], axis=1), N_EXPERTS - 1)
    tail0 = start_e + rows_e
    tailn = nblk_e * BM - rows_e
    i32 = lambda a: a.astype(I32)
    return (i32(off.reshape(nt * N_EXPERTS)), i32(cnt.reshape(nt * N_EXPERTS)),
            i32(base.reshape(nt * N_EXPERTS)), i32(jnp.sum(cnt, axis=1)), i32(tail0), i32(tailn),
            i32(blk_e), i32(j), i32(n_act.reshape(1)))


def kernel(x_prompt, x_sample, c_prompt, c_sample, cache_att_k, cache_att_v, state_ret, w_ada, b_ada,
           g_norm_mix, g_norm_ffn, w_in, g_q, g_k, rel_bias, g_ret_out, w_out, w_router, b_router,
           w_up, b_up, w_down, b_down):
    nb, seq, d = x_prompt.shape
    ndb, dseq, _ = x_sample.shape
    assert d == D_MODEL and ndb * dseq == TM and dseq == CHUNK
    assert seq % TM == 0 and seq >= ATT_WINDOW and cache_att_k.shape[2] == ATT_WINDOW
    assert w_ada.shape[0] == 1
    rp = nb * seq
    ntp = rp // TM
    nt = ntp + 1
    tps = seq // TM

    xp = x_prompt.reshape(rp, d)
    xs = x_sample.reshape(TM, d)

    m = _ada(jnp.concatenate([c_prompt, c_sample], axis=0), w_ada[0], b_ada[0])
    mods = _group_mods(m.reshape(nb + ndb, N_ADA, d), nb, ndb)
    shift_m, scale_m, gate_m, shift_f, scale_f, gate_f = [mods[a] for a in range(N_ADA)]

    cos_t, sin_t = _rotary_tables(seq, ndb, dseq)
    bd = jnp.asarray(np.kron(np.eye(N_HEADS_ATT // 2), np.ones((HEAD_DIM_ATT, HEAD_DIM_ATT))), BF16)
    tile8 = lambda g: jnp.tile(g.astype(F32), N_HEADS_ATT).reshape(1, GROUP_W)
    (qa, ka_t, va, qb, kb, vb, gb, kp_tail, vp_tail, ks_new, vs_new) = _inproj(
        xp, xs, shift_m, scale_m, g_norm_mix[0].reshape(1, d), w_in[0].astype(BF16), bd,
        tile8(g_q[0]) * (HEAD_DIM_ATT ** -0.5), tile8(g_k[0]), cos_t, sin_t, nb, tps)

    rev = _rel_bias_reversed(rel_bias[0])
    att_p = _attn_prompt(qa, ka_t, va, rev, nb, seq)
    att_s = _attn_sample(qa, ks_new, va,
                         cache_att_k[0].reshape(ndb, ATT_WINDOW, GROUP_W).astype(BF16),
                         cache_att_v[0].reshape(ndb, ATT_WINDOW, GROUP_W).astype(BF16),
                         rev, rp)

    g_ro = g_ret_out[0].astype(F32).reshape(1, GROUP_W)
    zero_state = jnp.zeros((nb, N_HEADS_RET, HEAD_DIM_RET, HEAD_DIM_RET), F32)
    ret_p, state_p = _ret(qb, kb, vb, gb, zero_state, g_ro, RET_CB, 0, nb, seq // RET_CB, "ret_prompt")
    ret_s, state_s = _ret(qb, kb, vb, gb, state_ret[0].astype(F32), g_ro, CHUNK, rp, ndb, 1, "ret_sample")

    upper = jnp.asarray(np.triu(np.ones((TM, TM)), 1), BF16)
    lower = jnp.asarray(np.tril(np.ones((N_EXPERTS, N_EXPERTS)), -1), BF16)
    x1, h2, slot, cols, cnt = _outproj(
        att_p, att_s, ret_p, ret_s, xp, xs, gate_m, shift_f, scale_f, g_norm_ffn[0].reshape(1, d),
        w_out[0].astype(BF16), w_router[0].T.astype(BF16), b_router[0].astype(F32).reshape(N_EXPERTS, 1),
        upper, lower, nb, tps)

    n_blocks = (TOP_K * (rp + TM) + nt * N_EXPERTS * (SEG_ALIGN - 1)) // BM + 1 + N_EXPERTS
    off, cntf, base, tot, tail0, tailn, blk_e, blk_i, n_act = _routing_tables(cnt[:, :, 0], n_blocks)
    xb = _dispatch(h2, slot, off, cntf, base, tot, tail0, tailn, n_act, n_blocks)
    yb = _experts(xb, blk_e, blk_i, n_act, w_up[0], b_up[0], w_down[0], b_down[0])
    out_p, out_s = _combine(yb, cols, x1, gate_f, off, cntf, base, tot, ntp, nb, tps)

    heads = (N_HEADS_ATT, HEAD_DIM_ATT)
    return (out_p.reshape(nb, seq, d), out_s.reshape(ndb, dseq, d),
            kp_tail.reshape(1, nb, ATT_WINDOW, *heads), vp_tail.reshape(1, nb, ATT_WINDOW, *heads),
            state_p[None],
            ks_new.reshape(1, ndb, dseq, *heads), vs_new.reshape(1, ndb, dseq, *heads),
            state_s[None])
```

```python
import functools

import numpy as np
import jax
import jax.numpy as jnp
from jax import lax
from jax.experimental import pallas as pl
from jax.experimental.pallas import tpu as pltpu

F32 = jnp.float32
BF16 = jnp.bfloat16
I32 = jnp.int32

D_MODEL = 1024
GROUP_W = 512
N_SLOTS = 7
N_HEADS_ATT = 8
HEAD_DIM_ATT = 64
N_HEADS_RET = 4
HEAD_DIM_RET = 128
CHUNK = 64
ATT_WINDOW = 512
MAX_REL = 256
PAST_LEN = 2048
RET_DECAY_OFFSET = 5.0
ROPE_BASE = 10000.0
N_EXPERTS = 32
TOP_K = 4
D_FF = 1024
SWIGLU_LIMIT = 7.0
SWIGLU_ALPHA = 1.702
N_ADA = 6
NORM_EPS = 1e-6
NEG_INF = -1e30
LOG2_E = 1.4426950408889634

TM = 512
GROUPS_PER_TILE = TM // CHUNK
ATT_QB = 256
RET_CB = 256
SEG_ALIGN = 16
CAP = TOP_K * TM + N_EXPERTS * SEG_ALIGN
BM = 512
VMEM_LIMIT = 56 * 1024 * 1024


def _cparams(sem, vmem=None):
    return pltpu.CompilerParams(dimension_semantics=sem, vmem_limit_bytes=vmem)


def _ada_kernel(c_ref, w_ref, b_ref, o_ref):
    c = c_ref[...]
    s = c * jax.nn.sigmoid(c)
    o_ref[...] = jnp.dot(s.astype(BF16), w_ref[...].astype(BF16),
                         preferred_element_type=F32) + b_ref[...]


def _ada(c_all, w_ada, b_ada):
    n, d = c_all.shape
    cols = w_ada.shape[1]
    tn = 1536
    return pl.pallas_call(
        _ada_kernel,
        grid=(cols // tn,),
        in_specs=[pl.BlockSpec((n, d), lambda j: (0, 0)),
                  pl.BlockSpec((d, tn), lambda j: (0, j)),
                  pl.BlockSpec((1, tn), lambda j: (0, j))],
        out_specs=pl.BlockSpec((n, tn), lambda j: (0, j)),
        out_shape=jax.ShapeDtypeStruct((n, cols), F32),
        compiler_params=_cparams(("arbitrary",), VMEM_LIMIT),
        name="ada",
    )(c_all, w_ada, b_ada.reshape(1, cols))


def _rms_rows(x, g):
    ms = jnp.mean(x * x, axis=-1, keepdims=True)
    return x * lax.rsqrt(ms + NORM_EPS) * g


def _mod_row(ntp, tps, nb):
    return lambda i, *_: (jnp.where(i < ntp, i // tps, nb), 0)


def _per_group(x, fn, *mods):
    x3 = x.reshape(GROUPS_PER_TILE, CHUNK, x.shape[-1])
    y3 = fn(x3, *[m[:, None, :] for m in mods])
    return y3.reshape(x.shape)


def _inproj_kernel(ntp, xp_ref, xs_ref, sh_ref, sc_ref, gn_ref, w_ref, bd_ref, gq_ref, gk_ref,
                   cos_ref, sin_ref,
                   qa_ref, ka_ref, va_ref, qb_ref, kb_ref, vb_ref, gb_ref,
                   kpt_ref, vpt_ref, kst_ref, vst_ref):
    i = pl.program_id(0)
    is_p = i < ntp
    x = jnp.where(is_p, xp_ref[...], xs_ref[...])
    y = _rms_rows(x, gn_ref[...])
    h = _per_group(y, lambda a, sh, sc: a * (1.0 + sc) + sh, sh_ref[...], sc_ref[...])
    hb = h.astype(BF16)

    def proj(s):
        return jnp.dot(hb, w_ref[:, s * GROUP_W:(s + 1) * GROUP_W], preferred_element_type=F32)

    def head_rms(z, g):
        zz = (z * z).astype(BF16)
        half = GROUP_W // 2
        ss = jnp.concatenate(
            [jnp.dot(zz[:, :half], bd_ref[...], preferred_element_type=F32),
             jnp.dot(zz[:, half:], bd_ref[...], preferred_element_type=F32)], axis=1)
        return z * lax.rsqrt(ss * (1.0 / HEAD_DIM_ATT) + NORM_EPS) * g

    cos = cos_ref[...]
    sin = sin_ref[...]

    def rot(z):
        outs = []
        for hh in range(N_HEADS_RET):
            zh = z[:, hh * HEAD_DIM_RET:(hh + 1) * HEAD_DIM_RET]
            outs.append(zh * cos + pltpu.roll(zh, HEAD_DIM_RET // 2, axis=1) * sin)
        return jnp.concatenate(outs, axis=1)

    qa_ref[...] = head_rms(proj(0), gq_ref[...]).astype(BF16)
    ka = head_rms(proj(1), gk_ref[...])
    ka_ref[...] = ka.T.astype(BF16)
    va = proj(2)
    va_ref[...] = va.astype(BF16)

    @pl.when(is_p)
    def _():
        kpt_ref[...] = ka
        vpt_ref[...] = va

    @pl.when(jnp.logical_not(is_p))
    def _():
        kst_ref[...] = ka
        vst_ref[...] = va

    qb_ref[...] = rot(proj(3)).astype(BF16)
    kb_ref[...] = (rot(proj(4)) * (HEAD_DIM_RET ** -0.5)).astype(BF16)
    vb_ref[...] = proj(5).astype(BF16)
    gb_ref[...] = proj(6).astype(BF16)


def _inproj(xp, xs, shift, scale, g_norm, w_in_b, bd, gq8, gk8, cos_t, sin_t, nb, tps):
    rp = xp.shape[0]
    ntp = rp // TM
    nt = ntp + 1
    r = rp + TM
    row = lambda i: (i, 0)
    full = lambda i: (0, 0)
    tab = lambda i: (jnp.where(i < ntp, i % tps, tps), 0)
    act = jax.ShapeDtypeStruct((r, GROUP_W), BF16)
    return pl.pallas_call(
        functools.partial(_inproj_kernel, ntp),
        grid=(nt,),
        in_specs=[pl.BlockSpec((TM, D_MODEL), lambda i: (jnp.minimum(i, ntp - 1), 0)),
                  pl.BlockSpec((TM, D_MODEL), full),
                  pl.BlockSpec((GROUPS_PER_TILE, D_MODEL), _mod_row(ntp, tps, nb)),
                  pl.BlockSpec((GROUPS_PER_TILE, D_MODEL), _mod_row(ntp, tps, nb)),
                  pl.BlockSpec((1, D_MODEL), full),
                  pl.BlockSpec((D_MODEL, N_SLOTS * GROUP_W), full),
                  pl.BlockSpec((GROUP_W // 2, GROUP_W // 2), full),
                  pl.BlockSpec((1, GROUP_W), full),
                  pl.BlockSpec((1, GROUP_W), full),
                  pl.BlockSpec((TM, HEAD_DIM_RET), tab),
                  pl.BlockSpec((TM, HEAD_DIM_RET), tab)],
        out_specs=[pl.BlockSpec((TM, GROUP_W), row), pl.BlockSpec((GROUP_W, TM), lambda i: (0, i))]
        + [pl.BlockSpec((TM, GROUP_W), row)] * 5 + [
            pl.BlockSpec((TM, GROUP_W), lambda i: (jnp.minimum(i // tps, nb - 1), 0)),
            pl.BlockSpec((TM, GROUP_W), lambda i: (jnp.minimum(i // tps, nb - 1), 0)),
            pl.BlockSpec((TM, GROUP_W), full),
            pl.BlockSpec((TM, GROUP_W), full)],
        out_shape=[act, jax.ShapeDtypeStruct((GROUP_W, r), BF16)] + [act] * 5
        + [jax.ShapeDtypeStruct((nb * TM, GROUP_W), F32)] * 2
        + [jax.ShapeDtypeStruct((TM, GROUP_W), F32)] * 2,
        compiler_params=_cparams(("arbitrary",), VMEM_LIMIT),
        name="inproj",
    )(xp, xs, shift, scale, g_norm, w_in_b, bd, gq8, gk8, cos_t, sin_t)


def _attn_heads(q, k, v, bias_ref, k_feature_major, first_valid_col=None):
    qb_rows, kb_rows = q.shape[0], v.shape[0]
    if first_valid_col is not None:
        valid = lax.broadcasted_iota(I32, (qb_rows, kb_rows), 1) >= first_valid_col
    pair_w = 2 * HEAD_DIM_ATT
    low = lax.broadcasted_iota(I32, (1, pair_w), 1) < HEAD_DIM_ATT
    outs = []
    for pp in range(N_HEADS_ATT // 2):
        ps = slice(pp * pair_w, (pp + 1) * pair_w)
        q2, v2 = q[:, ps], v[:, ps]
        k2 = k[ps, :] if k_feature_major else k[:, ps]
        halves = []
        for half in range(2):
            qm = jnp.where(low if half == 0 else jnp.logical_not(low), q2, jnp.zeros_like(q2))
            if k_feature_major:
                s = jnp.dot(qm, k2, preferred_element_type=F32)
            else:
                s = lax.dot_general(qm, k2, (((1,), (1,)), ((), ())), preferred_element_type=F32)
            s = s + bias_ref[2 * pp + half]
            if first_valid_col is not None:
                s = jnp.where(valid, s, NEG_INF)
            m = jnp.max(s, axis=-1, keepdims=True)
            e = jnp.exp2(s - m)
            l = jnp.sum(e, axis=-1, keepdims=True)
            halves.append(jnp.dot(e.astype(BF16), v2, preferred_element_type=F32) / l)
        outs.append(jnp.where(low, halves[0], halves[1]))
    return jnp.concatenate(outs, axis=1)


def _fill_band_bias(rev_ref, bias_scr):
    _, qb_rows, kb_rows = bias_scr.shape
    width = rev_ref.shape[1]
    q = lax.broadcasted_iota(I32, (qb_rows, kb_rows), 0)
    k = lax.broadcasted_iota(I32, (qb_rows, kb_rows), 1)
    qc = q >> 6
    kc = (k - ATT_WINDOW) >> 6
    band = (kc >= qc - ATT_WINDOW // CHUNK) & (kc <= qc)
    for hh in range(N_HEADS_ATT):
        rows = jnp.broadcast_to(rev_ref[hh:hh + 1, :], (qb_rows, width))
        toep = pltpu.roll(rows, width - MAX_REL, 1, stride=1, stride_axis=0)
        bias_scr[hh] = jnp.where(band, toep[:, :kb_rows] * LOG2_E, NEG_INF)


def _attn_sample_kernel(q_ref, kn_ref, vn_ref, kc_ref, vc_ref, rev_ref, o_ref, bias_scr):
    @pl.when(pl.program_id(0) == 0)
    def _():
        _fill_band_bias(rev_ref, bias_scr)

    k = jnp.concatenate([kc_ref[0], kn_ref[...].astype(BF16)], axis=0)
    v = jnp.concatenate([vc_ref[0], vn_ref[...]], axis=0)
    o_ref[...] = _attn_heads(q_ref[...], k, v, bias_scr, False).astype(BF16)


def _attn_sample(qa, ks_new, va, kc, vc, rev, rp):
    ndb = kc.shape[0]
    base = rp // CHUNK
    spec = pl.BlockSpec((CHUNK, GROUP_W), lambda b: (base + b, 0))
    cspec = pl.BlockSpec((1, ATT_WINDOW, GROUP_W), lambda b: (b, 0, 0))
    return pl.pallas_call(
        _attn_sample_kernel,
        grid=(ndb,),
        in_specs=[spec, pl.BlockSpec((CHUNK, GROUP_W), lambda b: (b, 0)), spec, cspec, cspec,
                  pl.BlockSpec(rev.shape, lambda b: (0, 0))],
        out_specs=pl.BlockSpec((CHUNK, GROUP_W), lambda b: (b, 0)),
        out_shape=jax.ShapeDtypeStruct((ndb * CHUNK, GROUP_W), BF16),
        scratch_shapes=[pltpu.VMEM((N_HEADS_ATT, CHUNK, ATT_WINDOW + CHUNK), F32)],
        compiler_params=_cparams(("arbitrary",), VMEM_LIMIT),
        name="attn_sample",
    )(qa, ks_new, va, kc, vc, rev)


def _ret_chunk(state_decay, q_ref, k_ref, v_ref, g_ref, dm_ref, xi_ref, zeta_ref, gro_ref, o_ref, s_scr):
    outs = []
    for hh in range(N_HEADS_RET):
        hs = slice(hh * HEAD_DIM_RET, (hh + 1) * HEAD_DIM_RET)
        q = q_ref[:, hs]
        k = k_ref[:, hs]
        v = v_ref[:, hs]
        st = s_scr[hh]
        sc = lax.dot_general(q, k, (((1,), (1,)), ((), ())), preferred_element_type=F32) * dm_ref[hh]
        inner = jnp.dot(sc.astype(BF16), v, preferred_element_type=F32)
        cross = jnp.dot(q, st.astype(BF16), preferred_element_type=F32) * xi_ref[:, hs]
        o = inner + cross
        kz = k.astype(F32) * zeta_ref[:, hs]
        s_scr[hh] = state_decay[hh] * st + jnp.dot(kz.T.astype(BF16), v, preferred_element_type=F32)
        mu = jnp.mean(o, axis=-1, keepdims=True)
        oc = o - mu
        var = jnp.mean(oc * oc, axis=-1, keepdims=True)
        outs.append(oc * lax.rsqrt(var + NORM_EPS))
    y = jnp.concatenate(outs, axis=1) * gro_ref[...]
    g = g_ref[...].astype(F32)
    o_ref[...] = (g * jax.nn.sigmoid(g) * y).astype(BF16)


def _ret_kernel(state_decay, q_ref, k_ref, v_ref, g_ref, s0_ref, dm_ref, xi_ref, zeta_ref,
                gro_ref, o_ref, sn_ref, s_scr):
    j = pl.program_id(1)

    @pl.when(j == 0)
    def _():
        s_scr[...] = s0_ref[0]

    _ret_chunk(state_decay, q_ref, k_ref, v_ref, g_ref, dm_ref, xi_ref, zeta_ref, gro_ref, o_ref, s_scr)

    @pl.when(j == pl.num_programs(1) - 1)
    def _():
        sn_ref[0] = s_scr[...]


def _mix_prompt_kernel(state_decay, q_ref, k0_ref, k1_ref, k2_ref, v0_ref, v1_ref, v2_ref, rev_ref,
                       rq_ref, rk_ref, rv_ref, rg_ref, s0_ref, dm_ref, xi_ref, zeta_ref, gro_ref,
                       att_ref, ret_ref, sn_ref, bias_scr, s_scr):
    j = pl.program_id(1)

    @pl.when((pl.program_id(0) == 0) & (j == 0))
    def _():
        _fill_band_bias(rev_ref, bias_scr)

    @pl.when(j == 0)
    def _():
        s_scr[...] = s0_ref[0]

    k = jnp.concatenate([k0_ref[...], k1_ref[...], k2_ref[...]], axis=1)
    v = jnp.concatenate([v0_ref[...], v1_ref[...], v2_ref[...]], axis=0)

    def block(first_valid_col):
        att_ref[...] = _attn_heads(q_ref[...], k, v, bias_scr, True, first_valid_col).astype(BF16)
        _ret_chunk(state_decay, rq_ref, rk_ref, rv_ref, rg_ref, dm_ref, xi_ref, zeta_ref, gro_ref,
                   ret_ref, s_scr)

    @pl.when(j >= 2)
    def _():
        block(None)

    @pl.when(j < 2)
    def _():
        block((2 - j) * ATT_QB)

    @pl.when(j == pl.num_programs(1) - 1)
    def _():
        sn_ref[0] = s_scr[...]


def _mix_prompt(qa, ka_t, va, rev, qb, kb, vb, gb, s0, g_ro, nb, seq):
    assert ATT_QB == RET_CB
    r = nb * seq
    nq = seq // ATT_QB
    dm, xi, zeta, state_decay = _ret_consts(RET_CB)
    blk = lambda back: (lambda b, j: (b * nq + jnp.maximum(j - back, 0), 0))
    spec = lambda back: pl.BlockSpec((ATT_QB, GROUP_W), blk(back))
    tspec = lambda back: pl.BlockSpec((GROUP_W, ATT_QB), lambda b, j: (0, b * nq + jnp.maximum(j - back, 0)))
    sspec = pl.BlockSpec((1, N_HEADS_RET, HEAD_DIM_RET, HEAD_DIM_RET), lambda b, j: (b, 0, 0, 0))
    full2 = lambda b, j: (0, 0)
    out = jax.ShapeDtypeStruct((r, GROUP_W), BF16)
    return pl.pallas_call(
        functools.partial(_mix_prompt_kernel, state_decay),
        grid=(nb, nq),
        in_specs=[spec(0), tspec(2), tspec(1), tspec(0), spec(2), spec(1), spec(0),
                  pl.BlockSpec(rev.shape, full2),
                  spec(0), spec(0), spec(0), spec(0), sspec,
                  pl.BlockSpec(dm.shape, lambda b, j: (0, 0, 0)),
                  pl.BlockSpec(xi.shape, full2), pl.BlockSpec(zeta.shape, full2),
                  pl.BlockSpec((1, GROUP_W), full2)],
        out_specs=[spec(0), spec(0), sspec],
        out_shape=[out, out, jax.ShapeDtypeStruct(s0.shape, F32)],
        scratch_shapes=[pltpu.VMEM((N_HEADS_ATT, ATT_QB, ATT_WINDOW + ATT_QB), F32),
                        pltpu.VMEM((N_HEADS_RET, HEAD_DIM_RET, HEAD_DIM_RET), F32)],
        compiler_params=_cparams(("arbitrary", "arbitrary"), VMEM_LIMIT),
        name="mix_prompt",
    )(qa, ka_t, ka_t, ka_t, va, va, va, rev, qb, kb, vb, gb, s0, dm, xi, zeta, g_ro)


def _ret_consts(cb):
    log_g = np.log1p(-np.exp2(-RET_DECAY_OFFSET - np.arange(N_HEADS_RET, dtype=np.float64)))
    n = np.arange(cb, dtype=np.float64)
    diff = n[:, None] - n[None, :]
    dm = np.where(diff[None] >= 0, np.exp(np.maximum(diff, 0.0)[None] * log_g[:, None, None]), 0.0)
    xi = np.exp((n + 1.0)[:, None] * log_g[None, :])
    zeta = np.exp((cb - 1.0 - n)[:, None] * log_g[None, :])
    rep = lambda a: np.repeat(a, HEAD_DIM_RET, axis=1)
    state_decay = tuple(float(v) for v in np.exp(cb * log_g))
    return (jnp.asarray(dm, F32), jnp.asarray(rep(xi), F32), jnp.asarray(rep(zeta), F32), state_decay)


def _ret(qb, kb, vb, gb, s0, g_ro, cb, row0, nb, nc, name):
    dm, xi, zeta, state_decay = _ret_consts(cb)
    base = row0 // cb
    spec = pl.BlockSpec((cb, GROUP_W), lambda b, j: (base + b * nc + j, 0))
    sspec = pl.BlockSpec((1, N_HEADS_RET, HEAD_DIM_RET, HEAD_DIM_RET), lambda b, j: (b, 0, 0, 0))
    full2 = lambda b, j: (0, 0)
    return pl.pallas_call(
        functools.partial(_ret_kernel, state_decay),
        grid=(nb, nc),
        in_specs=[spec, spec, spec, spec, sspec,
                  pl.BlockSpec(dm.shape, lambda b, j: (0, 0, 0)),
                  pl.BlockSpec(xi.shape, full2), pl.BlockSpec(zeta.shape, full2),
                  pl.BlockSpec((1, GROUP_W), full2)],
        out_specs=[pl.BlockSpec((cb, GROUP_W), lambda b, j: (b * nc + j, 0)), sspec],
        out_shape=[jax.ShapeDtypeStruct((nb * nc * cb, GROUP_W), BF16),
                   jax.ShapeDtypeStruct(s0.shape, F32)],
        scratch_shapes=[pltpu.VMEM((N_HEADS_RET, HEAD_DIM_RET, HEAD_DIM_RET), F32)],
        compiler_params=_cparams(("arbitrary", "arbitrary"), VMEM_LIMIT),
        name=name,
    )(qb, kb, vb, gb, s0, dm, xi, zeta, g_ro)


def _outproj_kernel(ntp, attp_ref, atts_ref, retp_ref, rets_ref, xp_ref, xs_ref, gm_ref, shf_ref, scf_ref,
                    gn_ref, wo_ref, wr_ref, br_ref, upper_ref, lower_ref,
                    x1_ref, h2_ref, slot_ref, cols_ref, cnt_ref):
    i = pl.program_id(0)
    is_p = i < ntp
    x = jnp.where(is_p, xp_ref[...], xs_ref[...])
    att = jnp.where(is_p, attp_ref[...], atts_ref[...])
    ret = jnp.where(is_p, retp_ref[...], rets_ref[...])
    mix = (jnp.dot(att, wo_ref[:GROUP_W, :], preferred_element_type=F32)
           + jnp.dot(ret, wo_ref[GROUP_W:, :], preferred_element_type=F32))
    x1 = _per_group(mix, lambda a, gm: a * gm, gm_ref[...]) + x
    x1_ref[...] = x1
    y = _rms_rows(x1, gn_ref[...])
    h2 = _per_group(y, lambda a, sh, sc: a * (1.0 + sc) + sh, shf_ref[...], scf_ref[...])
    h2b = h2.astype(BF16)
    h2_ref[...] = h2b

    logits = lax.dot_general(wr_ref[...], h2b, (((1,), (1,)), ((), ())),
                             preferred_element_type=F32) + br_ref[...]
    eidx = lax.broadcasted_iota(I32, logits.shape, 0).astype(F32)
    work = logits
    sel, top = [], []
    for _ in range(TOP_K):
        m = jnp.max(work, axis=0, keepdims=True)
        idx = jnp.min(jnp.where(work == m, eidx, float(N_EXPERTS)), axis=0, keepdims=True)
        hit = eidx == idx
        sel.append(hit)
        top.append(m)
        work = jnp.where(hit, -jnp.inf, work)
    ex = [jnp.exp(t - top[0]) for t in top]
    den = ex[0] + ex[1] + ex[2] + ex[3]
    gates = [e / den for e in ex]

    multi = (sel[0] | sel[1] | sel[2] | sel[3])
    multi_f = jnp.where(multi, 1.0, 0.0)
    rank = jnp.dot(multi_f.astype(BF16), upper_ref[...], preferred_element_type=F32)
    cnt = jnp.sum(multi_f, axis=1, keepdims=True)
    cnt_pad = jnp.floor((cnt + (SEG_ALIGN - 1.0)) * (1.0 / SEG_ALIGN)) * SEG_ALIGN
    cnt_pad_b = jnp.broadcast_to(cnt_pad, (N_EXPERTS, 128))
    seg_off = jnp.dot(lower_ref[...], cnt_pad_b.astype(BF16), preferred_element_type=F32)[:, :1]
    pos = seg_off + rank
    slots = [jnp.sum(jnp.where(s, pos, 0.0), axis=0, keepdims=True) for s in sel]
    slot_rows = jnp.concatenate(slots, axis=0)
    gate_rows = jnp.concatenate(gates, axis=0)
    slot_ref[0] = slot_rows.astype(I32)
    cnt_ref[0] = cnt_pad_b.astype(I32)
    both = jnp.concatenate([slot_rows, gate_rows, jnp.zeros((128 - 2 * TOP_K, TM), F32)], axis=0)
    cols_ref[0] = both.T


def _outproj(att_p, att_s, ret_p, ret_s, xp, xs, gate_m, shift_f, scale_f, g_norm, w_out_b, wr_t, br,
             upper, lower, nb, tps):
    rp = xp.shape[0]
    ntp = rp // TM
    nt = ntp + 1
    r = rp + TM
    row = lambda i: (i, 0)
    row3 = lambda i: (i, 0, 0)
    full = lambda i: (0, 0)
    prow = lambda i: (jnp.minimum(i, ntp - 1), 0)
    mod = pl.BlockSpec((GROUPS_PER_TILE, D_MODEL), _mod_row(ntp, tps, nb))
    return pl.pallas_call(
        functools.partial(_outproj_kernel, ntp),
        grid=(nt,),
        in_specs=[pl.BlockSpec((TM, GROUP_W), prow), pl.BlockSpec((TM, GROUP_W), full),
                  pl.BlockSpec((TM, GROUP_W), prow), pl.BlockSpec((TM, GROUP_W), full),
                  pl.BlockSpec((TM, D_MODEL), prow),
                  pl.BlockSpec((TM, D_MODEL), full),
                  mod, mod, mod,
                  pl.BlockSpec((1, D_MODEL), full),
                  pl.BlockSpec((D_MODEL, D_MODEL), full),
                  pl.BlockSpec((N_EXPERTS, D_MODEL), full),
                  pl.BlockSpec((N_EXPERTS, 1), full),
                  pl.BlockSpec((TM, TM), full),
                  pl.BlockSpec((N_EXPERTS, N_EXPERTS), full)],
        out_specs=[pl.BlockSpec((TM, D_MODEL), row), pl.BlockSpec((TM, D_MODEL), row),
                   pl.BlockSpec((1, TOP_K, TM), row3),
                   pl.BlockSpec((1, TM, 128), row3), pl.BlockSpec((1, N_EXPERTS, 128), row3)],
        out_shape=[jax.ShapeDtypeStruct((r, D_MODEL), F32), jax.ShapeDtypeStruct((r, D_MODEL), BF16),
                   jax.ShapeDtypeStruct((nt, TOP_K, TM), I32),
                   jax.ShapeDtypeStruct((nt, TM, 128), F32), jax.ShapeDtypeStruct((nt, N_EXPERTS, 128), I32)],
        compiler_params=_cparams(("arbitrary",), VMEM_LIMIT),
        name="outproj",
    )(att_p, att_s, ret_p, ret_s, xp, xs, gate_m, shift_f, scale_f, g_norm, w_out_b, wr_t, br, upper, lower)


def _rows_copy(n, src_rows, dst_rows, sem):
    size = pl.multiple_of(n, SEG_ALIGN)
    return pltpu.make_async_copy(src_rows(size), dst_rows(size), sem)


def _start_segments(t, cnt_ref, off_ref, base_ref, local_rows, sorted_rows, sem, to_sorted):
    def body(e, c):
        n = cnt_ref[t * N_EXPERTS + e]
        off = pl.multiple_of(off_ref[t * N_EXPERTS + e], SEG_ALIGN)
        base = pl.multiple_of(base_ref[t * N_EXPERTS + e], SEG_ALIGN)
        local = lambda z: local_rows(off, z)
        remote = lambda z: sorted_rows(base, z)

        @pl.when(n > 0)
        def _():
            (_rows_copy(n, local, remote, sem) if to_sorted else _rows_copy(n, remote, local, sem)).start()
        return c
    lax.fori_loop(0, N_EXPERTS, body, 0)


def _dispatch_kernel(nt, n_blocks, off_ref, cnt_ref, base_ref, tot_ref, tail0_ref, tailn_ref, na_ref,
                     h2_ref, slot_ref, xb_ref, xs_scr, zero_scr, sems, tail_sem):
    i = pl.program_id(0)
    cur = i % 2
    sorted_rows = lambda r, z: xb_ref.at[pl.ds(r, z), :]

    def start_tile(t, buf):
        _start_segments(t, cnt_ref, off_ref, base_ref, lambda r, z: xs_scr.at[buf, pl.ds(r, z), :],
                        sorted_rows, sems.at[buf], True)

    def wait_tile(t, buf):
        _rows_copy(tot_ref[t], lambda z: xs_scr.at[buf, pl.ds(0, z), :], lambda z: sorted_rows(0, z),
                   sems.at[buf]).wait()

    def tail_copies(wait):
        def body(e, c):
            base = pl.multiple_of(tail0_ref[e], SEG_ALIGN)

            @pl.when(tailn_ref[e] > 0)
            def _():
                cp = _rows_copy(tailn_ref[e], lambda z: zero_scr.at[pl.ds(0, z), :],
                                lambda z: sorted_rows(base, z), tail_sem)
                cp.wait() if wait else cp.start()
            return c
        lax.fori_loop(0, N_EXPERTS, body, 0)

        def unused(j, c):
            cp = pltpu.make_async_copy(zero_scr, sorted_rows(pl.multiple_of(j * BM, BM), BM), tail_sem)
            cp.wait() if wait else cp.start()
            return c
        lax.fori_loop(na_ref[0], n_blocks, unused, 0)

    @pl.when(i >= 2)
    def _():
        wait_tile(i - 2, cur)

    slot = slot_ref[0]
    srow = lax.broadcasted_iota(I32, (CAP, TM), 0)
    hit = (srow == slot[0:1]) | (srow == slot[1:2]) | (srow == slot[2:3]) | (srow == slot[3:4])
    onehot = jnp.where(hit, 1.0, 0.0).astype(BF16)
    xs_scr[cur] = jnp.dot(onehot, h2_ref[...], preferred_element_type=F32).astype(BF16)
    start_tile(i, cur)

    @pl.when(i == nt - 1)
    def _():
        zero_scr[...] = jnp.zeros_like(zero_scr)
        tail_copies(False)
        if nt >= 2:
            wait_tile(i - 1, 1 - cur)
        wait_tile(i, cur)
        tail_copies(True)


def _dispatch(h2, slot, off, cnt, base, tot, tail0, tailn, n_act, n_blocks):
    nt = slot.shape[0]
    n_rows = n_blocks * BM
    grid_spec = pltpu.PrefetchScalarGridSpec(
        num_scalar_prefetch=7,
        grid=(nt,),
        in_specs=[pl.BlockSpec((TM, D_MODEL), lambda i, *_: (i, 0)),
                  pl.BlockSpec((1, TOP_K, TM), lambda i, *_: (i, 0, 0))],
        out_specs=pl.BlockSpec(memory_space=pl.ANY),
        scratch_shapes=[pltpu.VMEM((2, CAP, D_MODEL), BF16),
                        pltpu.VMEM((BM, D_MODEL), BF16),
                        pltpu.SemaphoreType.DMA((2,)),
                        pltpu.SemaphoreType.DMA(())],
    )
    return pl.pallas_call(
        functools.partial(_dispatch_kernel, nt, n_blocks),
        grid_spec=grid_spec,
        out_shape=jax.ShapeDtypeStruct((n_rows, D_MODEL), BF16),
        compiler_params=_cparams(("arbitrary",), VMEM_LIMIT),
        name="dispatch",
    )(off, cnt, base, tot, tail0, tailn, n_act, h2, slot)


def _experts_kernel(be_ref, bi_ref, nx_ref, na_ref, x_ref, wu_hbm, bu_ref, wd_hbm, bd_ref, y_ref,
                    wu_stage, wd_stage, wu_scr, wd_scr, sems):
    j = pl.program_id(0)

    def weight_copies(e):
        return (pltpu.make_async_copy(wu_hbm.at[e], wu_stage, sems.at[0]),
                pltpu.make_async_copy(wd_hbm.at[e], wd_stage, sems.at[1]))

    @pl.when(j < na_ref[0])
    def _():
        e = be_ref[j]
        prev = be_ref[jnp.maximum(j - 1, 0)]

        @pl.when(j == 0)
        def _():
            for cp in weight_copies(e):
                cp.start()

        @pl.when((j == 0) | (e != prev))
        def _():
            for cp in weight_copies(e):
                cp.wait()
            wu_scr[...] = wu_stage[...].astype(BF16)
            wd_scr[...] = wd_stage[...].astype(BF16)

            @pl.when(nx_ref[j] != e)
            def _():
                for cp in weight_copies(nx_ref[j]):
                    cp.start()

        u = jnp.dot(x_ref[...], wu_scr[...], preferred_element_type=F32) + bu_ref[0]
        glu = jnp.minimum(u[:, :D_FF], SWIGLU_LIMIT)
        lin = jnp.clip(u[:, D_FF:], -SWIGLU_LIMIT, SWIGLU_LIMIT)
        act = glu * jax.nn.sigmoid(SWIGLU_ALPHA * glu) * (lin + 1.0)
        y = jnp.dot(act.astype(BF16), wd_scr[...], preferred_element_type=F32) + bd_ref[0]
        y_ref[...] = y.astype(BF16)

    @pl.when(j >= na_ref[0])
    def _():
        y_ref[...] = jnp.zeros_like(y_ref)


def _experts(xb, blk_e, blk_i, blk_nx, n_act, w_up, b_up, w_down, b_down):
    n_rows = xb.shape[0]
    nblk = n_rows // BM
    grid_spec = pltpu.PrefetchScalarGridSpec(
        num_scalar_prefetch=4,
        grid=(nblk,),
        in_specs=[pl.BlockSpec((BM, D_MODEL), lambda j, be, bi, nx, na: (bi[j], 0)),
                  pl.BlockSpec(memory_space=pl.ANY),
                  pl.BlockSpec((1, 1, 2 * D_FF), lambda j, be, bi, nx, na: (be[j], 0, 0)),
                  pl.BlockSpec(memory_space=pl.ANY),
                  pl.BlockSpec((1, 1, D_MODEL), lambda j, be, bi, nx, na: (be[j], 0, 0))],
        out_specs=pl.BlockSpec((BM, D_MODEL), lambda j, be, bi, nx, na: (j, 0)),
        scratch_shapes=[pltpu.VMEM((D_MODEL, 2 * D_FF), F32), pltpu.VMEM((D_FF, D_MODEL), F32),
                        pltpu.VMEM((D_MODEL, 2 * D_FF), BF16), pltpu.VMEM((D_FF, D_MODEL), BF16),
                        pltpu.SemaphoreType.DMA((2,))],
    )
    return pl.pallas_call(
        _experts_kernel,
        grid_spec=grid_spec,
        out_shape=jax.ShapeDtypeStruct((n_rows, D_MODEL), BF16),
        compiler_params=_cparams(("arbitrary",), VMEM_LIMIT),
        name="experts",
    )(blk_e, blk_i, blk_nx, n_act, xb, w_up, b_up.reshape(N_EXPERTS, 1, 2 * D_FF), w_down,
      b_down.reshape(N_EXPERTS, 1, D_MODEL))


def _combine_kernel(nt, ntp, off_ref, cnt_ref, base_ref, tot_ref, yb_ref, cols_ref, x1_ref, gf_ref,
                    op_ref, os_ref, ys_scr, sems):
    i = pl.program_id(0)
    cur = i % 2

    sorted_rows = lambda r, z: yb_ref.at[pl.ds(r, z), :]

    def start_tile(t, buf):
        _start_segments(t, cnt_ref, off_ref, base_ref, lambda r, z: ys_scr.at[buf, pl.ds(r, z), :],
                        sorted_rows, sems.at[buf], False)

    @pl.when(i == 0)
    def _():
        ys_scr[...] = jnp.zeros_like(ys_scr)
        start_tile(0, 0)

    @pl.when(i + 1 < nt)
    def _():
        start_tile(i + 1, 1 - cur)

    _rows_copy(tot_ref[i], lambda z: sorted_rows(0, z), lambda z: ys_scr.at[cur, pl.ds(0, z), :],
               sems.at[cur]).wait()

    cols = cols_ref[0]
    lane = lax.broadcasted_iota(I32, (TM, CAP), 1)
    w = jnp.zeros((TM, CAP), F32)
    for k in range(TOP_K):
        sk = cols[:, k:k + 1].astype(I32)
        gk = cols[:, TOP_K + k:TOP_K + k + 1]
        w = jnp.where(lane == sk, gk, w)
    y = jnp.dot(w.astype(BF16), ys_scr[cur], preferred_element_type=F32)
    out = x1_ref[...] + _per_group(y, lambda a, gf: a * gf, gf_ref[...])

    @pl.when(i < ntp)
    def _():
        op_ref[...] = out

    @pl.when(i >= ntp)
    def _():
        os_ref[...] = out


def _combine(yb, cols, x1, gate_f, off, cnt, base, tot, ntp, nb, tps):
    nt = cols.shape[0]
    grid_spec = pltpu.PrefetchScalarGridSpec(
        num_scalar_prefetch=4,
        grid=(nt,),
        in_specs=[pl.BlockSpec(memory_space=pl.ANY),
                  pl.BlockSpec((1, TM, 128), lambda i, *_: (i, 0, 0)),
                  pl.BlockSpec((TM, D_MODEL), lambda i, *_: (i, 0)),
                  pl.BlockSpec((GROUPS_PER_TILE, D_MODEL), _mod_row(ntp, tps, nb))],
        out_specs=[pl.BlockSpec((TM, D_MODEL), lambda i, *_: (jnp.minimum(i, ntp - 1), 0)),
                   pl.BlockSpec((TM, D_MODEL), lambda i, *_: (0, 0))],
        scratch_shapes=[pltpu.VMEM((2, CAP, D_MODEL), BF16), pltpu.SemaphoreType.DMA((2,))],
    )
    return pl.pallas_call(
        functools.partial(_combine_kernel, nt, ntp),
        grid_spec=grid_spec,
        out_shape=[jax.ShapeDtypeStruct((ntp * TM, D_MODEL), F32),
                   jax.ShapeDtypeStruct((TM, D_MODEL), F32)],
        compiler_params=_cparams(("arbitrary",), VMEM_LIMIT),
        name="combine",
    )(off, cnt, base, tot, yb, cols, x1, gate_f)


def _rotary_tables(seq, dec_batch, dec_seq):
    half = HEAD_DIM_RET // 2
    inv = ROPE_BASE ** (-np.arange(half, dtype=np.float64) / half)
    pos = np.concatenate([np.arange(seq), np.tile(PAST_LEN + np.arange(dec_seq), dec_batch)])
    ang = pos.astype(np.float64)[:, None] * inv[None, :]
    cos = np.concatenate([np.cos(ang), np.cos(ang)], axis=1)
    sin = np.concatenate([-np.sin(ang), np.sin(ang)], axis=1)
    return jnp.asarray(cos, F32), jnp.asarray(sin, F32)


def _rel_bias_reversed(rel_bias):
    heads = rel_bias.shape[0]
    ext = jnp.concatenate([rel_bias[:, 1:], jnp.broadcast_to(rel_bias[:, -1:], (heads, 2 * MAX_REL))], axis=1)
    return ext[:, ::-1].astype(F32)


def _group_mods(m, nb, ndb):
    assert ndb == GROUPS_PER_TILE
    mp = jnp.broadcast_to(m[:nb, None], (nb, GROUPS_PER_TILE) + m.shape[1:])
    allm = jnp.concatenate([mp.reshape((nb * GROUPS_PER_TILE,) + m.shape[1:]), m[nb:]], axis=0)
    return jnp.transpose(allm, (1, 0, 2))


def _routing_tables(cnt, n_blocks):
    nt = cnt.shape[0]
    off = jnp.cumsum(cnt, axis=1) - cnt
    rows_e = jnp.sum(cnt, axis=0)
    nblk_e = (rows_e + BM - 1) // BM
    blk_end = jnp.cumsum(nblk_e)
    start_e = (blk_end - nblk_e) * BM
    base = start_e[None, :] + jnp.cumsum(cnt, axis=0) - cnt
    n_act = blk_end[-1]
    j = jnp.minimum(jnp.arange(n_blocks), n_act - 1)
    blk_e = jnp.minimum(jnp.sum(blk_end[None, :] <= j[:, None], axis=1), N_EXPERTS - 1)
    later = jnp.where(blk_e[None, :] > blk_e[:, None], blk_e[None, :], N_EXPERTS)
    blk_nx = jnp.min(later, axis=1)
    blk_nx = jnp.where(blk_nx == N_EXPERTS, blk_e, blk_nx)
    tail0 = start_e + rows_e
    tailn = nblk_e * BM - rows_e
    i32 = lambda a: a.astype(I32)
    return (i32(off.reshape(nt * N_EXPERTS)), i32(cnt.reshape(nt * N_EXPERTS)),
            i32(base.reshape(nt * N_EXPERTS)), i32(jnp.sum(cnt, axis=1)), i32(tail0), i32(tailn),
            i32(blk_e), i32(j), i32(blk_nx), i32(n_act.reshape(1)))


def kernel(x_prompt, x_sample, c_prompt, c_sample, cache_att_k, cache_att_v, state_ret, w_ada, b_ada,
           g_norm_mix, g_norm_ffn, w_in, g_q, g_k, rel_bias, g_ret_out, w_out, w_router, b_router,
           w_up, b_up, w_down, b_down):
    nb, seq, d = x_prompt.shape
    ndb, dseq, _ = x_sample.shape
    assert d == D_MODEL and ndb * dseq == TM and dseq == CHUNK
    assert seq % TM == 0 and seq >= ATT_WINDOW and cache_att_k.shape[2] == ATT_WINDOW
    assert w_ada.shape[0] == 1
    rp = nb * seq
    ntp = rp // TM
    nt = ntp + 1
    tps = seq // TM

    xp = x_prompt.reshape(rp, d)
    xs = x_sample.reshape(TM, d)

    m = _ada(jnp.concatenate([c_prompt, c_sample], axis=0), w_ada[0], b_ada[0])
    mods = _group_mods(m.reshape(nb + ndb, N_ADA, d), nb, ndb)
    shift_m, scale_m, gate_m, shift_f, scale_f, gate_f = [mods[a] for a in range(N_ADA)]

    cos_t, sin_t = _rotary_tables(seq, ndb, dseq)
    bd = jnp.asarray(np.kron(np.eye(N_HEADS_ATT // 2), np.ones((HEAD_DIM_ATT, HEAD_DIM_ATT))), BF16)
    tile8 = lambda g: jnp.tile(g.astype(F32), N_HEADS_ATT).reshape(1, GROUP_W)
    (qa, ka_t, va, qb, kb, vb, gb, kp_tail, vp_tail, ks_new, vs_new) = _inproj(
        xp, xs, shift_m, scale_m, g_norm_mix[0].reshape(1, d), w_in[0].astype(BF16), bd,
        tile8(g_q[0]) * (HEAD_DIM_ATT ** -0.5 * LOG2_E), tile8(g_k[0]), cos_t, sin_t, nb, tps)

    rev = _rel_bias_reversed(rel_bias[0])
    g_ro = g_ret_out[0].astype(F32).reshape(1, GROUP_W)
    zero_state = jnp.zeros((nb, N_HEADS_RET, HEAD_DIM_RET, HEAD_DIM_RET), F32)
    att_p, ret_p, state_p = _mix_prompt(qa, ka_t, va, rev, qb, kb, vb, gb, zero_state, g_ro, nb, seq)
    att_s = _attn_sample(qa, ks_new, va,
                         cache_att_k[0].reshape(ndb, ATT_WINDOW, GROUP_W).astype(BF16),
                         cache_att_v[0].reshape(ndb, ATT_WINDOW, GROUP_W).astype(BF16),
                         rev, rp)

    ret_s, state_s = _ret(qb, kb, vb, gb, state_ret[0].astype(F32), g_ro, CHUNK, rp, ndb, 1, "ret_sample")

    upper = jnp.asarray(np.triu(np.ones((TM, TM)), 1), BF16)
    lower = jnp.asarray(np.tril(np.ones((N_EXPERTS, N_EXPERTS)), -1), BF16)
    x1, h2, slot, cols, cnt = _outproj(
        att_p, att_s, ret_p, ret_s, xp, xs, gate_m, shift_f, scale_f, g_norm_ffn[0].reshape(1, d),
        w_out[0].astype(BF16), w_router[0].T.astype(BF16), b_router[0].astype(F32).reshape(N_EXPERTS, 1),
        upper, lower, nb, tps)

    n_blocks = (TOP_K * (rp + TM) + nt * N_EXPERTS * (SEG_ALIGN - 1)) // BM + 1 + N_EXPERTS
    off, cntf, base, tot, tail0, tailn, blk_e, blk_i, blk_nx, n_act = _routing_tables(cnt[:, :, 0], n_blocks)
    xb = _dispatch(h2, slot, off, cntf, base, tot, tail0, tailn, n_act, n_blocks)
    yb = _experts(xb, blk_e, blk_i, blk_nx, n_act, w_up[0], b_up[0], w_down[0], b_down[0])
    out_p, out_s = _combine(yb, cols, x1, gate_f, off, cntf, base, tot, ntp, nb, tps)

    heads = (N_HEADS_ATT, HEAD_DIM_ATT)
    return (out_p.reshape(nb, seq, d), out_s.reshape(ndb, dseq, d),
            kp_tail.reshape(1, nb, ATT_WINDOW, *heads), vp_tail.reshape(1, nb, ATT_WINDOW, *heads),
            state_p[None],
            ks_new.reshape(1, ndb, dseq, *heads), vs_new.reshape(1, ndb, dseq, *heads),
            state_s[None])
```

```python
import functools

import numpy as np
import jax
import jax.numpy as jnp
from jax import lax
from jax.experimental import pallas as pl
from jax.experimental.pallas import tpu as pltpu

F32 = jnp.float32
BF16 = jnp.bfloat16
I32 = jnp.int32

D_MODEL = 1024
GROUP_W = 512
N_SLOTS = 7
N_HEADS_ATT = 8
HEAD_DIM_ATT = 64
N_HEADS_RET = 4
HEAD_DIM_RET = 128
CHUNK = 64
ATT_WINDOW = 512
MAX_REL = 256
PAST_LEN = 2048
RET_DECAY_OFFSET = 5.0
ROPE_BASE = 10000.0
N_EXPERTS = 32
TOP_K = 4
D_FF = 1024
SWIGLU_LIMIT = 7.0
SWIGLU_ALPHA = 1.702
N_ADA = 6
NORM_EPS = 1e-6
NEG_INF = -1e30
LOG2_E = 1.4426950408889634

TM = 512
GROUPS_PER_TILE = TM // CHUNK
ATT_QB = 256
RET_CB = 256
SEG_ALIGN = 16
CAP = TOP_K * TM + N_EXPERTS * SEG_ALIGN
BM = 512
VMEM_LIMIT = 56 * 1024 * 1024


def _cparams(sem, vmem=None):
    return pltpu.CompilerParams(dimension_semantics=sem, vmem_limit_bytes=vmem)


def _ada_kernel(c_ref, w_ref, b_ref, o_ref):
    c = c_ref[...]
    s = c * jax.nn.sigmoid(c)
    o_ref[...] = jnp.dot(s.astype(BF16), w_ref[...].astype(BF16),
                         preferred_element_type=F32) + b_ref[...]


def _ada(c_all, w_ada, b_ada):
    n, d = c_all.shape
    cols = w_ada.shape[1]
    tn = 1536
    return pl.pallas_call(
        _ada_kernel,
        grid=(cols // tn,),
        in_specs=[pl.BlockSpec((n, d), lambda j: (0, 0)),
                  pl.BlockSpec((d, tn), lambda j: (0, j)),
                  pl.BlockSpec((1, tn), lambda j: (0, j))],
        out_specs=pl.BlockSpec((n, tn), lambda j: (0, j)),
        out_shape=jax.ShapeDtypeStruct((n, cols), F32),
        compiler_params=_cparams(("arbitrary",), VMEM_LIMIT),
        name="ada",
    )(c_all, w_ada, b_ada.reshape(1, cols))


def _rms_rows(x, g):
    ms = jnp.mean(x * x, axis=-1, keepdims=True)
    return x * lax.rsqrt(ms + NORM_EPS) * g


def _mod_row(ntp, tps, nb):
    return lambda i, *_: (jnp.where(i < ntp, i // tps, nb), 0)


def _per_group(x, fn, *mods):
    x3 = x.reshape(GROUPS_PER_TILE, CHUNK, x.shape[-1])
    y3 = fn(x3, *[m[:, None, :] for m in mods])
    return y3.reshape(x.shape)


def _inproj_kernel(ntp, xp_ref, xs_ref, sh_ref, sc_ref, gn_ref, w_ref, bd_ref, gq_ref, gk_ref,
                   cos_ref, sin_ref,
                   qa_ref, ka_ref, va_ref, qb_ref, kb_ref, vb_ref, gb_ref,
                   kpt_ref, vpt_ref, kst_ref, vst_ref):
    i = pl.program_id(0)
    is_p = i < ntp
    x = jnp.where(is_p, xp_ref[...], xs_ref[...])
    y = _rms_rows(x, gn_ref[...])
    h = _per_group(y, lambda a, sh, sc: a * (1.0 + sc) + sh, sh_ref[...], sc_ref[...])
    hb = h.astype(BF16)

    def proj(s):
        return jnp.dot(hb, w_ref[:, s * GROUP_W:(s + 1) * GROUP_W], preferred_element_type=F32)

    def head_rms(z, g):
        zz = (z * z).astype(BF16)
        half = GROUP_W // 2
        ss = jnp.concatenate(
            [jnp.dot(zz[:, :half], bd_ref[...], preferred_element_type=F32),
             jnp.dot(zz[:, half:], bd_ref[...], preferred_element_type=F32)], axis=1)
        return z * lax.rsqrt(ss * (1.0 / HEAD_DIM_ATT) + NORM_EPS) * g

    cos = cos_ref[...]
    sin = sin_ref[...]

    def rot(z):
        outs = []
        for hh in range(N_HEADS_RET):
            zh = z[:, hh * HEAD_DIM_RET:(hh + 1) * HEAD_DIM_RET]
            outs.append(zh * cos + pltpu.roll(zh, HEAD_DIM_RET // 2, axis=1) * sin)
        return jnp.concatenate(outs, axis=1)

    qa_ref[...] = head_rms(proj(0), gq_ref[...]).astype(BF16)
    ka = head_rms(proj(1), gk_ref[...])
    ka_ref[...] = ka.T.astype(BF16)
    va = proj(2)
    va_ref[...] = va.astype(BF16)

    @pl.when(is_p)
    def _():
        kpt_ref[...] = ka
        vpt_ref[...] = va

    @pl.when(jnp.logical_not(is_p))
    def _():
        kst_ref[...] = ka
        vst_ref[...] = va

    qb_ref[...] = rot(proj(3)).astype(BF16)
    kb_ref[...] = (rot(proj(4)) * (HEAD_DIM_RET ** -0.5)).astype(BF16)
    vb_ref[...] = proj(5).astype(BF16)
    gb_ref[...] = proj(6).astype(BF16)


def _inproj(xp, xs, shift, scale, g_norm, w_in_b, bd, gq8, gk8, cos_t, sin_t, nb, tps):
    rp = xp.shape[0]
    ntp = rp // TM
    nt = ntp + 1
    r = rp + TM
    row = lambda i: (i, 0)
    full = lambda i: (0, 0)
    tab = lambda i: (jnp.where(i < ntp, i % tps, tps), 0)
    act = jax.ShapeDtypeStruct((r, GROUP_W), BF16)
    return pl.pallas_call(
        functools.partial(_inproj_kernel, ntp),
        grid=(nt,),
        in_specs=[pl.BlockSpec((TM, D_MODEL), lambda i: (jnp.minimum(i, ntp - 1), 0)),
                  pl.BlockSpec((TM, D_MODEL), full),
                  pl.BlockSpec((GROUPS_PER_TILE, D_MODEL), _mod_row(ntp, tps, nb)),
                  pl.BlockSpec((GROUPS_PER_TILE, D_MODEL), _mod_row(ntp, tps, nb)),
                  pl.BlockSpec((1, D_MODEL), full),
                  pl.BlockSpec((D_MODEL, N_SLOTS * GROUP_W), full),
                  pl.BlockSpec((GROUP_W // 2, GROUP_W // 2), full),
                  pl.BlockSpec((1, GROUP_W), full),
                  pl.BlockSpec((1, GROUP_W), full),
                  pl.BlockSpec((TM, HEAD_DIM_RET), tab),
                  pl.BlockSpec((TM, HEAD_DIM_RET), tab)],
        out_specs=[pl.BlockSpec((TM, GROUP_W), row), pl.BlockSpec((GROUP_W, TM), lambda i: (0, i))]
        + [pl.BlockSpec((TM, GROUP_W), row)] * 5 + [
            pl.BlockSpec((TM, GROUP_W), lambda i: (jnp.minimum(i // tps, nb - 1), 0)),
            pl.BlockSpec((TM, GROUP_W), lambda i: (jnp.minimum(i // tps, nb - 1), 0)),
            pl.BlockSpec((TM, GROUP_W), full),
            pl.BlockSpec((TM, GROUP_W), full)],
        out_shape=[act, jax.ShapeDtypeStruct((GROUP_W, r), BF16)] + [act] * 5
        + [jax.ShapeDtypeStruct((nb * TM, GROUP_W), F32)] * 2
        + [jax.ShapeDtypeStruct((TM, GROUP_W), F32)] * 2,
        compiler_params=_cparams(("arbitrary",), VMEM_LIMIT),
        name="inproj",
    )(xp, xs, shift, scale, g_norm, w_in_b, bd, gq8, gk8, cos_t, sin_t)


def _attn_heads(q, k, v, bias_ref, k_feature_major, first_valid_col=None):
    qb_rows, kb_rows = q.shape[0], v.shape[0]
    if qb_rows == 4 * CHUNK:
        half_rows, span = qb_rows // 2, kb_rows - 2 * CHUNK
        parts = [(0, 0), (half_rows, 2 * CHUNK)]
    else:
        half_rows, span = qb_rows, kb_rows
        parts = [(0, 0)]

    def softmax_part(s_full, hh, half, r0, c0):
        rs = half * qb_rows + r0
        s = s_full[rs:rs + half_rows, c0:c0 + span] + bias_ref[hh, r0:r0 + half_rows, c0:c0 + span]
        if first_valid_col is not None:
            col = lax.broadcasted_iota(I32, (half_rows, span), 1) + c0
            s = jnp.where(col >= first_valid_col, s, NEG_INF)
        m = jnp.max(s, axis=-1, keepdims=True)
        e = jnp.exp2(s - m)
        l = jnp.sum(e, axis=-1, keepdims=True)
        pad = [jnp.zeros((half_rows, c0), BF16)] if c0 else []
        pad_r = [jnp.zeros((half_rows, kb_rows - span - c0), BF16)] if kb_rows - span - c0 else []
        return jnp.concatenate(pad + [e.astype(BF16)] + pad_r, axis=1), l

    pair_w = 2 * HEAD_DIM_ATT
    low = lax.broadcasted_iota(I32, (1, pair_w), 1) < HEAD_DIM_ATT
    outs = []
    for pp in range(N_HEADS_ATT // 2):
        ps = slice(pp * pair_w, (pp + 1) * pair_w)
        q2, v2 = q[:, ps], v[:, ps]
        k2 = k[ps, :] if k_feature_major else k[:, ps]
        zero = jnp.zeros_like(q2)
        qs = jnp.concatenate([jnp.where(low, q2, zero), jnp.where(low, zero, q2)], axis=0)
        if k_feature_major:
            s = jnp.dot(qs, k2, preferred_element_type=F32)
        else:
            s = lax.dot_general(qs, k2, (((1,), (1,)), ((), ())), preferred_element_type=F32)
        es, ls = zip(*[softmax_part(s, 2 * pp + half, half, r0, c0)
                       for half in range(2) for r0, c0 in parts])
        o = jnp.dot(jnp.concatenate(es, axis=0), v2, preferred_element_type=F32) / jnp.concatenate(ls, axis=0)
        outs.append(jnp.where(low, o[:qb_rows], o[qb_rows:]))
    return jnp.concatenate(outs, axis=1)


def _fill_band_bias(rev_ref, bias_scr):
    _, qb_rows, kb_rows = bias_scr.shape
    width = rev_ref.shape[1]
    q = lax.broadcasted_iota(I32, (qb_rows, kb_rows), 0)
    k = lax.broadcasted_iota(I32, (qb_rows, kb_rows), 1)
    qc = q >> 6
    kc = (k - ATT_WINDOW) >> 6
    band = (kc >= qc - ATT_WINDOW // CHUNK) & (kc <= qc)
    for hh in range(N_HEADS_ATT):
        rows = jnp.broadcast_to(rev_ref[hh:hh + 1, :], (qb_rows, width))
        toep = pltpu.roll(rows, width - MAX_REL, 1, stride=1, stride_axis=0)
        bias_scr[hh] = jnp.where(band, toep[:, :kb_rows] * LOG2_E, NEG_INF)


def _attn_sample_kernel(q_ref, kn_ref, vn_ref, kc_ref, vc_ref, rev_ref, o_ref, bias_scr):
    @pl.when(pl.program_id(0) == 0)
    def _():
        _fill_band_bias(rev_ref, bias_scr)

    k = jnp.concatenate([kc_ref[0], kn_ref[...].astype(BF16)], axis=0)
    v = jnp.concatenate([vc_ref[0], vn_ref[...]], axis=0)
    o_ref[...] = _attn_heads(q_ref[...], k, v, bias_scr, False).astype(BF16)


def _attn_sample(qa, ks_new, va, kc, vc, rev, rp):
    ndb = kc.shape[0]
    base = rp // CHUNK
    spec = pl.BlockSpec((CHUNK, GROUP_W), lambda b: (base + b, 0))
    cspec = pl.BlockSpec((1, ATT_WINDOW, GROUP_W), lambda b: (b, 0, 0))
    return pl.pallas_call(
        _attn_sample_kernel,
        grid=(ndb,),
        in_specs=[spec, pl.BlockSpec((CHUNK, GROUP_W), lambda b: (b, 0)), spec, cspec, cspec,
                  pl.BlockSpec(rev.shape, lambda b: (0, 0))],
        out_specs=pl.BlockSpec((CHUNK, GROUP_W), lambda b: (b, 0)),
        out_shape=jax.ShapeDtypeStruct((ndb * CHUNK, GROUP_W), BF16),
        scratch_shapes=[pltpu.VMEM((N_HEADS_ATT, CHUNK, ATT_WINDOW + CHUNK), F32)],
        compiler_params=_cparams(("arbitrary",), VMEM_LIMIT),
        name="attn_sample",
    )(qa, ks_new, va, kc, vc, rev)


def _ret_chunk(state_decay, q_ref, k_ref, v_ref, g_ref, dm_ref, xi_ref, zeta_ref, gro_ref, o_ref, s_scr):
    outs = []
    for hh in range(N_HEADS_RET):
        hs = slice(hh * HEAD_DIM_RET, (hh + 1) * HEAD_DIM_RET)
        q = q_ref[:, hs]
        k = k_ref[:, hs]
        v = v_ref[:, hs]
        st = s_scr[hh]
        sc = lax.dot_general(q, k, (((1,), (1,)), ((), ())), preferred_element_type=F32) * dm_ref[hh]
        inner = jnp.dot(sc.astype(BF16), v, preferred_element_type=F32)
        cross = jnp.dot(q, st.astype(BF16), preferred_element_type=F32) * xi_ref[:, hs]
        o = inner + cross
        kz = k.astype(F32) * zeta_ref[:, hs]
        s_scr[hh] = state_decay[hh] * st + jnp.dot(kz.T.astype(BF16), v, preferred_element_type=F32)
        mu = jnp.mean(o, axis=-1, keepdims=True)
        oc = o - mu
        var = jnp.mean(oc * oc, axis=-1, keepdims=True)
        outs.append(oc * lax.rsqrt(var + NORM_EPS))
    y = jnp.concatenate(outs, axis=1) * gro_ref[...]
    g = g_ref[...].astype(F32)
    o_ref[...] = (g * jax.nn.sigmoid(g) * y).astype(BF16)


def _ret_kernel(state_decay, q_ref, k_ref, v_ref, g_ref, s0_ref, dm_ref, xi_ref, zeta_ref,
                gro_ref, o_ref, sn_ref, s_scr):
    j = pl.program_id(1)

    @pl.when(j == 0)
    def _():
        s_scr[...] = s0_ref[0]

    _ret_chunk(state_decay, q_ref, k_ref, v_ref, g_ref, dm_ref, xi_ref, zeta_ref, gro_ref, o_ref, s_scr)

    @pl.when(j == pl.num_programs(1) - 1)
    def _():
        sn_ref[0] = s_scr[...]


def _mix_prompt_kernel(state_decay, q_ref, k0_ref, k1_ref, k2_ref, v0_ref, v1_ref, v2_ref, rev_ref,
                       rq_ref, rk_ref, rv_ref, rg_ref, s0_ref, dm_ref, xi_ref, zeta_ref, gro_ref,
                       att_ref, ret_ref, sn_ref, bias_scr, s_scr):
    j = pl.program_id(1)

    @pl.when((pl.program_id(0) == 0) & (j == 0))
    def _():
        _fill_band_bias(rev_ref, bias_scr)

    @pl.when(j == 0)
    def _():
        s_scr[...] = s0_ref[0]

    k = jnp.concatenate([k0_ref[...], k1_ref[...], k2_ref[...]], axis=1)
    v = jnp.concatenate([v0_ref[...], v1_ref[...], v2_ref[...]], axis=0)

    def block(first_valid_col):
        att_ref[...] = _attn_heads(q_ref[...], k, v, bias_scr, True, first_valid_col).astype(BF16)
        _ret_chunk(state_decay, rq_ref, rk_ref, rv_ref, rg_ref, dm_ref, xi_ref, zeta_ref, gro_ref,
                   ret_ref, s_scr)

    @pl.when(j >= 2)
    def _():
        block(None)

    @pl.when(j < 2)
    def _():
        block((2 - j) * ATT_QB)

    @pl.when(j == pl.num_programs(1) - 1)
    def _():
        sn_ref[0] = s_scr[...]


def _mix_prompt(qa, ka_t, va, rev, qb, kb, vb, gb, s0, g_ro, nb, seq):
    assert ATT_QB == RET_CB
    r = nb * seq
    nq = seq // ATT_QB
    dm, xi, zeta, state_decay = _ret_consts(RET_CB)
    blk = lambda back: (lambda b, j: (b * nq + jnp.maximum(j - back, 0), 0))
    spec = lambda back: pl.BlockSpec((ATT_QB, GROUP_W), blk(back))
    tspec = lambda back: pl.BlockSpec((GROUP_W, ATT_QB), lambda b, j: (0, b * nq + jnp.maximum(j - back, 0)))
    sspec = pl.BlockSpec((1, N_HEADS_RET, HEAD_DIM_RET, HEAD_DIM_RET), lambda b, j: (b, 0, 0, 0))
    full2 = lambda b, j: (0, 0)
    out = jax.ShapeDtypeStruct((r, GROUP_W), BF16)
    return pl.pallas_call(
        functools.partial(_mix_prompt_kernel, state_decay),
        grid=(nb, nq),
        in_specs=[spec(0), tspec(2), tspec(1), tspec(0), spec(2), spec(1), spec(0),
                  pl.BlockSpec(rev.shape, full2),
                  spec(0), spec(0), spec(0), spec(0), sspec,
                  pl.BlockSpec(dm.shape, lambda b, j: (0, 0, 0)),
                  pl.BlockSpec(xi.shape, full2), pl.BlockSpec(zeta.shape, full2),
                  pl.BlockSpec((1, GROUP_W), full2)],
        out_specs=[spec(0), spec(0), sspec],
        out_shape=[out, out, jax.ShapeDtypeStruct(s0.shape, F32)],
        scratch_shapes=[pltpu.VMEM((N_HEADS_ATT, ATT_QB, ATT_WINDOW + ATT_QB), F32),
                        pltpu.VMEM((N_HEADS_RET, HEAD_DIM_RET, HEAD_DIM_RET), F32)],
        compiler_params=_cparams(("arbitrary", "arbitrary"), VMEM_LIMIT),
        name="mix_prompt",
    )(qa, ka_t, ka_t, ka_t, va, va, va, rev, qb, kb, vb, gb, s0, dm, xi, zeta, g_ro)


def _ret_consts(cb):
    log_g = np.log1p(-np.exp2(-RET_DECAY_OFFSET - np.arange(N_HEADS_RET, dtype=np.float64)))
    n = np.arange(cb, dtype=np.float64)
    diff = n[:, None] - n[None, :]
    dm = np.where(diff[None] >= 0, np.exp(np.maximum(diff, 0.0)[None] * log_g[:, None, None]), 0.0)
    xi = np.exp((n + 1.0)[:, None] * log_g[None, :])
    zeta = np.exp((cb - 1.0 - n)[:, None] * log_g[None, :])
    rep = lambda a: np.repeat(a, HEAD_DIM_RET, axis=1)
    state_decay = tuple(float(v) for v in np.exp(cb * log_g))
    return (jnp.asarray(dm, F32), jnp.asarray(rep(xi), F32), jnp.asarray(rep(zeta), F32), state_decay)


def _ret(qb, kb, vb, gb, s0, g_ro, cb, row0, nb, nc, name):
    dm, xi, zeta, state_decay = _ret_consts(cb)
    base = row0 // cb
    spec = pl.BlockSpec((cb, GROUP_W), lambda b, j: (base + b * nc + j, 0))
    sspec = pl.BlockSpec((1, N_HEADS_RET, HEAD_DIM_RET, HEAD_DIM_RET), lambda b, j: (b, 0, 0, 0))
    full2 = lambda b, j: (0, 0)
    return pl.pallas_call(
        functools.partial(_ret_kernel, state_decay),
        grid=(nb, nc),
        in_specs=[spec, spec, spec, spec, sspec,
                  pl.BlockSpec(dm.shape, lambda b, j: (0, 0, 0)),
                  pl.BlockSpec(xi.shape, full2), pl.BlockSpec(zeta.shape, full2),
                  pl.BlockSpec((1, GROUP_W), full2)],
        out_specs=[pl.BlockSpec((cb, GROUP_W), lambda b, j: (b * nc + j, 0)), sspec],
        out_shape=[jax.ShapeDtypeStruct((nb * nc * cb, GROUP_W), BF16),
                   jax.ShapeDtypeStruct(s0.shape, F32)],
        scratch_shapes=[pltpu.VMEM((N_HEADS_RET, HEAD_DIM_RET, HEAD_DIM_RET), F32)],
        compiler_params=_cparams(("arbitrary", "arbitrary"), VMEM_LIMIT),
        name=name,
    )(qb, kb, vb, gb, s0, dm, xi, zeta, g_ro)


def _outproj_kernel(ntp, attp_ref, atts_ref, retp_ref, rets_ref, xp_ref, xs_ref, gm_ref, shf_ref, scf_ref,
                    gn_ref, wo_ref, wr_ref, br_ref, upper_ref, lower_ref,
                    x1_ref, h2_ref, slot_ref, cols_ref, cnt_ref):
    i = pl.program_id(0)
    is_p = i < ntp
    x = jnp.where(is_p, xp_ref[...], xs_ref[...])
    att = jnp.where(is_p, attp_ref[...], atts_ref[...])
    ret = jnp.where(is_p, retp_ref[...], rets_ref[...])
    mix = (jnp.dot(att, wo_ref[:GROUP_W, :], preferred_element_type=F32)
           + jnp.dot(ret, wo_ref[GROUP_W:, :], preferred_element_type=F32))
    x1 = _per_group(mix, lambda a, gm: a * gm, gm_ref[...]) + x
    x1_ref[...] = x1
    y = _rms_rows(x1, gn_ref[...])
    h2 = _per_group(y, lambda a, sh, sc: a * (1.0 + sc) + sh, shf_ref[...], scf_ref[...])
    h2b = h2.astype(BF16)
    h2_ref[...] = h2b

    logits = lax.dot_general(wr_ref[...], h2b, (((1,), (1,)), ((), ())),
                             preferred_element_type=F32) + br_ref[...]
    eidx = lax.broadcasted_iota(I32, logits.shape, 0).astype(F32)
    work = logits
    sel, top = [], []
    for _ in range(TOP_K):
        m = jnp.max(work, axis=0, keepdims=True)
        idx = jnp.min(jnp.where(work == m, eidx, float(N_EXPERTS)), axis=0, keepdims=True)
        hit = eidx == idx
        sel.append(hit)
        top.append(m)
        work = jnp.where(hit, -jnp.inf, work)
    ex = [jnp.exp(t - top[0]) for t in top]
    den = ex[0] + ex[1] + ex[2] + ex[3]
    gates = [e / den for e in ex]

    multi = (sel[0] | sel[1] | sel[2] | sel[3])
    multi_f = jnp.where(multi, 1.0, 0.0)
    rank = jnp.dot(multi_f.astype(BF16), upper_ref[...], preferred_element_type=F32)
    cnt = jnp.sum(multi_f, axis=1, keepdims=True)
    cnt_pad = jnp.floor((cnt + (SEG_ALIGN - 1.0)) * (1.0 / SEG_ALIGN)) * SEG_ALIGN
    cnt_pad_b = jnp.broadcast_to(cnt_pad, (N_EXPERTS, 128))
    seg_off = jnp.dot(lower_ref[...], cnt_pad_b.astype(BF16), preferred_element_type=F32)[:, :1]
    pos = seg_off + rank
    slots = [jnp.sum(jnp.where(s, pos, 0.0), axis=0, keepdims=True) for s in sel]
    slot_rows = jnp.concatenate(slots, axis=0)
    gate_rows = jnp.concatenate(gates, axis=0)
    slot_ref[0] = slot_rows.astype(I32)
    cnt_ref[0] = cnt_pad_b.astype(I32)
    both = jnp.concatenate([slot_rows, gate_rows, jnp.zeros((128 - 2 * TOP_K, TM), F32)], axis=0)
    cols_ref[0] = both.T


def _outproj(att_p, att_s, ret_p, ret_s, xp, xs, gate_m, shift_f, scale_f, g_norm, w_out_b, wr_t, br,
             upper, lower, nb, tps):
    rp = xp.shape[0]
    ntp = rp // TM
    nt = ntp + 1
    r = rp + TM
    row = lambda i: (i, 0)
    row3 = lambda i: (i, 0, 0)
    full = lambda i: (0, 0)
    prow = lambda i: (jnp.minimum(i, ntp - 1), 0)
    mod = pl.BlockSpec((GROUPS_PER_TILE, D_MODEL), _mod_row(ntp, tps, nb))
    return pl.pallas_call(
        functools.partial(_outproj_kernel, ntp),
        grid=(nt,),
        in_specs=[pl.BlockSpec((TM, GROUP_W), prow), pl.BlockSpec((TM, GROUP_W), full),
                  pl.BlockSpec((TM, GROUP_W), prow), pl.BlockSpec((TM, GROUP_W), full),
                  pl.BlockSpec((TM, D_MODEL), prow),
                  pl.BlockSpec((TM, D_MODEL), full),
                  mod, mod, mod,
                  pl.BlockSpec((1, D_MODEL), full),
                  pl.BlockSpec((D_MODEL, D_MODEL), full),
                  pl.BlockSpec((N_EXPERTS, D_MODEL), full),
                  pl.BlockSpec((N_EXPERTS, 1), full),
                  pl.BlockSpec((TM, TM), full),
                  pl.BlockSpec((N_EXPERTS, N_EXPERTS), full)],
        out_specs=[pl.BlockSpec((TM, D_MODEL), row), pl.BlockSpec((TM, D_MODEL), row),
                   pl.BlockSpec((1, TOP_K, TM), row3),
                   pl.BlockSpec((1, TM, 128), row3), pl.BlockSpec((1, N_EXPERTS, 128), row3)],
        out_shape=[jax.ShapeDtypeStruct((r, D_MODEL), F32), jax.ShapeDtypeStruct((r, D_MODEL), BF16),
                   jax.ShapeDtypeStruct((nt, TOP_K, TM), I32),
                   jax.ShapeDtypeStruct((nt, TM, 128), F32), jax.ShapeDtypeStruct((nt, N_EXPERTS, 128), I32)],
        compiler_params=_cparams(("arbitrary",), VMEM_LIMIT),
        name="outproj",
    )(att_p, att_s, ret_p, ret_s, xp, xs, gate_m, shift_f, scale_f, g_norm, w_out_b, wr_t, br, upper, lower)


def _rows_copy(n, src_rows, dst_rows, sem):
    size = pl.multiple_of(n, SEG_ALIGN)
    return pltpu.make_async_copy(src_rows(size), dst_rows(size), sem)


def _start_segments(t, cnt_ref, off_ref, base_ref, local_rows, sorted_rows, sem, to_sorted):
    def body(e, c):
        n = cnt_ref[t * N_EXPERTS + e]
        off = pl.multiple_of(off_ref[t * N_EXPERTS + e], SEG_ALIGN)
        base = pl.multiple_of(base_ref[t * N_EXPERTS + e], SEG_ALIGN)
        local = lambda z: local_rows(off, z)
        remote = lambda z: sorted_rows(base, z)

        @pl.when(n > 0)
        def _():
            (_rows_copy(n, local, remote, sem) if to_sorted else _rows_copy(n, remote, local, sem)).start()
        return c
    lax.fori_loop(0, N_EXPERTS, body, 0)


def _dispatch_kernel(nt, n_blocks, off_ref, cnt_ref, base_ref, tot_ref, tail0_ref, tailn_ref, na_ref,
                     h2_ref, slot_ref, slotn_ref, xb_ref, xs_scr, hot_scr, zero_scr, sems, tail_sem):
    i = pl.program_id(0)
    cur = i % 2
    sorted_rows = lambda r, z: xb_ref.at[pl.ds(r, z), :]

    def start_tile(t, buf):
        _start_segments(t, cnt_ref, off_ref, base_ref, lambda r, z: xs_scr.at[buf, pl.ds(r, z), :],
                        sorted_rows, sems.at[buf], True)

    def wait_tile(t, buf):
        _rows_copy(tot_ref[t], lambda z: xs_scr.at[buf, pl.ds(0, z), :], lambda z: sorted_rows(0, z),
                   sems.at[buf]).wait()

    def tail_copies(wait):
        def body(e, c):
            base = pl.multiple_of(tail0_ref[e], SEG_ALIGN)

            @pl.when(tailn_ref[e] > 0)
            def _():
                cp = _rows_copy(tailn_ref[e], lambda z: zero_scr.at[pl.ds(0, z), :],
                                lambda z: sorted_rows(base, z), tail_sem)
                cp.wait() if wait else cp.start()
            return c
        lax.fori_loop(0, N_EXPERTS, body, 0)

        def unused(j, c):
            cp = pltpu.make_async_copy(zero_scr, sorted_rows(pl.multiple_of(j * BM, BM), BM), tail_sem)
            cp.wait() if wait else cp.start()
            return c
        lax.fori_loop(na_ref[0], n_blocks, unused, 0)

    @pl.when(i >= 2)
    def _():
        wait_tile(i - 2, cur)

    def onehot(slot):
        srow = lax.broadcasted_iota(I32, (CAP, TM), 0)
        hit = (srow == slot[0:1]) | (srow == slot[1:2]) | (srow == slot[2:3]) | (srow == slot[3:4])
        return jnp.where(hit, 1.0, 0.0).astype(BF16)

    @pl.when(i == 0)
    def _():
        hot_scr[0] = onehot(slot_ref[0])

    xs_scr[cur] = jnp.dot(hot_scr[cur], h2_ref[...], preferred_element_type=F32).astype(BF16)
    hot_scr[1 - cur] = onehot(slotn_ref[0])
    start_tile(i, cur)

    @pl.when(i == nt - 1)
    def _():
        zero_scr[...] = jnp.zeros_like(zero_scr)
        tail_copies(False)
        if nt >= 2:
            wait_tile(i - 1, 1 - cur)
        wait_tile(i, cur)
        tail_copies(True)


def _dispatch(h2, slot, off, cnt, base, tot, tail0, tailn, n_act, n_blocks):
    nt = slot.shape[0]
    n_rows = n_blocks * BM
    grid_spec = pltpu.PrefetchScalarGridSpec(
        num_scalar_prefetch=7,
        grid=(nt,),
        in_specs=[pl.BlockSpec((TM, D_MODEL), lambda i, *_: (i, 0)),
                  pl.BlockSpec((1, TOP_K, TM), lambda i, *_: (i, 0, 0)),
                  pl.BlockSpec((1, TOP_K, TM), lambda i, *_: (jnp.minimum(i + 1, nt - 1), 0, 0))],
        out_specs=pl.BlockSpec(memory_space=pl.ANY),
        scratch_shapes=[pltpu.VMEM((2, CAP, D_MODEL), BF16),
                        pltpu.VMEM((2, CAP, TM), BF16),
                        pltpu.VMEM((BM, D_MODEL), BF16),
                        pltpu.SemaphoreType.DMA((2,)),
                        pltpu.SemaphoreType.DMA(())],
    )
    return pl.pallas_call(
        functools.partial(_dispatch_kernel, nt, n_blocks),
        grid_spec=grid_spec,
        out_shape=jax.ShapeDtypeStruct((n_rows, D_MODEL), BF16),
        compiler_params=_cparams(("arbitrary",), VMEM_LIMIT),
        name="dispatch",
    )(off, cnt, base, tot, tail0, tailn, n_act, h2, slot, slot)


def _experts_kernel(be_ref, bi_ref, nx_ref, na_ref, x_ref, wu_hbm, bu_ref, wd_hbm, bd_ref, y_ref,
                    wu_stage, wd_stage, wu_scr, wd_scr, sems):
    j = pl.program_id(0)

    def weight_copies(e):
        return (pltpu.make_async_copy(wu_hbm.at[e], wu_stage, sems.at[0]),
                pltpu.make_async_copy(wd_hbm.at[e], wd_stage, sems.at[1]))

    @pl.when(j < na_ref[0])
    def _():
        e = be_ref[j]
        prev = be_ref[jnp.maximum(j - 1, 0)]

        @pl.when(j == 0)
        def _():
            for cp in weight_copies(e):
                cp.start()

        @pl.when((j == 0) | (e != prev))
        def _():
            for cp in weight_copies(e):
                cp.wait()
            wu_scr[...] = wu_stage[...].astype(BF16)
            wd_scr[...] = wd_stage[...].astype(BF16)

            @pl.when(nx_ref[j] != e)
            def _():
                for cp in weight_copies(nx_ref[j]):
                    cp.start()

        u = jnp.dot(x_ref[...], wu_scr[...], preferred_element_type=F32) + bu_ref[0]
        glu = jnp.minimum(u[:, :D_FF], SWIGLU_LIMIT)
        lin = jnp.clip(u[:, D_FF:], -SWIGLU_LIMIT, SWIGLU_LIMIT)
        act = glu * jax.nn.sigmoid(SWIGLU_ALPHA * glu) * (lin + 1.0)
        y = jnp.dot(act.astype(BF16), wd_scr[...], preferred_element_type=F32) + bd_ref[0]
        y_ref[...] = y.astype(BF16)

    @pl.when(j >= na_ref[0])
    def _():
        y_ref[...] = jnp.zeros_like(y_ref)


def _experts(xb, blk_e, blk_i, blk_nx, n_act, w_up, b_up, w_down, b_down):
    n_rows = xb.shape[0]
    nblk = n_rows // BM
    grid_spec = pltpu.PrefetchScalarGridSpec(
        num_scalar_prefetch=4,
        grid=(nblk,),
        in_specs=[pl.BlockSpec((BM, D_MODEL), lambda j, be, bi, nx, na: (bi[j], 0)),
                  pl.BlockSpec(memory_space=pl.ANY),
                  pl.BlockSpec((1, 1, 2 * D_FF), lambda j, be, bi, nx, na: (be[j], 0, 0)),
                  pl.BlockSpec(memory_space=pl.ANY),
                  pl.BlockSpec((1, 1, D_MODEL), lambda j, be, bi, nx, na: (be[j], 0, 0))],
        out_specs=pl.BlockSpec((BM, D_MODEL), lambda j, be, bi, nx, na: (j, 0)),
        scratch_shapes=[pltpu.VMEM((D_MODEL, 2 * D_FF), F32), pltpu.VMEM((D_FF, D_MODEL), F32),
                        pltpu.VMEM((D_MODEL, 2 * D_FF), BF16), pltpu.VMEM((D_FF, D_MODEL), BF16),
                        pltpu.SemaphoreType.DMA((2,))],
    )
    return pl.pallas_call(
        _experts_kernel,
        grid_spec=grid_spec,
        out_shape=jax.ShapeDtypeStruct((n_rows, D_MODEL), BF16),
        compiler_params=_cparams(("arbitrary",), VMEM_LIMIT),
        name="experts",
    )(blk_e, blk_i, blk_nx, n_act, xb, w_up, b_up.reshape(N_EXPERTS, 1, 2 * D_FF), w_down,
      b_down.reshape(N_EXPERTS, 1, D_MODEL))


def _combine_kernel(nt, ntp, off_ref, cnt_ref, base_ref, tot_ref, yb_ref, cols_ref, x1_ref, gf_ref,
                    op_ref, os_ref, ys_scr, sems):
    i = pl.program_id(0)
    cur = i % 2

    sorted_rows = lambda r, z: yb_ref.at[pl.ds(r, z), :]

    def start_tile(t, buf):
        _start_segments(t, cnt_ref, off_ref, base_ref, lambda r, z: ys_scr.at[buf, pl.ds(r, z), :],
                        sorted_rows, sems.at[buf], False)

    @pl.when(i == 0)
    def _():
        ys_scr[...] = jnp.zeros_like(ys_scr)
        start_tile(0, 0)

    @pl.when(i + 1 < nt)
    def _():
        start_tile(i + 1, 1 - cur)

    _rows_copy(tot_ref[i], lambda z: sorted_rows(0, z), lambda z: ys_scr.at[cur, pl.ds(0, z), :],
               sems.at[cur]).wait()

    cols = cols_ref[0]
    lane = lax.broadcasted_iota(I32, (TM, CAP), 1)
    w = jnp.zeros((TM, CAP), F32)
    for k in range(TOP_K):
        sk = cols[:, k:k + 1].astype(I32)
        gk = cols[:, TOP_K + k:TOP_K + k + 1]
        w = jnp.where(lane == sk, gk, w)
    y = jnp.dot(w.astype(BF16), ys_scr[cur], preferred_element_type=F32)
    out = x1_ref[...] + _per_group(y, lambda a, gf: a * gf, gf_ref[...])

    @pl.when(i < ntp)
    def _():
        op_ref[...] = out

    @pl.when(i >= ntp)
    def _():
        os_ref[...] = out


def _combine(yb, cols, x1, gate_f, off, cnt, base, tot, ntp, nb, tps):
    nt = cols.shape[0]
    grid_spec = pltpu.PrefetchScalarGridSpec(
        num_scalar_prefetch=4,
        grid=(nt,),
        in_specs=[pl.BlockSpec(memory_space=pl.ANY),
                  pl.BlockSpec((1, TM, 128), lambda i, *_: (i, 0, 0)),
                  pl.BlockSpec((TM, D_MODEL), lambda i, *_: (i, 0)),
                  pl.BlockSpec((GROUPS_PER_TILE, D_MODEL), _mod_row(ntp, tps, nb))],
        out_specs=[pl.BlockSpec((TM, D_MODEL), lambda i, *_: (jnp.minimum(i, ntp - 1), 0)),
                   pl.BlockSpec((TM, D_MODEL), lambda i, *_: (0, 0))],
        scratch_shapes=[pltpu.VMEM((2, CAP, D_MODEL), BF16), pltpu.SemaphoreType.DMA((2,))],
    )
    return pl.pallas_call(
        functools.partial(_combine_kernel, nt, ntp),
        grid_spec=grid_spec,
        out_shape=[jax.ShapeDtypeStruct((ntp * TM, D_MODEL), F32),
                   jax.ShapeDtypeStruct((TM, D_MODEL), F32)],
        compiler_params=_cparams(("arbitrary",), VMEM_LIMIT),
        name="combine",
    )(off, cnt, base, tot, yb, cols, x1, gate_f)


def _rotary_tables(seq, dec_batch, dec_seq):
    half = HEAD_DIM_RET // 2
    inv = ROPE_BASE ** (-np.arange(half, dtype=np.float64) / half)
    pos = np.concatenate([np.arange(seq), np.tile(PAST_LEN + np.arange(dec_seq), dec_batch)])
    ang = pos.astype(np.float64)[:, None] * inv[None, :]
    cos = np.concatenate([np.cos(ang), np.cos(ang)], axis=1)
    sin = np.concatenate([-np.sin(ang), np.sin(ang)], axis=1)
    return jnp.asarray(cos, F32), jnp.asarray(sin, F32)


def _rel_bias_reversed(rel_bias):
    heads = rel_bias.shape[0]
    ext = jnp.concatenate([rel_bias[:, 1:], jnp.broadcast_to(rel_bias[:, -1:], (heads, 2 * MAX_REL))], axis=1)
    return ext[:, ::-1].astype(F32)


def _group_mods(m, nb, ndb):
    assert ndb == GROUPS_PER_TILE
    mp = jnp.broadcast_to(m[:nb, None], (nb, GROUPS_PER_TILE) + m.shape[1:])
    allm = jnp.concatenate([mp.reshape((nb * GROUPS_PER_TILE,) + m.shape[1:]), m[nb:]], axis=0)
    return jnp.transpose(allm, (1, 0, 2))


def _routing_tables(cnt, n_blocks):
    nt = cnt.shape[0]
    off = jnp.cumsum(cnt, axis=1) - cnt
    rows_e = jnp.sum(cnt, axis=0)
    nblk_e = (rows_e + BM - 1) // BM
    blk_end = jnp.cumsum(nblk_e)
    start_e = (blk_end - nblk_e) * BM
    base = start_e[None, :] + jnp.cumsum(cnt, axis=0) - cnt
    n_act = blk_end[-1]
    j = jnp.minimum(jnp.arange(n_blocks), n_act - 1)
    blk_e = jnp.minimum(jnp.sum(blk_end[None, :] <= j[:, None], axis=1), N_EXPERTS - 1)
    later = jnp.where(blk_e[None, :] > blk_e[:, None], blk_e[None, :], N_EXPERTS)
    blk_nx = jnp.min(later, axis=1)
    blk_nx = jnp.where(blk_nx == N_EXPERTS, blk_e, blk_nx)
    tail0 = start_e + rows_e
    tailn = nblk_e * BM - rows_e
    i32 = lambda a: a.astype(I32)
    return (i32(off.reshape(nt * N_EXPERTS)), i32(cnt.reshape(nt * N_EXPERTS)),
            i32(base.reshape(nt * N_EXPERTS)), i32(jnp.sum(cnt, axis=1)), i32(tail0), i32(tailn),
            i32(blk_e), i32(j), i32(blk_nx), i32(n_act.reshape(1)))


def kernel(x_prompt, x_sample, c_prompt, c_sample, cache_att_k, cache_att_v, state_ret, w_ada, b_ada,
           g_norm_mix, g_norm_ffn, w_in, g_q, g_k, rel_bias, g_ret_out, w_out, w_router, b_router,
           w_up, b_up, w_down, b_down):
    nb, seq, d = x_prompt.shape
    ndb, dseq, _ = x_sample.shape
    assert d == D_MODEL and ndb * dseq == TM and dseq == CHUNK
    assert seq % TM == 0 and seq >= ATT_WINDOW and cache_att_k.shape[2] == ATT_WINDOW
    assert w_ada.shape[0] == 1
    rp = nb * seq
    ntp = rp // TM
    nt = ntp + 1
    tps = seq // TM

    xp = x_prompt.reshape(rp, d)
    xs = x_sample.reshape(TM, d)

    m = _ada(jnp.concatenate([c_prompt, c_sample], axis=0), w_ada[0], b_ada[0])
    mods = _group_mods(m.reshape(nb + ndb, N_ADA, d), nb, ndb)
    shift_m, scale_m, gate_m, shift_f, scale_f, gate_f = [mods[a] for a in range(N_ADA)]

    cos_t, sin_t = _rotary_tables(seq, ndb, dseq)
    bd = jnp.asarray(np.kron(np.eye(N_HEADS_ATT // 2), np.ones((HEAD_DIM_ATT, HEAD_DIM_ATT))), BF16)
    tile8 = lambda g: jnp.tile(g.astype(F32), N_HEADS_ATT).reshape(1, GROUP_W)
    (qa, ka_t, va, qb, kb, vb, gb, kp_tail, vp_tail, ks_new, vs_new) = _inproj(
        xp, xs, shift_m, scale_m, g_norm_mix[0].reshape(1, d), w_in[0].astype(BF16), bd,
        tile8(g_q[0]) * (HEAD_DIM_ATT ** -0.5 * LOG2_E), tile8(g_k[0]), cos_t, sin_t, nb, tps)

    rev = _rel_bias_reversed(rel_bias[0])
    g_ro = g_ret_out[0].astype(F32).reshape(1, GROUP_W)
    zero_state = jnp.zeros((nb, N_HEADS_RET, HEAD_DIM_RET, HEAD_DIM_RET), F32)
    att_p, ret_p, state_p = _mix_prompt(qa, ka_t, va, rev, qb, kb, vb, gb, zero_state, g_ro, nb, seq)
    att_s = _attn_sample(qa, ks_new, va,
                         cache_att_k[0].reshape(ndb, ATT_WINDOW, GROUP_W).astype(BF16),
                         cache_att_v[0].reshape(ndb, ATT_WINDOW, GROUP_W).astype(BF16),
                         rev, rp)

    ret_s, state_s = _ret(qb, kb, vb, gb, state_ret[0].astype(F32), g_ro, CHUNK, rp, ndb, 1, "ret_sample")

    upper = jnp.asarray(np.triu(np.ones((TM, TM)), 1), BF16)
    lower = jnp.asarray(np.tril(np.ones((N_EXPERTS, N_EXPERTS)), -1), BF16)
    x1, h2, slot, cols, cnt = _outproj(
        att_p, att_s, ret_p, ret_s, xp, xs, gate_m, shift_f, scale_f, g_norm_ffn[0].reshape(1, d),
        w_out[0].astype(BF16), w_router[0].T.astype(BF16), b_router[0].astype(F32).reshape(N_EXPERTS, 1),
        upper, lower, nb, tps)

    n_blocks = (TOP_K * (rp + TM) + nt * N_EXPERTS * (SEG_ALIGN - 1)) // BM + 1 + N_EXPERTS
    off, cntf, base, tot, tail0, tailn, blk_e, blk_i, blk_nx, n_act = _routing_tables(cnt[:, :, 0], n_blocks)
    xb = _dispatch(h2, slot, off, cntf, base, tot, tail0, tailn, n_act, n_blocks)
    yb = _experts(xb, blk_e, blk_i, blk_nx, n_act, w_up[0], b_up[0], w_down[0], b_down[0])
    out_p, out_s = _combine(yb, cols, x1, gate_f, off, cntf, base, tot, ntp, nb, tps)

    heads = (N_HEADS_ATT, HEAD_DIM_ATT)
    return (out_p.reshape(nb, seq, d), out_s.reshape(ndb, dseq, d),
            kp_tail.reshape(1, nb, ATT_WINDOW, *heads), vp_tail.reshape(1, nb, ATT_WINDOW, *heads),
            state_p[None],
            ks_new.reshape(1, ndb, dseq, *heads), vs_new.reshape(1, ndb, dseq, *heads),
            state_s[None])
```

```python
import functools

import numpy as np
import jax
import jax.numpy as jnp
from jax import lax
from jax.experimental import pallas as pl
from jax.experimental.pallas import tpu as pltpu

F32 = jnp.float32
BF16 = jnp.bfloat16
I32 = jnp.int32

D_MODEL = 1024
GROUP_W = 512
N_SLOTS = 7
N_HEADS_ATT = 8
HEAD_DIM_ATT = 64
N_HEADS_RET = 4
HEAD_DIM_RET = 128
CHUNK = 64
ATT_WINDOW = 512
MAX_REL = 256
PAST_LEN = 2048
RET_DECAY_OFFSET = 5.0
ROPE_BASE = 10000.0
N_EXPERTS = 32
TOP_K = 4
D_FF = 1024
SWIGLU_LIMIT = 7.0
SWIGLU_ALPHA = 1.702
N_ADA = 6
NORM_EPS = 1e-6
NEG_INF = -1e30
LOG2_E = 1.4426950408889634

TM = 512
GROUPS_PER_TILE = TM // CHUNK
ATT_QB = 256
RET_CB = 256
SEG_ALIGN = 16
CAP = TOP_K * TM + N_EXPERTS * SEG_ALIGN
BM = 512
UP_CHUNK = 256
VMEM_LIMIT = 56 * 1024 * 1024


def _cparams(sem, vmem=None):
    return pltpu.CompilerParams(dimension_semantics=sem, vmem_limit_bytes=vmem)


def _ada_kernel(c_ref, w_ref, b_ref, o_ref):
    c = c_ref[...]
    s = c * jax.nn.sigmoid(c)
    o_ref[...] = jnp.dot(s.astype(BF16), w_ref[...].astype(BF16),
                         preferred_element_type=F32) + b_ref[...]


def _ada(c_all, w_ada, b_ada):
    n, d = c_all.shape
    cols = w_ada.shape[1]
    tn = 1536
    return pl.pallas_call(
        _ada_kernel,
        grid=(cols // tn,),
        in_specs=[pl.BlockSpec((n, d), lambda j: (0, 0)),
                  pl.BlockSpec((d, tn), lambda j: (0, j)),
                  pl.BlockSpec((1, tn), lambda j: (0, j))],
        out_specs=pl.BlockSpec((n, tn), lambda j: (0, j)),
        out_shape=jax.ShapeDtypeStruct((n, cols), F32),
        compiler_params=_cparams(("arbitrary",), VMEM_LIMIT),
        name="ada",
    )(c_all, w_ada, b_ada.reshape(1, cols))


def _rms_rows(x, g):
    ms = jnp.mean(x * x, axis=-1, keepdims=True)
    return x * lax.rsqrt(ms + NORM_EPS) * g


def _mod_row(ntp, tps, nb):
    return lambda i, *_: (jnp.where(i < ntp, i // tps, nb), 0)


def _per_group(x, fn, *mods):
    x3 = x.reshape(GROUPS_PER_TILE, CHUNK, x.shape[-1])
    y3 = fn(x3, *[m[:, None, :] for m in mods])
    return y3.reshape(x.shape)


def _inproj_kernel(ntp, xp_ref, xs_ref, sh_ref, sc_ref, gn_ref, w_ref, bd_ref, gq_ref, gk_ref,
                   cos_ref, sin_ref,
                   qa_ref, ka_ref, va_ref, qb_ref, kb_ref, vb_ref, gb_ref,
                   kpt_ref, vpt_ref, kst_ref, vst_ref):
    i = pl.program_id(0)
    is_p = i < ntp
    x = jnp.where(is_p, xp_ref[...], xs_ref[...])
    y = _rms_rows(x, gn_ref[...])
    h = _per_group(y, lambda a, sh, sc: a * (1.0 + sc) + sh, sh_ref[...], sc_ref[...])
    hb = h.astype(BF16)

    def proj(s):
        return jnp.dot(hb, w_ref[:, s * GROUP_W:(s + 1) * GROUP_W], preferred_element_type=F32)

    def head_rms(z, g):
        zz = (z * z).astype(BF16)
        half = GROUP_W // 2
        ss = jnp.concatenate(
            [jnp.dot(zz[:, :half], bd_ref[...], preferred_element_type=F32),
             jnp.dot(zz[:, half:], bd_ref[...], preferred_element_type=F32)], axis=1)
        return z * lax.rsqrt(ss * (1.0 / HEAD_DIM_ATT) + NORM_EPS) * g

    cos = cos_ref[...]
    sin = sin_ref[...]

    def rot(z):
        outs = []
        for hh in range(N_HEADS_RET):
            zh = z[:, hh * HEAD_DIM_RET:(hh + 1) * HEAD_DIM_RET]
            outs.append(zh * cos + pltpu.roll(zh, HEAD_DIM_RET // 2, axis=1) * sin)
        return jnp.concatenate(outs, axis=1)

    qa_ref[...] = head_rms(proj(0), gq_ref[...]).astype(BF16)
    ka = head_rms(proj(1), gk_ref[...])
    ka_ref[...] = ka.T.astype(BF16)
    va = proj(2)
    va_ref[...] = va.astype(BF16)

    @pl.when(is_p)
    def _():
        kpt_ref[...] = ka
        vpt_ref[...] = va

    @pl.when(jnp.logical_not(is_p))
    def _():
        kst_ref[...] = ka
        vst_ref[...] = va

    qb_ref[...] = rot(proj(3)).astype(BF16)
    kb_ref[...] = (rot(proj(4)) * (HEAD_DIM_RET ** -0.5)).astype(BF16)
    vb_ref[...] = proj(5).astype(BF16)
    gb_ref[...] = proj(6).astype(BF16)


def _inproj(xp, xs, shift, scale, g_norm, w_in_b, bd, gq8, gk8, cos_t, sin_t, nb, tps):
    rp = xp.shape[0]
    ntp = rp // TM
    nt = ntp + 1
    r = rp + TM
    row = lambda i: (i, 0)
    full = lambda i: (0, 0)
    tab = lambda i: (jnp.where(i < ntp, i % tps, tps), 0)
    act = jax.ShapeDtypeStruct((r, GROUP_W), BF16)
    return pl.pallas_call(
        functools.partial(_inproj_kernel, ntp),
        grid=(nt,),
        in_specs=[pl.BlockSpec((TM, D_MODEL), lambda i: (jnp.minimum(i, ntp - 1), 0)),
                  pl.BlockSpec((TM, D_MODEL), full),
                  pl.BlockSpec((GROUPS_PER_TILE, D_MODEL), _mod_row(ntp, tps, nb)),
                  pl.BlockSpec((GROUPS_PER_TILE, D_MODEL), _mod_row(ntp, tps, nb)),
                  pl.BlockSpec((1, D_MODEL), full),
                  pl.BlockSpec((D_MODEL, N_SLOTS * GROUP_W), full),
                  pl.BlockSpec((GROUP_W // 2, GROUP_W // 2), full),
                  pl.BlockSpec((1, GROUP_W), full),
                  pl.BlockSpec((1, GROUP_W), full),
                  pl.BlockSpec((TM, HEAD_DIM_RET), tab),
                  pl.BlockSpec((TM, HEAD_DIM_RET), tab)],
        out_specs=[pl.BlockSpec((TM, GROUP_W), row), pl.BlockSpec((GROUP_W, TM), lambda i: (0, i))]
        + [pl.BlockSpec((TM, GROUP_W), row)] * 5 + [
            pl.BlockSpec((TM, GROUP_W), lambda i: (jnp.minimum(i // tps, nb - 1), 0)),
            pl.BlockSpec((TM, GROUP_W), lambda i: (jnp.minimum(i // tps, nb - 1), 0)),
            pl.BlockSpec((TM, GROUP_W), full),
            pl.BlockSpec((TM, GROUP_W), full)],
        out_shape=[act, jax.ShapeDtypeStruct((GROUP_W, r), BF16)] + [act] * 5
        + [jax.ShapeDtypeStruct((nb * TM, GROUP_W), F32)] * 2
        + [jax.ShapeDtypeStruct((TM, GROUP_W), F32)] * 2,
        compiler_params=_cparams(("arbitrary",), VMEM_LIMIT),
        name="inproj",
    )(xp, xs, shift, scale, g_norm, w_in_b, bd, gq8, gk8, cos_t, sin_t)


def _attn_heads(q, k, v, bias_ref, first_valid_col=None):
    qb_rows, kb_rows = q.shape[0], v.shape[0]
    assert qb_rows == 4 * CHUNK
    half_rows, span = qb_rows // 2, kb_rows - 2 * CHUNK
    parts = [(0, 0), (half_rows, 2 * CHUNK)]

    def softmax_part(s_full, hh, half, r0, c0):
        rs = half * qb_rows + r0
        s = s_full[rs:rs + half_rows, c0:c0 + span] + bias_ref[hh, r0:r0 + half_rows, c0:c0 + span]
        if first_valid_col is not None:
            col = lax.broadcasted_iota(I32, (half_rows, span), 1) + c0
            s = jnp.where(col >= first_valid_col, s, NEG_INF)
        m = jnp.max(s, axis=-1, keepdims=True)
        e = jnp.exp2(s - m)
        l = jnp.sum(e, axis=-1, keepdims=True)
        pad = [jnp.zeros((half_rows, c0), BF16)] if c0 else []
        pad_r = [jnp.zeros((half_rows, kb_rows - span - c0), BF16)] if kb_rows - span - c0 else []
        return jnp.concatenate(pad + [e.astype(BF16)] + pad_r, axis=1), l

    pair_w = 2 * HEAD_DIM_ATT
    low = lax.broadcasted_iota(I32, (1, pair_w), 1) < HEAD_DIM_ATT
    outs = []
    for pp in range(N_HEADS_ATT // 2):
        ps = slice(pp * pair_w, (pp + 1) * pair_w)
        q2, v2 = q[:, ps], v[:, ps]
        zero = jnp.zeros_like(q2)
        qs = jnp.concatenate([jnp.where(low, q2, zero), jnp.where(low, zero, q2)], axis=0)
        s = jnp.dot(qs, k[ps, :], preferred_element_type=F32)
        es, ls = zip(*[softmax_part(s, 2 * pp + half, half, r0, c0)
                       for half in range(2) for r0, c0 in parts])
        o = jnp.dot(jnp.concatenate(es, axis=0), v2, preferred_element_type=F32) / jnp.concatenate(ls, axis=0)
        outs.append(jnp.where(low, o[:qb_rows], o[qb_rows:]))
    return jnp.concatenate(outs, axis=1)


def _fill_band_bias(rev_ref, bias_scr):
    _, qb_rows, kb_rows = bias_scr.shape
    width = rev_ref.shape[1]
    q = lax.broadcasted_iota(I32, (qb_rows, kb_rows), 0)
    k = lax.broadcasted_iota(I32, (qb_rows, kb_rows), 1)
    qc = q >> 6
    kc = (k - ATT_WINDOW) >> 6
    band = (kc >= qc - ATT_WINDOW // CHUNK) & (kc <= qc)
    for hh in range(N_HEADS_ATT):
        rows = jnp.broadcast_to(rev_ref[hh:hh + 1, :], (qb_rows, width))
        toep = pltpu.roll(rows, width - MAX_REL, 1, stride=1, stride_axis=0)
        bias_scr[hh] = jnp.where(band, toep[:, :kb_rows] * LOG2_E, NEG_INF)


def _attn_sample_kernel(q_ref, kn_ref, vn_ref, kc_ref, vc_ref, rev_ref, o_ref, bias_scr):
    @pl.when(pl.program_id(0) == 0)
    def _():
        _fill_band_bias(rev_ref, bias_scr)

    q = q_ref[...]
    outs = []
    for hh in range(N_HEADS_ATT):
        hs = slice(hh * HEAD_DIM_ATT, (hh + 1) * HEAD_DIM_ATT)
        k = jnp.concatenate([kc_ref[0, :, hs], kn_ref[:, hs].astype(BF16)], axis=0)
        v = jnp.concatenate([vc_ref[0, :, hs], vn_ref[:, hs].astype(BF16)], axis=0)
        s = lax.dot_general(q[:, hs], k, (((1,), (1,)), ((), ())), preferred_element_type=F32)
        s = s + bias_scr[hh]
        m = jnp.max(s, axis=-1, keepdims=True)
        e = jnp.exp2(s - m)
        l = jnp.sum(e, axis=-1, keepdims=True)
        outs.append(jnp.dot(e.astype(BF16), v, preferred_element_type=F32) / l)
    o_ref[...] = jnp.concatenate(outs, axis=1).astype(BF16)


def _attn_sample(qa, ks_new, vs_new, kc, vc, rev, rp):
    ndb = kc.shape[0]
    base = rp // CHUNK
    spec = pl.BlockSpec((CHUNK, GROUP_W), lambda b: (base + b, 0))
    new = pl.BlockSpec((CHUNK, GROUP_W), lambda b: (b, 0))
    cspec = pl.BlockSpec((1, ATT_WINDOW, GROUP_W), lambda b: (b, 0, 0))
    return pl.pallas_call(
        _attn_sample_kernel,
        grid=(ndb,),
        in_specs=[spec, new, new, cspec, cspec,
                  pl.BlockSpec(rev.shape, lambda b: (0, 0))],
        out_specs=pl.BlockSpec((CHUNK, GROUP_W), lambda b: (b, 0)),
        out_shape=jax.ShapeDtypeStruct((ndb * CHUNK, GROUP_W), BF16),
        scratch_shapes=[pltpu.VMEM((N_HEADS_ATT, CHUNK, ATT_WINDOW + CHUNK), F32)],
        compiler_params=_cparams(("arbitrary",), VMEM_LIMIT),
        name="attn_sample",
    )(qa, ks_new, vs_new, kc, vc, rev)


def _ret_chunk(state_decay, q_ref, k_ref, v_ref, g_ref, dm_ref, xi_ref, zeta_ref, gro_ref, o_ref, s_scr):
    outs = []
    for hh in range(N_HEADS_RET):
        hs = slice(hh * HEAD_DIM_RET, (hh + 1) * HEAD_DIM_RET)
        q = q_ref[:, hs]
        k = k_ref[:, hs]
        v = v_ref[:, hs]
        st = s_scr[hh]
        sc = lax.dot_general(q, k, (((1,), (1,)), ((), ())), preferred_element_type=F32) * dm_ref[hh]
        inner = jnp.dot(sc.astype(BF16), v, preferred_element_type=F32)
        cross = jnp.dot(q, st.astype(BF16), preferred_element_type=F32) * xi_ref[:, hs]
        o = inner + cross
        kz = k.astype(F32) * zeta_ref[:, hs]
        s_scr[hh] = state_decay[hh] * st + jnp.dot(kz.T.astype(BF16), v, preferred_element_type=F32)
        mu = jnp.mean(o, axis=-1, keepdims=True)
        oc = o - mu
        var = jnp.mean(oc * oc, axis=-1, keepdims=True)
        outs.append(oc * lax.rsqrt(var + NORM_EPS))
    y = jnp.concatenate(outs, axis=1) * gro_ref[...]
    g = g_ref[...].astype(F32)
    o_ref[...] = (g * jax.nn.sigmoid(g) * y).astype(BF16)


def _ret_kernel(state_decay, q_ref, k_ref, v_ref, g_ref, s0_ref, dm_ref, xi_ref, zeta_ref,
                gro_ref, o_ref, sn_ref, s_scr):
    j = pl.program_id(1)

    @pl.when(j == 0)
    def _():
        s_scr[...] = s0_ref[0]

    _ret_chunk(state_decay, q_ref, k_ref, v_ref, g_ref, dm_ref, xi_ref, zeta_ref, gro_ref, o_ref, s_scr)

    @pl.when(j == pl.num_programs(1) - 1)
    def _():
        sn_ref[0] = s_scr[...]


def _mix_prompt_kernel(state_decay, q_ref, k0_ref, k1_ref, k2_ref, v0_ref, v1_ref, v2_ref, rev_ref,
                       rq_ref, rk_ref, rv_ref, rg_ref, s0_ref, dm_ref, xi_ref, zeta_ref, gro_ref,
                       att_ref, ret_ref, sn_ref, bias_scr, s_scr):
    j = pl.program_id(1)

    @pl.when((pl.program_id(0) == 0) & (j == 0))
    def _():
        _fill_band_bias(rev_ref, bias_scr)

    @pl.when(j == 0)
    def _():
        s_scr[...] = s0_ref[0]

    k = jnp.concatenate([k0_ref[...], k1_ref[...], k2_ref[...]], axis=1)
    v = jnp.concatenate([v0_ref[...], v1_ref[...], v2_ref[...]], axis=0)

    def block(first_valid_col):
        att_ref[...] = _attn_heads(q_ref[...], k, v, bias_scr, first_valid_col).astype(BF16)
        _ret_chunk(state_decay, rq_ref, rk_ref, rv_ref, rg_ref, dm_ref, xi_ref, zeta_ref, gro_ref,
                   ret_ref, s_scr)

    @pl.when(j >= 2)
    def _():
        block(None)

    @pl.when(j < 2)
    def _():
        block((2 - j) * ATT_QB)

    @pl.when(j == pl.num_programs(1) - 1)
    def _():
        sn_ref[0] = s_scr[...]


def _mix_prompt(qa, ka_t, va, rev, qb, kb, vb, gb, s0, g_ro, nb, seq):
    assert ATT_QB == RET_CB
    r = nb * seq
    nq = seq // ATT_QB
    dm, xi, zeta, state_decay = _ret_consts(RET_CB)
    blk = lambda back: (lambda b, j: (b * nq + jnp.maximum(j - back, 0), 0))
    spec = lambda back: pl.BlockSpec((ATT_QB, GROUP_W), blk(back))
    tspec = lambda back: pl.BlockSpec((GROUP_W, ATT_QB), lambda b, j: (0, b * nq + jnp.maximum(j - back, 0)))
    sspec = pl.BlockSpec((1, N_HEADS_RET, HEAD_DIM_RET, HEAD_DIM_RET), lambda b, j: (b, 0, 0, 0))
    full2 = lambda b, j: (0, 0)
    out = jax.ShapeDtypeStruct((r, GROUP_W), BF16)
    return pl.pallas_call(
        functools.partial(_mix_prompt_kernel, state_decay),
        grid=(nb, nq),
        in_specs=[spec(0), tspec(2), tspec(1), tspec(0), spec(2), spec(1), spec(0),
                  pl.BlockSpec(rev.shape, full2),
                  spec(0), spec(0), spec(0), spec(0), sspec,
                  pl.BlockSpec(dm.shape, lambda b, j: (0, 0, 0)),
                  pl.BlockSpec(xi.shape, full2), pl.BlockSpec(zeta.shape, full2),
                  pl.BlockSpec((1, GROUP_W), full2)],
        out_specs=[spec(0), spec(0), sspec],
        out_shape=[out, out, jax.ShapeDtypeStruct(s0.shape, F32)],
        scratch_shapes=[pltpu.VMEM((N_HEADS_ATT, ATT_QB, ATT_WINDOW + ATT_QB), F32),
                        pltpu.VMEM((N_HEADS_RET, HEAD_DIM_RET, HEAD_DIM_RET), F32)],
        compiler_params=_cparams(("arbitrary", "arbitrary"), VMEM_LIMIT),
        name="mix_prompt",
    )(qa, ka_t, ka_t, ka_t, va, va, va, rev, qb, kb, vb, gb, s0, dm, xi, zeta, g_ro)


def _ret_consts(cb):
    log_g = np.log1p(-np.exp2(-RET_DECAY_OFFSET - np.arange(N_HEADS_RET, dtype=np.float64)))
    n = np.arange(cb, dtype=np.float64)
    diff = n[:, None] - n[None, :]
    dm = np.where(diff[None] >= 0, np.exp(np.maximum(diff, 0.0)[None] * log_g[:, None, None]), 0.0)
    xi = np.exp((n + 1.0)[:, None] * log_g[None, :])
    zeta = np.exp((cb - 1.0 - n)[:, None] * log_g[None, :])
    rep = lambda a: np.repeat(a, HEAD_DIM_RET, axis=1)
    state_decay = tuple(float(v) for v in np.exp(cb * log_g))
    return (jnp.asarray(dm, F32), jnp.asarray(rep(xi), F32), jnp.asarray(rep(zeta), F32), state_decay)


def _ret(qb, kb, vb, gb, s0, g_ro, cb, row0, nb, nc, name):
    dm, xi, zeta, state_decay = _ret_consts(cb)
    base = row0 // cb
    spec = pl.BlockSpec((cb, GROUP_W), lambda b, j: (base + b * nc + j, 0))
    sspec = pl.BlockSpec((1, N_HEADS_RET, HEAD_DIM_RET, HEAD_DIM_RET), lambda b, j: (b, 0, 0, 0))
    full2 = lambda b, j: (0, 0)
    return pl.pallas_call(
        functools.partial(_ret_kernel, state_decay),
        grid=(nb, nc),
        in_specs=[spec, spec, spec, spec, sspec,
                  pl.BlockSpec(dm.shape, lambda b, j: (0, 0, 0)),
                  pl.BlockSpec(xi.shape, full2), pl.BlockSpec(zeta.shape, full2),
                  pl.BlockSpec((1, GROUP_W), full2)],
        out_specs=[pl.BlockSpec((cb, GROUP_W), lambda b, j: (b * nc + j, 0)), sspec],
        out_shape=[jax.ShapeDtypeStruct((nb * nc * cb, GROUP_W), BF16),
                   jax.ShapeDtypeStruct(s0.shape, F32)],
        scratch_shapes=[pltpu.VMEM((N_HEADS_RET, HEAD_DIM_RET, HEAD_DIM_RET), F32)],
        compiler_params=_cparams(("arbitrary", "arbitrary"), VMEM_LIMIT),
        name=name,
    )(qb, kb, vb, gb, s0, dm, xi, zeta, g_ro)


def _outproj_kernel(npp, attp_ref, atts_ref, retp_ref, rets_ref, xp_ref, xs_ref, gm_ref, shf_ref, scf_ref,
                    gn_ref, wo_ref, wr_ref, br_ref, upper_ref, lower_ref,
                    x1_ref, h2_ref, slot_ref, cols_ref, cnt_ref):
    is_p = pl.program_id(0) < npp
    subs = range(2)
    rows = [slice(sub * TM, (sub + 1) * TM) for sub in subs]

    def pick(p_ref, s_ref, sub):
        return jnp.where(is_p, p_ref[rows[sub], :], s_ref[...])

    mix = [jnp.dot(pick(attp_ref, atts_ref, sub), wo_ref[:GROUP_W, :], preferred_element_type=F32)
           + jnp.dot(pick(retp_ref, rets_ref, sub), wo_ref[GROUP_W:, :], preferred_element_type=F32)
           for sub in subs]
    h2b = []
    for sub in subs:
        x1 = _per_group(mix[sub], lambda a, gm: a * gm, gm_ref[...]) + pick(xp_ref, xs_ref, sub)
        x1_ref[rows[sub], :] = x1
        y = _rms_rows(x1, gn_ref[...])
        h2 = _per_group(y, lambda a, sh, sc: a * (1.0 + sc) + sh, shf_ref[...], scf_ref[...])
        h2b.append(h2.astype(BF16))
        h2_ref[rows[sub], :] = h2b[sub]

    work = [lax.dot_general(wr_ref[...], h2b[sub], (((1,), (1,)), ((), ())),
                            preferred_element_type=F32) + br_ref[...] for sub in subs]
    eidx = lax.broadcasted_iota(I32, work[0].shape, 0).astype(F32)
    sel = [[] for _ in subs]
    top = [[] for _ in subs]
    for _ in range(TOP_K):
        for sub in subs:
            m = jnp.max(work[sub], axis=0, keepdims=True)
            idx = jnp.min(jnp.where(work[sub] == m, eidx, float(N_EXPERTS)), axis=0, keepdims=True)
            hit = eidx == idx
            sel[sub].append(hit)
            top[sub].append(m)
            work[sub] = jnp.where(hit, -jnp.inf, work[sub])

    for sub in subs:
        ex = [jnp.exp(t - top[sub][0]) for t in top[sub]]
        den = ex[0] + ex[1] + ex[2] + ex[3]
        gates = [e / den for e in ex]
        hits = sel[sub]
        multi_f = jnp.where(hits[0] | hits[1] | hits[2] | hits[3], 1.0, 0.0)
        rank = jnp.dot(multi_f.astype(BF16), upper_ref[...], preferred_element_type=F32)
        cnt = jnp.sum(multi_f, axis=1, keepdims=True)
        cnt_pad = jnp.floor((cnt + (SEG_ALIGN - 1.0)) * (1.0 / SEG_ALIGN)) * SEG_ALIGN
        cnt_pad_b = jnp.broadcast_to(cnt_pad, (N_EXPERTS, 128))
        seg_off = jnp.dot(lower_ref[...], cnt_pad_b.astype(BF16), preferred_element_type=F32)[:, :1]
        pos = seg_off + rank
        slot_rows = jnp.concatenate(
            [jnp.sum(jnp.where(h, pos, 0.0), axis=0, keepdims=True) for h in hits], axis=0)
        gate_rows = jnp.concatenate(gates, axis=0)
        slot_ref[sub] = slot_rows.astype(I32)
        cnt_ref[sub] = cnt_pad_b.astype(I32)
        both = jnp.concatenate([slot_rows, gate_rows, jnp.zeros((128 - 2 * TOP_K, TM), F32)], axis=0)
        cols_ref[sub] = both.T


def _outproj(att_p, att_s, ret_p, ret_s, xp, xs, gate_m, shift_f, scale_f, g_norm, w_out_b, wr_t, br,
             upper, lower, nb, tps):
    rp = xp.shape[0]
    ntp = rp // TM
    assert ntp % 2 == 0 and tps % 2 == 0
    npp = ntp // 2
    nt2 = ntp + 2
    r = nt2 * TM
    row = lambda p: (p, 0)
    row3 = lambda p: (p, 0, 0)
    full = lambda p: (0, 0)
    prow = lambda p: (jnp.minimum(p, npp - 1), 0)
    mod = pl.BlockSpec((GROUPS_PER_TILE, D_MODEL), _mod_row(npp, tps // 2, nb))
    return pl.pallas_call(
        functools.partial(_outproj_kernel, npp),
        grid=(npp + 1,),
        in_specs=[pl.BlockSpec((2 * TM, GROUP_W), prow), pl.BlockSpec((TM, GROUP_W), full),
                  pl.BlockSpec((2 * TM, GROUP_W), prow), pl.BlockSpec((TM, GROUP_W), full),
                  pl.BlockSpec((2 * TM, D_MODEL), prow),
                  pl.BlockSpec((TM, D_MODEL), full),
                  mod, mod, mod,
                  pl.BlockSpec((1, D_MODEL), full),
                  pl.BlockSpec((D_MODEL, D_MODEL), full),
                  pl.BlockSpec((N_EXPERTS, D_MODEL), full),
                  pl.BlockSpec((N_EXPERTS, 1), full),
                  pl.BlockSpec((TM, TM), full),
                  pl.BlockSpec((N_EXPERTS, N_EXPERTS), full)],
        out_specs=[pl.BlockSpec((2 * TM, D_MODEL), row), pl.BlockSpec((2 * TM, D_MODEL), row),
                   pl.BlockSpec((2, TOP_K, TM), row3),
                   pl.BlockSpec((2, TM, 128), row3), pl.BlockSpec((2, N_EXPERTS, 128), row3)],
        out_shape=[jax.ShapeDtypeStruct((r, D_MODEL), F32), jax.ShapeDtypeStruct((r, D_MODEL), BF16),
                   jax.ShapeDtypeStruct((nt2, TOP_K, TM), I32),
                   jax.ShapeDtypeStruct((nt2, TM, 128), F32), jax.ShapeDtypeStruct((nt2, N_EXPERTS, 128), I32)],
        compiler_params=_cparams(("arbitrary",), VMEM_LIMIT),
        name="outproj",
    )(att_p, att_s, ret_p, ret_s, xp, xs, gate_m, shift_f, scale_f, g_norm, w_out_b, wr_t, br, upper, lower)


def _rows_copy(n, src_rows, dst_rows, sem):
    size = pl.multiple_of(n, SEG_ALIGN)
    return pltpu.make_async_copy(src_rows(size), dst_rows(size), sem)


def _start_segments(t, cnt_ref, off_ref, base_ref, local_rows, sorted_rows, sem, to_sorted):
    def body(e, c):
        n = cnt_ref[t * N_EXPERTS + e]
        off = pl.multiple_of(off_ref[t * N_EXPERTS + e], SEG_ALIGN)
        base = pl.multiple_of(base_ref[t * N_EXPERTS + e], SEG_ALIGN)
        local = lambda z: local_rows(off, z)
        remote = lambda z: sorted_rows(base, z)

        @pl.when(n > 0)
        def _():
            (_rows_copy(n, local, remote, sem) if to_sorted else _rows_copy(n, remote, local, sem)).start()
        return c
    lax.fori_loop(0, N_EXPERTS, body, 0)


def _dispatch_kernel(nt, n_blocks, off_ref, cnt_ref, base_ref, tot_ref, tail0_ref, tailn_ref, na_ref,
                     h2_ref, slot_ref, slotn_ref, xb_ref, xs_scr, hot_scr, zero_scr, sems, tail_sem):
    i = pl.program_id(0)
    cur = i % 2
    sorted_rows = lambda r, z: xb_ref.at[pl.ds(r, z), :]

    def start_tile(t, buf):
        _start_segments(t, cnt_ref, off_ref, base_ref, lambda r, z: xs_scr.at[buf, pl.ds(r, z), :],
                        sorted_rows, sems.at[buf], True)

    def wait_tile(t, buf):
        _rows_copy(tot_ref[t], lambda z: xs_scr.at[buf, pl.ds(0, z), :], lambda z: sorted_rows(0, z),
                   sems.at[buf]).wait()

    def tail_copies(wait):
        def body(e, c):
            base = pl.multiple_of(tail0_ref[e], SEG_ALIGN)

            @pl.when(tailn_ref[e] > 0)
            def _():
                cp = _rows_copy(tailn_ref[e], lambda z: zero_scr.at[pl.ds(0, z), :],
                                lambda z: sorted_rows(base, z), tail_sem)
                cp.wait() if wait else cp.start()
            return c
        lax.fori_loop(0, N_EXPERTS, body, 0)

        def unused(j, c):
            cp = pltpu.make_async_copy(zero_scr, sorted_rows(pl.multiple_of(j * BM, BM), BM), tail_sem)
            cp.wait() if wait else cp.start()
            return c
        lax.fori_loop(na_ref[0], n_blocks, unused, 0)

    @pl.when(i >= 2)
    def _():
        wait_tile(i - 2, cur)

    def onehot(slot):
        srow = lax.broadcasted_iota(I32, (CAP, TM), 0)
        hit = (srow == slot[0:1]) | (srow == slot[1:2]) | (srow == slot[2:3]) | (srow == slot[3:4])
        return jnp.where(hit, 1.0, 0.0).astype(BF16)

    @pl.when(i == 0)
    def _():
        hot_scr[0] = onehot(slot_ref[0])

    xs_scr[cur] = jnp.dot(hot_scr[cur], h2_ref[...], preferred_element_type=F32).astype(BF16)
    hot_scr[1 - cur] = onehot(slotn_ref[0])
    start_tile(i, cur)

    @pl.when(i == nt - 1)
    def _():
        zero_scr[...] = jnp.zeros_like(zero_scr)
        tail_copies(False)
        if nt >= 2:
            wait_tile(i - 1, 1 - cur)
        wait_tile(i, cur)
        tail_copies(True)


def _dispatch(h2, slot, off, cnt, base, tot, tail0, tailn, n_act, n_blocks):
    nt = tot.shape[0]
    n_rows = n_blocks * BM
    grid_spec = pltpu.PrefetchScalarGridSpec(
        num_scalar_prefetch=7,
        grid=(nt,),
        in_specs=[pl.BlockSpec((TM, D_MODEL), lambda i, *_: (i, 0)),
                  pl.BlockSpec((1, TOP_K, TM), lambda i, *_: (i, 0, 0)),
                  pl.BlockSpec((1, TOP_K, TM), lambda i, *_: (jnp.minimum(i + 1, nt - 1), 0, 0))],
        out_specs=pl.BlockSpec(memory_space=pl.ANY),
        scratch_shapes=[pltpu.VMEM((2, CAP, D_MODEL), BF16),
                        pltpu.VMEM((2, CAP, TM), BF16),
                        pltpu.VMEM((BM, D_MODEL), BF16),
                        pltpu.SemaphoreType.DMA((2,)),
                        pltpu.SemaphoreType.DMA(())],
    )
    return pl.pallas_call(
        functools.partial(_dispatch_kernel, nt, n_blocks),
        grid_spec=grid_spec,
        out_shape=jax.ShapeDtypeStruct((n_rows, D_MODEL), BF16),
        compiler_params=_cparams(("arbitrary",), VMEM_LIMIT),
        name="dispatch",
    )(off, cnt, base, tot, tail0, tailn, n_act, h2, slot, slot)


def _experts_kernel(be_ref, bi_ref, nx_ref, na_ref, x_ref, wu_hbm, bu_ref, wd_hbm, bd_ref, y_ref,
                    wu_stage, wd_stage, wu_scr, wd_scr, sems):
    j = pl.program_id(0)

    def weight_copies(e):
        return (pltpu.make_async_copy(wu_hbm.at[e], wu_stage, sems.at[0]),
                pltpu.make_async_copy(wd_hbm.at[e], wd_stage, sems.at[1]))

    @pl.when(j < na_ref[0])
    def _():
        e = be_ref[j]
        prev = be_ref[jnp.maximum(j - 1, 0)]

        @pl.when(j == 0)
        def _():
            for cp in weight_copies(e):
                cp.start()

        @pl.when((j == 0) | (e != prev))
        def _():
            for cp in weight_copies(e):
                cp.wait()
            wu_scr[...] = wu_stage[...].astype(BF16)
            wd_scr[...] = wd_stage[...].astype(BF16)

            @pl.when(nx_ref[j] != e)
            def _():
                for cp in weight_copies(nx_ref[j]):
                    cp.start()

        x = x_ref[...]
        acts = []
        for c in range(D_FF // UP_CHUNK):
            gs = slice(c * UP_CHUNK, (c + 1) * UP_CHUNK)
            ls = slice(D_FF + c * UP_CHUNK, D_FF + (c + 1) * UP_CHUNK)
            ug = jnp.dot(x, wu_scr[:, gs], preferred_element_type=F32) + bu_ref[0, :, gs]
            ul = jnp.dot(x, wu_scr[:, ls], preferred_element_type=F32) + bu_ref[0, :, ls]
            glu = jnp.minimum(ug, SWIGLU_LIMIT)
            lin = jnp.clip(ul, -SWIGLU_LIMIT, SWIGLU_LIMIT)
            acts.append((glu * jax.nn.sigmoid(SWIGLU_ALPHA * glu) * (lin + 1.0)).astype(BF16))
        act = jnp.concatenate(acts, axis=1)
        y = jnp.dot(act, wd_scr[...], preferred_element_type=F32) + bd_ref[0]
        y_ref[...] = y.astype(BF16)

    @pl.when(j >= na_ref[0])
    def _():
        y_ref[...] = jnp.zeros_like(y_ref)


def _experts(xb, blk_e, blk_i, blk_nx, n_act, w_up, b_up, w_down, b_down):
    n_rows = xb.shape[0]
    nblk = n_rows // BM
    grid_spec = pltpu.PrefetchScalarGridSpec(
        num_scalar_prefetch=4,
        grid=(nblk,),
        in_specs=[pl.BlockSpec((BM, D_MODEL), lambda j, be, bi, nx, na: (bi[j], 0)),
                  pl.BlockSpec(memory_space=pl.ANY),
                  pl.BlockSpec((1, 1, 2 * D_FF), lambda j, be, bi, nx, na: (be[j], 0, 0)),
                  pl.BlockSpec(memory_space=pl.ANY),
                  pl.BlockSpec((1, 1, D_MODEL), lambda j, be, bi, nx, na: (be[j], 0, 0))],
        out_specs=pl.BlockSpec((BM, D_MODEL), lambda j, be, bi, nx, na: (j, 0)),
        scratch_shapes=[pltpu.VMEM((D_MODEL, 2 * D_FF), F32), pltpu.VMEM((D_FF, D_MODEL), F32),
                        pltpu.VMEM((D_MODEL, 2 * D_FF), BF16), pltpu.VMEM((D_FF, D_MODEL), BF16),
                        pltpu.SemaphoreType.DMA((2,))],
    )
    return pl.pallas_call(
        _experts_kernel,
        grid_spec=grid_spec,
        out_shape=jax.ShapeDtypeStruct((n_rows, D_MODEL), BF16),
        compiler_params=_cparams(("arbitrary",), VMEM_LIMIT),
        name="experts",
    )(blk_e, blk_i, blk_nx, n_act, xb, w_up, b_up.reshape(N_EXPERTS, 1, 2 * D_FF), w_down,
      b_down.reshape(N_EXPERTS, 1, D_MODEL))


def _combine_kernel(nt, ntp, off_ref, cnt_ref, base_ref, tot_ref, yb_ref, cols_ref, x1_ref, gf_ref,
                    op_ref, os_ref, ys_scr, sems):
    i = pl.program_id(0)
    cur = i % 2

    sorted_rows = lambda r, z: yb_ref.at[pl.ds(r, z), :]

    def start_tile(t, buf):
        _start_segments(t, cnt_ref, off_ref, base_ref, lambda r, z: ys_scr.at[buf, pl.ds(r, z), :],
                        sorted_rows, sems.at[buf], False)

    @pl.when(i == 0)
    def _():
        ys_scr[...] = jnp.zeros_like(ys_scr)
        start_tile(0, 0)

    @pl.when(i + 1 < nt)
    def _():
        start_tile(i + 1, 1 - cur)

    _rows_copy(tot_ref[i], lambda z: sorted_rows(0, z), lambda z: ys_scr.at[cur, pl.ds(0, z), :],
               sems.at[cur]).wait()

    cols = cols_ref[0]
    lane = lax.broadcasted_iota(I32, (TM, CAP), 1)
    w = jnp.zeros((TM, CAP), F32)
    for k in range(TOP_K):
        sk = cols[:, k:k + 1].astype(I32)
        gk = cols[:, TOP_K + k:TOP_K + k + 1]
        w = jnp.where(lane == sk, gk, w)
    y = jnp.dot(w.astype(BF16), ys_scr[cur], preferred_element_type=F32)
    out = x1_ref[...] + _per_group(y, lambda a, gf: a * gf, gf_ref[...])

    @pl.when(i < ntp)
    def _():
        op_ref[...] = out

    @pl.when(i >= ntp)
    def _():
        os_ref[...] = out


def _combine(yb, cols, x1, gate_f, off, cnt, base, tot, ntp, nb, tps):
    nt = tot.shape[0]
    grid_spec = pltpu.PrefetchScalarGridSpec(
        num_scalar_prefetch=4,
        grid=(nt,),
        in_specs=[pl.BlockSpec(memory_space=pl.ANY),
                  pl.BlockSpec((1, TM, 128), lambda i, *_: (i, 0, 0)),
                  pl.BlockSpec((TM, D_MODEL), lambda i, *_: (i, 0)),
                  pl.BlockSpec((GROUPS_PER_TILE, D_MODEL), _mod_row(ntp, tps, nb))],
        out_specs=[pl.BlockSpec((TM, D_MODEL), lambda i, *_: (jnp.minimum(i, ntp - 1), 0)),
                   pl.BlockSpec((TM, D_MODEL), lambda i, *_: (0, 0))],
        scratch_shapes=[pltpu.VMEM((2, CAP, D_MODEL), BF16), pltpu.SemaphoreType.DMA((2,))],
    )
    return pl.pallas_call(
        functools.partial(_combine_kernel, nt, ntp),
        grid_spec=grid_spec,
        out_shape=[jax.ShapeDtypeStruct((ntp * TM, D_MODEL), F32),
                   jax.ShapeDtypeStruct((TM, D_MODEL), F32)],
        compiler_params=_cparams(("arbitrary",), VMEM_LIMIT),
        name="combine",
    )(off, cnt, base, tot, yb, cols, x1, gate_f)


def _rotary_tables(seq, dec_batch, dec_seq):
    half = HEAD_DIM_RET // 2
    inv = ROPE_BASE ** (-np.arange(half, dtype=np.float64) / half)
    pos = np.concatenate([np.arange(seq), np.tile(PAST_LEN + np.arange(dec_seq), dec_batch)])
    ang = pos.astype(np.float64)[:, None] * inv[None, :]
    cos = np.concatenate([np.cos(ang), np.cos(ang)], axis=1)
    sin = np.concatenate([-np.sin(ang), np.sin(ang)], axis=1)
    return jnp.asarray(cos, F32), jnp.asarray(sin, F32)


def _rel_bias_reversed(rel_bias):
    heads = rel_bias.shape[0]
    ext = jnp.concatenate([rel_bias[:, 1:], jnp.broadcast_to(rel_bias[:, -1:], (heads, 2 * MAX_REL))], axis=1)
    return ext[:, ::-1].astype(F32)


def _group_mods(m, nb, ndb):
    assert ndb == GROUPS_PER_TILE
    mp = jnp.broadcast_to(m[:nb, None], (nb, GROUPS_PER_TILE) + m.shape[1:])
    allm = jnp.concatenate([mp.reshape((nb * GROUPS_PER_TILE,) + m.shape[1:]), m[nb:]], axis=0)
    return jnp.transpose(allm, (1, 0, 2))


def _routing_tables(cnt, n_blocks):
    nt = cnt.shape[0]
    off = jnp.cumsum(cnt, axis=1) - cnt
    rows_e = jnp.sum(cnt, axis=0)
    nblk_e = (rows_e + BM - 1) // BM
    blk_end = jnp.cumsum(nblk_e)
    start_e = (blk_end - nblk_e) * BM
    base = start_e[None, :] + jnp.cumsum(cnt, axis=0) - cnt
    n_act = blk_end[-1]
    j = jnp.minimum(jnp.arange(n_blocks), n_act - 1)
    blk_e = jnp.minimum(jnp.sum(blk_end[None, :] <= j[:, None], axis=1), N_EXPERTS - 1)
    later = jnp.where(blk_e[None, :] > blk_e[:, None], blk_e[None, :], N_EXPERTS)
    blk_nx = jnp.min(later, axis=1)
    blk_nx = jnp.where(blk_nx == N_EXPERTS, blk_e, blk_nx)
    tail0 = start_e + rows_e
    tailn = nblk_e * BM - rows_e
    i32 = lambda a: a.astype(I32)
    return (i32(off.reshape(nt * N_EXPERTS)), i32(cnt.reshape(nt * N_EXPERTS)),
            i32(base.reshape(nt * N_EXPERTS)), i32(jnp.sum(cnt, axis=1)), i32(tail0), i32(tailn),
            i32(blk_e), i32(j), i32(blk_nx), i32(n_act.reshape(1)))


def kernel(x_prompt, x_sample, c_prompt, c_sample, cache_att_k, cache_att_v, state_ret, w_ada, b_ada,
           g_norm_mix, g_norm_ffn, w_in, g_q, g_k, rel_bias, g_ret_out, w_out, w_router, b_router,
           w_up, b_up, w_down, b_down):
    nb, seq, d = x_prompt.shape
    ndb, dseq, _ = x_sample.shape
    assert d == D_MODEL and ndb * dseq == TM and dseq == CHUNK
    assert seq % TM == 0 and seq >= ATT_WINDOW and cache_att_k.shape[2] == ATT_WINDOW
    assert w_ada.shape[0] == 1
    rp = nb * seq
    ntp = rp // TM
    nt = ntp + 1
    tps = seq // TM

    xp = x_prompt.reshape(rp, d)
    xs = x_sample.reshape(TM, d)

    m = _ada(jnp.concatenate([c_prompt, c_sample], axis=0), w_ada[0], b_ada[0])
    mods = _group_mods(m.reshape(nb + ndb, N_ADA, d), nb, ndb)
    shift_m, scale_m, gate_m, shift_f, scale_f, gate_f = [mods[a] for a in range(N_ADA)]

    cos_t, sin_t = _rotary_tables(seq, ndb, dseq)
    bd = jnp.asarray(np.kron(np.eye(N_HEADS_ATT // 2), np.ones((HEAD_DIM_ATT, HEAD_DIM_ATT))), BF16)
    tile8 = lambda g: jnp.tile(g.astype(F32), N_HEADS_ATT).reshape(1, GROUP_W)
    (qa, ka_t, va, qb, kb, vb, gb, kp_tail, vp_tail, ks_new, vs_new) = _inproj(
        xp, xs, shift_m, scale_m, g_norm_mix[0].reshape(1, d), w_in[0].astype(BF16), bd,
        tile8(g_q[0]) * (HEAD_DIM_ATT ** -0.5 * LOG2_E), tile8(g_k[0]), cos_t, sin_t, nb, tps)

    rev = _rel_bias_reversed(rel_bias[0])
    g_ro = g_ret_out[0].astype(F32).reshape(1, GROUP_W)
    zero_state = jnp.zeros((nb, N_HEADS_RET, HEAD_DIM_RET, HEAD_DIM_RET), F32)
    att_p, ret_p, state_p = _mix_prompt(qa, ka_t, va, rev, qb, kb, vb, gb, zero_state, g_ro, nb, seq)
    att_s = _attn_sample(qa, ks_new, vs_new,
                         cache_att_k[0].reshape(ndb, ATT_WINDOW, GROUP_W).astype(BF16),
                         cache_att_v[0].reshape(ndb, ATT_WINDOW, GROUP_W).astype(BF16), rev, rp)

    ret_s, state_s = _ret(qb, kb, vb, gb, state_ret[0].astype(F32), g_ro, CHUNK, rp, ndb, 1, "ret_sample")

    upper = jnp.asarray(np.triu(np.ones((TM, TM)), 1), BF16)
    lower = jnp.asarray(np.tril(np.ones((N_EXPERTS, N_EXPERTS)), -1), BF16)
    x1, h2, slot, cols, cnt = _outproj(
        att_p, att_s, ret_p, ret_s, xp, xs, gate_m, shift_f, scale_f, g_norm_ffn[0].reshape(1, d),
        w_out[0].astype(BF16), w_router[0].T.astype(BF16), b_router[0].astype(F32).reshape(N_EXPERTS, 1),
        upper, lower, nb, tps)

    n_blocks = (TOP_K * (rp + TM) + nt * N_EXPERTS * (SEG_ALIGN - 1)) // BM + 1 + N_EXPERTS
    off, cntf, base, tot, tail0, tailn, blk_e, blk_i, blk_nx, n_act = _routing_tables(cnt[:nt, :, 0], n_blocks)
    xb = _dispatch(h2, slot, off, cntf, base, tot, tail0, tailn, n_act, n_blocks)
    yb = _experts(xb, blk_e, blk_i, blk_nx, n_act, w_up[0], b_up[0], w_down[0], b_down[0])
    out_p, out_s = _combine(yb, cols, x1, gate_f, off, cntf, base, tot, ntp, nb, tps)

    heads = (N_HEADS_ATT, HEAD_DIM_ATT)
    return (out_p.reshape(nb, seq, d), out_s.reshape(ndb, dseq, d),
            kp_tail.reshape(1, nb, ATT_WINDOW, *heads), vp_tail.reshape(1, nb, ATT_WINDOW, *heads),
            state_p[None],
            ks_new.reshape(1, ndb, dseq, *heads), vs_new.reshape(1, ndb, dseq, *heads),
            state_s[None])
```

```python
import functools

import numpy as np
import jax
import jax.numpy as jnp
from jax import lax
from jax.experimental import pallas as pl
from jax.experimental.pallas import tpu as pltpu

F32 = jnp.float32
BF16 = jnp.bfloat16
I32 = jnp.int32

D_MODEL = 1024
GROUP_W = 512
N_SLOTS = 7
N_HEADS_ATT = 8
HEAD_DIM_ATT = 64
N_HEADS_RET = 4
HEAD_DIM_RET = 128
CHUNK = 64
ATT_WINDOW = 512
MAX_REL = 256
PAST_LEN = 2048
RET_DECAY_OFFSET = 5.0
ROPE_BASE = 10000.0
N_EXPERTS = 32
TOP_K = 4
D_FF = 1024
SWIGLU_LIMIT = 7.0
SWIGLU_ALPHA = 1.702
N_ADA = 6
NORM_EPS = 1e-6
NEG_INF = -1e30
LOG2_E = 1.4426950408889634

TM = 512
GROUPS_PER_TILE = TM // CHUNK
ATT_QB = 256
RET_CB = 256
SEG_ALIGN = 16
CAP = TOP_K * TM + N_EXPERTS * SEG_ALIGN
BM = 512
VMEM_LIMIT = 56 * 1024 * 1024


def _cparams(sem, vmem=None):
    return pltpu.CompilerParams(dimension_semantics=sem, vmem_limit_bytes=vmem)


def _ada_kernel(c_ref, w_ref, b_ref, o_ref):
    c = c_ref[...]
    s = c * jax.nn.sigmoid(c)
    o_ref[...] = jnp.dot(s.astype(BF16), w_ref[...].astype(BF16),
                         preferred_element_type=F32) + b_ref[...]


def _ada(c_all, w_ada, b_ada):
    n, d = c_all.shape
    cols = w_ada.shape[1]
    tn = 1536
    return pl.pallas_call(
        _ada_kernel,
        grid=(cols // tn,),
        in_specs=[pl.BlockSpec((n, d), lambda j: (0, 0)),
                  pl.BlockSpec((d, tn), lambda j: (0, j)),
                  pl.BlockSpec((1, tn), lambda j: (0, j))],
        out_specs=pl.BlockSpec((n, tn), lambda j: (0, j)),
        out_shape=jax.ShapeDtypeStruct((n, cols), F32),
        compiler_params=_cparams(("arbitrary",), VMEM_LIMIT),
        name="ada",
    )(c_all, w_ada, b_ada.reshape(1, cols))


def _rms_rows(x, g):
    ms = jnp.mean(x * x, axis=-1, keepdims=True)
    return x * lax.rsqrt(ms + NORM_EPS) * g


def _mod_row(ntp, tps, nb):
    return lambda i, *_: (jnp.where(i < ntp, i // tps, nb), 0)


def _per_group(x, fn, *mods):
    x3 = x.reshape(GROUPS_PER_TILE, CHUNK, x.shape[-1])
    y3 = fn(x3, *[m[:, None, :] for m in mods])
    return y3.reshape(x.shape)


def _inproj_kernel(ntp, xp_ref, xs_ref, sh_ref, sc_ref, gn_ref, w_ref, bd_ref, gq_ref, gk_ref,
                   cos_ref, sin_ref,
                   qa_ref, ka_ref, va_ref, qb_ref, kb_ref, vb_ref, gb_ref,
                   kpt_ref, vpt_ref, kst_ref, vst_ref):
    i = pl.program_id(0)
    is_p = i < ntp
    x = jnp.where(is_p, xp_ref[...], xs_ref[...])
    y = _rms_rows(x, gn_ref[...])
    h = _per_group(y, lambda a, sh, sc: a * (1.0 + sc) + sh, sh_ref[...], sc_ref[...])
    hb = h.astype(BF16)

    def proj(s):
        return jnp.dot(hb, w_ref[:, s * GROUP_W:(s + 1) * GROUP_W], preferred_element_type=F32)

    def head_rms(z, g):
        zz = (z * z).astype(BF16)
        half = GROUP_W // 2
        ss = jnp.concatenate(
            [jnp.dot(zz[:, :half], bd_ref[...], preferred_element_type=F32),
             jnp.dot(zz[:, half:], bd_ref[...], preferred_element_type=F32)], axis=1)
        return z * lax.rsqrt(ss * (1.0 / HEAD_DIM_ATT) + NORM_EPS) * g

    cos = cos_ref[...]
    sin = sin_ref[...]

    def rot(z):
        outs = []
        for hh in range(N_HEADS_RET):
            zh = z[:, hh * HEAD_DIM_RET:(hh + 1) * HEAD_DIM_RET]
            outs.append(zh * cos + pltpu.roll(zh, HEAD_DIM_RET // 2, axis=1) * sin)
        return jnp.concatenate(outs, axis=1)

    qa_ref[...] = head_rms(proj(0), gq_ref[...]).astype(BF16)
    ka = head_rms(proj(1), gk_ref[...])
    ka_ref[...] = ka.T.astype(BF16)
    va = proj(2)
    va_ref[...] = va.astype(BF16)

    @pl.when(is_p)
    def _():
        kpt_ref[...] = ka
        vpt_ref[...] = va

    @pl.when(jnp.logical_not(is_p))
    def _():
        kst_ref[...] = ka
        vst_ref[...] = va

    qb_ref[...] = rot(proj(3)).astype(BF16)
    kb_ref[...] = (rot(proj(4)) * (HEAD_DIM_RET ** -0.5)).astype(BF16)
    vb_ref[...] = proj(5).astype(BF16)
    gb_ref[...] = proj(6).astype(BF16)


def _inproj(xp, xs, shift, scale, g_norm, w_in_b, bd, gq8, gk8, cos_t, sin_t, nb, tps):
    rp = xp.shape[0]
    ntp = rp // TM
    nt = ntp + 1
    r = rp + TM
    row = lambda i: (i, 0)
    full = lambda i: (0, 0)
    tab = lambda i: (jnp.where(i < ntp, i % tps, tps), 0)
    act = jax.ShapeDtypeStruct((r, GROUP_W), BF16)
    return pl.pallas_call(
        functools.partial(_inproj_kernel, ntp),
        grid=(nt,),
        in_specs=[pl.BlockSpec((TM, D_MODEL), lambda i: (jnp.minimum(i, ntp - 1), 0)),
                  pl.BlockSpec((TM, D_MODEL), full),
                  pl.BlockSpec((GROUPS_PER_TILE, D_MODEL), _mod_row(ntp, tps, nb)),
                  pl.BlockSpec((GROUPS_PER_TILE, D_MODEL), _mod_row(ntp, tps, nb)),
                  pl.BlockSpec((1, D_MODEL), full),
                  pl.BlockSpec((D_MODEL, N_SLOTS * GROUP_W), full),
                  pl.BlockSpec((GROUP_W // 2, GROUP_W // 2), full),
                  pl.BlockSpec((1, GROUP_W), full),
                  pl.BlockSpec((1, GROUP_W), full),
                  pl.BlockSpec((TM, HEAD_DIM_RET), tab),
                  pl.BlockSpec((TM, HEAD_DIM_RET), tab)],
        out_specs=[pl.BlockSpec((TM, GROUP_W), row), pl.BlockSpec((GROUP_W, TM), lambda i: (0, i))]
        + [pl.BlockSpec((TM, GROUP_W), row)] * 5 + [
            pl.BlockSpec((TM, GROUP_W), lambda i: (jnp.minimum(i // tps, nb - 1), 0)),
            pl.BlockSpec((TM, GROUP_W), lambda i: (jnp.minimum(i // tps, nb - 1), 0)),
            pl.BlockSpec((TM, GROUP_W), full),
            pl.BlockSpec((TM, GROUP_W), full)],
        out_shape=[act, jax.ShapeDtypeStruct((GROUP_W, r), BF16)] + [act] * 5
        + [jax.ShapeDtypeStruct((nb * TM, GROUP_W), F32)] * 2
        + [jax.ShapeDtypeStruct((TM, GROUP_W), F32)] * 2,
        compiler_params=_cparams(("arbitrary",), VMEM_LIMIT),
        name="inproj",
    )(xp, xs, shift, scale, g_norm, w_in_b, bd, gq8, gk8, cos_t, sin_t)


def _attn_heads(q, k, v, bias_ref, first_valid_col=None):
    qb_rows, kb_rows = q.shape[0], v.shape[0]
    assert qb_rows == 4 * CHUNK
    half_rows, span = qb_rows // 2, kb_rows - 2 * CHUNK
    parts = [(0, 0), (half_rows, 2 * CHUNK)]

    def softmax_part(s_full, hh, half, r0, c0):
        rs = half * qb_rows + r0
        s = s_full[rs:rs + half_rows, c0:c0 + span] + bias_ref[hh, r0:r0 + half_rows, c0:c0 + span]
        if first_valid_col is not None:
            col = lax.broadcasted_iota(I32, (half_rows, span), 1) + c0
            s = jnp.where(col >= first_valid_col, s, NEG_INF)
        m = jnp.max(s, axis=-1, keepdims=True)
        e = jnp.exp2(s - m)
        l = jnp.sum(e, axis=-1, keepdims=True)
        pad = [jnp.zeros((half_rows, c0), BF16)] if c0 else []
        pad_r = [jnp.zeros((half_rows, kb_rows - span - c0), BF16)] if kb_rows - span - c0 else []
        return jnp.concatenate(pad + [e.astype(BF16)] + pad_r, axis=1), l

    pair_w = 2 * HEAD_DIM_ATT
    low = lax.broadcasted_iota(I32, (1, pair_w), 1) < HEAD_DIM_ATT
    outs = []
    for pp in range(N_HEADS_ATT // 2):
        ps = slice(pp * pair_w, (pp + 1) * pair_w)
        q2, v2 = q[:, ps], v[:, ps]
        zero = jnp.zeros_like(q2)
        qs = jnp.concatenate([jnp.where(low, q2, zero), jnp.where(low, zero, q2)], axis=0)
        s = jnp.dot(qs, k[ps, :], preferred_element_type=F32)
        es, ls = zip(*[softmax_part(s, 2 * pp + half, half, r0, c0)
                       for half in range(2) for r0, c0 in parts])
        o = jnp.dot(jnp.concatenate(es, axis=0), v2, preferred_element_type=F32) / jnp.concatenate(ls, axis=0)
        outs.append(jnp.where(low, o[:qb_rows], o[qb_rows:]))
    return jnp.concatenate(outs, axis=1)


def _fill_band_bias(rev_ref, bias_scr):
    _, qb_rows, kb_rows = bias_scr.shape
    width = rev_ref.shape[1]
    q = lax.broadcasted_iota(I32, (qb_rows, kb_rows), 0)
    k = lax.broadcasted_iota(I32, (qb_rows, kb_rows), 1)
    qc = q >> 6
    kc = (k - ATT_WINDOW) >> 6
    band = (kc >= qc - ATT_WINDOW // CHUNK) & (kc <= qc)
    for hh in range(N_HEADS_ATT):
        rows = jnp.broadcast_to(rev_ref[hh:hh + 1, :], (qb_rows, width))
        toep = pltpu.roll(rows, width - MAX_REL, 1, stride=1, stride_axis=0)
        bias_scr[hh] = jnp.where(band, toep[:, :kb_rows] * LOG2_E, NEG_INF)


def _attn_sample_kernel(q_ref, kn_ref, vn_ref, kc_ref, vc_ref, rev_ref, o_ref, bias_scr):
    @pl.when(pl.program_id(0) == 0)
    def _():
        _fill_band_bias(rev_ref, bias_scr)

    q = q_ref[...]
    outs = []
    for hh in range(N_HEADS_ATT):
        hs = slice(hh * HEAD_DIM_ATT, (hh + 1) * HEAD_DIM_ATT)
        head_rows = pl.ds(hh, ATT_WINDOW, stride=N_HEADS_ATT)
        k = jnp.concatenate([kc_ref[0, head_rows, :], kn_ref[:, hs]], axis=0).astype(BF16)
        v = jnp.concatenate([vc_ref[0, head_rows, :], vn_ref[:, hs]], axis=0).astype(BF16)
        s = lax.dot_general(q[:, hs], k, (((1,), (1,)), ((), ())), preferred_element_type=F32)
        s = s + bias_scr[hh]
        m = jnp.max(s, axis=-1, keepdims=True)
        e = jnp.exp2(s - m)
        l = jnp.sum(e, axis=-1, keepdims=True)
        outs.append(jnp.dot(e.astype(BF16), v, preferred_element_type=F32) / l)
    o_ref[...] = jnp.concatenate(outs, axis=1).astype(BF16)


def _attn_sample(qa, ks_new, vs_new, kc, vc, rev, rp):
    ndb = kc.shape[0]
    base = rp // CHUNK
    spec = pl.BlockSpec((CHUNK, GROUP_W), lambda b: (base + b, 0))
    new = pl.BlockSpec((CHUNK, GROUP_W), lambda b: (b, 0))
    cspec = pl.BlockSpec((1, ATT_WINDOW * N_HEADS_ATT, HEAD_DIM_ATT), lambda b: (b, 0, 0))
    return pl.pallas_call(
        _attn_sample_kernel,
        grid=(ndb,),
        in_specs=[spec, new, new, cspec, cspec,
                  pl.BlockSpec(rev.shape, lambda b: (0, 0))],
        out_specs=pl.BlockSpec((CHUNK, GROUP_W), lambda b: (b, 0)),
        out_shape=jax.ShapeDtypeStruct((ndb * CHUNK, GROUP_W), BF16),
        scratch_shapes=[pltpu.VMEM((N_HEADS_ATT, CHUNK, ATT_WINDOW + CHUNK), F32)],
        compiler_params=_cparams(("arbitrary",), VMEM_LIMIT),
        name="attn_sample",
    )(qa, ks_new, vs_new, kc, vc, rev)


def _ret_chunk(state_decay, q_ref, k_ref, v_ref, g_ref, dm_ref, xi_ref, zeta_ref, gro_ref, o_ref, s_scr):
    outs = []
    for hh in range(N_HEADS_RET):
        hs = slice(hh * HEAD_DIM_RET, (hh + 1) * HEAD_DIM_RET)
        q = q_ref[:, hs]
        k = k_ref[:, hs]
        v = v_ref[:, hs]
        st = s_scr[hh]
        sc = lax.dot_general(q, k, (((1,), (1,)), ((), ())), preferred_element_type=F32) * dm_ref[hh]
        inner = jnp.dot(sc.astype(BF16), v, preferred_element_type=F32)
        cross = jnp.dot(q, st.astype(BF16), preferred_element_type=F32) * xi_ref[:, hs]
        o = inner + cross
        kz = k.astype(F32) * zeta_ref[:, hs]
        s_scr[hh] = state_decay[hh] * st + jnp.dot(kz.T.astype(BF16), v, preferred_element_type=F32)
        mu = jnp.mean(o, axis=-1, keepdims=True)
        oc = o - mu
        var = jnp.mean(oc * oc, axis=-1, keepdims=True)
        outs.append(oc * lax.rsqrt(var + NORM_EPS))
    y = jnp.concatenate(outs, axis=1) * gro_ref[...]
    g = g_ref[...].astype(F32)
    o_ref[...] = (g * jax.nn.sigmoid(g) * y).astype(BF16)


def _ret_kernel(state_decay, q_ref, k_ref, v_ref, g_ref, s0_ref, dm_ref, xi_ref, zeta_ref,
                gro_ref, o_ref, sn_ref, s_scr):
    j = pl.program_id(1)

    @pl.when(j == 0)
    def _():
        s_scr[...] = s0_ref[0]

    _ret_chunk(state_decay, q_ref, k_ref, v_ref, g_ref, dm_ref, xi_ref, zeta_ref, gro_ref, o_ref, s_scr)

    @pl.when(j == pl.num_programs(1) - 1)
    def _():
        sn_ref[0] = s_scr[...]


def _mix_prompt_kernel(state_decay, q_ref, k0_ref, k1_ref, k2_ref, v0_ref, v1_ref, v2_ref, rev_ref,
                       rq_ref, rk_ref, rv_ref, rg_ref, s0_ref, dm_ref, xi_ref, zeta_ref, gro_ref,
                       att_ref, ret_ref, sn_ref, bias_scr, s_scr):
    j = pl.program_id(1)

    @pl.when((pl.program_id(0) == 0) & (j == 0))
    def _():
        _fill_band_bias(rev_ref, bias_scr)

    @pl.when(j == 0)
    def _():
        s_scr[...] = s0_ref[0]

    k = jnp.concatenate([k0_ref[...], k1_ref[...], k2_ref[...]], axis=1)
    v = jnp.concatenate([v0_ref[...], v1_ref[...], v2_ref[...]], axis=0)

    def block(first_valid_col):
        att_ref[...] = _attn_heads(q_ref[...], k, v, bias_scr, first_valid_col).astype(BF16)
        _ret_chunk(state_decay, rq_ref, rk_ref, rv_ref, rg_ref, dm_ref, xi_ref, zeta_ref, gro_ref,
                   ret_ref, s_scr)

    @pl.when(j >= 2)
    def _():
        block(None)

    @pl.when(j < 2)
    def _():
        block((2 - j) * ATT_QB)

    @pl.when(j == pl.num_programs(1) - 1)
    def _():
        sn_ref[0] = s_scr[...]


def _mix_prompt(qa, ka_t, va, rev, qb, kb, vb, gb, s0, g_ro, nb, seq):
    assert ATT_QB == RET_CB
    r = nb * seq
    nq = seq // ATT_QB
    dm, xi, zeta, state_decay = _ret_consts(RET_CB)
    blk = lambda back: (lambda b, j: (b * nq + jnp.maximum(j - back, 0), 0))
    spec = lambda back: pl.BlockSpec((ATT_QB, GROUP_W), blk(back))
    tspec = lambda back: pl.BlockSpec((GROUP_W, ATT_QB), lambda b, j: (0, b * nq + jnp.maximum(j - back, 0)))
    sspec = pl.BlockSpec((1, N_HEADS_RET, HEAD_DIM_RET, HEAD_DIM_RET), lambda b, j: (b, 0, 0, 0))
    full2 = lambda b, j: (0, 0)
    out = jax.ShapeDtypeStruct((r, GROUP_W), BF16)
    return pl.pallas_call(
        functools.partial(_mix_prompt_kernel, state_decay),
        grid=(nb, nq),
        in_specs=[spec(0), tspec(2), tspec(1), tspec(0), spec(2), spec(1), spec(0),
                  pl.BlockSpec(rev.shape, full2),
                  spec(0), spec(0), spec(0), spec(0), sspec,
                  pl.BlockSpec(dm.shape, lambda b, j: (0, 0, 0)),
                  pl.BlockSpec(xi.shape, full2), pl.BlockSpec(zeta.shape, full2),
                  pl.BlockSpec((1, GROUP_W), full2)],
        out_specs=[spec(0), spec(0), sspec],
        out_shape=[out, out, jax.ShapeDtypeStruct(s0.shape, F32)],
        scratch_shapes=[pltpu.VMEM((N_HEADS_ATT, ATT_QB, ATT_WINDOW + ATT_QB), F32),
                        pltpu.VMEM((N_HEADS_RET, HEAD_DIM_RET, HEAD_DIM_RET), F32)],
        compiler_params=_cparams(("arbitrary", "arbitrary"), VMEM_LIMIT),
        name="mix_prompt",
    )(qa, ka_t, ka_t, ka_t, va, va, va, rev, qb, kb, vb, gb, s0, dm, xi, zeta, g_ro)


def _ret_consts(cb):
    log_g = np.log1p(-np.exp2(-RET_DECAY_OFFSET - np.arange(N_HEADS_RET, dtype=np.float64)))
    n = np.arange(cb, dtype=np.float64)
    diff = n[:, None] - n[None, :]
    dm = np.where(diff[None] >= 0, np.exp(np.maximum(diff, 0.0)[None] * log_g[:, None, None]), 0.0)
    xi = np.exp((n + 1.0)[:, None] * log_g[None, :])
    zeta = np.exp((cb - 1.0 - n)[:, None] * log_g[None, :])
    rep = lambda a: np.repeat(a, HEAD_DIM_RET, axis=1)
    state_decay = tuple(float(v) for v in np.exp(cb * log_g))
    return (jnp.asarray(dm, F32), jnp.asarray(rep(xi), F32), jnp.asarray(rep(zeta), F32), state_decay)


def _ret(qb, kb, vb, gb, s0, g_ro, cb, row0, nb, nc, name):
    dm, xi, zeta, state_decay = _ret_consts(cb)
    base = row0 // cb
    spec = pl.BlockSpec((cb, GROUP_W), lambda b, j: (base + b * nc + j, 0))
    sspec = pl.BlockSpec((1, N_HEADS_RET, HEAD_DIM_RET, HEAD_DIM_RET), lambda b, j: (b, 0, 0, 0))
    full2 = lambda b, j: (0, 0)
    return pl.pallas_call(
        functools.partial(_ret_kernel, state_decay),
        grid=(nb, nc),
        in_specs=[spec, spec, spec, spec, sspec,
                  pl.BlockSpec(dm.shape, lambda b, j: (0, 0, 0)),
                  pl.BlockSpec(xi.shape, full2), pl.BlockSpec(zeta.shape, full2),
                  pl.BlockSpec((1, GROUP_W), full2)],
        out_specs=[pl.BlockSpec((cb, GROUP_W), lambda b, j: (b * nc + j, 0)), sspec],
        out_shape=[jax.ShapeDtypeStruct((nb * nc * cb, GROUP_W), BF16),
                   jax.ShapeDtypeStruct(s0.shape, F32)],
        scratch_shapes=[pltpu.VMEM((N_HEADS_RET, HEAD_DIM_RET, HEAD_DIM_RET), F32)],
        compiler_params=_cparams(("arbitrary", "arbitrary"), VMEM_LIMIT),
        name=name,
    )(qb, kb, vb, gb, s0, dm, xi, zeta, g_ro)


def _outproj_kernel(npp, attp_ref, atts_ref, retp_ref, rets_ref, xp_ref, xs_ref, gm_ref, shf_ref, scf_ref,
                    gn_ref, wo_ref, wr_ref, br_ref, upper_ref, lower_ref,
                    x1_ref, h2_ref, slot_ref, cols_ref, cnt_ref):
    is_p = pl.program_id(0) < npp
    subs = range(2)
    rows = [slice(sub * TM, (sub + 1) * TM) for sub in subs]

    def pick(p_ref, s_ref, sub):
        return jnp.where(is_p, p_ref[rows[sub], :], s_ref[...])

    mix = [jnp.dot(pick(attp_ref, atts_ref, sub), wo_ref[:GROUP_W, :], preferred_element_type=F32)
           + jnp.dot(pick(retp_ref, rets_ref, sub), wo_ref[GROUP_W:, :], preferred_element_type=F32)
           for sub in subs]
    h2b = []
    for sub in subs:
        x1 = _per_group(mix[sub], lambda a, gm: a * gm, gm_ref[...]) + pick(xp_ref, xs_ref, sub)
        x1_ref[rows[sub], :] = x1
        y = _rms_rows(x1, gn_ref[...])
        h2 = _per_group(y, lambda a, sh, sc: a * (1.0 + sc) + sh, shf_ref[...], scf_ref[...])
        h2b.append(h2.astype(BF16))
        h2_ref[rows[sub], :] = h2b[sub]

    work = [lax.dot_general(wr_ref[...], h2b[sub], (((1,), (1,)), ((), ())),
                            preferred_element_type=F32) + br_ref[...] for sub in subs]
    eidx = lax.broadcasted_iota(I32, work[0].shape, 0).astype(F32)
    sel = [[] for _ in subs]
    top = [[] for _ in subs]
    for _ in range(TOP_K):
        for sub in subs:
            m = jnp.max(work[sub], axis=0, keepdims=True)
            idx = jnp.min(jnp.where(work[sub] == m, eidx, float(N_EXPERTS)), axis=0, keepdims=True)
            hit = eidx == idx
            sel[sub].append(hit)
            top[sub].append(m)
            work[sub] = jnp.where(hit, -jnp.inf, work[sub])

    for sub in subs:
        ex = [jnp.exp(t - top[sub][0]) for t in top[sub]]
        den = ex[0] + ex[1] + ex[2] + ex[3]
        gates = [e / den for e in ex]
        hits = sel[sub]
        multi_f = jnp.where(hits[0] | hits[1] | hits[2] | hits[3], 1.0, 0.0)
        rank = jnp.dot(multi_f.astype(BF16), upper_ref[...], preferred_element_type=F32)
        cnt = jnp.sum(multi_f, axis=1, keepdims=True)
        cnt_pad = jnp.maximum(jnp.floor((cnt + (SEG_ALIGN - 1.0)) * (1.0 / SEG_ALIGN)), 1.0) * SEG_ALIGN
        cnt_pad_b = jnp.broadcast_to(cnt_pad, (N_EXPERTS, 128))
        seg_off = jnp.dot(lower_ref[...], cnt_pad_b.astype(BF16), preferred_element_type=F32)[:, :1]
        pos = seg_off + rank
        slot_rows = jnp.concatenate(
            [jnp.sum(jnp.where(h, pos, 0.0), axis=0, keepdims=True) for h in hits], axis=0)
        gate_rows = jnp.concatenate(gates, axis=0)
        slot_ref[sub] = slot_rows.astype(I32)
        cnt_ref[sub] = cnt_pad_b.astype(I32)
        both = jnp.concatenate([slot_rows, gate_rows, jnp.zeros((128 - 2 * TOP_K, TM), F32)], axis=0)
        cols_ref[sub] = both.T


def _outproj(att_p, att_s, ret_p, ret_s, xp, xs, gate_m, shift_f, scale_f, g_norm, w_out_b, wr_t, br,
             upper, lower, nb, tps):
    rp = xp.shape[0]
    ntp = rp // TM
    assert ntp % 2 == 0 and tps % 2 == 0
    npp = ntp // 2
    nt2 = ntp + 2
    r = nt2 * TM
    row = lambda p: (p, 0)
    row3 = lambda p: (p, 0, 0)
    full = lambda p: (0, 0)
    prow = lambda p: (jnp.minimum(p, npp - 1), 0)
    mod = pl.BlockSpec((GROUPS_PER_TILE, D_MODEL), _mod_row(npp, tps // 2, nb))
    return pl.pallas_call(
        functools.partial(_outproj_kernel, npp),
        grid=(npp + 1,),
        in_specs=[pl.BlockSpec((2 * TM, GROUP_W), prow), pl.BlockSpec((TM, GROUP_W), full),
                  pl.BlockSpec((2 * TM, GROUP_W), prow), pl.BlockSpec((TM, GROUP_W), full),
                  pl.BlockSpec((2 * TM, D_MODEL), prow),
                  pl.BlockSpec((TM, D_MODEL), full),
                  mod, mod, mod,
                  pl.BlockSpec((1, D_MODEL), full),
                  pl.BlockSpec((D_MODEL, D_MODEL), full),
                  pl.BlockSpec((N_EXPERTS, D_MODEL), full),
                  pl.BlockSpec((N_EXPERTS, 1), full),
                  pl.BlockSpec((TM, TM), full),
                  pl.BlockSpec((N_EXPERTS, N_EXPERTS), full)],
        out_specs=[pl.BlockSpec((2 * TM, D_MODEL), row), pl.BlockSpec((2 * TM, D_MODEL), row),
                   pl.BlockSpec((2, TOP_K, TM), row3),
                   pl.BlockSpec((2, TM, 128), row3), pl.BlockSpec((2, N_EXPERTS, 128), row3)],
        out_shape=[jax.ShapeDtypeStruct((r, D_MODEL), F32), jax.ShapeDtypeStruct((r, D_MODEL), BF16),
                   jax.ShapeDtypeStruct((nt2, TOP_K, TM), I32),
                   jax.ShapeDtypeStruct((nt2, TM, 128), F32), jax.ShapeDtypeStruct((nt2, N_EXPERTS, 128), I32)],
        compiler_params=_cparams(("arbitrary",), VMEM_LIMIT),
        name="outproj",
    )(att_p, att_s, ret_p, ret_s, xp, xs, gate_m, shift_f, scale_f, g_norm, w_out_b, wr_t, br, upper, lower)


def _rows_copy(n, src_rows, dst_rows, sem):
    size = pl.multiple_of(n, SEG_ALIGN)
    return pltpu.make_async_copy(src_rows(size), dst_rows(size), sem)


def _start_segments(t, cnt_ref, off_ref, base_ref, local_rows, sorted_rows, sem, to_sorted):
    for e in range(N_EXPERTS):
        n = cnt_ref[t * N_EXPERTS + e]
        off = pl.multiple_of(off_ref[t * N_EXPERTS + e], SEG_ALIGN)
        base = pl.multiple_of(base_ref[t * N_EXPERTS + e], SEG_ALIGN)
        local = lambda z, off=off: local_rows(off, z)
        remote = lambda z, base=base: sorted_rows(base, z)
        (_rows_copy(n, local, remote, sem) if to_sorted else _rows_copy(n, remote, local, sem)).start()


def _dispatch_kernel(nt, n_blocks, off_ref, cnt_ref, base_ref, tot_ref, tail0_ref, tailn_ref, na_ref,
                     h2_ref, slot_ref, slotn_ref, xb_ref, xs_scr, hot_scr, zero_scr, sems, tail_sem):
    i = pl.program_id(0)
    cur = i % 2
    sorted_rows = lambda r, z: xb_ref.at[pl.ds(r, z), :]

    def start_tile(t, buf):
        _start_segments(t, cnt_ref, off_ref, base_ref, lambda r, z: xs_scr.at[buf, pl.ds(r, z), :],
                        sorted_rows, sems.at[buf], True)

    def wait_tile(t, buf):
        _rows_copy(tot_ref[t], lambda z: xs_scr.at[buf, pl.ds(0, z), :], lambda z: sorted_rows(0, z),
                   sems.at[buf]).wait()

    def tail_copies(wait):
        def body(e, c):
            base = pl.multiple_of(tail0_ref[e], SEG_ALIGN)

            @pl.when(tailn_ref[e] > 0)
            def _():
                cp = _rows_copy(tailn_ref[e], lambda z: zero_scr.at[pl.ds(0, z), :],
                                lambda z: sorted_rows(base, z), tail_sem)
                cp.wait() if wait else cp.start()
            return c
        lax.fori_loop(0, N_EXPERTS, body, 0)

        def unused(j, c):
            cp = pltpu.make_async_copy(zero_scr, sorted_rows(pl.multiple_of(j * BM, BM), BM), tail_sem)
            cp.wait() if wait else cp.start()
            return c
        lax.fori_loop(na_ref[0], n_blocks, unused, 0)

    @pl.when(i >= 2)
    def _():
        wait_tile(i - 2, cur)

    def onehot(slot):
        srow = lax.broadcasted_iota(I32, (CAP, TM), 0)
        hit = (srow == slot[0:1]) | (srow == slot[1:2]) | (srow == slot[2:3]) | (srow == slot[3:4])
        return jnp.where(hit, 1.0, 0.0).astype(BF16)

    @pl.when(i == 0)
    def _():
        hot_scr[0] = onehot(slot_ref[0])

    xs_scr[cur] = jnp.dot(hot_scr[cur], h2_ref[...], preferred_element_type=F32).astype(BF16)
    hot_scr[1 - cur] = onehot(slotn_ref[0])
    start_tile(i, cur)

    @pl.when(i == nt - 1)
    def _():
        zero_scr[...] = jnp.zeros_like(zero_scr)
        tail_copies(False)
        if nt >= 2:
            wait_tile(i - 1, 1 - cur)
        wait_tile(i, cur)
        tail_copies(True)


def _dispatch(h2, slot, off, cnt, base, tot, tail0, tailn, n_act, n_blocks):
    nt = tot.shape[0]
    n_rows = n_blocks * BM
    grid_spec = pltpu.PrefetchScalarGridSpec(
        num_scalar_prefetch=7,
        grid=(nt,),
        in_specs=[pl.BlockSpec((TM, D_MODEL), lambda i, *_: (i, 0)),
                  pl.BlockSpec((1, TOP_K, TM), lambda i, *_: (i, 0, 0)),
                  pl.BlockSpec((1, TOP_K, TM), lambda i, *_: (jnp.minimum(i + 1, nt - 1), 0, 0))],
        out_specs=pl.BlockSpec(memory_space=pl.ANY),
        scratch_shapes=[pltpu.VMEM((2, CAP, D_MODEL), BF16),
                        pltpu.VMEM((2, CAP, TM), BF16),
                        pltpu.VMEM((BM, D_MODEL), BF16),
                        pltpu.SemaphoreType.DMA((2,)),
                        pltpu.SemaphoreType.DMA(())],
    )
    return pl.pallas_call(
        functools.partial(_dispatch_kernel, nt, n_blocks),
        grid_spec=grid_spec,
        out_shape=jax.ShapeDtypeStruct((n_rows, D_MODEL), BF16),
        compiler_params=_cparams(("arbitrary",), VMEM_LIMIT),
        name="dispatch",
    )(off, cnt, base, tot, tail0, tailn, n_act, h2, slot, slot)


def _experts_kernel(be_ref, bi_ref, nx_ref, na_ref, x_ref, wu_hbm, bu_ref, wd_hbm, bd_ref, y_ref,
                    wu_stage, wd_stage, wu_scr, wd_scr, sems):
    j = pl.program_id(0)

    def weight_copies(e):
        return (pltpu.make_async_copy(wu_hbm.at[e], wu_stage, sems.at[0]),
                pltpu.make_async_copy(wd_hbm.at[e], wd_stage, sems.at[1]))

    @pl.when(j < na_ref[0])
    def _():
        e = be_ref[j]
        prev = be_ref[jnp.maximum(j - 1, 0)]

        @pl.when(j == 0)
        def _():
            for cp in weight_copies(e):
                cp.start()

        @pl.when((j == 0) | (e != prev))
        def _():
            for cp in weight_copies(e):
                cp.wait()
            wu_scr[...] = wu_stage[...].astype(BF16)
            wd_scr[...] = wd_stage[...].astype(BF16)

            @pl.when(nx_ref[j] != e)
            def _():
                for cp in weight_copies(nx_ref[j]):
                    cp.start()

        u = jnp.dot(x_ref[...], wu_scr[...], preferred_element_type=F32) + bu_ref[0]
        glu = jnp.minimum(u[:, :D_FF], SWIGLU_LIMIT)
        lin = jnp.clip(u[:, D_FF:], -SWIGLU_LIMIT, SWIGLU_LIMIT)
        act = glu * jax.nn.sigmoid(SWIGLU_ALPHA * glu) * (lin + 1.0)
        y = jnp.dot(act.astype(BF16), wd_scr[...], preferred_element_type=F32) + bd_ref[0]
        y_ref[...] = y.astype(BF16)

    @pl.when(j >= na_ref[0])
    def _():
        y_ref[...] = jnp.zeros_like(y_ref)


def _experts(xb, blk_e, blk_i, blk_nx, n_act, w_up, b_up, w_down, b_down):
    n_rows = xb.shape[0]
    nblk = n_rows // BM
    grid_spec = pltpu.PrefetchScalarGridSpec(
        num_scalar_prefetch=4,
        grid=(nblk,),
        in_specs=[pl.BlockSpec((BM, D_MODEL), lambda j, be, bi, nx, na: (bi[j], 0)),
                  pl.BlockSpec(memory_space=pl.ANY),
                  pl.BlockSpec((1, 1, 2 * D_FF), lambda j, be, bi, nx, na: (be[j], 0, 0)),
                  pl.BlockSpec(memory_space=pl.ANY),
                  pl.BlockSpec((1, 1, D_MODEL), lambda j, be, bi, nx, na: (be[j], 0, 0))],
        out_specs=pl.BlockSpec((BM, D_MODEL), lambda j, be, bi, nx, na: (j, 0)),
        scratch_shapes=[pltpu.VMEM((D_MODEL, 2 * D_FF), F32), pltpu.VMEM((D_FF, D_MODEL), F32),
                        pltpu.VMEM((D_MODEL, 2 * D_FF), BF16), pltpu.VMEM((D_FF, D_MODEL), BF16),
                        pltpu.SemaphoreType.DMA((2,))],
    )
    return pl.pallas_call(
        _experts_kernel,
        grid_spec=grid_spec,
        out_shape=jax.ShapeDtypeStruct((n_rows, D_MODEL), BF16),
        compiler_params=_cparams(("arbitrary",), VMEM_LIMIT),
        name="experts",
    )(blk_e, blk_i, blk_nx, n_act, xb, w_up, b_up.reshape(N_EXPERTS, 1, 2 * D_FF), w_down,
      b_down.reshape(N_EXPERTS, 1, D_MODEL))


def _combine_kernel(nt, ntp, off_ref, cnt_ref, base_ref, tot_ref, yb_ref, cols_ref, x1_ref, gf_ref,
                    op_ref, os_ref, ys_scr, sems):
    i = pl.program_id(0)
    cur = i % 2

    sorted_rows = lambda r, z: yb_ref.at[pl.ds(r, z), :]

    def start_tile(t, buf):
        _start_segments(t, cnt_ref, off_ref, base_ref, lambda r, z: ys_scr.at[buf, pl.ds(r, z), :],
                        sorted_rows, sems.at[buf], False)

    @pl.when(i == 0)
    def _():
        ys_scr[...] = jnp.zeros_like(ys_scr)
        start_tile(0, 0)

    def wait_tile(t, buf):
        _rows_copy(tot_ref[t], lambda z: sorted_rows(0, z), lambda z: ys_scr.at[buf, pl.ds(0, z), :],
                   sems.at[buf]).wait()

    nxt = jnp.minimum(i + 1, nt - 1)
    wait_tile(i, cur)
    start_tile(nxt, 1 - cur)

    @pl.when(i == nt - 1)
    def _():
        wait_tile(nxt, 1 - cur)

    cols = cols_ref[0]
    lane = lax.broadcasted_iota(I32, (TM, CAP), 1)
    w = jnp.zeros((TM, CAP), F32)
    for k in range(TOP_K):
        sk = cols[:, k:k + 1].astype(I32)
        gk = cols[:, TOP_K + k:TOP_K + k + 1]
        w = jnp.where(lane == sk, gk, w)
    y = jnp.dot(w.astype(BF16), ys_scr[cur], preferred_element_type=F32)
    out = x1_ref[...] + _per_group(y, lambda a, gf: a * gf, gf_ref[...])

    @pl.when(i < ntp)
    def _():
        op_ref[...] = out

    @pl.when(i >= ntp)
    def _():
        os_ref[...] = out


def _combine(yb, cols, x1, gate_f, off, cnt, base, tot, ntp, nb, tps):
    nt = tot.shape[0]
    grid_spec = pltpu.PrefetchScalarGridSpec(
        num_scalar_prefetch=4,
        grid=(nt,),
        in_specs=[pl.BlockSpec(memory_space=pl.ANY),
                  pl.BlockSpec((1, TM, 128), lambda i, *_: (i, 0, 0)),
                  pl.BlockSpec((TM, D_MODEL), lambda i, *_: (i, 0)),
                  pl.BlockSpec((GROUPS_PER_TILE, D_MODEL), _mod_row(ntp, tps, nb))],
        out_specs=[pl.BlockSpec((TM, D_MODEL), lambda i, *_: (jnp.minimum(i, ntp - 1), 0)),
                   pl.BlockSpec((TM, D_MODEL), lambda i, *_: (0, 0))],
        scratch_shapes=[pltpu.VMEM((2, CAP, D_MODEL), BF16), pltpu.SemaphoreType.DMA((2,))],
    )
    return pl.pallas_call(
        functools.partial(_combine_kernel, nt, ntp),
        grid_spec=grid_spec,
        out_shape=[jax.ShapeDtypeStruct((ntp * TM, D_MODEL), F32),
                   jax.ShapeDtypeStruct((TM, D_MODEL), F32)],
        compiler_params=_cparams(("arbitrary",), VMEM_LIMIT),
        name="combine",
    )(off, cnt, base, tot, yb, cols, x1, gate_f)


def _rotary_tables(seq, dec_batch, dec_seq):
    half = HEAD_DIM_RET // 2
    inv = ROPE_BASE ** (-np.arange(half, dtype=np.float64) / half)
    pos = np.concatenate([np.arange(seq), np.tile(PAST_LEN + np.arange(dec_seq), dec_batch)])
    ang = pos.astype(np.float64)[:, None] * inv[None, :]
    cos = np.concatenate([np.cos(ang), np.cos(ang)], axis=1)
    sin = np.concatenate([-np.sin(ang), np.sin(ang)], axis=1)
    return jnp.asarray(cos, F32), jnp.asarray(sin, F32)


def _rel_bias_reversed(rel_bias):
    heads = rel_bias.shape[0]
    ext = jnp.concatenate([rel_bias[:, 1:], jnp.broadcast_to(rel_bias[:, -1:], (heads, 2 * MAX_REL))], axis=1)
    return ext[:, ::-1].astype(F32)


def _group_mods(m, nb, ndb):
    assert ndb == GROUPS_PER_TILE
    mp = jnp.broadcast_to(m[:nb, None], (nb, GROUPS_PER_TILE) + m.shape[1:])
    allm = jnp.concatenate([mp.reshape((nb * GROUPS_PER_TILE,) + m.shape[1:]), m[nb:]], axis=0)
    return jnp.transpose(allm, (1, 0, 2))


def _routing_tables(cnt, n_blocks):
    nt = cnt.shape[0]
    off = jnp.cumsum(cnt, axis=1) - cnt
    rows_e = jnp.sum(cnt, axis=0)
    nblk_e = (rows_e + BM - 1) // BM
    blk_end = jnp.cumsum(nblk_e)
    start_e = (blk_end - nblk_e) * BM
    base = start_e[None, :] + jnp.cumsum(cnt, axis=0) - cnt
    n_act = blk_end[-1]
    j = jnp.minimum(jnp.arange(n_blocks), n_act - 1)
    blk_e = jnp.minimum(jnp.sum(blk_end[None, :] <= j[:, None], axis=1), N_EXPERTS - 1)
    later = jnp.where(blk_e[None, :] > blk_e[:, None], blk_e[None, :], N_EXPERTS)
    blk_nx = jnp.min(later, axis=1)
    blk_nx = jnp.where(blk_nx == N_EXPERTS, blk_e, blk_nx)
    tail0 = start_e + rows_e
    tailn = nblk_e * BM - rows_e
    i32 = lambda a: a.astype(I32)
    return (i32(off.reshape(nt * N_EXPERTS)), i32(cnt.reshape(nt * N_EXPERTS)),
            i32(base.reshape(nt * N_EXPERTS)), i32(jnp.sum(cnt, axis=1)), i32(tail0), i32(tailn),
            i32(blk_e), i32(j), i32(blk_nx), i32(n_act.reshape(1)))


def kernel(x_prompt, x_sample, c_prompt, c_sample, cache_att_k, cache_att_v, state_ret, w_ada, b_ada,
           g_norm_mix, g_norm_ffn, w_in, g_q, g_k, rel_bias, g_ret_out, w_out, w_router, b_router,
           w_up, b_up, w_down, b_down):
    nb, seq, d = x_prompt.shape
    ndb, dseq, _ = x_sample.shape
    assert d == D_MODEL and ndb * dseq == TM and dseq == CHUNK
    assert seq % TM == 0 and seq >= ATT_WINDOW and cache_att_k.shape[2] == ATT_WINDOW
    assert w_ada.shape[0] == 1
    rp = nb * seq
    ntp = rp // TM
    nt = ntp + 1
    tps = seq // TM

    xp = x_prompt.reshape(rp, d)
    xs = x_sample.reshape(TM, d)

    m = _ada(jnp.concatenate([c_prompt, c_sample], axis=0), w_ada[0], b_ada[0])
    mods = _group_mods(m.reshape(nb + ndb, N_ADA, d), nb, ndb)
    shift_m, scale_m, gate_m, shift_f, scale_f, gate_f = [mods[a] for a in range(N_ADA)]

    cos_t, sin_t = _rotary_tables(seq, ndb, dseq)
    bd = jnp.asarray(np.kron(np.eye(N_HEADS_ATT // 2), np.ones((HEAD_DIM_ATT, HEAD_DIM_ATT))), BF16)
    tile8 = lambda g: jnp.tile(g.astype(F32), N_HEADS_ATT).reshape(1, GROUP_W)
    (qa, ka_t, va, qb, kb, vb, gb, kp_tail, vp_tail, ks_new, vs_new) = _inproj(
        xp, xs, shift_m, scale_m, g_norm_mix[0].reshape(1, d), w_in[0].astype(BF16), bd,
        tile8(g_q[0]) * (HEAD_DIM_ATT ** -0.5 * LOG2_E), tile8(g_k[0]), cos_t, sin_t, nb, tps)

    rev = _rel_bias_reversed(rel_bias[0])
    g_ro = g_ret_out[0].astype(F32).reshape(1, GROUP_W)
    zero_state = jnp.zeros((nb, N_HEADS_RET, HEAD_DIM_RET, HEAD_DIM_RET), F32)
    att_p, ret_p, state_p = _mix_prompt(qa, ka_t, va, rev, qb, kb, vb, gb, zero_state, g_ro, nb, seq)
    att_s = _attn_sample(qa, ks_new, vs_new,
                         cache_att_k[0].reshape(ndb, ATT_WINDOW * N_HEADS_ATT, HEAD_DIM_ATT),
                         cache_att_v[0].reshape(ndb, ATT_WINDOW * N_HEADS_ATT, HEAD_DIM_ATT), rev, rp)

    ret_s, state_s = _ret(qb, kb, vb, gb, state_ret[0].astype(F32), g_ro, CHUNK, rp, ndb, 1, "ret_sample")

    upper = jnp.asarray(np.triu(np.ones((TM, TM)), 1), BF16)
    lower = jnp.asarray(np.tril(np.ones((N_EXPERTS, N_EXPERTS)), -1), BF16)
    x1, h2, slot, cols, cnt = _outproj(
        att_p, att_s, ret_p, ret_s, xp, xs, gate_m, shift_f, scale_f, g_norm_ffn[0].reshape(1, d),
        w_out[0].astype(BF16), w_router[0].T.astype(BF16), b_router[0].astype(F32).reshape(N_EXPERTS, 1),
        upper, lower, nb, tps)

    n_blocks = (TOP_K * (rp + TM) + nt * N_EXPERTS * SEG_ALIGN) // BM + 1 + N_EXPERTS
    off, cntf, base, tot, tail0, tailn, blk_e, blk_i, blk_nx, n_act = _routing_tables(cnt[:nt, :, 0], n_blocks)
    xb = _dispatch(h2, slot, off, cntf, base, tot, tail0, tailn, n_act, n_blocks)
    yb = _experts(xb, blk_e, blk_i, blk_nx, n_act, w_up[0], b_up[0], w_down[0], b_down[0])
    out_p, out_s = _combine(yb, cols, x1, gate_f, off, cntf, base, tot, ntp, nb, tps)

    heads = (N_HEADS_ATT, HEAD_DIM_ATT)
    return (out_p.reshape(nb, seq, d), out_s.reshape(ndb, dseq, d),
            kp_tail.reshape(1, nb, ATT_WINDOW, *heads), vp_tail.reshape(1, nb, ATT_WINDOW, *heads),
            state_p[None],
            ks_new.reshape(1, ndb, dseq, *heads), vs_new.reshape(1, ndb, dseq, *heads),
            state_s[None])
```

```python
import functools

import numpy as np
import jax
import jax.numpy as jnp
from jax import lax
from jax.experimental import pallas as pl
from jax.experimental.pallas import tpu as pltpu

F32 = jnp.float32
BF16 = jnp.bfloat16
I32 = jnp.int32

D_MODEL = 1024
GROUP_W = 512
N_SLOTS = 7
N_HEADS_ATT = 8
HEAD_DIM_ATT = 64
N_HEADS_RET = 4
HEAD_DIM_RET = 128
CHUNK = 64
ATT_WINDOW = 512
MAX_REL = 256
PAST_LEN = 2048
RET_DECAY_OFFSET = 5.0
ROPE_BASE = 10000.0
N_EXPERTS = 32
TOP_K = 4
D_FF = 1024
SWIGLU_LIMIT = 7.0
SWIGLU_ALPHA = 1.702
N_ADA = 6
NORM_EPS = 1e-6
NEG_INF = -1e30
LOG2_E = 1.4426950408889634

TM = 512
GROUPS_PER_TILE = TM // CHUNK
ATT_QB = 256
RET_CB = 256
SEG_ALIGN = 16
CAP = TOP_K * TM + N_EXPERTS * SEG_ALIGN
BM = 512
VMEM_LIMIT = 56 * 1024 * 1024


def _cparams(sem, vmem=None):
    return pltpu.CompilerParams(dimension_semantics=sem, vmem_limit_bytes=vmem)


def _ada_kernel(c_ref, w_ref, b_ref, o_ref):
    c = c_ref[...]
    s = c * jax.nn.sigmoid(c)
    o_ref[...] = jnp.dot(s.astype(BF16), w_ref[...].astype(BF16),
                         preferred_element_type=F32) + b_ref[...]


def _ada(c_all, w_ada, b_ada):
    n, d = c_all.shape
    cols = w_ada.shape[1]
    tn = 1536
    return pl.pallas_call(
        _ada_kernel,
        grid=(cols // tn,),
        in_specs=[pl.BlockSpec((n, d), lambda j: (0, 0)),
                  pl.BlockSpec((d, tn), lambda j: (0, j)),
                  pl.BlockSpec((1, tn), lambda j: (0, j))],
        out_specs=pl.BlockSpec((n, tn), lambda j: (0, j)),
        out_shape=jax.ShapeDtypeStruct((n, cols), F32),
        compiler_params=_cparams(("arbitrary",), VMEM_LIMIT),
        name="ada",
    )(c_all, w_ada, b_ada.reshape(1, cols))


def _rms_rows(x, g):
    ms = jnp.mean(x * x, axis=-1, keepdims=True)
    return x * lax.rsqrt(ms + NORM_EPS) * g


def _mod_row(ntp, tps, nb):
    return lambda i, *_: (jnp.where(i < ntp, i // tps, nb), 0)


def _per_group(x, fn, *mods):
    x3 = x.reshape(GROUPS_PER_TILE, CHUNK, x.shape[-1])
    y3 = fn(x3, *[m[:, None, :] for m in mods])
    return y3.reshape(x.shape)


def _inproj_kernel(ntp, xp_ref, xs_ref, sh_ref, sc_ref, gn_ref, w_ref, bd_ref, gq_ref, gk_ref,
                   cos_ref, sin_ref,
                   qa_ref, ka_ref, va_ref, qb_ref, kb_ref, vb_ref, gb_ref,
                   kpt_ref, vpt_ref, kst_ref, vst_ref):
    i = pl.program_id(0)
    is_p = i < ntp
    x = jnp.where(is_p, xp_ref[...], xs_ref[...])
    y = _rms_rows(x, gn_ref[...])
    h = _per_group(y, lambda a, sh, sc: a * (1.0 + sc) + sh, sh_ref[...], sc_ref[...])
    hb = h.astype(BF16)

    def proj(s):
        return jnp.dot(hb, w_ref[:, s * GROUP_W:(s + 1) * GROUP_W], preferred_element_type=F32)

    def head_rms(z, g):
        zz = (z * z).astype(BF16)
        half = GROUP_W // 2
        ss = jnp.concatenate(
            [jnp.dot(zz[:, :half], bd_ref[...], preferred_element_type=F32),
             jnp.dot(zz[:, half:], bd_ref[...], preferred_element_type=F32)], axis=1)
        return z * lax.rsqrt(ss * (1.0 / HEAD_DIM_ATT) + NORM_EPS) * g

    cos = cos_ref[...]
    sin = sin_ref[...]

    def rot(z):
        outs = []
        for hh in range(N_HEADS_RET):
            zh = z[:, hh * HEAD_DIM_RET:(hh + 1) * HEAD_DIM_RET]
            outs.append(zh * cos + pltpu.roll(zh, HEAD_DIM_RET // 2, axis=1) * sin)
        return jnp.concatenate(outs, axis=1)

    qa_ref[...] = head_rms(proj(0), gq_ref[...]).astype(BF16)
    ka = head_rms(proj(1), gk_ref[...])
    ka_ref[...] = ka.T.astype(BF16)
    va = proj(2)
    va_ref[...] = va.astype(BF16)

    @pl.when(is_p)
    def _():
        kpt_ref[...] = ka
        vpt_ref[...] = va

    @pl.when(jnp.logical_not(is_p))
    def _():
        kst_ref[...] = ka
        vst_ref[...] = va

    qb_ref[...] = rot(proj(3)).astype(BF16)
    kb_ref[...] = (rot(proj(4)) * (HEAD_DIM_RET ** -0.5)).astype(BF16)
    vb_ref[...] = proj(5).astype(BF16)
    gb_ref[...] = proj(6).astype(BF16)


def _inproj(xp, xs, shift, scale, g_norm, w_in_b, bd, gq8, gk8, cos_t, sin_t, nb, tps):
    rp = xp.shape[0]
    ntp = rp // TM
    nt = ntp + 1
    r = rp + TM
    row = lambda i: (i, 0)
    full = lambda i: (0, 0)
    tab = lambda i: (jnp.where(i < ntp, i % tps, tps), 0)
    tail_spec = pl.BlockSpec((TM, GROUP_W), lambda i: (jnp.minimum(i // tps, nb - 1), 0))
    act = jax.ShapeDtypeStruct((r, GROUP_W), BF16)
    return pl.pallas_call(
        functools.partial(_inproj_kernel, ntp),
        grid=(nt,),
        in_specs=[pl.BlockSpec((TM, D_MODEL), lambda i: (jnp.minimum(i, ntp - 1), 0)),
                  pl.BlockSpec((TM, D_MODEL), full),
                  pl.BlockSpec((GROUPS_PER_TILE, D_MODEL), _mod_row(ntp, tps, nb)),
                  pl.BlockSpec((GROUPS_PER_TILE, D_MODEL), _mod_row(ntp, tps, nb)),
                  pl.BlockSpec((1, D_MODEL), full),
                  pl.BlockSpec((D_MODEL, N_SLOTS * GROUP_W), full),
                  pl.BlockSpec((GROUP_W // 2, GROUP_W // 2), full),
                  pl.BlockSpec((1, GROUP_W), full),
                  pl.BlockSpec((1, GROUP_W), full),
                  pl.BlockSpec((TM, HEAD_DIM_RET), tab),
                  pl.BlockSpec((TM, HEAD_DIM_RET), tab)],
        out_specs=[pl.BlockSpec((TM, GROUP_W), row), pl.BlockSpec((GROUP_W, TM), lambda i: (0, i))]
        + [pl.BlockSpec((TM, GROUP_W), row)] * 5 + [
            tail_spec, tail_spec,
            pl.BlockSpec((TM, GROUP_W), full),
            pl.BlockSpec((TM, GROUP_W), full)],
        out_shape=[act, jax.ShapeDtypeStruct((GROUP_W, r), BF16)] + [act] * 5
        + [jax.ShapeDtypeStruct((nb * TM, GROUP_W), F32)] * 2
        + [jax.ShapeDtypeStruct((TM, GROUP_W), F32)] * 2,
        compiler_params=_cparams(("arbitrary",), VMEM_LIMIT),
        name="inproj",
    )(xp, xs, shift, scale, g_norm, w_in_b, bd, gq8, gk8, cos_t, sin_t)


def _attn_heads(q, k, v, bias_ref, first_valid_col=None):
    qb_rows, kb_rows = q.shape[0], v.shape[0]
    assert qb_rows == 4 * CHUNK
    half_rows, span = qb_rows // 2, kb_rows - 2 * CHUNK
    parts = [(0, 0), (half_rows, 2 * CHUNK)]

    def softmax_part(s_full, hh, half, r0, c0):
        rs = half * qb_rows + r0
        s = s_full[rs:rs + half_rows, c0:c0 + span] + bias_ref[hh, r0:r0 + half_rows, c0:c0 + span]
        if first_valid_col is not None:
            col = lax.broadcasted_iota(I32, (half_rows, span), 1) + c0
            s = jnp.where(col >= first_valid_col, s, NEG_INF)
        m = jnp.max(s, axis=-1, keepdims=True)
        e = jnp.exp2(s - m)
        l = jnp.sum(e, axis=-1, keepdims=True)
        pad = [jnp.zeros((half_rows, c0), BF16)] if c0 else []
        pad_r = [jnp.zeros((half_rows, kb_rows - span - c0), BF16)] if kb_rows - span - c0 else []
        return jnp.concatenate(pad + [e.astype(BF16)] + pad_r, axis=1), l

    pair_w = 2 * HEAD_DIM_ATT
    low = lax.broadcasted_iota(I32, (1, pair_w), 1) < HEAD_DIM_ATT
    outs = []
    for pp in range(N_HEADS_ATT // 2):
        ps = slice(pp * pair_w, (pp + 1) * pair_w)
        q2, v2 = q[:, ps], v[:, ps]
        zero = jnp.zeros_like(q2)
        qs = jnp.concatenate([jnp.where(low, q2, zero), jnp.where(low, zero, q2)], axis=0)
        s = jnp.dot(qs, k[ps, :], preferred_element_type=F32)
        es, ls = zip(*[softmax_part(s, 2 * pp + half, half, r0, c0)
                       for half in range(2) for r0, c0 in parts])
        o = jnp.dot(jnp.concatenate(es, axis=0), v2, preferred_element_type=F32) / jnp.concatenate(ls, axis=0)
        outs.append(jnp.where(low, o[:qb_rows], o[qb_rows:]))
    return jnp.concatenate(outs, axis=1)


def _fill_band_bias(rev_ref, bias_scr):
    _, qb_rows, kb_rows = bias_scr.shape
    width = rev_ref.shape[1]
    q = lax.broadcasted_iota(I32, (qb_rows, kb_rows), 0)
    k = lax.broadcasted_iota(I32, (qb_rows, kb_rows), 1)
    qc = q >> 6
    kc = (k - ATT_WINDOW) >> 6
    band = (kc >= qc - ATT_WINDOW // CHUNK) & (kc <= qc)
    for hh in range(N_HEADS_ATT):
        rows = jnp.broadcast_to(rev_ref[hh:hh + 1, :], (qb_rows, width))
        toep = pltpu.roll(rows, width - MAX_REL, 1, stride=1, stride_axis=0)
        bias_scr[hh] = jnp.where(band, toep[:, :kb_rows] * LOG2_E, NEG_INF)


def _attn_sample_kernel(q_ref, kn_ref, vn_ref, kc_ref, vc_ref, rev_ref, o_ref, bias_scr):
    @pl.when(pl.program_id(0) == 0)
    def _():
        _fill_band_bias(rev_ref, bias_scr)

    q = q_ref[...]
    outs = []
    for hh in range(N_HEADS_ATT):
        hs = slice(hh * HEAD_DIM_ATT, (hh + 1) * HEAD_DIM_ATT)
        k = jnp.concatenate([kc_ref[0, :, hs], kn_ref[:, hs].astype(BF16)], axis=0)
        v = jnp.concatenate([vc_ref[0, :, hs], vn_ref[:, hs].astype(BF16)], axis=0)
        s = lax.dot_general(q[:, hs], k, (((1,), (1,)), ((), ())), preferred_element_type=F32)
        s = s + bias_scr[hh]
        m = jnp.max(s, axis=-1, keepdims=True)
        e = jnp.exp2(s - m)
        l = jnp.sum(e, axis=-1, keepdims=True)
        outs.append(jnp.dot(e.astype(BF16), v, preferred_element_type=F32) / l)
    o_ref[...] = jnp.concatenate(outs, axis=1).astype(BF16)


def _attn_sample(qa, ks_new, vs_new, kc, vc, rev, rp):
    ndb = kc.shape[0]
    base = rp // CHUNK
    spec = pl.BlockSpec((CHUNK, GROUP_W), lambda b: (base + b, 0))
    new = pl.BlockSpec((CHUNK, GROUP_W), lambda b: (b, 0))
    cspec = pl.BlockSpec((1, ATT_WINDOW, GROUP_W), lambda b: (b, 0, 0))
    return pl.pallas_call(
        _attn_sample_kernel,
        grid=(ndb,),
        in_specs=[spec, new, new, cspec, cspec,
                  pl.BlockSpec(rev.shape, lambda b: (0, 0))],
        out_specs=pl.BlockSpec((CHUNK, GROUP_W), lambda b: (b, 0)),
        out_shape=jax.ShapeDtypeStruct((ndb * CHUNK, GROUP_W), BF16),
        scratch_shapes=[pltpu.VMEM((N_HEADS_ATT, CHUNK, ATT_WINDOW + CHUNK), F32)],
        compiler_params=_cparams(("arbitrary",), VMEM_LIMIT),
        name="attn_sample",
    )(qa, ks_new, vs_new, kc, vc, rev)


def _ret_chunk(state_decay, q_ref, k_ref, v_ref, g_ref, dm_ref, xi_ref, zeta_ref, gro_ref, o_ref, s_scr):
    outs = []
    for hh in range(N_HEADS_RET):
        hs = slice(hh * HEAD_DIM_RET, (hh + 1) * HEAD_DIM_RET)
        q = q_ref[:, hs]
        k = k_ref[:, hs]
        v = v_ref[:, hs]
        st = s_scr[hh]
        sc = lax.dot_general(q, k, (((1,), (1,)), ((), ())), preferred_element_type=F32) * dm_ref[hh]
        inner = jnp.dot(sc.astype(BF16), v, preferred_element_type=F32)
        cross = jnp.dot(q, st.astype(BF16), preferred_element_type=F32) * xi_ref[:, hs]
        o = inner + cross
        kz = k.astype(F32) * zeta_ref[:, hs]
        s_scr[hh] = state_decay[hh] * st + jnp.dot(kz.T.astype(BF16), v, preferred_element_type=F32)
        mu = jnp.mean(o, axis=-1, keepdims=True)
        oc = o - mu
        var = jnp.mean(oc * oc, axis=-1, keepdims=True)
        outs.append(oc * lax.rsqrt(var + NORM_EPS))
    y = jnp.concatenate(outs, axis=1) * gro_ref[...]
    g = g_ref[...].astype(F32)
    o_ref[...] = (g * jax.nn.sigmoid(g) * y).astype(BF16)


def _ret_kernel(state_decay, q_ref, k_ref, v_ref, g_ref, s0_ref, dm_ref, xi_ref, zeta_ref,
                gro_ref, o_ref, sn_ref, s_scr):
    j = pl.program_id(1)

    @pl.when(j == 0)
    def _():
        s_scr[...] = s0_ref[0]

    _ret_chunk(state_decay, q_ref, k_ref, v_ref, g_ref, dm_ref, xi_ref, zeta_ref, gro_ref, o_ref, s_scr)

    @pl.when(j == pl.num_programs(1) - 1)
    def _():
        sn_ref[0] = s_scr[...]


def _mix_prompt_kernel(state_decay, q_ref, k0_ref, k1_ref, k2_ref, v0_ref, v1_ref, v2_ref, rev_ref,
                       rq_ref, rk_ref, rv_ref, rg_ref, s0_ref, dm_ref, xi_ref, zeta_ref, gro_ref,
                       att_ref, ret_ref, sn_ref, bias_scr, s_scr):
    j = pl.program_id(1)

    @pl.when((pl.program_id(0) == 0) & (j == 0))
    def _():
        _fill_band_bias(rev_ref, bias_scr)

    @pl.when(j == 0)
    def _():
        s_scr[...] = s0_ref[0]

    k = jnp.concatenate([k0_ref[...], k1_ref[...], k2_ref[...]], axis=1)
    v = jnp.concatenate([v0_ref[...], v1_ref[...], v2_ref[...]], axis=0)

    def block(first_valid_col):
        att_ref[...] = _attn_heads(q_ref[...], k, v, bias_scr, first_valid_col).astype(BF16)
        _ret_chunk(state_decay, rq_ref, rk_ref, rv_ref, rg_ref, dm_ref, xi_ref, zeta_ref, gro_ref,
                   ret_ref, s_scr)

    @pl.when(j >= 2)
    def _():
        block(None)

    @pl.when(j < 2)
    def _():
        block((2 - j) * ATT_QB)

    @pl.when(j == pl.num_programs(1) - 1)
    def _():
        sn_ref[0] = s_scr[...]


def _mix_prompt(qa, ka_t, va, rev, qb, kb, vb, gb, s0, g_ro, nb, seq):
    assert ATT_QB == RET_CB
    r = nb * seq
    nq = seq // ATT_QB
    dm, xi, zeta, state_decay = _ret_consts(RET_CB)
    blk = lambda back: (lambda b, j: (b * nq + jnp.maximum(j - back, 0), 0))
    spec = lambda back: pl.BlockSpec((ATT_QB, GROUP_W), blk(back))
    tspec = lambda back: pl.BlockSpec((GROUP_W, ATT_QB), lambda b, j: (0, b * nq + jnp.maximum(j - back, 0)))
    sspec = pl.BlockSpec((1, N_HEADS_RET, HEAD_DIM_RET, HEAD_DIM_RET), lambda b, j: (b, 0, 0, 0))
    full2 = lambda b, j: (0, 0)
    out = jax.ShapeDtypeStruct((r, GROUP_W), BF16)
    return pl.pallas_call(
        functools.partial(_mix_prompt_kernel, state_decay),
        grid=(nb, nq),
        in_specs=[spec(0), tspec(2), tspec(1), tspec(0), spec(2), spec(1), spec(0),
                  pl.BlockSpec(rev.shape, full2),
                  spec(0), spec(0), spec(0), spec(0), sspec,
                  pl.BlockSpec(dm.shape, lambda b, j: (0, 0, 0)),
                  pl.BlockSpec(xi.shape, full2), pl.BlockSpec(zeta.shape, full2),
                  pl.BlockSpec((1, GROUP_W), full2)],
        out_specs=[spec(0), spec(0), sspec],
        out_shape=[out, out, jax.ShapeDtypeStruct(s0.shape, F32)],
        scratch_shapes=[pltpu.VMEM((N_HEADS_ATT, ATT_QB, ATT_WINDOW + ATT_QB), F32),
                        pltpu.VMEM((N_HEADS_RET, HEAD_DIM_RET, HEAD_DIM_RET), F32)],
        compiler_params=_cparams(("arbitrary", "arbitrary"), VMEM_LIMIT),
        name="mix_prompt",
    )(qa, ka_t, ka_t, ka_t, va, va, va, rev, qb, kb, vb, gb, s0, dm, xi, zeta, g_ro)


def _ret_consts(cb):
    log_g = np.log1p(-np.exp2(-RET_DECAY_OFFSET - np.arange(N_HEADS_RET, dtype=np.float64)))
    n = np.arange(cb, dtype=np.float64)
    diff = n[:, None] - n[None, :]
    dm = np.where(diff[None] >= 0, np.exp(np.maximum(diff, 0.0)[None] * log_g[:, None, None]), 0.0)
    xi = np.exp((n + 1.0)[:, None] * log_g[None, :])
    zeta = np.exp((cb - 1.0 - n)[:, None] * log_g[None, :])
    rep = lambda a: np.repeat(a, HEAD_DIM_RET, axis=1)
    state_decay = tuple(float(v) for v in np.exp(cb * log_g))
    return (jnp.asarray(dm, F32), jnp.asarray(rep(xi), F32), jnp.asarray(rep(zeta), F32), state_decay)


def _ret(qb, kb, vb, gb, s0, g_ro, cb, row0, nb, nc, name):
    dm, xi, zeta, state_decay = _ret_consts(cb)
    base = row0 // cb
    spec = pl.BlockSpec((cb, GROUP_W), lambda b, j: (base + b * nc + j, 0))
    sspec = pl.BlockSpec((1, N_HEADS_RET, HEAD_DIM_RET, HEAD_DIM_RET), lambda b, j: (b, 0, 0, 0))
    full2 = lambda b, j: (0, 0)
    return pl.pallas_call(
        functools.partial(_ret_kernel, state_decay),
        grid=(nb, nc),
        in_specs=[spec, spec, spec, spec, sspec,
                  pl.BlockSpec(dm.shape, lambda b, j: (0, 0, 0)),
                  pl.BlockSpec(xi.shape, full2), pl.BlockSpec(zeta.shape, full2),
                  pl.BlockSpec((1, GROUP_W), full2)],
        out_specs=[pl.BlockSpec((cb, GROUP_W), lambda b, j: (b * nc + j, 0)), sspec],
        out_shape=[jax.ShapeDtypeStruct((nb * nc * cb, GROUP_W), BF16),
                   jax.ShapeDtypeStruct(s0.shape, F32)],
        scratch_shapes=[pltpu.VMEM((N_HEADS_RET, HEAD_DIM_RET, HEAD_DIM_RET), F32)],
        compiler_params=_cparams(("arbitrary", "arbitrary"), VMEM_LIMIT),
        name=name,
    )(qb, kb, vb, gb, s0, dm, xi, zeta, g_ro)


def _outproj_kernel(npp, attp_ref, atts_ref, retp_ref, rets_ref, xp_ref, xs_ref, gm_ref, shf_ref, scf_ref,
                    gn_ref, wo_ref, wr_ref, br_ref, upper_ref, lower_ref,
                    x1_ref, h2_ref, slot_ref, cols_ref, cnt_ref):
    is_p = pl.program_id(0) < npp
    subs = range(2)
    rows = [slice(sub * TM, (sub + 1) * TM) for sub in subs]

    def pick(p_ref, s_ref, sub):
        return jnp.where(is_p, p_ref[rows[sub], :], s_ref[...])

    mix = [jnp.dot(pick(attp_ref, atts_ref, sub), wo_ref[:GROUP_W, :], preferred_element_type=F32)
           + jnp.dot(pick(retp_ref, rets_ref, sub), wo_ref[GROUP_W:, :], preferred_element_type=F32)
           for sub in subs]
    h2b = []
    for sub in subs:
        x1 = _per_group(mix[sub], lambda a, gm: a * gm, gm_ref[...]) + pick(xp_ref, xs_ref, sub)
        x1_ref[rows[sub], :] = x1
        y = _rms_rows(x1, gn_ref[...])
        h2 = _per_group(y, lambda a, sh, sc: a * (1.0 + sc) + sh, shf_ref[...], scf_ref[...])
        h2b.append(h2.astype(BF16))
        h2_ref[rows[sub], :] = h2b[sub]

    work = [lax.dot_general(wr_ref[...], h2b[sub], (((1,), (1,)), ((), ())),
                            preferred_element_type=F32) + br_ref[...] for sub in subs]
    eidx = lax.broadcasted_iota(I32, work[0].shape, 0).astype(F32)
    sel = [[] for _ in subs]
    top = [[] for _ in subs]
    for _ in range(TOP_K):
        for sub in subs:
            m = jnp.max(work[sub], axis=0, keepdims=True)
            idx = jnp.min(jnp.where(work[sub] == m, eidx, float(N_EXPERTS)), axis=0, keepdims=True)
            hit = eidx == idx
            sel[sub].append(hit)
            top[sub].append(m)
            work[sub] = jnp.where(hit, -jnp.inf, work[sub])

    for sub in subs:
        ex = [jnp.exp(t - top[sub][0]) for t in top[sub]]
        den = ex[0] + ex[1] + ex[2] + ex[3]
        gates = [e / den for e in ex]
        hits = sel[sub]
        multi_f = jnp.where(hits[0] | hits[1] | hits[2] | hits[3], 1.0, 0.0)
        rank = jnp.dot(multi_f.astype(BF16), upper_ref[...], preferred_element_type=F32)
        cnt = jnp.sum(multi_f, axis=1, keepdims=True)
        cnt_pad = jnp.maximum(jnp.floor((cnt + (SEG_ALIGN - 1.0)) * (1.0 / SEG_ALIGN)), 1.0) * SEG_ALIGN
        cnt_pad_b = jnp.broadcast_to(cnt_pad, (N_EXPERTS, 128))
        seg_off = jnp.dot(lower_ref[...], cnt_pad_b.astype(BF16), preferred_element_type=F32)[:, :1]
        pos = seg_off + rank
        slot_rows = jnp.concatenate(
            [jnp.sum(jnp.where(h, pos, 0.0), axis=0, keepdims=True) for h in hits], axis=0)
        gate_rows = jnp.concatenate(gates, axis=0)
        slot_ref[sub] = slot_rows.astype(I32)
        cnt_ref[sub] = cnt_pad_b.astype(I32)
        both = jnp.concatenate([slot_rows, gate_rows, jnp.zeros((128 - 2 * TOP_K, TM), F32)], axis=0)
        cols_ref[sub] = both.T


def _outproj(att_p, att_s, ret_p, ret_s, xp, xs, gate_m, shift_f, scale_f, g_norm, w_out_b, wr_t, br,
             upper, lower, nb, tps):
    rp = xp.shape[0]
    ntp = rp // TM
    assert ntp % 2 == 0 and tps % 2 == 0
    npp = ntp // 2
    nt2 = ntp + 2
    r = nt2 * TM
    row = lambda p: (p, 0)
    row3 = lambda p: (p, 0, 0)
    full = lambda p: (0, 0)
    prow = lambda p: (jnp.minimum(p, npp - 1), 0)
    mod = pl.BlockSpec((GROUPS_PER_TILE, D_MODEL), _mod_row(npp, tps // 2, nb))
    return pl.pallas_call(
        functools.partial(_outproj_kernel, npp),
        grid=(npp + 1,),
        in_specs=[pl.BlockSpec((2 * TM, GROUP_W), prow), pl.BlockSpec((TM, GROUP_W), full),
                  pl.BlockSpec((2 * TM, GROUP_W), prow), pl.BlockSpec((TM, GROUP_W), full),
                  pl.BlockSpec((2 * TM, D_MODEL), prow),
                  pl.BlockSpec((TM, D_MODEL), full),
                  mod, mod, mod,
                  pl.BlockSpec((1, D_MODEL), full),
                  pl.BlockSpec((D_MODEL, D_MODEL), full),
                  pl.BlockSpec((N_EXPERTS, D_MODEL), full),
                  pl.BlockSpec((N_EXPERTS, 1), full),
                  pl.BlockSpec((TM, TM), full),
                  pl.BlockSpec((N_EXPERTS, N_EXPERTS), full)],
        out_specs=[pl.BlockSpec((2 * TM, D_MODEL), row), pl.BlockSpec((2 * TM, D_MODEL), row),
                   pl.BlockSpec((2, TOP_K, TM), row3),
                   pl.BlockSpec((2, TM, 128), row3), pl.BlockSpec((2, N_EXPERTS, 128), row3)],
        out_shape=[jax.ShapeDtypeStruct((r, D_MODEL), F32), jax.ShapeDtypeStruct((r, D_MODEL), BF16),
                   jax.ShapeDtypeStruct((nt2, TOP_K, TM), I32),
                   jax.ShapeDtypeStruct((nt2, TM, 128), F32), jax.ShapeDtypeStruct((nt2, N_EXPERTS, 128), I32)],
        compiler_params=_cparams(("arbitrary",), VMEM_LIMIT),
        name="outproj",
    )(att_p, att_s, ret_p, ret_s, xp, xs, gate_m, shift_f, scale_f, g_norm, w_out_b, wr_t, br, upper, lower)


def _rows_copy(n, src_rows, dst_rows, sem):
    size = pl.multiple_of(n, SEG_ALIGN)
    return pltpu.make_async_copy(src_rows(size), dst_rows(size), sem)


def _start_segments(t, cnt_ref, off_ref, base_ref, local_rows, sorted_rows, sem, to_sorted):
    for e in range(N_EXPERTS):
        n = cnt_ref[t * N_EXPERTS + e]
        off = pl.multiple_of(off_ref[t * N_EXPERTS + e], SEG_ALIGN)
        base = pl.multiple_of(base_ref[t * N_EXPERTS + e], SEG_ALIGN)
        local = lambda z, off=off: local_rows(off, z)
        remote = lambda z, base=base: sorted_rows(base, z)
        (_rows_copy(n, local, remote, sem) if to_sorted else _rows_copy(n, remote, local, sem)).start()


def _dispatch_kernel(nt, n_blocks, off_ref, cnt_ref, base_ref, tot_ref, tail0_ref, tailn_ref, na_ref,
                     h2_ref, slot_ref, slotn_ref, xb_ref, xs_scr, hot_scr, zero_scr, sems, tail_sem):
    i = pl.program_id(0)
    cur = i % 2
    sorted_rows = lambda r, z: xb_ref.at[pl.ds(r, z), :]

    def start_tile(t, buf):
        _start_segments(t, cnt_ref, off_ref, base_ref, lambda r, z: xs_scr.at[buf, pl.ds(r, z), :],
                        sorted_rows, sems.at[buf], True)

    def wait_tile(t, buf):
        _rows_copy(tot_ref[t], lambda z: xs_scr.at[buf, pl.ds(0, z), :], lambda z: sorted_rows(0, z),
                   sems.at[buf]).wait()

    def tail_copies(wait):
        def body(e, c):
            base = pl.multiple_of(tail0_ref[e], SEG_ALIGN)

            @pl.when(tailn_ref[e] > 0)
            def _():
                cp = _rows_copy(tailn_ref[e], lambda z: zero_scr.at[pl.ds(0, z), :],
                                lambda z: sorted_rows(base, z), tail_sem)
                cp.wait() if wait else cp.start()
            return c
        lax.fori_loop(0, N_EXPERTS, body, 0)

        def unused(j, c):
            cp = pltpu.make_async_copy(zero_scr, sorted_rows(pl.multiple_of(j * BM, BM), BM), tail_sem)
            cp.wait() if wait else cp.start()
            return c
        lax.fori_loop(na_ref[0], n_blocks, unused, 0)

    @pl.when(i >= 2)
    def _():
        wait_tile(i - 2, cur)

    def onehot(slot):
        srow = lax.broadcasted_iota(I32, (CAP, TM), 0)
        hit = (srow == slot[0:1]) | (srow == slot[1:2]) | (srow == slot[2:3]) | (srow == slot[3:4])
        return jnp.where(hit, 1.0, 0.0).astype(BF16)

    @pl.when(i == 0)
    def _():
        hot_scr[0] = onehot(slot_ref[0])
        zero_scr[...] = jnp.zeros_like(zero_scr)
        tail_copies(False)

    xs_scr[cur] = jnp.dot(hot_scr[cur], h2_ref[...], preferred_element_type=F32).astype(BF16)
    hot_scr[1 - cur] = onehot(slotn_ref[0])
    start_tile(i, cur)

    @pl.when(i == nt - 1)
    def _():
        if nt >= 2:
            wait_tile(i - 1, 1 - cur)
        wait_tile(i, cur)
        tail_copies(True)


def _dispatch(h2, slot, off, cnt, base, tot, tail0, tailn, n_act, n_blocks):
    nt = tot.shape[0]
    n_rows = n_blocks * BM
    grid_spec = pltpu.PrefetchScalarGridSpec(
        num_scalar_prefetch=7,
        grid=(nt,),
        in_specs=[pl.BlockSpec((TM, D_MODEL), lambda i, *_: (i, 0)),
                  pl.BlockSpec((1, TOP_K, TM), lambda i, *_: (i, 0, 0)),
                  pl.BlockSpec((1, TOP_K, TM), lambda i, *_: (jnp.minimum(i + 1, nt - 1), 0, 0))],
        out_specs=pl.BlockSpec(memory_space=pl.ANY),
        scratch_shapes=[pltpu.VMEM((2, CAP, D_MODEL), BF16),
                        pltpu.VMEM((2, CAP, TM), BF16),
                        pltpu.VMEM((BM, D_MODEL), BF16),
                        pltpu.SemaphoreType.DMA((2,)),
                        pltpu.SemaphoreType.DMA(())],
    )
    return pl.pallas_call(
        functools.partial(_dispatch_kernel, nt, n_blocks),
        grid_spec=grid_spec,
        out_shape=jax.ShapeDtypeStruct((n_rows, D_MODEL), BF16),
        compiler_params=_cparams(("arbitrary",), VMEM_LIMIT),
        name="dispatch",
    )(off, cnt, base, tot, tail0, tailn, n_act, h2, slot, slot)


def _experts_kernel(be_ref, bi_ref, nx_ref, na_ref, x_ref, wu_hbm, bu_ref, wd_hbm, bd_ref, y_ref,
                    wu_stage, wd_stage, wu_scr, wd_scr, sems):
    j = pl.program_id(0)

    def weight_copies(e):
        return (pltpu.make_async_copy(wu_hbm.at[e], wu_stage, sems.at[0]),
                pltpu.make_async_copy(wd_hbm.at[e], wd_stage, sems.at[1]))

    @pl.when(j < na_ref[0])
    def _():
        e = be_ref[j]
        prev = be_ref[jnp.maximum(j - 1, 0)]

        @pl.when(j == 0)
        def _():
            for cp in weight_copies(e):
                cp.start()

        @pl.when((j == 0) | (e != prev))
        def _():
            for cp in weight_copies(e):
                cp.wait()
            wu_scr[...] = wu_stage[...].astype(BF16)
            wd_scr[...] = wd_stage[...].astype(BF16)

            @pl.when(nx_ref[j] != e)
            def _():
                for cp in weight_copies(nx_ref[j]):
                    cp.start()

        u = jnp.dot(x_ref[...], wu_scr[...], preferred_element_type=F32) + bu_ref[0]
        glu = jnp.minimum(u[:, :D_FF], SWIGLU_LIMIT)
        lin = jnp.clip(u[:, D_FF:], -SWIGLU_LIMIT, SWIGLU_LIMIT)
        act = glu * jax.nn.sigmoid(SWIGLU_ALPHA * glu) * (lin + 1.0)
        y = jnp.dot(act.astype(BF16), wd_scr[...], preferred_element_type=F32) + bd_ref[0]
        y_ref[...] = y.astype(BF16)


def _experts(xb, blk_e, blk_i, blk_nx, n_act, w_up, b_up, w_down, b_down):
    n_rows = xb.shape[0]
    nblk = n_rows // BM
    grid_spec = pltpu.PrefetchScalarGridSpec(
        num_scalar_prefetch=4,
        grid=(nblk,),
        in_specs=[pl.BlockSpec((BM, D_MODEL), lambda j, be, bi, nx, na: (bi[j], 0)),
                  pl.BlockSpec(memory_space=pl.ANY),
                  pl.BlockSpec((1, 1, 2 * D_FF), lambda j, be, bi, nx, na: (be[j], 0, 0)),
                  pl.BlockSpec(memory_space=pl.ANY),
                  pl.BlockSpec((1, 1, D_MODEL), lambda j, be, bi, nx, na: (be[j], 0, 0))],
        out_specs=pl.BlockSpec((BM, D_MODEL), lambda j, be, bi, nx, na: (bi[j], 0)),
        scratch_shapes=[pltpu.VMEM((D_MODEL, 2 * D_FF), F32), pltpu.VMEM((D_FF, D_MODEL), F32),
                        pltpu.VMEM((D_MODEL, 2 * D_FF), BF16), pltpu.VMEM((D_FF, D_MODEL), BF16),
                        pltpu.SemaphoreType.DMA((2,))],
    )
    return pl.pallas_call(
        _experts_kernel,
        grid_spec=grid_spec,
        out_shape=jax.ShapeDtypeStruct((n_rows, D_MODEL), BF16),
        input_output_aliases={4: 0},
        compiler_params=_cparams(("arbitrary",), VMEM_LIMIT),
        name="experts",
    )(blk_e, blk_i, blk_nx, n_act, xb, w_up, b_up.reshape(N_EXPERTS, 1, 2 * D_FF), w_down,
      b_down.reshape(N_EXPERTS, 1, D_MODEL))


def _combine_kernel(nt, ntp, off_ref, cnt_ref, base_ref, tot_ref, yb_ref, cols_ref, x1_ref, gf_ref,
                    op_ref, os_ref, ys_scr, sems):
    i = pl.program_id(0)
    cur = i % 2

    sorted_rows = lambda r, z: yb_ref.at[pl.ds(r, z), :]

    def start_tile(t, buf):
        _start_segments(t, cnt_ref, off_ref, base_ref, lambda r, z: ys_scr.at[buf, pl.ds(r, z), :],
                        sorted_rows, sems.at[buf], False)

    @pl.when(i == 0)
    def _():
        ys_scr[...] = jnp.zeros_like(ys_scr)
        start_tile(0, 0)

    def wait_tile(t, buf):
        _rows_copy(tot_ref[t], lambda z: sorted_rows(0, z), lambda z: ys_scr.at[buf, pl.ds(0, z), :],
                   sems.at[buf]).wait()

    nxt = jnp.minimum(i + 1, nt - 1)
    wait_tile(i, cur)
    start_tile(nxt, 1 - cur)

    @pl.when(i == nt - 1)
    def _():
        wait_tile(nxt, 1 - cur)

    cols = cols_ref[0]
    lane = lax.broadcasted_iota(I32, (TM, CAP), 1)
    w = jnp.zeros((TM, CAP), F32)
    for k in range(TOP_K):
        sk = cols[:, k:k + 1].astype(I32)
        gk = cols[:, TOP_K + k:TOP_K + k + 1]
        w = jnp.where(lane == sk, gk, w)
    y = jnp.dot(w.astype(BF16), ys_scr[cur], preferred_element_type=F32)
    out = x1_ref[...] + _per_group(y, lambda a, gf: a * gf, gf_ref[...])

    @pl.when(i < ntp)
    def _():
        op_ref[...] = out

    @pl.when(i >= ntp)
    def _():
        os_ref[...] = out


def _combine(yb, cols, x1, gate_f, off, cnt, base, tot, ntp, nb, tps):
    nt = tot.shape[0]
    grid_spec = pltpu.PrefetchScalarGridSpec(
        num_scalar_prefetch=4,
        grid=(nt,),
        in_specs=[pl.BlockSpec(memory_space=pl.ANY),
                  pl.BlockSpec((1, TM, 128), lambda i, *_: (i, 0, 0)),
                  pl.BlockSpec((TM, D_MODEL), lambda i, *_: (i, 0)),
                  pl.BlockSpec((GROUPS_PER_TILE, D_MODEL), _mod_row(ntp, tps, nb))],
        out_specs=[pl.BlockSpec((TM, D_MODEL), lambda i, *_: (jnp.minimum(i, ntp - 1), 0)),
                   pl.BlockSpec((TM, D_MODEL), lambda i, *_: (0, 0))],
        scratch_shapes=[pltpu.VMEM((2, CAP, D_MODEL), BF16), pltpu.SemaphoreType.DMA((2,))],
    )
    return pl.pallas_call(
        functools.partial(_combine_kernel, nt, ntp),
        grid_spec=grid_spec,
        out_shape=[jax.ShapeDtypeStruct((ntp * TM, D_MODEL), F32),
                   jax.ShapeDtypeStruct((TM, D_MODEL), F32)],
        compiler_params=_cparams(("arbitrary",), VMEM_LIMIT),
        name="combine",
    )(off, cnt, base, tot, yb, cols, x1, gate_f)


def _rotary_tables(seq, dec_batch, dec_seq):
    half = HEAD_DIM_RET // 2
    inv = ROPE_BASE ** (-np.arange(half, dtype=np.float64) / half)
    pos = np.concatenate([np.arange(seq), np.tile(PAST_LEN + np.arange(dec_seq), dec_batch)])
    ang = pos.astype(np.float64)[:, None] * inv[None, :]
    cos = np.concatenate([np.cos(ang), np.cos(ang)], axis=1)
    sin = np.concatenate([-np.sin(ang), np.sin(ang)], axis=1)
    return jnp.asarray(cos, F32), jnp.asarray(sin, F32)


def _rel_bias_reversed(rel_bias):
    heads = rel_bias.shape[0]
    ext = jnp.concatenate([rel_bias[:, 1:], jnp.broadcast_to(rel_bias[:, -1:], (heads, 2 * MAX_REL))], axis=1)
    return ext[:, ::-1].astype(F32)


def _group_mods(m, nb, ndb):
    assert ndb == GROUPS_PER_TILE
    mp = jnp.broadcast_to(m[:nb, None], (nb, GROUPS_PER_TILE) + m.shape[1:])
    allm = jnp.concatenate([mp.reshape((nb * GROUPS_PER_TILE,) + m.shape[1:]), m[nb:]], axis=0)
    return jnp.transpose(allm, (1, 0, 2))


def _routing_tables(cnt, n_blocks):
    nt = cnt.shape[0]
    off = jnp.cumsum(cnt, axis=1) - cnt
    rows_e = jnp.sum(cnt, axis=0)
    nblk_e = (rows_e + BM - 1) // BM
    blk_end = jnp.cumsum(nblk_e)
    start_e = (blk_end - nblk_e) * BM
    base = start_e[None, :] + jnp.cumsum(cnt, axis=0) - cnt
    n_act = blk_end[-1]
    j = jnp.minimum(jnp.arange(n_blocks), n_act - 1)
    blk_e = jnp.minimum(jnp.sum(blk_end[None, :] <= j[:, None], axis=1), N_EXPERTS - 1)
    later = jnp.where(blk_e[None, :] > blk_e[:, None], blk_e[None, :], N_EXPERTS)
    blk_nx = jnp.min(later, axis=1)
    blk_nx = jnp.where(blk_nx == N_EXPERTS, blk_e, blk_nx)
    tail0 = start_e + rows_e
    tailn = nblk_e * BM - rows_e
    i32 = lambda a: a.astype(I32)
    return (i32(off.reshape(nt * N_EXPERTS)), i32(cnt.reshape(nt * N_EXPERTS)),
            i32(base.reshape(nt * N_EXPERTS)), i32(jnp.sum(cnt, axis=1)), i32(tail0), i32(tailn),
            i32(blk_e), i32(j), i32(blk_nx), i32(n_act.reshape(1)))


def kernel(x_prompt, x_sample, c_prompt, c_sample, cache_att_k, cache_att_v, state_ret, w_ada, b_ada,
           g_norm_mix, g_norm_ffn, w_in, g_q, g_k, rel_bias, g_ret_out, w_out, w_router, b_router,
           w_up, b_up, w_down, b_down):
    nb, seq, d = x_prompt.shape
    ndb, dseq, _ = x_sample.shape
    assert d == D_MODEL and ndb * dseq == TM and dseq == CHUNK
    assert seq % TM == 0 and seq >= ATT_WINDOW and cache_att_k.shape[2] == ATT_WINDOW
    assert w_ada.shape[0] == 1
    rp = nb * seq
    ntp = rp // TM
    nt = ntp + 1
    tps = seq // TM

    xp = x_prompt.reshape(rp, d)
    xs = x_sample.reshape(TM, d)

    m = _ada(jnp.concatenate([c_prompt, c_sample], axis=0), w_ada[0], b_ada[0])
    mods = _group_mods(m.reshape(nb + ndb, N_ADA, d), nb, ndb)
    shift_m, scale_m, gate_m, shift_f, scale_f, gate_f = [mods[a] for a in range(N_ADA)]

    cos_t, sin_t = _rotary_tables(seq, ndb, dseq)
    bd = jnp.asarray(np.kron(np.eye(N_HEADS_ATT // 2), np.ones((HEAD_DIM_ATT, HEAD_DIM_ATT))), BF16)
    tile8 = lambda g: jnp.tile(g.astype(F32), N_HEADS_ATT).reshape(1, GROUP_W)
    (qa, ka_t, va, qb, kb, vb, gb, kp_tail, vp_tail, ks_new, vs_new) = _inproj(
        xp, xs, shift_m, scale_m, g_norm_mix[0].reshape(1, d), w_in[0].astype(BF16), bd,
        tile8(g_q[0]) * (HEAD_DIM_ATT ** -0.5 * LOG2_E), tile8(g_k[0]), cos_t, sin_t, nb, tps)

    rev = _rel_bias_reversed(rel_bias[0])
    g_ro = g_ret_out[0].astype(F32).reshape(1, GROUP_W)
    zero_state = jnp.zeros((nb, N_HEADS_RET, HEAD_DIM_RET, HEAD_DIM_RET), F32)
    att_p, ret_p, state_p = _mix_prompt(qa, ka_t, va, rev, qb, kb, vb, gb, zero_state, g_ro, nb, seq)
    att_s = _attn_sample(qa, ks_new, vs_new,
                         cache_att_k[0].reshape(ndb, ATT_WINDOW, GROUP_W).astype(BF16),
                         cache_att_v[0].reshape(ndb, ATT_WINDOW, GROUP_W).astype(BF16), rev, rp)

    ret_s, state_s = _ret(qb, kb, vb, gb, state_ret[0].astype(F32), g_ro, CHUNK, rp, ndb, 1, "ret_sample")

    upper = jnp.asarray(np.triu(np.ones((TM, TM)), 1), BF16)
    lower = jnp.asarray(np.tril(np.ones((N_EXPERTS, N_EXPERTS)), -1), BF16)
    x1, h2, slot, cols, cnt = _outproj(
        att_p, att_s, ret_p, ret_s, xp, xs, gate_m, shift_f, scale_f, g_norm_ffn[0].reshape(1, d),
        w_out[0].astype(BF16), w_router[0].T.astype(BF16), b_router[0].astype(F32).reshape(N_EXPERTS, 1),
        upper, lower, nb, tps)

    n_blocks = (TOP_K * (rp + TM) + nt * N_EXPERTS * SEG_ALIGN) // BM + 1 + N_EXPERTS
    off, cntf, base, tot, tail0, tailn, blk_e, blk_i, blk_nx, n_act = _routing_tables(cnt[:nt, :, 0], n_blocks)
    xb = _dispatch(h2, slot, off, cntf, base, tot, tail0, tailn, n_act, n_blocks)
    yb = _experts(xb, blk_e, blk_i, blk_nx, n_act, w_up[0], b_up[0], w_down[0], b_down[0])
    out_p, out_s = _combine(yb, cols, x1, gate_f, off, cntf, base, tot, ntp, nb, tps)

    heads = (N_HEADS_ATT, HEAD_DIM_ATT)
    return (out_p.reshape(nb, seq, d), out_s.reshape(ndb, dseq, d),
            kp_tail.reshape(1, nb, ATT_WINDOW, *heads), vp_tail.reshape(1, nb, ATT_WINDOW, *heads),
            state_p[None],
            ks_new.reshape(1, ndb, dseq, *heads), vs_new.reshape(1, ndb, dseq, *heads),
            state_s[None])
```

```python
import functools

import numpy as np
import jax
import jax.numpy as jnp
from jax import lax
from jax.experimental import pallas as pl
from jax.experimental.pallas import tpu as pltpu

F32 = jnp.float32
BF16 = jnp.bfloat16
I32 = jnp.int32

D_MODEL = 1024
GROUP_W = 512
N_SLOTS = 7
N_HEADS_ATT = 8
HEAD_DIM_ATT = 64
N_HEADS_RET = 4
HEAD_DIM_RET = 128
CHUNK = 64
ATT_WINDOW = 512
MAX_REL = 256
PAST_LEN = 2048
RET_DECAY_OFFSET = 5.0
ROPE_BASE = 10000.0
N_EXPERTS = 32
TOP_K = 4
D_FF = 1024
SWIGLU_LIMIT = 7.0
SWIGLU_ALPHA = 1.702
N_ADA = 6
NORM_EPS = 1e-6
NEG_INF = -1e30
LOG2_E = 1.4426950408889634

TM = 512
GROUPS_PER_TILE = TM // CHUNK
ATT_QB = 256
RET_CB = 256
SEG_ALIGN = 16
CAP = TOP_K * TM + N_EXPERTS * SEG_ALIGN
BM = 512
VMEM_LIMIT = 56 * 1024 * 1024


def _cparams(sem, vmem=None):
    return pltpu.CompilerParams(dimension_semantics=sem, vmem_limit_bytes=vmem)


def _ada_kernel(c_ref, w_ref, b_ref, o_ref):
    c = c_ref[...]
    s = c * jax.nn.sigmoid(c)
    o_ref[...] = jnp.dot(s.astype(BF16), w_ref[...].astype(BF16),
                         preferred_element_type=F32) + b_ref[...]


def _ada(c_all, w_ada, b_ada):
    n, d = c_all.shape
    cols = w_ada.shape[1]
    tn = 1536
    return pl.pallas_call(
        _ada_kernel,
        grid=(cols // tn,),
        in_specs=[pl.BlockSpec((n, d), lambda j: (0, 0)),
                  pl.BlockSpec((d, tn), lambda j: (0, j)),
                  pl.BlockSpec((1, tn), lambda j: (0, j))],
        out_specs=pl.BlockSpec((n, tn), lambda j: (0, j)),
        out_shape=jax.ShapeDtypeStruct((n, cols), F32),
        compiler_params=_cparams(("arbitrary",), VMEM_LIMIT),
        name="ada",
    )(c_all, w_ada, b_ada.reshape(1, cols))


def _rms_rows(x, g):
    ms = jnp.mean(x * x, axis=-1, keepdims=True)
    return x * lax.rsqrt(ms + NORM_EPS) * g


def _mod_row(ntp, tps, nb):
    return lambda i, *_: (jnp.where(i < ntp, i // tps, nb), 0)


def _per_group(x, fn, *mods):
    x3 = x.reshape(GROUPS_PER_TILE, CHUNK, x.shape[-1])
    y3 = fn(x3, *[m[:, None, :] for m in mods])
    return y3.reshape(x.shape)


def _inproj_kernel(ntp, xp_ref, xs_ref, sh_ref, sc_ref, gn_ref, w_ref, bd_ref, gq_ref, gk_ref,
                   cos_ref, sin_ref,
                   qa_ref, ka_ref, va_ref, qb_ref, kb_ref, vb_ref, gb_ref,
                   kpt_ref, vpt_ref, kst_ref, vst_ref):
    i = pl.program_id(0)
    is_p = i < ntp
    x = jnp.where(is_p, xp_ref[...], xs_ref[...])
    y = _rms_rows(x, gn_ref[...])
    h = _per_group(y, lambda a, sh, sc: a * (1.0 + sc) + sh, sh_ref[...], sc_ref[...])
    hb = h.astype(BF16)

    def proj(s):
        return jnp.dot(hb, w_ref[:, s * GROUP_W:(s + 1) * GROUP_W], preferred_element_type=F32)

    def head_rms(z, g):
        zz = (z * z).astype(BF16)
        half = GROUP_W // 2
        ss = jnp.concatenate(
            [jnp.dot(zz[:, :half], bd_ref[...], preferred_element_type=F32),
             jnp.dot(zz[:, half:], bd_ref[...], preferred_element_type=F32)], axis=1)
        return z * lax.rsqrt(ss * (1.0 / HEAD_DIM_ATT) + NORM_EPS) * g

    cos = cos_ref[...]
    sin = sin_ref[...]

    def rot(z):
        outs = []
        for hh in range(N_HEADS_RET):
            zh = z[:, hh * HEAD_DIM_RET:(hh + 1) * HEAD_DIM_RET]
            outs.append(zh * cos + pltpu.roll(zh, HEAD_DIM_RET // 2, axis=1) * sin)
        return jnp.concatenate(outs, axis=1)

    qa_ref[...] = head_rms(proj(0), gq_ref[...]).astype(BF16)
    ka = head_rms(proj(1), gk_ref[...])
    ka_ref[...] = ka.T.astype(BF16)
    va = proj(2)
    va_ref[...] = va.astype(BF16)

    @pl.when(is_p)
    def _():
        kpt_ref[...] = ka
        vpt_ref[...] = va

    @pl.when(jnp.logical_not(is_p))
    def _():
        kst_ref[...] = ka
        vst_ref[...] = va

    qb_ref[...] = rot(proj(3)).astype(BF16)
    kb_ref[...] = (rot(proj(4)) * (HEAD_DIM_RET ** -0.5)).astype(BF16)
    vb_ref[...] = proj(5).astype(BF16)
    gb_ref[...] = proj(6).astype(BF16)


def _inproj(xp, xs, shift, scale, g_norm, w_in_b, bd, gq8, gk8, cos_t, sin_t, nb, tps):
    rp = xp.shape[0]
    ntp = rp // TM
    nt = ntp + 1
    r = rp + TM
    row = lambda i: (i, 0)
    full = lambda i: (0, 0)
    tab = lambda i: (jnp.where(i < ntp, i % tps, tps), 0)
    tail_spec = pl.BlockSpec((TM, GROUP_W), lambda i: (jnp.minimum(i // tps, nb - 1), 0))
    act = jax.ShapeDtypeStruct((r, GROUP_W), BF16)
    return pl.pallas_call(
        functools.partial(_inproj_kernel, ntp),
        grid=(nt,),
        in_specs=[pl.BlockSpec((TM, D_MODEL), lambda i: (jnp.minimum(i, ntp - 1), 0)),
                  pl.BlockSpec((TM, D_MODEL), full),
                  pl.BlockSpec((GROUPS_PER_TILE, D_MODEL), _mod_row(ntp, tps, nb)),
                  pl.BlockSpec((GROUPS_PER_TILE, D_MODEL), _mod_row(ntp, tps, nb)),
                  pl.BlockSpec((1, D_MODEL), full),
                  pl.BlockSpec((D_MODEL, N_SLOTS * GROUP_W), full),
                  pl.BlockSpec((GROUP_W // 2, GROUP_W // 2), full),
                  pl.BlockSpec((1, GROUP_W), full),
                  pl.BlockSpec((1, GROUP_W), full),
                  pl.BlockSpec((TM, HEAD_DIM_RET), tab),
                  pl.BlockSpec((TM, HEAD_DIM_RET), tab)],
        out_specs=[pl.BlockSpec((TM, GROUP_W), row), pl.BlockSpec((GROUP_W, TM), lambda i: (0, i))]
        + [pl.BlockSpec((TM, GROUP_W), row)] * 5 + [
            tail_spec, tail_spec,
            pl.BlockSpec((TM, GROUP_W), full),
            pl.BlockSpec((TM, GROUP_W), full)],
        out_shape=[act, jax.ShapeDtypeStruct((GROUP_W, r), BF16)] + [act] * 5
        + [jax.ShapeDtypeStruct((nb * TM, GROUP_W), F32)] * 2
        + [jax.ShapeDtypeStruct((TM, GROUP_W), F32)] * 2,
        compiler_params=_cparams(("arbitrary",), VMEM_LIMIT),
        name="inproj",
    )(xp, xs, shift, scale, g_norm, w_in_b, bd, gq8, gk8, cos_t, sin_t)


def _attn_heads(q, k, v, bias_ref, first_valid_col=None):
    qb_rows, kb_rows = q.shape[0], v.shape[0]
    assert qb_rows == 4 * CHUNK
    half_rows, span = qb_rows // 2, kb_rows - 2 * CHUNK
    parts = [(0, 0), (half_rows, 2 * CHUNK)]

    def softmax_part(s_full, hh, half, r0, c0):
        rs = half * qb_rows + r0
        s = s_full[rs:rs + half_rows, c0:c0 + span] + bias_ref[hh, r0:r0 + half_rows, c0:c0 + span]
        if first_valid_col is not None:
            col = lax.broadcasted_iota(I32, (half_rows, span), 1) + c0
            s = jnp.where(col >= first_valid_col, s, NEG_INF)
        m = jnp.max(s, axis=-1, keepdims=True)
        e = jnp.exp2(s - m)
        l = jnp.sum(e, axis=-1, keepdims=True)
        pad = [jnp.zeros((half_rows, c0), BF16)] if c0 else []
        pad_r = [jnp.zeros((half_rows, kb_rows - span - c0), BF16)] if kb_rows - span - c0 else []
        return jnp.concatenate(pad + [e.astype(BF16)] + pad_r, axis=1), l

    pair_w = 2 * HEAD_DIM_ATT
    low = lax.broadcasted_iota(I32, (1, pair_w), 1) < HEAD_DIM_ATT
    outs = []
    for pp in range(N_HEADS_ATT // 2):
        ps = slice(pp * pair_w, (pp + 1) * pair_w)
        q2, v2 = q[:, ps], v[:, ps]
        zero = jnp.zeros_like(q2)
        qs = jnp.concatenate([jnp.where(low, q2, zero), jnp.where(low, zero, q2)], axis=0)
        s = jnp.dot(qs, k[ps, :], preferred_element_type=F32)
        es, ls = zip(*[softmax_part(s, 2 * pp + half, half, r0, c0)
                       for half in range(2) for r0, c0 in parts])
        o = jnp.dot(jnp.concatenate(es, axis=0), v2, preferred_element_type=F32) / jnp.concatenate(ls, axis=0)
        outs.append(jnp.where(low, o[:qb_rows], o[qb_rows:]))
    return jnp.concatenate(outs, axis=1)


def _fill_band_bias(rev_ref, bias_scr):
    _, qb_rows, kb_rows = bias_scr.shape
    width = rev_ref.shape[1]
    q = lax.broadcasted_iota(I32, (qb_rows, kb_rows), 0)
    k = lax.broadcasted_iota(I32, (qb_rows, kb_rows), 1)
    qc = q >> 6
    kc = (k - ATT_WINDOW) >> 6
    band = (kc >= qc - ATT_WINDOW // CHUNK) & (kc <= qc)
    for hh in range(N_HEADS_ATT):
        rows = jnp.broadcast_to(rev_ref[hh:hh + 1, :], (qb_rows, width))
        toep = pltpu.roll(rows, width - MAX_REL, 1, stride=1, stride_axis=0)
        bias_scr[hh] = jnp.where(band, toep[:, :kb_rows] * LOG2_E, NEG_INF)


def _attn_sample_kernel(q_ref, kn_ref, vn_ref, kc_ref, vc_ref, rev_ref, o_ref, bias_scr):
    @pl.when(pl.program_id(0) == 0)
    def _():
        _fill_band_bias(rev_ref, bias_scr)

    q = q_ref[...]
    outs = []
    for hh in range(N_HEADS_ATT):
        hs = slice(hh * HEAD_DIM_ATT, (hh + 1) * HEAD_DIM_ATT)
        k = jnp.concatenate([kc_ref[0, :, hs], kn_ref[:, hs].astype(BF16)], axis=0)
        v = jnp.concatenate([vc_ref[0, :, hs], vn_ref[:, hs].astype(BF16)], axis=0)
        s = lax.dot_general(q[:, hs], k, (((1,), (1,)), ((), ())), preferred_element_type=F32)
        s = s + bias_scr[hh]
        m = jnp.max(s, axis=-1, keepdims=True)
        e = jnp.exp2(s - m)
        l = jnp.sum(e, axis=-1, keepdims=True)
        outs.append(jnp.dot(e.astype(BF16), v, preferred_element_type=F32) / l)
    o_ref[...] = jnp.concatenate(outs, axis=1).astype(BF16)


def _attn_sample(qa, ks_new, vs_new, kc, vc, rev, rp):
    ndb = kc.shape[0]
    base = rp // CHUNK
    spec = pl.BlockSpec((CHUNK, GROUP_W), lambda b: (base + b, 0))
    new = pl.BlockSpec((CHUNK, GROUP_W), lambda b: (b, 0))
    cspec = pl.BlockSpec((1, ATT_WINDOW, GROUP_W), lambda b: (b, 0, 0))
    return pl.pallas_call(
        _attn_sample_kernel,
        grid=(ndb,),
        in_specs=[spec, new, new, cspec, cspec,
                  pl.BlockSpec(rev.shape, lambda b: (0, 0))],
        out_specs=pl.BlockSpec((CHUNK, GROUP_W), lambda b: (b, 0)),
        out_shape=jax.ShapeDtypeStruct((ndb * CHUNK, GROUP_W), BF16),
        scratch_shapes=[pltpu.VMEM((N_HEADS_ATT, CHUNK, ATT_WINDOW + CHUNK), F32)],
        compiler_params=_cparams(("arbitrary",), VMEM_LIMIT),
        name="attn_sample",
    )(qa, ks_new, vs_new, kc, vc, rev)


def _ret_chunk(state_decay, q_ref, k_ref, v_ref, g_ref, dm_ref, xi_ref, zeta_ref, gro_ref, o_ref, s_scr):
    outs = []
    for hh in range(N_HEADS_RET):
        hs = slice(hh * HEAD_DIM_RET, (hh + 1) * HEAD_DIM_RET)
        q = q_ref[:, hs]
        k = k_ref[:, hs]
        v = v_ref[:, hs]
        st = s_scr[hh]
        sc = lax.dot_general(q, k, (((1,), (1,)), ((), ())), preferred_element_type=F32) * dm_ref[hh]
        inner = jnp.dot(sc.astype(BF16), v, preferred_element_type=F32)
        cross = jnp.dot(q, st.astype(BF16), preferred_element_type=F32) * xi_ref[:, hs]
        o = inner + cross
        kz = k.astype(F32) * zeta_ref[:, hs]
        s_scr[hh] = state_decay[hh] * st + jnp.dot(kz.T.astype(BF16), v, preferred_element_type=F32)
        mu = jnp.mean(o, axis=-1, keepdims=True)
        oc = o - mu
        var = jnp.mean(oc * oc, axis=-1, keepdims=True)
        outs.append(oc * lax.rsqrt(var + NORM_EPS))
    y = jnp.concatenate(outs, axis=1) * gro_ref[...]
    g = g_ref[...].astype(F32)
    o_ref[...] = (g * jax.nn.sigmoid(g) * y).astype(BF16)


def _ret_kernel(state_decay, q_ref, k_ref, v_ref, g_ref, s0_ref, dm_ref, xi_ref, zeta_ref,
                gro_ref, o_ref, sn_ref, s_scr):
    j = pl.program_id(1)

    @pl.when(j == 0)
    def _():
        s_scr[...] = s0_ref[0]

    _ret_chunk(state_decay, q_ref, k_ref, v_ref, g_ref, dm_ref, xi_ref, zeta_ref, gro_ref, o_ref, s_scr)

    @pl.when(j == pl.num_programs(1) - 1)
    def _():
        sn_ref[0] = s_scr[...]


def _mix_prompt_kernel(state_decay, q_ref, k0_ref, k1_ref, k2_ref, v0_ref, v1_ref, v2_ref, rev_ref,
                       rq_ref, rk_ref, rv_ref, rg_ref, s0_ref, dm_ref, xi_ref, zeta_ref, gro_ref,
                       att_ref, ret_ref, sn_ref, bias_scr, s_scr):
    j = pl.program_id(1)

    @pl.when((pl.program_id(0) == 0) & (j == 0))
    def _():
        _fill_band_bias(rev_ref, bias_scr)

    @pl.when(j == 0)
    def _():
        s_scr[...] = s0_ref[0]

    k = jnp.concatenate([k0_ref[...], k1_ref[...], k2_ref[...]], axis=1)
    v = jnp.concatenate([v0_ref[...], v1_ref[...], v2_ref[...]], axis=0)

    def block(first_valid_col):
        att_ref[...] = _attn_heads(q_ref[...], k, v, bias_scr, first_valid_col).astype(BF16)
        _ret_chunk(state_decay, rq_ref, rk_ref, rv_ref, rg_ref, dm_ref, xi_ref, zeta_ref, gro_ref,
                   ret_ref, s_scr)

    @pl.when(j >= 2)
    def _():
        block(None)

    @pl.when(j < 2)
    def _():
        block((2 - j) * ATT_QB)

    @pl.when(j == pl.num_programs(1) - 1)
    def _():
        sn_ref[0] = s_scr[...]


def _mix_prompt(qa, ka_t, va, rev, qb, kb, vb, gb, s0, g_ro, nb, seq):
    assert ATT_QB == RET_CB
    r = nb * seq
    nq = seq // ATT_QB
    dm, xi, zeta, state_decay = _ret_consts(RET_CB)
    blk = lambda back: (lambda b, j: (b * nq + jnp.maximum(j - back, 0), 0))
    spec = lambda back: pl.BlockSpec((ATT_QB, GROUP_W), blk(back))
    tspec = lambda back: pl.BlockSpec((GROUP_W, ATT_QB), lambda b, j: (0, b * nq + jnp.maximum(j - back, 0)))
    sspec = pl.BlockSpec((1, N_HEADS_RET, HEAD_DIM_RET, HEAD_DIM_RET), lambda b, j: (b, 0, 0, 0))
    full2 = lambda b, j: (0, 0)
    out = jax.ShapeDtypeStruct((r, GROUP_W), BF16)
    return pl.pallas_call(
        functools.partial(_mix_prompt_kernel, state_decay),
        grid=(nb, nq),
        in_specs=[spec(0), tspec(2), tspec(1), tspec(0), spec(2), spec(1), spec(0),
                  pl.BlockSpec(rev.shape, full2),
                  spec(0), spec(0), spec(0), spec(0), sspec,
                  pl.BlockSpec(dm.shape, lambda b, j: (0, 0, 0)),
                  pl.BlockSpec(xi.shape, full2), pl.BlockSpec(zeta.shape, full2),
                  pl.BlockSpec((1, GROUP_W), full2)],
        out_specs=[spec(0), spec(0), sspec],
        out_shape=[out, out, jax.ShapeDtypeStruct(s0.shape, F32)],
        scratch_shapes=[pltpu.VMEM((N_HEADS_ATT, ATT_QB, ATT_WINDOW + ATT_QB), F32),
                        pltpu.VMEM((N_HEADS_RET, HEAD_DIM_RET, HEAD_DIM_RET), F32)],
        compiler_params=_cparams(("arbitrary", "arbitrary"), VMEM_LIMIT),
        name="mix_prompt",
    )(qa, ka_t, ka_t, ka_t, va, va, va, rev, qb, kb, vb, gb, s0, dm, xi, zeta, g_ro)


def _ret_consts(cb):
    log_g = np.log1p(-np.exp2(-RET_DECAY_OFFSET - np.arange(N_HEADS_RET, dtype=np.float64)))
    n = np.arange(cb, dtype=np.float64)
    diff = n[:, None] - n[None, :]
    dm = np.where(diff[None] >= 0, np.exp(np.maximum(diff, 0.0)[None] * log_g[:, None, None]), 0.0)
    xi = np.exp((n + 1.0)[:, None] * log_g[None, :])
    zeta = np.exp((cb - 1.0 - n)[:, None] * log_g[None, :])
    rep = lambda a: np.repeat(a, HEAD_DIM_RET, axis=1)
    state_decay = tuple(float(v) for v in np.exp(cb * log_g))
    return (jnp.asarray(dm, F32), jnp.asarray(rep(xi), F32), jnp.asarray(rep(zeta), F32), state_decay)


def _ret(qb, kb, vb, gb, s0, g_ro, cb, row0, nb, nc, name):
    dm, xi, zeta, state_decay = _ret_consts(cb)
    base = row0 // cb
    spec = pl.BlockSpec((cb, GROUP_W), lambda b, j: (base + b * nc + j, 0))
    sspec = pl.BlockSpec((1, N_HEADS_RET, HEAD_DIM_RET, HEAD_DIM_RET), lambda b, j: (b, 0, 0, 0))
    full2 = lambda b, j: (0, 0)
    return pl.pallas_call(
        functools.partial(_ret_kernel, state_decay),
        grid=(nb, nc),
        in_specs=[spec, spec, spec, spec, sspec,
                  pl.BlockSpec(dm.shape, lambda b, j: (0, 0, 0)),
                  pl.BlockSpec(xi.shape, full2), pl.BlockSpec(zeta.shape, full2),
                  pl.BlockSpec((1, GROUP_W), full2)],
        out_specs=[pl.BlockSpec((cb, GROUP_W), lambda b, j: (b * nc + j, 0)), sspec],
        out_shape=[jax.ShapeDtypeStruct((nb * nc * cb, GROUP_W), BF16),
                   jax.ShapeDtypeStruct(s0.shape, F32)],
        scratch_shapes=[pltpu.VMEM((N_HEADS_RET, HEAD_DIM_RET, HEAD_DIM_RET), F32)],
        compiler_params=_cparams(("arbitrary", "arbitrary"), VMEM_LIMIT),
        name=name,
    )(qb, kb, vb, gb, s0, dm, xi, zeta, g_ro)


def _outproj_kernel(npp, attp_ref, atts_ref, retp_ref, rets_ref, xp_ref, xs_ref, gm_ref, shf_ref, scf_ref,
                    gn_ref, wo_ref, wr_ref, br_ref, upper_ref, lower_ref,
                    x1_ref, h2_ref, slot_ref, cols_ref, cnt_ref):
    is_p = pl.program_id(0) < npp
    subs = range(2)
    rows = [slice(sub * TM, (sub + 1) * TM) for sub in subs]

    def pick(p_ref, s_ref, sub):
        return jnp.where(is_p, p_ref[rows[sub], :], s_ref[...])

    mix = [jnp.dot(pick(attp_ref, atts_ref, sub), wo_ref[:GROUP_W, :], preferred_element_type=F32)
           + jnp.dot(pick(retp_ref, rets_ref, sub), wo_ref[GROUP_W:, :], preferred_element_type=F32)
           for sub in subs]
    h2b = []
    for sub in subs:
        x1 = _per_group(mix[sub], lambda a, gm: a * gm, gm_ref[...]) + pick(xp_ref, xs_ref, sub)
        x1_ref[rows[sub], :] = x1
        y = _rms_rows(x1, gn_ref[...])
        h2 = _per_group(y, lambda a, sh, sc: a * (1.0 + sc) + sh, shf_ref[...], scf_ref[...])
        h2b.append(h2.astype(BF16))
        h2_ref[rows[sub], :] = h2b[sub]

    work = [lax.dot_general(wr_ref[...], h2b[sub], (((1,), (1,)), ((), ())),
                            preferred_element_type=F32) + br_ref[...] for sub in subs]
    eidx = lax.broadcasted_iota(I32, work[0].shape, 0).astype(F32)
    sel = [[] for _ in subs]
    top = [[] for _ in subs]
    for _ in range(TOP_K):
        for sub in subs:
            m = jnp.max(work[sub], axis=0, keepdims=True)
            idx = jnp.min(jnp.where(work[sub] == m, eidx, float(N_EXPERTS)), axis=0, keepdims=True)
            hit = eidx == idx
            sel[sub].append(hit)
            top[sub].append(m)
            work[sub] = jnp.where(hit, -jnp.inf, work[sub])

    for sub in subs:
        ex = [jnp.exp(t - top[sub][0]) for t in top[sub]]
        den = ex[0] + ex[1] + ex[2] + ex[3]
        gates = [e / den for e in ex]
        hits = sel[sub]
        multi_f = jnp.where(hits[0] | hits[1] | hits[2] | hits[3], 1.0, 0.0)
        rank = jnp.dot(multi_f.astype(BF16), upper_ref[...], preferred_element_type=F32)
        cnt = jnp.sum(multi_f, axis=1, keepdims=True)
        cnt_pad = jnp.maximum(jnp.floor((cnt + (SEG_ALIGN - 1.0)) * (1.0 / SEG_ALIGN)), 1.0) * SEG_ALIGN
        cnt_pad_b = jnp.broadcast_to(cnt_pad, (N_EXPERTS, 128))
        seg_off = jnp.dot(lower_ref[...], cnt_pad_b.astype(BF16), preferred_element_type=F32)[:, :1]
        pos = seg_off + rank
        slot_rows = jnp.concatenate(
            [jnp.sum(jnp.where(h, pos, 0.0), axis=0, keepdims=True) for h in hits], axis=0)
        gate_rows = jnp.concatenate(gates, axis=0)
        slot_ref[sub] = slot_rows.astype(I32)
        cnt_ref[sub] = cnt_pad_b.astype(I32)
        both = jnp.concatenate([slot_rows, gate_rows, jnp.zeros((128 - 2 * TOP_K, TM), F32)], axis=0)
        cols_ref[sub] = both.T


def _outproj(att_p, att_s, ret_p, ret_s, xp, xs, gate_m, shift_f, scale_f, g_norm, w_out_b, wr_t, br,
             upper, lower, nb, tps):
    rp = xp.shape[0]
    ntp = rp // TM
    assert ntp % 2 == 0 and tps % 2 == 0
    npp = ntp // 2
    nt2 = ntp + 2
    r = nt2 * TM
    row = lambda p: (p, 0)
    row3 = lambda p: (p, 0, 0)
    full = lambda p: (0, 0)
    prow = lambda p: (jnp.minimum(p, npp - 1), 0)
    mod = pl.BlockSpec((GROUPS_PER_TILE, D_MODEL), _mod_row(npp, tps // 2, nb))
    return pl.pallas_call(
        functools.partial(_outproj_kernel, npp),
        grid=(npp + 1,),
        in_specs=[pl.BlockSpec((2 * TM, GROUP_W), prow), pl.BlockSpec((TM, GROUP_W), full),
                  pl.BlockSpec((2 * TM, GROUP_W), prow), pl.BlockSpec((TM, GROUP_W), full),
                  pl.BlockSpec((2 * TM, D_MODEL), prow),
                  pl.BlockSpec((TM, D_MODEL), full),
                  mod, mod, mod,
                  pl.BlockSpec((1, D_MODEL), full),
                  pl.BlockSpec((D_MODEL, D_MODEL), full),
                  pl.BlockSpec((N_EXPERTS, D_MODEL), full),
                  pl.BlockSpec((N_EXPERTS, 1), full),
                  pl.BlockSpec((TM, TM), full),
                  pl.BlockSpec((N_EXPERTS, N_EXPERTS), full)],
        out_specs=[pl.BlockSpec((2 * TM, D_MODEL), row), pl.BlockSpec((2 * TM, D_MODEL), row),
                   pl.BlockSpec((2, TOP_K, TM), row3),
                   pl.BlockSpec((2, TM, 128), row3), pl.BlockSpec((2, N_EXPERTS, 128), row3)],
        out_shape=[jax.ShapeDtypeStruct((r, D_MODEL), F32), jax.ShapeDtypeStruct((r, D_MODEL), BF16),
                   jax.ShapeDtypeStruct((nt2, TOP_K, TM), I32),
                   jax.ShapeDtypeStruct((nt2, TM, 128), F32), jax.ShapeDtypeStruct((nt2, N_EXPERTS, 128), I32)],
        compiler_params=_cparams(("arbitrary",), VMEM_LIMIT),
        name="outproj",
    )(att_p, att_s, ret_p, ret_s, xp, xs, gate_m, shift_f, scale_f, g_norm, w_out_b, wr_t, br, upper, lower)


def _rows_copy(n, src_rows, dst_rows, sem):
    size = pl.multiple_of(n, SEG_ALIGN)
    return pltpu.make_async_copy(src_rows(size), dst_rows(size), sem)


def _start_segments(t, cnt_ref, off_ref, base_ref, local_rows, sorted_rows, sem, to_sorted):
    for e in range(N_EXPERTS):
        n = cnt_ref[t * N_EXPERTS + e]
        off = pl.multiple_of(off_ref[t * N_EXPERTS + e], SEG_ALIGN)
        base = pl.multiple_of(base_ref[t * N_EXPERTS + e], SEG_ALIGN)
        local = lambda z, off=off: local_rows(off, z)
        remote = lambda z, base=base: sorted_rows(base, z)
        (_rows_copy(n, local, remote, sem) if to_sorted else _rows_copy(n, remote, local, sem)).start()


def _dispatch_kernel(nt, n_blocks, off_ref, cnt_ref, base_ref, tot_ref, tail0_ref, tailn_ref, na_ref,
                     h2_ref, slot_ref, slotn_ref, xb_ref, xs_scr, hot_scr, zero_scr, sems, tail_sem):
    i = pl.program_id(0)
    cur = i % 2
    sorted_rows = lambda r, z: xb_ref.at[pl.ds(r, z), :]

    def start_tile(t, buf):
        _start_segments(t, cnt_ref, off_ref, base_ref, lambda r, z: xs_scr.at[buf, pl.ds(r, z), :],
                        sorted_rows, sems.at[buf], True)

    def wait_tile(t, buf):
        _rows_copy(tot_ref[t], lambda z: xs_scr.at[buf, pl.ds(0, z), :], lambda z: sorted_rows(0, z),
                   sems.at[buf]).wait()

    def tail_copies(wait):
        def body(e, c):
            base = pl.multiple_of(tail0_ref[e], SEG_ALIGN)

            @pl.when(tailn_ref[e] > 0)
            def _():
                cp = _rows_copy(tailn_ref[e], lambda z: zero_scr.at[pl.ds(0, z), :],
                                lambda z: sorted_rows(base, z), tail_sem)
                cp.wait() if wait else cp.start()
            return c
        lax.fori_loop(0, N_EXPERTS, body, 0)

        def unused(j, c):
            cp = pltpu.make_async_copy(zero_scr, sorted_rows(pl.multiple_of(j * BM, BM), BM), tail_sem)
            cp.wait() if wait else cp.start()
            return c
        lax.fori_loop(na_ref[0], n_blocks, unused, 0)

    @pl.when(i >= 2)
    def _():
        wait_tile(i - 2, cur)

    def onehot(slot):
        srow = lax.broadcasted_iota(I32, (CAP, TM), 0)
        hit = (srow == slot[0:1]) | (srow == slot[1:2]) | (srow == slot[2:3]) | (srow == slot[3:4])
        return jnp.where(hit, 1.0, 0.0).astype(BF16)

    @pl.when(i == 0)
    def _():
        hot_scr[0] = onehot(slot_ref[0])
        zero_scr[...] = jnp.zeros_like(zero_scr)
        tail_copies(False)

    xs_scr[cur] = jnp.dot(hot_scr[cur], h2_ref[...], preferred_element_type=F32).astype(BF16)
    hot_scr[1 - cur] = onehot(slotn_ref[0])
    start_tile(i, cur)

    @pl.when(i == nt - 1)
    def _():
        if nt >= 2:
            wait_tile(i - 1, 1 - cur)
        wait_tile(i, cur)
        tail_copies(True)


def _dispatch(h2, slot, off, cnt, base, tot, tail0, tailn, n_act, n_blocks):
    nt = tot.shape[0]
    n_rows = n_blocks * BM
    grid_spec = pltpu.PrefetchScalarGridSpec(
        num_scalar_prefetch=7,
        grid=(nt,),
        in_specs=[pl.BlockSpec((TM, D_MODEL), lambda i, *_: (i, 0)),
                  pl.BlockSpec((1, TOP_K, TM), lambda i, *_: (i, 0, 0)),
                  pl.BlockSpec((1, TOP_K, TM), lambda i, *_: (jnp.minimum(i + 1, nt - 1), 0, 0))],
        out_specs=pl.BlockSpec(memory_space=pl.ANY),
        scratch_shapes=[pltpu.VMEM((2, CAP, D_MODEL), BF16),
                        pltpu.VMEM((2, CAP, TM), BF16),
                        pltpu.VMEM((BM, D_MODEL), BF16),
                        pltpu.SemaphoreType.DMA((2,)),
                        pltpu.SemaphoreType.DMA(())],
    )
    return pl.pallas_call(
        functools.partial(_dispatch_kernel, nt, n_blocks),
        grid_spec=grid_spec,
        out_shape=jax.ShapeDtypeStruct((n_rows, D_MODEL), BF16),
        compiler_params=_cparams(("arbitrary",), VMEM_LIMIT),
        name="dispatch",
    )(off, cnt, base, tot, tail0, tailn, n_act, h2, slot, slot)


def _experts_kernel(be_ref, bi_ref, nx_ref, half_ref, na_ref, x_ref, wu_hbm, bu_ref, wd_hbm, bd_ref, y_ref,
                    wu_stage, wd_stage, wu_scr, wd_scr, sems):
    j = pl.program_id(0)

    def weight_copies(e):
        return (pltpu.make_async_copy(wu_hbm.at[e], wu_stage, sems.at[0]),
                pltpu.make_async_copy(wd_hbm.at[e], wd_stage, sems.at[1]))

    @pl.when(j < na_ref[0])
    def _():
        e = be_ref[j]
        prev = be_ref[jnp.maximum(j - 1, 0)]

        @pl.when(j == 0)
        def _():
            for cp in weight_copies(e):
                cp.start()

        @pl.when((j == 0) | (e != prev))
        def _():
            for cp in weight_copies(e):
                cp.wait()
            wu_scr[...] = wu_stage[...].astype(BF16)
            wd_scr[...] = wd_stage[...].astype(BF16)

            @pl.when(nx_ref[j] != e)
            def _():
                for cp in weight_copies(nx_ref[j]):
                    cp.start()

        def ffn(rows):
            u = jnp.dot(x_ref[rows, :], wu_scr[...], preferred_element_type=F32) + bu_ref[0]
            glu = jnp.minimum(u[:, :D_FF], SWIGLU_LIMIT)
            lin = jnp.clip(u[:, D_FF:], -SWIGLU_LIMIT, SWIGLU_LIMIT)
            act = glu * jax.nn.sigmoid(SWIGLU_ALPHA * glu) * (lin + 1.0)
            y = jnp.dot(act.astype(BF16), wd_scr[...], preferred_element_type=F32) + bd_ref[0]
            y_ref[rows, :] = y.astype(BF16)

        @pl.when(half_ref[j] == 0)
        def _():
            ffn(slice(0, BM))

        @pl.when(half_ref[j] != 0)
        def _():
            ffn(slice(0, BM // 2))
            y_ref[BM // 2:, :] = jnp.zeros((BM // 2, D_MODEL), BF16)


def _experts(xb, blk_e, blk_i, blk_nx, blk_half, n_act, w_up, b_up, w_down, b_down):
    n_rows = xb.shape[0]
    nblk = n_rows // BM
    grid_spec = pltpu.PrefetchScalarGridSpec(
        num_scalar_prefetch=5,
        grid=(nblk,),
        in_specs=[pl.BlockSpec((BM, D_MODEL), lambda j, be, bi, *_: (bi[j], 0)),
                  pl.BlockSpec(memory_space=pl.ANY),
                  pl.BlockSpec((1, 1, 2 * D_FF), lambda j, be, *_: (be[j], 0, 0)),
                  pl.BlockSpec(memory_space=pl.ANY),
                  pl.BlockSpec((1, 1, D_MODEL), lambda j, be, *_: (be[j], 0, 0))],
        out_specs=pl.BlockSpec((BM, D_MODEL), lambda j, be, bi, *_: (bi[j], 0)),
        scratch_shapes=[pltpu.VMEM((D_MODEL, 2 * D_FF), F32), pltpu.VMEM((D_FF, D_MODEL), F32),
                        pltpu.VMEM((D_MODEL, 2 * D_FF), BF16), pltpu.VMEM((D_FF, D_MODEL), BF16),
                        pltpu.SemaphoreType.DMA((2,))],
    )
    return pl.pallas_call(
        _experts_kernel,
        grid_spec=grid_spec,
        out_shape=jax.ShapeDtypeStruct((n_rows, D_MODEL), BF16),
        input_output_aliases={5: 0},
        compiler_params=_cparams(("arbitrary",), VMEM_LIMIT),
        name="experts",
    )(blk_e, blk_i, blk_nx, blk_half, n_act, xb, w_up, b_up.reshape(N_EXPERTS, 1, 2 * D_FF), w_down,
      b_down.reshape(N_EXPERTS, 1, D_MODEL))


def _combine_kernel(nt, ntp, off_ref, cnt_ref, base_ref, tot_ref, yb_ref, cols_ref, x1_ref, gf_ref,
                    op_ref, os_ref, ys_scr, sems):
    i = pl.program_id(0)
    cur = i % 2

    sorted_rows = lambda r, z: yb_ref.at[pl.ds(r, z), :]

    def start_tile(t, buf):
        _start_segments(t, cnt_ref, off_ref, base_ref, lambda r, z: ys_scr.at[buf, pl.ds(r, z), :],
                        sorted_rows, sems.at[buf], False)

    @pl.when(i == 0)
    def _():
        ys_scr[...] = jnp.zeros_like(ys_scr)
        start_tile(0, 0)

    def wait_tile(t, buf):
        _rows_copy(tot_ref[t], lambda z: sorted_rows(0, z), lambda z: ys_scr.at[buf, pl.ds(0, z), :],
                   sems.at[buf]).wait()

    nxt = jnp.minimum(i + 1, nt - 1)
    wait_tile(i, cur)
    start_tile(nxt, 1 - cur)

    @pl.when(i == nt - 1)
    def _():
        wait_tile(nxt, 1 - cur)

    cols = cols_ref[0]
    lane = lax.broadcasted_iota(I32, (TM, CAP), 1)
    w = jnp.zeros((TM, CAP), F32)
    for k in range(TOP_K):
        sk = cols[:, k:k + 1].astype(I32)
        gk = cols[:, TOP_K + k:TOP_K + k + 1]
        w = jnp.where(lane == sk, gk, w)
    y = jnp.dot(w.astype(BF16), ys_scr[cur], preferred_element_type=F32)
    out = x1_ref[...] + _per_group(y, lambda a, gf: a * gf, gf_ref[...])

    @pl.when(i < ntp)
    def _():
        op_ref[...] = out

    @pl.when(i >= ntp)
    def _():
        os_ref[...] = out


def _combine(yb, cols, x1, gate_f, off, cnt, base, tot, ntp, nb, tps):
    nt = tot.shape[0]
    grid_spec = pltpu.PrefetchScalarGridSpec(
        num_scalar_prefetch=4,
        grid=(nt,),
        in_specs=[pl.BlockSpec(memory_space=pl.ANY),
                  pl.BlockSpec((1, TM, 128), lambda i, *_: (i, 0, 0)),
                  pl.BlockSpec((TM, D_MODEL), lambda i, *_: (i, 0)),
                  pl.BlockSpec((GROUPS_PER_TILE, D_MODEL), _mod_row(ntp, tps, nb))],
        out_specs=[pl.BlockSpec((TM, D_MODEL), lambda i, *_: (jnp.minimum(i, ntp - 1), 0)),
                   pl.BlockSpec((TM, D_MODEL), lambda i, *_: (0, 0))],
        scratch_shapes=[pltpu.VMEM((2, CAP, D_MODEL), BF16), pltpu.SemaphoreType.DMA((2,))],
    )
    return pl.pallas_call(
        functools.partial(_combine_kernel, nt, ntp),
        grid_spec=grid_spec,
        out_shape=[jax.ShapeDtypeStruct((ntp * TM, D_MODEL), F32),
                   jax.ShapeDtypeStruct((TM, D_MODEL), F32)],
        compiler_params=_cparams(("arbitrary",), VMEM_LIMIT),
        name="combine",
    )(off, cnt, base, tot, yb, cols, x1, gate_f)


def _rotary_tables(seq, dec_batch, dec_seq):
    half = HEAD_DIM_RET // 2
    inv = ROPE_BASE ** (-np.arange(half, dtype=np.float64) / half)
    pos = np.concatenate([np.arange(seq), np.tile(PAST_LEN + np.arange(dec_seq), dec_batch)])
    ang = pos.astype(np.float64)[:, None] * inv[None, :]
    cos = np.concatenate([np.cos(ang), np.cos(ang)], axis=1)
    sin = np.concatenate([-np.sin(ang), np.sin(ang)], axis=1)
    return jnp.asarray(cos, F32), jnp.asarray(sin, F32)


def _rel_bias_reversed(rel_bias):
    heads = rel_bias.shape[0]
    ext = jnp.concatenate([rel_bias[:, 1:], jnp.broadcast_to(rel_bias[:, -1:], (heads, 2 * MAX_REL))], axis=1)
    return ext[:, ::-1].astype(F32)


def _group_mods(m, nb, ndb):
    assert ndb == GROUPS_PER_TILE
    mp = jnp.broadcast_to(m[:nb, None], (nb, GROUPS_PER_TILE) + m.shape[1:])
    allm = jnp.concatenate([mp.reshape((nb * GROUPS_PER_TILE,) + m.shape[1:]), m[nb:]], axis=0)
    return jnp.transpose(allm, (1, 0, 2))


def _routing_tables(cnt, n_blocks):
    nt = cnt.shape[0]
    off = jnp.cumsum(cnt, axis=1) - cnt
    rows_e = jnp.sum(cnt, axis=0)
    nblk_e = (rows_e + BM - 1) // BM
    blk_end = jnp.cumsum(nblk_e)
    start_e = (blk_end - nblk_e) * BM
    base = start_e[None, :] + jnp.cumsum(cnt, axis=0) - cnt
    n_act = blk_end[-1]
    j = jnp.minimum(jnp.arange(n_blocks), n_act - 1)
    blk_e = jnp.minimum(jnp.sum(blk_end[None, :] <= j[:, None], axis=1), N_EXPERTS - 1)
    later = jnp.where(blk_e[None, :] > blk_e[:, None], blk_e[None, :], N_EXPERTS)
    blk_nx = jnp.min(later, axis=1)
    blk_nx = jnp.where(blk_nx == N_EXPERTS, blk_e, blk_nx)
    tail0 = start_e + rows_e
    tailn = nblk_e * BM - rows_e
    blk_rows = rows_e[blk_e] - (j * BM - start_e[blk_e])
    blk_half = blk_rows <= BM // 2
    i32 = lambda a: a.astype(I32)
    return (i32(off.reshape(nt * N_EXPERTS)), i32(cnt.reshape(nt * N_EXPERTS)),
            i32(base.reshape(nt * N_EXPERTS)), i32(jnp.sum(cnt, axis=1)), i32(tail0), i32(tailn),
            i32(blk_e), i32(j), i32(blk_nx), i32(blk_half), i32(n_act.reshape(1)))


def kernel(x_prompt, x_sample, c_prompt, c_sample, cache_att_k, cache_att_v, state_ret, w_ada, b_ada,
           g_norm_mix, g_norm_ffn, w_in, g_q, g_k, rel_bias, g_ret_out, w_out, w_router, b_router,
           w_up, b_up, w_down, b_down):
    nb, seq, d = x_prompt.shape
    ndb, dseq, _ = x_sample.shape
    assert d == D_MODEL and ndb * dseq == TM and dseq == CHUNK
    assert seq % TM == 0 and seq >= ATT_WINDOW and cache_att_k.shape[2] == ATT_WINDOW
    assert w_ada.shape[0] == 1
    rp = nb * seq
    ntp = rp // TM
    nt = ntp + 1
    tps = seq // TM

    xp = x_prompt.reshape(rp, d)
    xs = x_sample.reshape(TM, d)

    m = _ada(jnp.concatenate([c_prompt, c_sample], axis=0), w_ada[0], b_ada[0])
    mods = _group_mods(m.reshape(nb + ndb, N_ADA, d), nb, ndb)
    shift_m, scale_m, gate_m, shift_f, scale_f, gate_f = [mods[a] for a in range(N_ADA)]

    cos_t, sin_t = _rotary_tables(seq, ndb, dseq)
    bd = jnp.asarray(np.kron(np.eye(N_HEADS_ATT // 2), np.ones((HEAD_DIM_ATT, HEAD_DIM_ATT))), BF16)
    tile8 = lambda g: jnp.tile(g.astype(F32), N_HEADS_ATT).reshape(1, GROUP_W)
    (qa, ka_t, va, qb, kb, vb, gb, kp_tail, vp_tail, ks_new, vs_new) = _inproj(
        xp, xs, shift_m, scale_m, g_norm_mix[0].reshape(1, d), w_in[0].astype(BF16), bd,
        tile8(g_q[0]) * (HEAD_DIM_ATT ** -0.5 * LOG2_E), tile8(g_k[0]), cos_t, sin_t, nb, tps)

    rev = _rel_bias_reversed(rel_bias[0])
    g_ro = g_ret_out[0].astype(F32).reshape(1, GROUP_W)
    zero_state = jnp.zeros((nb, N_HEADS_RET, HEAD_DIM_RET, HEAD_DIM_RET), F32)
    att_p, ret_p, state_p = _mix_prompt(qa, ka_t, va, rev, qb, kb, vb, gb, zero_state, g_ro, nb, seq)
    att_s = _attn_sample(qa, ks_new, vs_new,
                         cache_att_k[0].reshape(ndb, ATT_WINDOW, GROUP_W).astype(BF16),
                         cache_att_v[0].reshape(ndb, ATT_WINDOW, GROUP_W).astype(BF16), rev, rp)

    ret_s, state_s = _ret(qb, kb, vb, gb, state_ret[0].astype(F32), g_ro, CHUNK, rp, ndb, 1, "ret_sample")

    upper = jnp.asarray(np.triu(np.ones((TM, TM)), 1), BF16)
    lower = jnp.asarray(np.tril(np.ones((N_EXPERTS, N_EXPERTS)), -1), BF16)
    x1, h2, slot, cols, cnt = _outproj(
        att_p, att_s, ret_p, ret_s, xp, xs, gate_m, shift_f, scale_f, g_norm_ffn[0].reshape(1, d),
        w_out[0].astype(BF16), w_router[0].T.astype(BF16), b_router[0].astype(F32).reshape(N_EXPERTS, 1),
        upper, lower, nb, tps)

    n_blocks = (TOP_K * (rp + TM) + nt * N_EXPERTS * SEG_ALIGN) // BM + 1 + N_EXPERTS
    (off, cntf, base, tot, tail0, tailn, blk_e, blk_i, blk_nx, blk_half,
     n_act) = _routing_tables(cnt[:nt, :, 0], n_blocks)
    xb = _dispatch(h2, slot, off, cntf, base, tot, tail0, tailn, n_act, n_blocks)
    yb = _experts(xb, blk_e, blk_i, blk_nx, blk_half, n_act, w_up[0], b_up[0], w_down[0], b_down[0])
    out_p, out_s = _combine(yb, cols, x1, gate_f, off, cntf, base, tot, ntp, nb, tps)

    heads = (N_HEADS_ATT, HEAD_DIM_ATT)
    return (out_p.reshape(nb, seq, d), out_s.reshape(ndb, dseq, d),
            kp_tail.reshape(1, nb, ATT_WINDOW, *heads), vp_tail.reshape(1, nb, ATT_WINDOW, *heads),
            state_p[None],
            ks_new.reshape(1, ndb, dseq, *heads), vs_new.reshape(1, ndb, dseq, *heads),
            state_s[None])
```

```python
import functools

import numpy as np
import jax
import jax.numpy as jnp
from jax import lax
from jax.experimental import pallas as pl
from jax.experimental.pallas import tpu as pltpu

F32 = jnp.float32
BF16 = jnp.bfloat16
I32 = jnp.int32

D_MODEL = 1024
GROUP_W = 512
N_SLOTS = 7
N_HEADS_ATT = 8
HEAD_DIM_ATT = 64
N_HEADS_RET = 4
HEAD_DIM_RET = 128
CHUNK = 64
ATT_WINDOW = 512
MAX_REL = 256
PAST_LEN = 2048
RET_DECAY_OFFSET = 5.0
ROPE_BASE = 10000.0
N_EXPERTS = 32
TOP_K = 4
D_FF = 1024
SWIGLU_LIMIT = 7.0
SWIGLU_ALPHA = 1.702
N_ADA = 6
NORM_EPS = 1e-6
NEG_INF = -1e30
LOG2_E = 1.4426950408889634

TM = 512
GROUPS_PER_TILE = TM // CHUNK
ATT_QB = 256
RET_CB = 256
SEG_ALIGN = 16
CAP = TOP_K * TM + N_EXPERTS * SEG_ALIGN
CAP_SHORT = CAP - 256
BM = 512
VMEM_LIMIT = 56 * 1024 * 1024


def _cparams(sem, vmem=None):
    return pltpu.CompilerParams(dimension_semantics=sem, vmem_limit_bytes=vmem)


def _ada_kernel(c_ref, w_ref, b_ref, o_ref):
    c = c_ref[...]
    s = c * jax.nn.sigmoid(c)
    o_ref[...] = jnp.dot(s.astype(BF16), w_ref[...].astype(BF16),
                         preferred_element_type=F32) + b_ref[...]


def _ada(c_all, w_ada, b_ada):
    n, d = c_all.shape
    cols = w_ada.shape[1]
    tn = 1536
    return pl.pallas_call(
        _ada_kernel,
        grid=(cols // tn,),
        in_specs=[pl.BlockSpec((n, d), lambda j: (0, 0)),
                  pl.BlockSpec((d, tn), lambda j: (0, j)),
                  pl.BlockSpec((1, tn), lambda j: (0, j))],
        out_specs=pl.BlockSpec((n, tn), lambda j: (0, j)),
        out_shape=jax.ShapeDtypeStruct((n, cols), F32),
        compiler_params=_cparams(("arbitrary",), VMEM_LIMIT),
        name="ada",
    )(c_all, w_ada, b_ada.reshape(1, cols))


def _rms_rows(x, g):
    ms = jnp.mean(x * x, axis=-1, keepdims=True)
    return x * lax.rsqrt(ms + NORM_EPS) * g


def _mod_row(ntp, tps, nb):
    return lambda i, *_: (jnp.where(i < ntp, i // tps, nb), 0)


def _per_group(x, fn, *mods):
    x3 = x.reshape(GROUPS_PER_TILE, CHUNK, x.shape[-1])
    y3 = fn(x3, *[m[:, None, :] for m in mods])
    return y3.reshape(x.shape)


def _inproj_kernel(ntp, xp_ref, xs_ref, sh_ref, sc_ref, gn_ref, w_ref, bd_ref, gq_ref, gk_ref,
                   cos_ref, sin_ref,
                   qa_ref, ka_ref, va_ref, qb_ref, kb_ref, vb_ref, gb_ref,
                   kpt_ref, vpt_ref, kst_ref, vst_ref):
    i = pl.program_id(0)
    is_p = i < ntp
    x = jnp.where(is_p, xp_ref[...], xs_ref[...])
    y = _rms_rows(x, gn_ref[...])
    h = _per_group(y, lambda a, sh, sc: a * (1.0 + sc) + sh, sh_ref[...], sc_ref[...])
    hb = h.astype(BF16)

    def proj(s):
        return jnp.dot(hb, w_ref[:, s * GROUP_W:(s + 1) * GROUP_W], preferred_element_type=F32)

    def head_rms(z, g):
        zz = (z * z).astype(BF16)
        half = GROUP_W // 2
        ss = jnp.concatenate(
            [jnp.dot(zz[:, :half], bd_ref[...], preferred_element_type=F32),
             jnp.dot(zz[:, half:], bd_ref[...], preferred_element_type=F32)], axis=1)
        return z * lax.rsqrt(ss * (1.0 / HEAD_DIM_ATT) + NORM_EPS) * g

    cos = cos_ref[...]
    sin = sin_ref[...]

    def rot(z):
        outs = []
        for hh in range(N_HEADS_RET):
            zh = z[:, hh * HEAD_DIM_RET:(hh + 1) * HEAD_DIM_RET]
            outs.append(zh * cos + pltpu.roll(zh, HEAD_DIM_RET // 2, axis=1) * sin)
        return jnp.concatenate(outs, axis=1)

    qa_ref[...] = head_rms(proj(0), gq_ref[...]).astype(BF16)
    ka = head_rms(proj(1), gk_ref[...])
    ka_ref[...] = ka.T.astype(BF16)
    va = proj(2)
    va_ref[...] = va.astype(BF16)

    @pl.when(is_p)
    def _():
        kpt_ref[...] = ka
        vpt_ref[...] = va

    @pl.when(jnp.logical_not(is_p))
    def _():
        kst_ref[...] = ka
        vst_ref[...] = va

    qb_ref[...] = rot(proj(3)).astype(BF16)
    kb_ref[...] = (rot(proj(4)) * (HEAD_DIM_RET ** -0.5)).astype(BF16)
    vb_ref[...] = proj(5).astype(BF16)
    gb_ref[...] = proj(6).astype(BF16)


def _inproj(xp, xs, shift, scale, g_norm, w_in_b, bd, gq8, gk8, cos_t, sin_t, nb, tps):
    rp = xp.shape[0]
    ntp = rp // TM
    nt = ntp + 1
    r = rp + TM
    row = lambda i: (i, 0)
    full = lambda i: (0, 0)
    tab = lambda i: (jnp.where(i < ntp, i % tps, tps), 0)
    tail_spec = pl.BlockSpec((TM, GROUP_W), lambda i: (jnp.minimum(i // tps, nb - 1), 0))
    act = jax.ShapeDtypeStruct((r, GROUP_W), BF16)
    return pl.pallas_call(
        functools.partial(_inproj_kernel, ntp),
        grid=(nt,),
        in_specs=[pl.BlockSpec((TM, D_MODEL), lambda i: (jnp.minimum(i, ntp - 1), 0)),
                  pl.BlockSpec((TM, D_MODEL), full),
                  pl.BlockSpec((GROUPS_PER_TILE, D_MODEL), _mod_row(ntp, tps, nb)),
                  pl.BlockSpec((GROUPS_PER_TILE, D_MODEL), _mod_row(ntp, tps, nb)),
                  pl.BlockSpec((1, D_MODEL), full),
                  pl.BlockSpec((D_MODEL, N_SLOTS * GROUP_W), full),
                  pl.BlockSpec((GROUP_W // 2, GROUP_W // 2), full),
                  pl.BlockSpec((1, GROUP_W), full),
                  pl.BlockSpec((1, GROUP_W), full),
                  pl.BlockSpec((TM, HEAD_DIM_RET), tab),
                  pl.BlockSpec((TM, HEAD_DIM_RET), tab)],
        out_specs=[pl.BlockSpec((TM, GROUP_W), row), pl.BlockSpec((GROUP_W, TM), lambda i: (0, i))]
        + [pl.BlockSpec((TM, GROUP_W), row)] * 5 + [
            tail_spec, tail_spec,
            pl.BlockSpec((TM, GROUP_W), full),
            pl.BlockSpec((TM, GROUP_W), full)],
        out_shape=[act, jax.ShapeDtypeStruct((GROUP_W, r), BF16)] + [act] * 5
        + [jax.ShapeDtypeStruct((nb * TM, GROUP_W), F32)] * 2
        + [jax.ShapeDtypeStruct((TM, GROUP_W), F32)] * 2,
        compiler_params=_cparams(("arbitrary",), VMEM_LIMIT),
        name="inproj",
    )(xp, xs, shift, scale, g_norm, w_in_b, bd, gq8, gk8, cos_t, sin_t)


def _attn_heads(q, k, v, bias_ref, first_valid_col=None):
    qb_rows, kb_rows = q.shape[0], v.shape[0]
    assert qb_rows == 4 * CHUNK
    half_rows, span = qb_rows // 2, kb_rows - 2 * CHUNK
    parts = [(0, 0), (half_rows, 2 * CHUNK)]

    def softmax_part(s_full, hh, half, r0, c0):
        rs = half * qb_rows + r0
        s = s_full[rs:rs + half_rows, c0:c0 + span] + bias_ref[hh, r0:r0 + half_rows, c0:c0 + span]
        if first_valid_col is not None:
            col = lax.broadcasted_iota(I32, (half_rows, span), 1) + c0
            s = jnp.where(col >= first_valid_col, s, NEG_INF)
        m = jnp.max(s, axis=-1, keepdims=True)
        e = jnp.exp2(s - m)
        l = jnp.sum(e, axis=-1, keepdims=True)
        pad = [jnp.zeros((half_rows, c0), BF16)] if c0 else []
        pad_r = [jnp.zeros((half_rows, kb_rows - span - c0), BF16)] if kb_rows - span - c0 else []
        return jnp.concatenate(pad + [e.astype(BF16)] + pad_r, axis=1), l

    pair_w = 2 * HEAD_DIM_ATT
    low = lax.broadcasted_iota(I32, (1, pair_w), 1) < HEAD_DIM_ATT
    outs = []
    for pp in range(N_HEADS_ATT // 2):
        ps = slice(pp * pair_w, (pp + 1) * pair_w)
        q2, v2 = q[:, ps], v[:, ps]
        zero = jnp.zeros_like(q2)
        qs = jnp.concatenate([jnp.where(low, q2, zero), jnp.where(low, zero, q2)], axis=0)
        s = jnp.dot(qs, k[ps, :], preferred_element_type=F32)
        es, ls = zip(*[softmax_part(s, 2 * pp + half, half, r0, c0)
                       for half in range(2) for r0, c0 in parts])
        o = jnp.dot(jnp.concatenate(es, axis=0), v2, preferred_element_type=F32) / jnp.concatenate(ls, axis=0)
        outs.append(jnp.where(low, o[:qb_rows], o[qb_rows:]))
    return jnp.concatenate(outs, axis=1)


def _fill_band_bias(rev_ref, bias_scr):
    _, qb_rows, kb_rows = bias_scr.shape
    width = rev_ref.shape[1]
    q = lax.broadcasted_iota(I32, (qb_rows, kb_rows), 0)
    k = lax.broadcasted_iota(I32, (qb_rows, kb_rows), 1)
    qc = q >> 6
    kc = (k - ATT_WINDOW) >> 6
    band = (kc >= qc - ATT_WINDOW // CHUNK) & (kc <= qc)
    for hh in range(N_HEADS_ATT):
        rows = jnp.broadcast_to(rev_ref[hh:hh + 1, :], (qb_rows, width))
        toep = pltpu.roll(rows, width - MAX_REL, 1, stride=1, stride_axis=0)
        bias_scr[hh] = jnp.where(band, toep[:, :kb_rows] * LOG2_E, NEG_INF)


def _attn_sample_kernel(q_ref, kn_ref, vn_ref, kc_ref, vc_ref, rev_ref, o_ref, bias_scr):
    @pl.when(pl.program_id(0) == 0)
    def _():
        _fill_band_bias(rev_ref, bias_scr)

    q = q_ref[...]
    outs = []
    for hh in range(N_HEADS_ATT):
        hs = slice(hh * HEAD_DIM_ATT, (hh + 1) * HEAD_DIM_ATT)
        k = jnp.concatenate([kc_ref[0, :, hs], kn_ref[:, hs].astype(BF16)], axis=0)
        v = jnp.concatenate([vc_ref[0, :, hs], vn_ref[:, hs].astype(BF16)], axis=0)
        s = lax.dot_general(q[:, hs], k, (((1,), (1,)), ((), ())), preferred_element_type=F32)
        s = s + bias_scr[hh]
        m = jnp.max(s, axis=-1, keepdims=True)
        e = jnp.exp2(s - m)
        l = jnp.sum(e, axis=-1, keepdims=True)
        outs.append(jnp.dot(e.astype(BF16), v, preferred_element_type=F32) / l)
    o_ref[...] = jnp.concatenate(outs, axis=1).astype(BF16)


def _attn_sample(qa, ks_new, vs_new, kc, vc, rev, rp):
    ndb = kc.shape[0]
    base = rp // CHUNK
    spec = pl.BlockSpec((CHUNK, GROUP_W), lambda b: (base + b, 0))
    new = pl.BlockSpec((CHUNK, GROUP_W), lambda b: (b, 0))
    cspec = pl.BlockSpec((1, ATT_WINDOW, GROUP_W), lambda b: (b, 0, 0))
    return pl.pallas_call(
        _attn_sample_kernel,
        grid=(ndb,),
        in_specs=[spec, new, new, cspec, cspec,
                  pl.BlockSpec(rev.shape, lambda b: (0, 0))],
        out_specs=pl.BlockSpec((CHUNK, GROUP_W), lambda b: (b, 0)),
        out_shape=jax.ShapeDtypeStruct((ndb * CHUNK, GROUP_W), BF16),
        scratch_shapes=[pltpu.VMEM((N_HEADS_ATT, CHUNK, ATT_WINDOW + CHUNK), F32)],
        compiler_params=_cparams(("arbitrary",), VMEM_LIMIT),
        name="attn_sample",
    )(qa, ks_new, vs_new, kc, vc, rev)


def _ret_chunk(state_decay, q_ref, k_ref, v_ref, g_ref, dm_ref, xi_ref, zeta_ref, gro_ref, o_ref, s_scr):
    outs = []
    for hh in range(N_HEADS_RET):
        hs = slice(hh * HEAD_DIM_RET, (hh + 1) * HEAD_DIM_RET)
        q = q_ref[:, hs]
        k = k_ref[:, hs]
        v = v_ref[:, hs]
        st = s_scr[hh]
        sc = lax.dot_general(q, k, (((1,), (1,)), ((), ())), preferred_element_type=F32) * dm_ref[hh]
        inner = jnp.dot(sc.astype(BF16), v, preferred_element_type=F32)
        cross = jnp.dot(q, st.astype(BF16), preferred_element_type=F32) * xi_ref[:, hs]
        o = inner + cross
        kz = k.astype(F32) * zeta_ref[:, hs]
        s_scr[hh] = state_decay[hh] * st + jnp.dot(kz.T.astype(BF16), v, preferred_element_type=F32)
        mu = jnp.mean(o, axis=-1, keepdims=True)
        oc = o - mu
        var = jnp.mean(oc * oc, axis=-1, keepdims=True)
        outs.append(oc * lax.rsqrt(var + NORM_EPS))
    y = jnp.concatenate(outs, axis=1) * gro_ref[...]
    g = g_ref[...].astype(F32)
    o_ref[...] = (g * jax.nn.sigmoid(g) * y).astype(BF16)


def _ret_kernel(state_decay, q_ref, k_ref, v_ref, g_ref, s0_ref, dm_ref, xi_ref, zeta_ref,
                gro_ref, o_ref, sn_ref, s_scr):
    j = pl.program_id(1)

    @pl.when(j == 0)
    def _():
        s_scr[...] = s0_ref[0]

    _ret_chunk(state_decay, q_ref, k_ref, v_ref, g_ref, dm_ref, xi_ref, zeta_ref, gro_ref, o_ref, s_scr)

    @pl.when(j == pl.num_programs(1) - 1)
    def _():
        sn_ref[0] = s_scr[...]


def _mix_prompt_kernel(state_decay, q_ref, k0_ref, k1_ref, k2_ref, v0_ref, v1_ref, v2_ref, rev_ref,
                       rq_ref, rk_ref, rv_ref, rg_ref, s0_ref, dm_ref, xi_ref, zeta_ref, gro_ref,
                       att_ref, ret_ref, sn_ref, bias_scr, s_scr):
    j = pl.program_id(1)

    @pl.when((pl.program_id(0) == 0) & (j == 0))
    def _():
        _fill_band_bias(rev_ref, bias_scr)

    @pl.when(j == 0)
    def _():
        s_scr[...] = s0_ref[0]

    k = jnp.concatenate([k0_ref[...], k1_ref[...], k2_ref[...]], axis=1)
    v = jnp.concatenate([v0_ref[...], v1_ref[...], v2_ref[...]], axis=0)

    def block(first_valid_col):
        att_ref[...] = _attn_heads(q_ref[...], k, v, bias_scr, first_valid_col).astype(BF16)
        _ret_chunk(state_decay, rq_ref, rk_ref, rv_ref, rg_ref, dm_ref, xi_ref, zeta_ref, gro_ref,
                   ret_ref, s_scr)

    @pl.when(j >= 2)
    def _():
        block(None)

    @pl.when(j < 2)
    def _():
        block((2 - j) * ATT_QB)

    @pl.when(j == pl.num_programs(1) - 1)
    def _():
        sn_ref[0] = s_scr[...]


def _mix_prompt(qa, ka_t, va, rev, qb, kb, vb, gb, s0, g_ro, nb, seq):
    assert ATT_QB == RET_CB
    r = nb * seq
    nq = seq // ATT_QB
    dm, xi, zeta, state_decay = _ret_consts(RET_CB)
    blk = lambda back: (lambda b, j: (b * nq + jnp.maximum(j - back, 0), 0))
    spec = lambda back: pl.BlockSpec((ATT_QB, GROUP_W), blk(back))
    tspec = lambda back: pl.BlockSpec((GROUP_W, ATT_QB), lambda b, j: (0, b * nq + jnp.maximum(j - back, 0)))
    sspec = pl.BlockSpec((1, N_HEADS_RET, HEAD_DIM_RET, HEAD_DIM_RET), lambda b, j: (b, 0, 0, 0))
    full2 = lambda b, j: (0, 0)
    out = jax.ShapeDtypeStruct((r, GROUP_W), BF16)
    return pl.pallas_call(
        functools.partial(_mix_prompt_kernel, state_decay),
        grid=(nb, nq),
        in_specs=[spec(0), tspec(2), tspec(1), tspec(0), spec(2), spec(1), spec(0),
                  pl.BlockSpec(rev.shape, full2),
                  spec(0), spec(0), spec(0), spec(0), sspec,
                  pl.BlockSpec(dm.shape, lambda b, j: (0, 0, 0)),
                  pl.BlockSpec(xi.shape, full2), pl.BlockSpec(zeta.shape, full2),
                  pl.BlockSpec((1, GROUP_W), full2)],
        out_specs=[spec(0), spec(0), sspec],
        out_shape=[out, out, jax.ShapeDtypeStruct(s0.shape, F32)],
        scratch_shapes=[pltpu.VMEM((N_HEADS_ATT, ATT_QB, ATT_WINDOW + ATT_QB), F32),
                        pltpu.VMEM((N_HEADS_RET, HEAD_DIM_RET, HEAD_DIM_RET), F32)],
        compiler_params=_cparams(("arbitrary", "arbitrary"), VMEM_LIMIT),
        name="mix_prompt",
    )(qa, ka_t, ka_t, ka_t, va, va, va, rev, qb, kb, vb, gb, s0, dm, xi, zeta, g_ro)


def _ret_consts(cb):
    log_g = np.log1p(-np.exp2(-RET_DECAY_OFFSET - np.arange(N_HEADS_RET, dtype=np.float64)))
    n = np.arange(cb, dtype=np.float64)
    diff = n[:, None] - n[None, :]
    dm = np.where(diff[None] >= 0, np.exp(np.maximum(diff, 0.0)[None] * log_g[:, None, None]), 0.0)
    xi = np.exp((n + 1.0)[:, None] * log_g[None, :])
    zeta = np.exp((cb - 1.0 - n)[:, None] * log_g[None, :])
    rep = lambda a: np.repeat(a, HEAD_DIM_RET, axis=1)
    state_decay = tuple(float(v) for v in np.exp(cb * log_g))
    return (jnp.asarray(dm, F32), jnp.asarray(rep(xi), F32), jnp.asarray(rep(zeta), F32), state_decay)


def _ret(qb, kb, vb, gb, s0, g_ro, cb, row0, nb, nc, name):
    dm, xi, zeta, state_decay = _ret_consts(cb)
    base = row0 // cb
    spec = pl.BlockSpec((cb, GROUP_W), lambda b, j: (base + b * nc + j, 0))
    sspec = pl.BlockSpec((1, N_HEADS_RET, HEAD_DIM_RET, HEAD_DIM_RET), lambda b, j: (b, 0, 0, 0))
    full2 = lambda b, j: (0, 0)
    return pl.pallas_call(
        functools.partial(_ret_kernel, state_decay),
        grid=(nb, nc),
        in_specs=[spec, spec, spec, spec, sspec,
                  pl.BlockSpec(dm.shape, lambda b, j: (0, 0, 0)),
                  pl.BlockSpec(xi.shape, full2), pl.BlockSpec(zeta.shape, full2),
                  pl.BlockSpec((1, GROUP_W), full2)],
        out_specs=[pl.BlockSpec((cb, GROUP_W), lambda b, j: (b * nc + j, 0)), sspec],
        out_shape=[jax.ShapeDtypeStruct((nb * nc * cb, GROUP_W), BF16),
                   jax.ShapeDtypeStruct(s0.shape, F32)],
        scratch_shapes=[pltpu.VMEM((N_HEADS_RET, HEAD_DIM_RET, HEAD_DIM_RET), F32)],
        compiler_params=_cparams(("arbitrary", "arbitrary"), VMEM_LIMIT),
        name=name,
    )(qb, kb, vb, gb, s0, dm, xi, zeta, g_ro)


def _outproj_kernel(npp, attp_ref, atts_ref, retp_ref, rets_ref, xp_ref, xs_ref, gm_ref, shf_ref, scf_ref,
                    gn_ref, wo_ref, wr_ref, br_ref, upper_ref, lower_ref,
                    x1_ref, h2_ref, slot_ref, cols_ref, cnt_ref):
    is_p = pl.program_id(0) < npp
    subs = range(2)
    rows = [slice(sub * TM, (sub + 1) * TM) for sub in subs]

    def pick(p_ref, s_ref, sub):
        return jnp.where(is_p, p_ref[rows[sub], :], s_ref[...])

    mix = [jnp.dot(pick(attp_ref, atts_ref, sub), wo_ref[:GROUP_W, :], preferred_element_type=F32)
           + jnp.dot(pick(retp_ref, rets_ref, sub), wo_ref[GROUP_W:, :], preferred_element_type=F32)
           for sub in subs]
    h2b = []
    for sub in subs:
        x1 = _per_group(mix[sub], lambda a, gm: a * gm, gm_ref[...]) + pick(xp_ref, xs_ref, sub)
        x1_ref[rows[sub], :] = x1
        y = _rms_rows(x1, gn_ref[...])
        h2 = _per_group(y, lambda a, sh, sc: a * (1.0 + sc) + sh, shf_ref[...], scf_ref[...])
        h2b.append(h2.astype(BF16))
        h2_ref[rows[sub], :] = h2b[sub]

    work = [lax.dot_general(wr_ref[...], h2b[sub], (((1,), (1,)), ((), ())),
                            preferred_element_type=F32) + br_ref[...] for sub in subs]
    eidx = lax.broadcasted_iota(I32, work[0].shape, 0).astype(F32)
    sel = [[] for _ in subs]
    top = [[] for _ in subs]
    for _ in range(TOP_K):
        for sub in subs:
            m = jnp.max(work[sub], axis=0, keepdims=True)
            idx = jnp.min(jnp.where(work[sub] == m, eidx, float(N_EXPERTS)), axis=0, keepdims=True)
            hit = eidx == idx
            sel[sub].append(hit)
            top[sub].append(m)
            work[sub] = jnp.where(hit, -jnp.inf, work[sub])

    for sub in subs:
        ex = [jnp.exp(t - top[sub][0]) for t in top[sub]]
        den = ex[0] + ex[1] + ex[2] + ex[3]
        gates = [e / den for e in ex]
        hits = sel[sub]
        multi_f = jnp.where(hits[0] | hits[1] | hits[2] | hits[3], 1.0, 0.0)
        rank = jnp.dot(multi_f.astype(BF16), upper_ref[...], preferred_element_type=F32)
        cnt = jnp.sum(multi_f, axis=1, keepdims=True)
        cnt_pad = jnp.maximum(jnp.floor((cnt + (SEG_ALIGN - 1.0)) * (1.0 / SEG_ALIGN)), 1.0) * SEG_ALIGN
        cnt_pad_b = jnp.broadcast_to(cnt_pad, (N_EXPERTS, 128))
        seg_off = jnp.dot(lower_ref[...], cnt_pad_b.astype(BF16), preferred_element_type=F32)[:, :1]
        pos = seg_off + rank
        slot_rows = jnp.concatenate(
            [jnp.sum(jnp.where(h, pos, 0.0), axis=0, keepdims=True) for h in hits], axis=0)
        gate_rows = jnp.concatenate(gates, axis=0)
        slot_ref[sub] = slot_rows.astype(I32)
        cnt_ref[sub] = cnt_pad_b.astype(I32)
        both = jnp.concatenate([slot_rows, gate_rows, jnp.zeros((128 - 2 * TOP_K, TM), F32)], axis=0)
        cols_ref[sub] = both.T


def _outproj(att_p, att_s, ret_p, ret_s, xp, xs, gate_m, shift_f, scale_f, g_norm, w_out_b, wr_t, br,
             upper, lower, nb, tps):
    rp = xp.shape[0]
    ntp = rp // TM
    assert ntp % 2 == 0 and tps % 2 == 0
    npp = ntp // 2
    nt2 = ntp + 2
    r = nt2 * TM
    row = lambda p: (p, 0)
    row3 = lambda p: (p, 0, 0)
    full = lambda p: (0, 0)
    prow = lambda p: (jnp.minimum(p, npp - 1), 0)
    mod = pl.BlockSpec((GROUPS_PER_TILE, D_MODEL), _mod_row(npp, tps // 2, nb))
    return pl.pallas_call(
        functools.partial(_outproj_kernel, npp),
        grid=(npp + 1,),
        in_specs=[pl.BlockSpec((2 * TM, GROUP_W), prow), pl.BlockSpec((TM, GROUP_W), full),
                  pl.BlockSpec((2 * TM, GROUP_W), prow), pl.BlockSpec((TM, GROUP_W), full),
                  pl.BlockSpec((2 * TM, D_MODEL), prow),
                  pl.BlockSpec((TM, D_MODEL), full),
                  mod, mod, mod,
                  pl.BlockSpec((1, D_MODEL), full),
                  pl.BlockSpec((D_MODEL, D_MODEL), full),
                  pl.BlockSpec((N_EXPERTS, D_MODEL), full),
                  pl.BlockSpec((N_EXPERTS, 1), full),
                  pl.BlockSpec((TM, TM), full),
                  pl.BlockSpec((N_EXPERTS, N_EXPERTS), full)],
        out_specs=[pl.BlockSpec((2 * TM, D_MODEL), row), pl.BlockSpec((2 * TM, D_MODEL), row),
                   pl.BlockSpec((2, TOP_K, TM), row3),
                   pl.BlockSpec((2, TM, 128), row3), pl.BlockSpec((2, N_EXPERTS, 128), row3)],
        out_shape=[jax.ShapeDtypeStruct((r, D_MODEL), F32), jax.ShapeDtypeStruct((r, D_MODEL), BF16),
                   jax.ShapeDtypeStruct((nt2, TOP_K, TM), I32),
                   jax.ShapeDtypeStruct((nt2, TM, 128), F32), jax.ShapeDtypeStruct((nt2, N_EXPERTS, 128), I32)],
        compiler_params=_cparams(("arbitrary",), VMEM_LIMIT),
        name="outproj",
    )(att_p, att_s, ret_p, ret_s, xp, xs, gate_m, shift_f, scale_f, g_norm, w_out_b, wr_t, br, upper, lower)


def _rows_copy(n, src_rows, dst_rows, sem):
    size = pl.multiple_of(n, SEG_ALIGN)
    return pltpu.make_async_copy(src_rows(size), dst_rows(size), sem)


def _start_segments(t, cnt_ref, off_ref, base_ref, local_rows, sorted_rows, sem, to_sorted):
    for e in range(N_EXPERTS):
        n = cnt_ref[t * N_EXPERTS + e]
        off = pl.multiple_of(off_ref[t * N_EXPERTS + e], SEG_ALIGN)
        base = pl.multiple_of(base_ref[t * N_EXPERTS + e], SEG_ALIGN)
        local = lambda z, off=off: local_rows(off, z)
        remote = lambda z, base=base: sorted_rows(base, z)
        (_rows_copy(n, local, remote, sem) if to_sorted else _rows_copy(n, remote, local, sem)).start()


def _dispatch_kernel(nt, n_blocks, off_ref, cnt_ref, base_ref, tot_ref, tail0_ref, tailn_ref, na_ref,
                     h2_ref, slot_ref, slotn_ref, xb_ref, xs_scr, hot_scr, zero_scr, sems, tail_sem):
    i = pl.program_id(0)
    cur = i % 2
    sorted_rows = lambda r, z: xb_ref.at[pl.ds(r, z), :]

    def start_tile(t, buf):
        _start_segments(t, cnt_ref, off_ref, base_ref, lambda r, z: xs_scr.at[buf, pl.ds(r, z), :],
                        sorted_rows, sems.at[buf], True)

    def wait_tile(t, buf):
        _rows_copy(tot_ref[t], lambda z: xs_scr.at[buf, pl.ds(0, z), :], lambda z: sorted_rows(0, z),
                   sems.at[buf]).wait()

    def tail_copies(wait):
        def body(e, c):
            base = pl.multiple_of(tail0_ref[e], SEG_ALIGN)

            @pl.when(tailn_ref[e] > 0)
            def _():
                cp = _rows_copy(tailn_ref[e], lambda z: zero_scr.at[pl.ds(0, z), :],
                                lambda z: sorted_rows(base, z), tail_sem)
                cp.wait() if wait else cp.start()
            return c
        lax.fori_loop(0, N_EXPERTS, body, 0)

        def unused(j, c):
            cp = pltpu.make_async_copy(zero_scr, sorted_rows(pl.multiple_of(j * BM, BM), BM), tail_sem)
            cp.wait() if wait else cp.start()
            return c
        lax.fori_loop(na_ref[0], n_blocks, unused, 0)

    @pl.when(i >= 2)
    def _():
        wait_tile(i - 2, cur)

    def onehot(slot):
        srow = lax.broadcasted_iota(I32, (CAP, TM), 0)
        hit = (srow == slot[0:1]) | (srow == slot[1:2]) | (srow == slot[2:3]) | (srow == slot[3:4])
        return jnp.where(hit, 1.0, 0.0).astype(BF16)

    @pl.when(i == 0)
    def _():
        hot_scr[0] = onehot(slot_ref[0])
        zero_scr[...] = jnp.zeros_like(zero_scr)
        tail_copies(False)

    def sort_rows(rows):
        xs_scr[cur, :rows, :] = jnp.dot(hot_scr[cur, :rows, :], h2_ref[...],
                                        preferred_element_type=F32).astype(BF16)
        hot_scr[1 - cur] = onehot(slotn_ref[0])
        start_tile(i, cur)

    @pl.when(tot_ref[i] <= CAP_SHORT)
    def _():
        sort_rows(CAP_SHORT)

    @pl.when(tot_ref[i] > CAP_SHORT)
    def _():
        sort_rows(CAP)

    @pl.when(i == nt - 1)
    def _():
        if nt >= 2:
            wait_tile(i - 1, 1 - cur)
        wait_tile(i, cur)
        tail_copies(True)


def _dispatch(h2, slot, off, cnt, base, tot, tail0, tailn, n_act, n_blocks):
    nt = tot.shape[0]
    n_rows = n_blocks * BM
    grid_spec = pltpu.PrefetchScalarGridSpec(
        num_scalar_prefetch=7,
        grid=(nt,),
        in_specs=[pl.BlockSpec((TM, D_MODEL), lambda i, *_: (i, 0)),
                  pl.BlockSpec((1, TOP_K, TM), lambda i, *_: (i, 0, 0)),
                  pl.BlockSpec((1, TOP_K, TM), lambda i, *_: (jnp.minimum(i + 1, nt - 1), 0, 0))],
        out_specs=pl.BlockSpec(memory_space=pl.ANY),
        scratch_shapes=[pltpu.VMEM((2, CAP, D_MODEL), BF16),
                        pltpu.VMEM((2, CAP, TM), BF16),
                        pltpu.VMEM((BM, D_MODEL), BF16),
                        pltpu.SemaphoreType.DMA((2,)),
                        pltpu.SemaphoreType.DMA(())],
    )
    return pl.pallas_call(
        functools.partial(_dispatch_kernel, nt, n_blocks),
        grid_spec=grid_spec,
        out_shape=jax.ShapeDtypeStruct((n_rows, D_MODEL), BF16),
        compiler_params=_cparams(("arbitrary",), VMEM_LIMIT),
        name="dispatch",
    )(off, cnt, base, tot, tail0, tailn, n_act, h2, slot, slot)


def _experts_kernel(be_ref, bi_ref, nx_ref, half_ref, na_ref, x_ref, wu_hbm, bu_ref, wd_hbm, bd_ref, y_ref,
                    wu_stage, wd_stage, wu_scr, wd_scr, sems):
    j = pl.program_id(0)

    def weight_copies(e):
        return (pltpu.make_async_copy(wu_hbm.at[e], wu_stage, sems.at[0]),
                pltpu.make_async_copy(wd_hbm.at[e], wd_stage, sems.at[1]))

    @pl.when(j < na_ref[0])
    def _():
        e = be_ref[j]
        prev = be_ref[jnp.maximum(j - 1, 0)]

        @pl.when(j == 0)
        def _():
            for cp in weight_copies(e):
                cp.start()

        @pl.when((j == 0) | (e != prev))
        def _():
            for cp in weight_copies(e):
                cp.wait()
            wu_scr[...] = wu_stage[...].astype(BF16)
            wd_scr[...] = wd_stage[...].astype(BF16)

            @pl.when(nx_ref[j] != e)
            def _():
                for cp in weight_copies(nx_ref[j]):
                    cp.start()

        def ffn(rows):
            u = jnp.dot(x_ref[rows, :], wu_scr[...], preferred_element_type=F32) + bu_ref[0]
            glu = jnp.minimum(u[:, :D_FF], SWIGLU_LIMIT)
            lin = jnp.clip(u[:, D_FF:], -SWIGLU_LIMIT, SWIGLU_LIMIT)
            act = glu * jax.nn.sigmoid(SWIGLU_ALPHA * glu) * (lin + 1.0)
            y = jnp.dot(act.astype(BF16), wd_scr[...], preferred_element_type=F32) + bd_ref[0]
            y_ref[rows, :] = y.astype(BF16)

        @pl.when(half_ref[j] == 0)
        def _():
            ffn(slice(0, BM))

        @pl.when(half_ref[j] != 0)
        def _():
            ffn(slice(0, BM // 2))
            y_ref[BM // 2:, :] = jnp.zeros((BM // 2, D_MODEL), BF16)


def _experts(xb, blk_e, blk_i, blk_nx, blk_half, n_act, w_up, b_up, w_down, b_down):
    n_rows = xb.shape[0]
    nblk = n_rows // BM
    grid_spec = pltpu.PrefetchScalarGridSpec(
        num_scalar_prefetch=5,
        grid=(nblk,),
        in_specs=[pl.BlockSpec((BM, D_MODEL), lambda j, be, bi, *_: (bi[j], 0)),
                  pl.BlockSpec(memory_space=pl.ANY),
                  pl.BlockSpec((1, 1, 2 * D_FF), lambda j, be, *_: (be[j], 0, 0)),
                  pl.BlockSpec(memory_space=pl.ANY),
                  pl.BlockSpec((1, 1, D_MODEL), lambda j, be, *_: (be[j], 0, 0))],
        out_specs=pl.BlockSpec((BM, D_MODEL), lambda j, be, bi, *_: (bi[j], 0)),
        scratch_shapes=[pltpu.VMEM((D_MODEL, 2 * D_FF), F32), pltpu.VMEM((D_FF, D_MODEL), F32),
                        pltpu.VMEM((D_MODEL, 2 * D_FF), BF16), pltpu.VMEM((D_FF, D_MODEL), BF16),
                        pltpu.SemaphoreType.DMA((2,))],
    )
    return pl.pallas_call(
        _experts_kernel,
        grid_spec=grid_spec,
        out_shape=jax.ShapeDtypeStruct((n_rows, D_MODEL), BF16),
        input_output_aliases={5: 0},
        compiler_params=_cparams(("arbitrary",), VMEM_LIMIT),
        name="experts",
    )(blk_e, blk_i, blk_nx, blk_half, n_act, xb, w_up, b_up.reshape(N_EXPERTS, 1, 2 * D_FF), w_down,
      b_down.reshape(N_EXPERTS, 1, D_MODEL))


def _combine_kernel(nt, ntp, off_ref, cnt_ref, base_ref, tot_ref, yb_ref, cols_ref, x1_ref, gf_ref,
                    op_ref, os_ref, ys_scr, sems):
    i = pl.program_id(0)
    cur = i % 2

    sorted_rows = lambda r, z: yb_ref.at[pl.ds(r, z), :]

    def start_tile(t, buf):
        _start_segments(t, cnt_ref, off_ref, base_ref, lambda r, z: ys_scr.at[buf, pl.ds(r, z), :],
                        sorted_rows, sems.at[buf], False)

    @pl.when(i == 0)
    def _():
        ys_scr[...] = jnp.zeros_like(ys_scr)
        start_tile(0, 0)

    def wait_tile(t, buf):
        _rows_copy(tot_ref[t], lambda z: sorted_rows(0, z), lambda z: ys_scr.at[buf, pl.ds(0, z), :],
                   sems.at[buf]).wait()

    nxt = jnp.minimum(i + 1, nt - 1)
    wait_tile(i, cur)
    start_tile(nxt, 1 - cur)

    @pl.when(i == nt - 1)
    def _():
        wait_tile(nxt, 1 - cur)

    def weighted_sum(rows):
        cols = cols_ref[0]
        lane = lax.broadcasted_iota(I32, (TM, rows), 1)
        w = jnp.zeros((TM, rows), F32)
        for k in range(TOP_K):
            sk = cols[:, k:k + 1].astype(I32)
            gk = cols[:, TOP_K + k:TOP_K + k + 1]
            w = jnp.where(lane == sk, gk, w)
        y = jnp.dot(w.astype(BF16), ys_scr[cur, :rows, :], preferred_element_type=F32)
        out = x1_ref[...] + _per_group(y, lambda a, gf: a * gf, gf_ref[...])

        @pl.when(i < ntp)
        def _():
            op_ref[...] = out

        @pl.when(i >= ntp)
        def _():
            os_ref[...] = out

    @pl.when(tot_ref[i] <= CAP_SHORT)
    def _():
        weighted_sum(CAP_SHORT)

    @pl.when(tot_ref[i] > CAP_SHORT)
    def _():
        weighted_sum(CAP)


def _combine(yb, cols, x1, gate_f, off, cnt, base, tot, ntp, nb, tps):
    nt = tot.shape[0]
    grid_spec = pltpu.PrefetchScalarGridSpec(
        num_scalar_prefetch=4,
        grid=(nt,),
        in_specs=[pl.BlockSpec(memory_space=pl.ANY),
                  pl.BlockSpec((1, TM, 128), lambda i, *_: (i, 0, 0)),
                  pl.BlockSpec((TM, D_MODEL), lambda i, *_: (i, 0)),
                  pl.BlockSpec((GROUPS_PER_TILE, D_MODEL), _mod_row(ntp, tps, nb))],
        out_specs=[pl.BlockSpec((TM, D_MODEL), lambda i, *_: (jnp.minimum(i, ntp - 1), 0)),
                   pl.BlockSpec((TM, D_MODEL), lambda i, *_: (0, 0))],
        scratch_shapes=[pltpu.VMEM((2, CAP, D_MODEL), BF16), pltpu.SemaphoreType.DMA((2,))],
    )
    return pl.pallas_call(
        functools.partial(_combine_kernel, nt, ntp),
        grid_spec=grid_spec,
        out_shape=[jax.ShapeDtypeStruct((ntp * TM, D_MODEL), F32),
                   jax.ShapeDtypeStruct((TM, D_MODEL), F32)],
        compiler_params=_cparams(("arbitrary",), VMEM_LIMIT),
        name="combine",
    )(off, cnt, base, tot, yb, cols, x1, gate_f)


def _rotary_tables(seq, dec_batch, dec_seq):
    half = HEAD_DIM_RET // 2
    inv = ROPE_BASE ** (-np.arange(half, dtype=np.float64) / half)
    pos = np.concatenate([np.arange(seq), np.tile(PAST_LEN + np.arange(dec_seq), dec_batch)])
    ang = pos.astype(np.float64)[:, None] * inv[None, :]
    cos = np.concatenate([np.cos(ang), np.cos(ang)], axis=1)
    sin = np.concatenate([-np.sin(ang), np.sin(ang)], axis=1)
    return jnp.asarray(cos, F32), jnp.asarray(sin, F32)


def _rel_bias_reversed(rel_bias):
    heads = rel_bias.shape[0]
    ext = jnp.concatenate([rel_bias[:, 1:], jnp.broadcast_to(rel_bias[:, -1:], (heads, 2 * MAX_REL))], axis=1)
    return ext[:, ::-1].astype(F32)


def _group_mods(m, nb, ndb):
    assert ndb == GROUPS_PER_TILE
    mp = jnp.broadcast_to(m[:nb, None], (nb, GROUPS_PER_TILE) + m.shape[1:])
    allm = jnp.concatenate([mp.reshape((nb * GROUPS_PER_TILE,) + m.shape[1:]), m[nb:]], axis=0)
    return jnp.transpose(allm, (1, 0, 2))


def _routing_tables(cnt, n_blocks):
    nt = cnt.shape[0]
    off = jnp.cumsum(cnt, axis=1) - cnt
    rows_e = jnp.sum(cnt, axis=0)
    nblk_e = (rows_e + BM - 1) // BM
    blk_end = jnp.cumsum(nblk_e)
    start_e = (blk_end - nblk_e) * BM
    base = start_e[None, :] + jnp.cumsum(cnt, axis=0) - cnt
    n_act = blk_end[-1]
    j = jnp.minimum(jnp.arange(n_blocks), n_act - 1)
    blk_e = jnp.minimum(jnp.sum(blk_end[None, :] <= j[:, None], axis=1), N_EXPERTS - 1)
    later = jnp.where(blk_e[None, :] > blk_e[:, None], blk_e[None, :], N_EXPERTS)
    blk_nx = jnp.min(later, axis=1)
    blk_nx = jnp.where(blk_nx == N_EXPERTS, blk_e, blk_nx)
    tail0 = start_e + rows_e
    tailn = nblk_e * BM - rows_e
    mine = blk_e[:, None] == jnp.arange(N_EXPERTS)[None, :]
    blk_rows = jnp.sum(jnp.where(mine, (rows_e + start_e)[None, :], 0), axis=1) - j * BM
    blk_half = blk_rows <= BM // 2
    i32 = lambda a: a.astype(I32)
    return (i32(off.reshape(nt * N_EXPERTS)), i32(cnt.reshape(nt * N_EXPERTS)),
            i32(base.reshape(nt * N_EXPERTS)), i32(jnp.sum(cnt, axis=1)), i32(tail0), i32(tailn),
            i32(blk_e), i32(j), i32(blk_nx), i32(blk_half), i32(n_act.reshape(1)))


def kernel(x_prompt, x_sample, c_prompt, c_sample, cache_att_k, cache_att_v, state_ret, w_ada, b_ada,
           g_norm_mix, g_norm_ffn, w_in, g_q, g_k, rel_bias, g_ret_out, w_out, w_router, b_router,
           w_up, b_up, w_down, b_down):
    nb, seq, d = x_prompt.shape
    ndb, dseq, _ = x_sample.shape
    assert d == D_MODEL and ndb * dseq == TM and dseq == CHUNK
    assert seq % TM == 0 and seq >= ATT_WINDOW and cache_att_k.shape[2] == ATT_WINDOW
    assert w_ada.shape[0] == 1
    rp = nb * seq
    ntp = rp // TM
    nt = ntp + 1
    tps = seq // TM

    xp = x_prompt.reshape(rp, d)
    xs = x_sample.reshape(TM, d)

    m = _ada(jnp.concatenate([c_prompt, c_sample], axis=0), w_ada[0], b_ada[0])
    mods = _group_mods(m.reshape(nb + ndb, N_ADA, d), nb, ndb)
    shift_m, scale_m, gate_m, shift_f, scale_f, gate_f = [mods[a] for a in range(N_ADA)]

    cos_t, sin_t = _rotary_tables(seq, ndb, dseq)
    bd = jnp.asarray(np.kron(np.eye(N_HEADS_ATT // 2), np.ones((HEAD_DIM_ATT, HEAD_DIM_ATT))), BF16)
    tile8 = lambda g: jnp.tile(g.astype(F32), N_HEADS_ATT).reshape(1, GROUP_W)
    (qa, ka_t, va, qb, kb, vb, gb, kp_tail, vp_tail, ks_new, vs_new) = _inproj(
        xp, xs, shift_m, scale_m, g_norm_mix[0].reshape(1, d), w_in[0].astype(BF16), bd,
        tile8(g_q[0]) * (HEAD_DIM_ATT ** -0.5 * LOG2_E), tile8(g_k[0]), cos_t, sin_t, nb, tps)

    rev = _rel_bias_reversed(rel_bias[0])
    g_ro = g_ret_out[0].astype(F32).reshape(1, GROUP_W)
    zero_state = jnp.zeros((nb, N_HEADS_RET, HEAD_DIM_RET, HEAD_DIM_RET), F32)
    att_p, ret_p, state_p = _mix_prompt(qa, ka_t, va, rev, qb, kb, vb, gb, zero_state, g_ro, nb, seq)
    att_s = _attn_sample(qa, ks_new, vs_new,
                         cache_att_k[0].reshape(ndb, ATT_WINDOW, GROUP_W).astype(BF16),
                         cache_att_v[0].reshape(ndb, ATT_WINDOW, GROUP_W).astype(BF16), rev, rp)

    ret_s, state_s = _ret(qb, kb, vb, gb, state_ret[0].astype(F32), g_ro, CHUNK, rp, ndb, 1, "ret_sample")

    upper = jnp.asarray(np.triu(np.ones((TM, TM)), 1), BF16)
    lower = jnp.asarray(np.tril(np.ones((N_EXPERTS, N_EXPERTS)), -1), BF16)
    x1, h2, slot, cols, cnt = _outproj(
        att_p, att_s, ret_p, ret_s, xp, xs, gate_m, shift_f, scale_f, g_norm_ffn[0].reshape(1, d),
        w_out[0].astype(BF16), w_router[0].T.astype(BF16), b_router[0].astype(F32).reshape(N_EXPERTS, 1),
        upper, lower, nb, tps)

    n_blocks = (TOP_K * (rp + TM) + nt * N_EXPERTS * SEG_ALIGN) // BM + 1 + N_EXPERTS
    (off, cntf, base, tot, tail0, tailn, blk_e, blk_i, blk_nx, blk_half,
     n_act) = _routing_tables(cnt[:nt, :, 0], n_blocks)
    xb = _dispatch(h2, slot, off, cntf, base, tot, tail0, tailn, n_act, n_blocks)
    yb = _experts(xb, blk_e, blk_i, blk_nx, blk_half, n_act, w_up[0], b_up[0], w_down[0], b_down[0])
    out_p, out_s = _combine(yb, cols, x1, gate_f, off, cntf, base, tot, ntp, nb, tps)

    heads = (N_HEADS_ATT, HEAD_DIM_ATT)
    return (out_p.reshape(nb, seq, d), out_s.reshape(ndb, dseq, d),
            kp_tail.reshape(1, nb, ATT_WINDOW, *heads), vp_tail.reshape(1, nb, ATT_WINDOW, *heads),
            state_p[None],
            ks_new.reshape(1, ndb, dseq, *heads), vs_new.reshape(1, ndb, dseq, *heads),
            state_s[None])
```

```python
import functools

import numpy as np
import jax
import jax.numpy as jnp
from jax import lax
from jax.experimental import pallas as pl
from jax.experimental.pallas import tpu as pltpu

F32 = jnp.float32
BF16 = jnp.bfloat16
I32 = jnp.int32

D_MODEL = 1024
GROUP_W = 512
N_SLOTS = 7
N_HEADS_ATT = 8
HEAD_DIM_ATT = 64
N_HEADS_RET = 4
HEAD_DIM_RET = 128
CHUNK = 64
ATT_WINDOW = 512
MAX_REL = 256
PAST_LEN = 2048
RET_DECAY_OFFSET = 5.0
ROPE_BASE = 10000.0
N_EXPERTS = 32
TOP_K = 4
D_FF = 1024
SWIGLU_LIMIT = 7.0
SWIGLU_ALPHA = 1.702
N_ADA = 6
NORM_EPS = 1e-6
NEG_INF = -1e30
LOG2_E = 1.4426950408889634

TM = 512
GROUPS_PER_TILE = TM // CHUNK
ATT_QB = 256
RET_CB = 256
SEG_ALIGN = 16
CAP = TOP_K * TM + N_EXPERTS * SEG_ALIGN
CAP_SHORT = CAP - 256
BM = 512
VMEM_LIMIT = 56 * 1024 * 1024


def _cparams(sem, vmem=None):
    return pltpu.CompilerParams(dimension_semantics=sem, vmem_limit_bytes=vmem)


def _ada_kernel(c_ref, w_ref, b_ref, o_ref):
    c = c_ref[...]
    s = c * jax.nn.sigmoid(c)
    o_ref[...] = jnp.dot(s.astype(BF16), w_ref[...].astype(BF16),
                         preferred_element_type=F32) + b_ref[...]


def _ada(c_all, w_ada, b_ada):
    n, d = c_all.shape
    cols = w_ada.shape[1]
    tn = 1536
    return pl.pallas_call(
        _ada_kernel,
        grid=(cols // tn,),
        in_specs=[pl.BlockSpec((n, d), lambda j: (0, 0)),
                  pl.BlockSpec((d, tn), lambda j: (0, j)),
                  pl.BlockSpec((1, tn), lambda j: (0, j))],
        out_specs=pl.BlockSpec((n, tn), lambda j: (0, j)),
        out_shape=jax.ShapeDtypeStruct((n, cols), F32),
        compiler_params=_cparams(("arbitrary",), VMEM_LIMIT),
        name="ada",
    )(c_all, w_ada, b_ada.reshape(1, cols))


def _rms_rows(x, g):
    ms = jnp.mean(x * x, axis=-1, keepdims=True)
    return x * lax.rsqrt(ms + NORM_EPS) * g


def _mod_row(ntp, tps, nb):
    return lambda i, *_: (jnp.where(i < ntp, i // tps, nb), 0)


def _per_group(x, fn, *mods):
    x3 = x.reshape(GROUPS_PER_TILE, CHUNK, x.shape[-1])
    y3 = fn(x3, *[m[:, None, :] for m in mods])
    return y3.reshape(x.shape)


def _inproj_kernel(npp, xp_ref, xs_ref, sh_ref, sc_ref, gn_ref, w_ref, bd_ref, gq_ref, gk_ref,
                   cos_ref, sin_ref,
                   qa_ref, ka_ref, va_ref, qb_ref, kb_ref, vb_ref, gb_ref,
                   kpt_ref, vpt_ref, kst_ref, vst_ref):
    is_p = pl.program_id(0) < npp
    subs = range(2)
    rows = [slice(sub * TM, (sub + 1) * TM) for sub in subs]

    hb = []
    for sub in subs:
        x = jnp.where(is_p, xp_ref[rows[sub], :], xs_ref[...])
        y = _rms_rows(x, gn_ref[...])
        h = _per_group(y, lambda a, sh, sc: a * (1.0 + sc) + sh, sh_ref[...], sc_ref[...])
        hb.append(h.astype(BF16))

    def proj(s):
        return [jnp.dot(hb[sub], w_ref[:, s * GROUP_W:(s + 1) * GROUP_W], preferred_element_type=F32)
                for sub in subs]

    def head_rms(z, g):
        zz = (z * z).astype(BF16)
        half = GROUP_W // 2
        ss = jnp.concatenate(
            [jnp.dot(zz[:, :half], bd_ref[...], preferred_element_type=F32),
             jnp.dot(zz[:, half:], bd_ref[...], preferred_element_type=F32)], axis=1)
        return z * lax.rsqrt(ss * (1.0 / HEAD_DIM_ATT) + NORM_EPS) * g

    def rot(z, sub):
        cos = cos_ref[rows[sub], :]
        sin = sin_ref[rows[sub], :]
        outs = []
        for hh in range(N_HEADS_RET):
            zh = z[:, hh * HEAD_DIM_RET:(hh + 1) * HEAD_DIM_RET]
            outs.append(zh * cos + pltpu.roll(zh, HEAD_DIM_RET // 2, axis=1) * sin)
        return jnp.concatenate(outs, axis=1)

    for sub, z in zip(subs, proj(0)):
        qa_ref[rows[sub], :] = head_rms(z, gq_ref[...]).astype(BF16)
    ka = [head_rms(z, gk_ref[...]) for z in proj(1)]
    for sub in subs:
        ka_ref[:, rows[sub]] = ka[sub].T.astype(BF16)
    va = proj(2)
    for sub in subs:
        va_ref[rows[sub], :] = va[sub].astype(BF16)

    @pl.when(is_p)
    def _():
        kpt_ref[...] = ka[1]
        vpt_ref[...] = va[1]

    @pl.when(jnp.logical_not(is_p))
    def _():
        kst_ref[...] = ka[0]
        vst_ref[...] = va[0]

    for sub, z in zip(subs, proj(3)):
        qb_ref[rows[sub], :] = rot(z, sub).astype(BF16)
    for sub, z in zip(subs, proj(4)):
        kb_ref[rows[sub], :] = (rot(z, sub) * (HEAD_DIM_RET ** -0.5)).astype(BF16)
    for sub, z in zip(subs, proj(5)):
        vb_ref[rows[sub], :] = z.astype(BF16)
    for sub, z in zip(subs, proj(6)):
        gb_ref[rows[sub], :] = z.astype(BF16)


def _inproj(xp, xs, shift, scale, g_norm, w_in_b, bd, gq8, gk8, cos_t, sin_t, nb, tps):
    rp = xp.shape[0]
    ntp = rp // TM
    assert ntp % 2 == 0 and tps % 2 == 0
    npp = ntp // 2
    r = (ntp + 2) * TM
    row = lambda p: (p, 0)
    full = lambda p: (0, 0)
    tab = lambda p: (jnp.where(p < npp, p % (tps // 2), tps // 2), 0)
    mod = pl.BlockSpec((GROUPS_PER_TILE, D_MODEL), _mod_row(npp, tps // 2, nb))
    tail_spec = pl.BlockSpec((TM, GROUP_W), lambda p: (jnp.minimum(p // (tps // 2), nb - 1), 0))
    act = jax.ShapeDtypeStruct((r, GROUP_W), BF16)
    return pl.pallas_call(
        functools.partial(_inproj_kernel, npp),
        grid=(npp + 1,),
        in_specs=[pl.BlockSpec((2 * TM, D_MODEL), lambda p: (jnp.minimum(p, npp - 1), 0)),
                  pl.BlockSpec((TM, D_MODEL), full),
                  mod, mod,
                  pl.BlockSpec((1, D_MODEL), full),
                  pl.BlockSpec((D_MODEL, N_SLOTS * GROUP_W), full),
                  pl.BlockSpec((GROUP_W // 2, GROUP_W // 2), full),
                  pl.BlockSpec((1, GROUP_W), full),
                  pl.BlockSpec((1, GROUP_W), full),
                  pl.BlockSpec((2 * TM, HEAD_DIM_RET), tab),
                  pl.BlockSpec((2 * TM, HEAD_DIM_RET), tab)],
        out_specs=[pl.BlockSpec((2 * TM, GROUP_W), row), pl.BlockSpec((GROUP_W, 2 * TM), lambda p: (0, p))]
        + [pl.BlockSpec((2 * TM, GROUP_W), row)] * 5 + [
            tail_spec, tail_spec,
            pl.BlockSpec((TM, GROUP_W), full),
            pl.BlockSpec((TM, GROUP_W), full)],
        out_shape=[act, jax.ShapeDtypeStruct((GROUP_W, r), BF16)] + [act] * 5
        + [jax.ShapeDtypeStruct((nb * TM, GROUP_W), F32)] * 2
        + [jax.ShapeDtypeStruct((TM, GROUP_W), F32)] * 2,
        compiler_params=_cparams(("arbitrary",), VMEM_LIMIT),
        name="inproj",
    )(xp, xs, shift, scale, g_norm, w_in_b, bd, gq8, gk8, cos_t, sin_t)


def _attn_heads(q, k, v, bias_ref, first_valid_col=None):
    qb_rows, kb_rows = q.shape[0], v.shape[0]
    assert qb_rows == 4 * CHUNK
    half_rows, span = qb_rows // 2, kb_rows - 2 * CHUNK
    parts = [(0, 0), (half_rows, 2 * CHUNK)]

    def softmax_part(s_full, hh, half, r0, c0):
        rs = half * qb_rows + r0
        s = s_full[rs:rs + half_rows, c0:c0 + span] + bias_ref[hh, r0:r0 + half_rows, c0:c0 + span]
        if first_valid_col is not None:
            col = lax.broadcasted_iota(I32, (half_rows, span), 1) + c0
            s = jnp.where(col >= first_valid_col, s, NEG_INF)
        m = jnp.max(s, axis=-1, keepdims=True)
        e = jnp.exp2(s - m)
        l = jnp.sum(e, axis=-1, keepdims=True)
        pad = [jnp.zeros((half_rows, c0), BF16)] if c0 else []
        pad_r = [jnp.zeros((half_rows, kb_rows - span - c0), BF16)] if kb_rows - span - c0 else []
        return jnp.concatenate(pad + [e.astype(BF16)] + pad_r, axis=1), l

    pair_w = 2 * HEAD_DIM_ATT
    low = lax.broadcasted_iota(I32, (1, pair_w), 1) < HEAD_DIM_ATT
    outs = []
    for pp in range(N_HEADS_ATT // 2):
        ps = slice(pp * pair_w, (pp + 1) * pair_w)
        q2, v2 = q[:, ps], v[:, ps]
        zero = jnp.zeros_like(q2)
        qs = jnp.concatenate([jnp.where(low, q2, zero), jnp.where(low, zero, q2)], axis=0)
        s = jnp.dot(qs, k[ps, :], preferred_element_type=F32)
        es, ls = zip(*[softmax_part(s, 2 * pp + half, half, r0, c0)
                       for half in range(2) for r0, c0 in parts])
        o = jnp.dot(jnp.concatenate(es, axis=0), v2, preferred_element_type=F32) / jnp.concatenate(ls, axis=0)
        outs.append(jnp.where(low, o[:qb_rows], o[qb_rows:]))
    return jnp.concatenate(outs, axis=1)


def _fill_band_bias(rev_ref, bias_scr):
    _, qb_rows, kb_rows = bias_scr.shape
    width = rev_ref.shape[1]
    q = lax.broadcasted_iota(I32, (qb_rows, kb_rows), 0)
    k = lax.broadcasted_iota(I32, (qb_rows, kb_rows), 1)
    qc = q >> 6
    kc = (k - ATT_WINDOW) >> 6
    band = (kc >= qc - ATT_WINDOW // CHUNK) & (kc <= qc)
    for hh in range(N_HEADS_ATT):
        rows = jnp.broadcast_to(rev_ref[hh:hh + 1, :], (qb_rows, width))
        toep = pltpu.roll(rows, width - MAX_REL, 1, stride=1, stride_axis=0)
        bias_scr[hh] = jnp.where(band, toep[:, :kb_rows] * LOG2_E, NEG_INF)


def _attn_sample_kernel(q_ref, kn_ref, vn_ref, kc_ref, vc_ref, rev_ref, o_ref, bias_scr):
    @pl.when(pl.program_id(0) == 0)
    def _():
        _fill_band_bias(rev_ref, bias_scr)

    q = q_ref[...]
    outs = []
    for hh in range(N_HEADS_ATT):
        hs = slice(hh * HEAD_DIM_ATT, (hh + 1) * HEAD_DIM_ATT)
        k = jnp.concatenate([kc_ref[0, :, hs], kn_ref[:, hs].astype(BF16)], axis=0)
        v = jnp.concatenate([vc_ref[0, :, hs], vn_ref[:, hs].astype(BF16)], axis=0)
        s = lax.dot_general(q[:, hs], k, (((1,), (1,)), ((), ())), preferred_element_type=F32)
        s = s + bias_scr[hh]
        m = jnp.max(s, axis=-1, keepdims=True)
        e = jnp.exp2(s - m)
        l = jnp.sum(e, axis=-1, keepdims=True)
        outs.append(jnp.dot(e.astype(BF16), v, preferred_element_type=F32) / l)
    o_ref[...] = jnp.concatenate(outs, axis=1).astype(BF16)


def _attn_sample(qa, ks_new, vs_new, kc, vc, rev, rp):
    ndb = kc.shape[0]
    base = rp // CHUNK
    spec = pl.BlockSpec((CHUNK, GROUP_W), lambda b: (base + b, 0))
    new = pl.BlockSpec((CHUNK, GROUP_W), lambda b: (b, 0))
    cspec = pl.BlockSpec((1, ATT_WINDOW, GROUP_W), lambda b: (b, 0, 0))
    return pl.pallas_call(
        _attn_sample_kernel,
        grid=(ndb,),
        in_specs=[spec, new, new, cspec, cspec,
                  pl.BlockSpec(rev.shape, lambda b: (0, 0))],
        out_specs=pl.BlockSpec((CHUNK, GROUP_W), lambda b: (b, 0)),
        out_shape=jax.ShapeDtypeStruct((ndb * CHUNK, GROUP_W), BF16),
        scratch_shapes=[pltpu.VMEM((N_HEADS_ATT, CHUNK, ATT_WINDOW + CHUNK), F32)],
        compiler_params=_cparams(("arbitrary",), VMEM_LIMIT),
        name="attn_sample",
    )(qa, ks_new, vs_new, kc, vc, rev)


def _ret_chunk(state_decay, q_ref, k_ref, v_ref, g_ref, dm_ref, xi_ref, zeta_ref, gro_ref, o_ref, s_scr):
    outs = []
    for hh in range(N_HEADS_RET):
        hs = slice(hh * HEAD_DIM_RET, (hh + 1) * HEAD_DIM_RET)
        q = q_ref[:, hs]
        k = k_ref[:, hs]
        v = v_ref[:, hs]
        st = s_scr[hh]
        sc = lax.dot_general(q, k, (((1,), (1,)), ((), ())), preferred_element_type=F32) * dm_ref[hh]
        inner = jnp.dot(sc.astype(BF16), v, preferred_element_type=F32)
        cross = jnp.dot(q, st.astype(BF16), preferred_element_type=F32) * xi_ref[:, hs]
        o = inner + cross
        kz = k.astype(F32) * zeta_ref[:, hs]
        s_scr[hh] = state_decay[hh] * st + jnp.dot(kz.T.astype(BF16), v, preferred_element_type=F32)
        mu = jnp.mean(o, axis=-1, keepdims=True)
        oc = o - mu
        var = jnp.mean(oc * oc, axis=-1, keepdims=True)
        outs.append(oc * lax.rsqrt(var + NORM_EPS))
    y = jnp.concatenate(outs, axis=1) * gro_ref[...]
    g = g_ref[...].astype(F32)
    o_ref[...] = (g * jax.nn.sigmoid(g) * y).astype(BF16)


def _ret_kernel(state_decay, q_ref, k_ref, v_ref, g_ref, s0_ref, dm_ref, xi_ref, zeta_ref,
                gro_ref, o_ref, sn_ref, s_scr):
    j = pl.program_id(1)

    @pl.when(j == 0)
    def _():
        s_scr[...] = s0_ref[0]

    _ret_chunk(state_decay, q_ref, k_ref, v_ref, g_ref, dm_ref, xi_ref, zeta_ref, gro_ref, o_ref, s_scr)

    @pl.when(j == pl.num_programs(1) - 1)
    def _():
        sn_ref[0] = s_scr[...]


def _mix_prompt_kernel(state_decay, q_ref, k0_ref, k1_ref, k2_ref, v0_ref, v1_ref, v2_ref, rev_ref,
                       rq_ref, rk_ref, rv_ref, rg_ref, s0_ref, dm_ref, xi_ref, zeta_ref, gro_ref,
                       att_ref, ret_ref, sn_ref, bias_scr, s_scr):
    j = pl.program_id(1)

    @pl.when((pl.program_id(0) == 0) & (j == 0))
    def _():
        _fill_band_bias(rev_ref, bias_scr)

    @pl.when(j == 0)
    def _():
        s_scr[...] = s0_ref[0]

    k = jnp.concatenate([k0_ref[...], k1_ref[...], k2_ref[...]], axis=1)
    v = jnp.concatenate([v0_ref[...], v1_ref[...], v2_ref[...]], axis=0)

    def block(first_valid_col):
        att_ref[...] = _attn_heads(q_ref[...], k, v, bias_scr, first_valid_col).astype(BF16)
        _ret_chunk(state_decay, rq_ref, rk_ref, rv_ref, rg_ref, dm_ref, xi_ref, zeta_ref, gro_ref,
                   ret_ref, s_scr)

    @pl.when(j >= 2)
    def _():
        block(None)

    @pl.when(j < 2)
    def _():
        block((2 - j) * ATT_QB)

    @pl.when(j == pl.num_programs(1) - 1)
    def _():
        sn_ref[0] = s_scr[...]


def _mix_prompt(qa, ka_t, va, rev, qb, kb, vb, gb, s0, g_ro, nb, seq):
    assert ATT_QB == RET_CB
    r = nb * seq
    nq = seq // ATT_QB
    dm, xi, zeta, state_decay = _ret_consts(RET_CB)
    blk = lambda back: (lambda b, j: (b * nq + jnp.maximum(j - back, 0), 0))
    spec = lambda back: pl.BlockSpec((ATT_QB, GROUP_W), blk(back))
    tspec = lambda back: pl.BlockSpec((GROUP_W, ATT_QB), lambda b, j: (0, b * nq + jnp.maximum(j - back, 0)))
    sspec = pl.BlockSpec((1, N_HEADS_RET, HEAD_DIM_RET, HEAD_DIM_RET), lambda b, j: (b, 0, 0, 0))
    full2 = lambda b, j: (0, 0)
    out = jax.ShapeDtypeStruct((r, GROUP_W), BF16)
    return pl.pallas_call(
        functools.partial(_mix_prompt_kernel, state_decay),
        grid=(nb, nq),
        in_specs=[spec(0), tspec(2), tspec(1), tspec(0), spec(2), spec(1), spec(0),
                  pl.BlockSpec(rev.shape, full2),
                  spec(0), spec(0), spec(0), spec(0), sspec,
                  pl.BlockSpec(dm.shape, lambda b, j: (0, 0, 0)),
                  pl.BlockSpec(xi.shape, full2), pl.BlockSpec(zeta.shape, full2),
                  pl.BlockSpec((1, GROUP_W), full2)],
        out_specs=[spec(0), spec(0), sspec],
        out_shape=[out, out, jax.ShapeDtypeStruct(s0.shape, F32)],
        scratch_shapes=[pltpu.VMEM((N_HEADS_ATT, ATT_QB, ATT_WINDOW + ATT_QB), F32),
                        pltpu.VMEM((N_HEADS_RET, HEAD_DIM_RET, HEAD_DIM_RET), F32)],
        compiler_params=_cparams(("arbitrary", "arbitrary"), VMEM_LIMIT),
        name="mix_prompt",
    )(qa, ka_t, ka_t, ka_t, va, va, va, rev, qb, kb, vb, gb, s0, dm, xi, zeta, g_ro)


def _ret_consts(cb):
    log_g = np.log1p(-np.exp2(-RET_DECAY_OFFSET - np.arange(N_HEADS_RET, dtype=np.float64)))
    n = np.arange(cb, dtype=np.float64)
    diff = n[:, None] - n[None, :]
    dm = np.where(diff[None] >= 0, np.exp(np.maximum(diff, 0.0)[None] * log_g[:, None, None]), 0.0)
    xi = np.exp((n + 1.0)[:, None] * log_g[None, :])
    zeta = np.exp((cb - 1.0 - n)[:, None] * log_g[None, :])
    rep = lambda a: np.repeat(a, HEAD_DIM_RET, axis=1)
    state_decay = tuple(float(v) for v in np.exp(cb * log_g))
    return (jnp.asarray(dm, F32), jnp.asarray(rep(xi), F32), jnp.asarray(rep(zeta), F32), state_decay)


def _ret(qb, kb, vb, gb, s0, g_ro, cb, row0, nb, nc, name):
    dm, xi, zeta, state_decay = _ret_consts(cb)
    base = row0 // cb
    spec = pl.BlockSpec((cb, GROUP_W), lambda b, j: (base + b * nc + j, 0))
    sspec = pl.BlockSpec((1, N_HEADS_RET, HEAD_DIM_RET, HEAD_DIM_RET), lambda b, j: (b, 0, 0, 0))
    full2 = lambda b, j: (0, 0)
    return pl.pallas_call(
        functools.partial(_ret_kernel, state_decay),
        grid=(nb, nc),
        in_specs=[spec, spec, spec, spec, sspec,
                  pl.BlockSpec(dm.shape, lambda b, j: (0, 0, 0)),
                  pl.BlockSpec(xi.shape, full2), pl.BlockSpec(zeta.shape, full2),
                  pl.BlockSpec((1, GROUP_W), full2)],
        out_specs=[pl.BlockSpec((cb, GROUP_W), lambda b, j: (b * nc + j, 0)), sspec],
        out_shape=[jax.ShapeDtypeStruct((nb * nc * cb, GROUP_W), BF16),
                   jax.ShapeDtypeStruct(s0.shape, F32)],
        scratch_shapes=[pltpu.VMEM((N_HEADS_RET, HEAD_DIM_RET, HEAD_DIM_RET), F32)],
        compiler_params=_cparams(("arbitrary", "arbitrary"), VMEM_LIMIT),
        name=name,
    )(qb, kb, vb, gb, s0, dm, xi, zeta, g_ro)


def _outproj_kernel(npp, attp_ref, atts_ref, retp_ref, rets_ref, xp_ref, xs_ref, gm_ref, shf_ref, scf_ref,
                    gn_ref, wo_ref, wr_ref, br_ref, upper_ref, lower_ref,
                    x1_ref, h2_ref, slot_ref, cols_ref, cnt_ref):
    is_p = pl.program_id(0) < npp
    subs = range(2)
    rows = [slice(sub * TM, (sub + 1) * TM) for sub in subs]

    def pick(p_ref, s_ref, sub):
        return jnp.where(is_p, p_ref[rows[sub], :], s_ref[...])

    mix = [jnp.dot(pick(attp_ref, atts_ref, sub), wo_ref[:GROUP_W, :], preferred_element_type=F32)
           + jnp.dot(pick(retp_ref, rets_ref, sub), wo_ref[GROUP_W:, :], preferred_element_type=F32)
           for sub in subs]
    h2b = []
    for sub in subs:
        x1 = _per_group(mix[sub], lambda a, gm: a * gm, gm_ref[...]) + pick(xp_ref, xs_ref, sub)
        x1_ref[rows[sub], :] = x1
        y = _rms_rows(x1, gn_ref[...])
        h2 = _per_group(y, lambda a, sh, sc: a * (1.0 + sc) + sh, shf_ref[...], scf_ref[...])
        h2b.append(h2.astype(BF16))
        h2_ref[rows[sub], :] = h2b[sub]

    work = [lax.dot_general(wr_ref[...], h2b[sub], (((1,), (1,)), ((), ())),
                            preferred_element_type=F32) + br_ref[...] for sub in subs]
    eidx = lax.broadcasted_iota(I32, work[0].shape, 0).astype(F32)
    sel = [[] for _ in subs]
    top = [[] for _ in subs]
    for _ in range(TOP_K):
        for sub in subs:
            m = jnp.max(work[sub], axis=0, keepdims=True)
            idx = jnp.min(jnp.where(work[sub] == m, eidx, float(N_EXPERTS)), axis=0, keepdims=True)
            hit = eidx == idx
            sel[sub].append(hit)
            top[sub].append(m)
            work[sub] = jnp.where(hit, -jnp.inf, work[sub])

    for sub in subs:
        ex = [jnp.exp(t - top[sub][0]) for t in top[sub]]
        den = ex[0] + ex[1] + ex[2] + ex[3]
        gates = [e / den for e in ex]
        hits = sel[sub]
        multi_f = jnp.where(hits[0] | hits[1] | hits[2] | hits[3], 1.0, 0.0)
        rank = jnp.dot(multi_f.astype(BF16), upper_ref[...], preferred_element_type=F32)
        cnt = jnp.sum(multi_f, axis=1, keepdims=True)
        cnt_pad = jnp.maximum(jnp.floor((cnt + (SEG_ALIGN - 1.0)) * (1.0 / SEG_ALIGN)), 1.0) * SEG_ALIGN
        cnt_pad_b = jnp.broadcast_to(cnt_pad, (N_EXPERTS, 128))
        seg_off = jnp.dot(lower_ref[...], cnt_pad_b.astype(BF16), preferred_element_type=F32)[:, :1]
        pos = seg_off + rank
        slot_rows = jnp.concatenate(
            [jnp.sum(jnp.where(h, pos, 0.0), axis=0, keepdims=True) for h in hits], axis=0)
        gate_rows = jnp.concatenate(gates, axis=0)
        slot_ref[sub] = slot_rows.astype(I32)
        cnt_ref[sub] = cnt_pad_b.astype(I32)
        both = jnp.concatenate([slot_rows, gate_rows, jnp.zeros((128 - 2 * TOP_K, TM), F32)], axis=0)
        cols_ref[sub] = both.T


def _outproj(att_p, att_s, ret_p, ret_s, xp, xs, gate_m, shift_f, scale_f, g_norm, w_out_b, wr_t, br,
             upper, lower, nb, tps):
    rp = xp.shape[0]
    ntp = rp // TM
    assert ntp % 2 == 0 and tps % 2 == 0
    npp = ntp // 2
    nt2 = ntp + 2
    r = nt2 * TM
    row = lambda p: (p, 0)
    row3 = lambda p: (p, 0, 0)
    full = lambda p: (0, 0)
    prow = lambda p: (jnp.minimum(p, npp - 1), 0)
    mod = pl.BlockSpec((GROUPS_PER_TILE, D_MODEL), _mod_row(npp, tps // 2, nb))
    return pl.pallas_call(
        functools.partial(_outproj_kernel, npp),
        grid=(npp + 1,),
        in_specs=[pl.BlockSpec((2 * TM, GROUP_W), prow), pl.BlockSpec((TM, GROUP_W), full),
                  pl.BlockSpec((2 * TM, GROUP_W), prow), pl.BlockSpec((TM, GROUP_W), full),
                  pl.BlockSpec((2 * TM, D_MODEL), prow),
                  pl.BlockSpec((TM, D_MODEL), full),
                  mod, mod, mod,
                  pl.BlockSpec((1, D_MODEL), full),
                  pl.BlockSpec((D_MODEL, D_MODEL), full),
                  pl.BlockSpec((N_EXPERTS, D_MODEL), full),
                  pl.BlockSpec((N_EXPERTS, 1), full),
                  pl.BlockSpec((TM, TM), full),
                  pl.BlockSpec((N_EXPERTS, N_EXPERTS), full)],
        out_specs=[pl.BlockSpec((2 * TM, D_MODEL), row), pl.BlockSpec((2 * TM, D_MODEL), row),
                   pl.BlockSpec((2, TOP_K, TM), row3),
                   pl.BlockSpec((2, TM, 128), row3), pl.BlockSpec((2, N_EXPERTS, 128), row3)],
        out_shape=[jax.ShapeDtypeStruct((r, D_MODEL), F32), jax.ShapeDtypeStruct((r, D_MODEL), BF16),
                   jax.ShapeDtypeStruct((nt2, TOP_K, TM), I32),
                   jax.ShapeDtypeStruct((nt2, TM, 128), F32), jax.ShapeDtypeStruct((nt2, N_EXPERTS, 128), I32)],
        compiler_params=_cparams(("arbitrary",), VMEM_LIMIT),
        name="outproj",
    )(att_p, att_s, ret_p, ret_s, xp, xs, gate_m, shift_f, scale_f, g_norm, w_out_b, wr_t, br, upper, lower)


def _rows_copy(n, src_rows, dst_rows, sem):
    size = pl.multiple_of(n, SEG_ALIGN)
    return pltpu.make_async_copy(src_rows(size), dst_rows(size), sem)


def _start_segments(t, cnt_ref, off_ref, base_ref, local_rows, sorted_rows, sem, to_sorted):
    for e in range(N_EXPERTS):
        n = cnt_ref[t * N_EXPERTS + e]
        off = pl.multiple_of(off_ref[t * N_EXPERTS + e], SEG_ALIGN)
        base = pl.multiple_of(base_ref[t * N_EXPERTS + e], SEG_ALIGN)
        local = lambda z, off=off: local_rows(off, z)
        remote = lambda z, base=base: sorted_rows(base, z)
        (_rows_copy(n, local, remote, sem) if to_sorted else _rows_copy(n, remote, local, sem)).start()


def _dispatch_kernel(nt, n_blocks, off_ref, cnt_ref, base_ref, tot_ref, tail0_ref, tailn_ref, na_ref,
                     h2_ref, slot_ref, slotn_ref, xb_ref, xs_scr, hot_scr, zero_scr, sems, tail_sem):
    i = pl.program_id(0)
    cur = i % 2
    sorted_rows = lambda r, z: xb_ref.at[pl.ds(r, z), :]

    def start_tile(t, buf):
        _start_segments(t, cnt_ref, off_ref, base_ref, lambda r, z: xs_scr.at[buf, pl.ds(r, z), :],
                        sorted_rows, sems.at[buf], True)

    def wait_tile(t, buf):
        _rows_copy(tot_ref[t], lambda z: xs_scr.at[buf, pl.ds(0, z), :], lambda z: sorted_rows(0, z),
                   sems.at[buf]).wait()

    def tail_copies(wait):
        def body(e, c):
            base = pl.multiple_of(tail0_ref[e], SEG_ALIGN)

            @pl.when(tailn_ref[e] > 0)
            def _():
                cp = _rows_copy(tailn_ref[e], lambda z: zero_scr.at[pl.ds(0, z), :],
                                lambda z: sorted_rows(base, z), tail_sem)
                cp.wait() if wait else cp.start()
            return c
        lax.fori_loop(0, N_EXPERTS, body, 0)

        def unused(j, c):
            cp = pltpu.make_async_copy(zero_scr, sorted_rows(pl.multiple_of(j * BM, BM), BM), tail_sem)
            cp.wait() if wait else cp.start()
            return c
        lax.fori_loop(na_ref[0], n_blocks, unused, 0)

    @pl.when(i >= 2)
    def _():
        wait_tile(i - 2, cur)

    def onehot(slot):
        srow = lax.broadcasted_iota(I32, (CAP, TM), 0)
        hit = (srow == slot[0:1]) | (srow == slot[1:2]) | (srow == slot[2:3]) | (srow == slot[3:4])
        return jnp.where(hit, 1.0, 0.0).astype(BF16)

    @pl.when(i == 0)
    def _():
        hot_scr[0] = onehot(slot_ref[0])
        zero_scr[...] = jnp.zeros_like(zero_scr)
        tail_copies(False)

    def sort_rows(rows):
        xs_scr[cur, :rows, :] = jnp.dot(hot_scr[cur, :rows, :], h2_ref[...],
                                        preferred_element_type=F32).astype(BF16)
        hot_scr[1 - cur] = onehot(slotn_ref[0])
        start_tile(i, cur)

    @pl.when(tot_ref[i] <= CAP_SHORT)
    def _():
        sort_rows(CAP_SHORT)

    @pl.when(tot_ref[i] > CAP_SHORT)
    def _():
        sort_rows(CAP)

    @pl.when(i == nt - 1)
    def _():
        if nt >= 2:
            wait_tile(i - 1, 1 - cur)
        wait_tile(i, cur)
        tail_copies(True)


def _dispatch(h2, slot, off, cnt, base, tot, tail0, tailn, n_act, n_blocks):
    nt = tot.shape[0]
    n_rows = n_blocks * BM
    grid_spec = pltpu.PrefetchScalarGridSpec(
        num_scalar_prefetch=7,
        grid=(nt,),
        in_specs=[pl.BlockSpec((TM, D_MODEL), lambda i, *_: (i, 0)),
                  pl.BlockSpec((1, TOP_K, TM), lambda i, *_: (i, 0, 0)),
                  pl.BlockSpec((1, TOP_K, TM), lambda i, *_: (jnp.minimum(i + 1, nt - 1), 0, 0))],
        out_specs=pl.BlockSpec(memory_space=pl.ANY),
        scratch_shapes=[pltpu.VMEM((2, CAP, D_MODEL), BF16),
                        pltpu.VMEM((2, CAP, TM), BF16),
                        pltpu.VMEM((BM, D_MODEL), BF16),
                        pltpu.SemaphoreType.DMA((2,)),
                        pltpu.SemaphoreType.DMA(())],
    )
    return pl.pallas_call(
        functools.partial(_dispatch_kernel, nt, n_blocks),
        grid_spec=grid_spec,
        out_shape=jax.ShapeDtypeStruct((n_rows, D_MODEL), BF16),
        compiler_params=_cparams(("arbitrary",), VMEM_LIMIT),
        name="dispatch",
    )(off, cnt, base, tot, tail0, tailn, n_act, h2, slot, slot)


def _experts_kernel(be_ref, bi_ref, nx_ref, half_ref, na_ref, x_ref, wu_hbm, bu_ref, wd_hbm, bd_ref, y_ref,
                    wu_stage, wd_stage, wu_scr, wd_scr, sems):
    j = pl.program_id(0)

    def weight_copies(e):
        return (pltpu.make_async_copy(wu_hbm.at[e], wu_stage, sems.at[0]),
                pltpu.make_async_copy(wd_hbm.at[e], wd_stage, sems.at[1]))

    @pl.when(j < na_ref[0])
    def _():
        e = be_ref[j]
        prev = be_ref[jnp.maximum(j - 1, 0)]

        @pl.when(j == 0)
        def _():
            for cp in weight_copies(e):
                cp.start()

        @pl.when((j == 0) | (e != prev))
        def _():
            for cp in weight_copies(e):
                cp.wait()
            wu_scr[...] = wu_stage[...].astype(BF16)
            wd_scr[...] = wd_stage[...].astype(BF16)

            @pl.when(nx_ref[j] != e)
            def _():
                for cp in weight_copies(nx_ref[j]):
                    cp.start()

        def ffn(rows):
            u = jnp.dot(x_ref[rows, :], wu_scr[...], preferred_element_type=F32) + bu_ref[0]
            glu = jnp.minimum(u[:, :D_FF], SWIGLU_LIMIT)
            lin = jnp.clip(u[:, D_FF:], -SWIGLU_LIMIT, SWIGLU_LIMIT)
            act = glu * jax.nn.sigmoid(SWIGLU_ALPHA * glu) * (lin + 1.0)
            y = jnp.dot(act.astype(BF16), wd_scr[...], preferred_element_type=F32) + bd_ref[0]
            y_ref[rows, :] = y.astype(BF16)

        @pl.when(half_ref[j] == 0)
        def _():
            ffn(slice(0, BM))

        @pl.when(half_ref[j] != 0)
        def _():
            ffn(slice(0, BM // 2))
            y_ref[BM // 2:, :] = jnp.zeros((BM // 2, D_MODEL), BF16)


def _experts(xb, blk_e, blk_i, blk_nx, blk_half, n_act, w_up, b_up, w_down, b_down):
    n_rows = xb.shape[0]
    nblk = n_rows // BM
    grid_spec = pltpu.PrefetchScalarGridSpec(
        num_scalar_prefetch=5,
        grid=(nblk,),
        in_specs=[pl.BlockSpec((BM, D_MODEL), lambda j, be, bi, *_: (bi[j], 0)),
                  pl.BlockSpec(memory_space=pl.ANY),
                  pl.BlockSpec((1, 1, 2 * D_FF), lambda j, be, *_: (be[j], 0, 0)),
                  pl.BlockSpec(memory_space=pl.ANY),
                  pl.BlockSpec((1, 1, D_MODEL), lambda j, be, *_: (be[j], 0, 0))],
        out_specs=pl.BlockSpec((BM, D_MODEL), lambda j, be, bi, *_: (bi[j], 0)),
        scratch_shapes=[pltpu.VMEM((D_MODEL, 2 * D_FF), F32), pltpu.VMEM((D_FF, D_MODEL), F32),
                        pltpu.VMEM((D_MODEL, 2 * D_FF), BF16), pltpu.VMEM((D_FF, D_MODEL), BF16),
                        pltpu.SemaphoreType.DMA((2,))],
    )
    return pl.pallas_call(
        _experts_kernel,
        grid_spec=grid_spec,
        out_shape=jax.ShapeDtypeStruct((n_rows, D_MODEL), BF16),
        input_output_aliases={5: 0},
        compiler_params=_cparams(("arbitrary",), VMEM_LIMIT),
        name="experts",
    )(blk_e, blk_i, blk_nx, blk_half, n_act, xb, w_up, b_up.reshape(N_EXPERTS, 1, 2 * D_FF), w_down,
      b_down.reshape(N_EXPERTS, 1, D_MODEL))


def _combine_kernel(nt, ntp, off_ref, cnt_ref, base_ref, tot_ref, yb_ref, cols_ref, x1_ref, gf_ref,
                    op_ref, os_ref, ys_scr, sems):
    i = pl.program_id(0)
    cur = i % 2

    sorted_rows = lambda r, z: yb_ref.at[pl.ds(r, z), :]

    def start_tile(t, buf):
        _start_segments(t, cnt_ref, off_ref, base_ref, lambda r, z: ys_scr.at[buf, pl.ds(r, z), :],
                        sorted_rows, sems.at[buf], False)

    @pl.when(i == 0)
    def _():
        ys_scr[...] = jnp.zeros_like(ys_scr)
        start_tile(0, 0)

    def wait_tile(t, buf):
        _rows_copy(tot_ref[t], lambda z: sorted_rows(0, z), lambda z: ys_scr.at[buf, pl.ds(0, z), :],
                   sems.at[buf]).wait()

    nxt = jnp.minimum(i + 1, nt - 1)
    wait_tile(i, cur)
    start_tile(nxt, 1 - cur)

    @pl.when(i == nt - 1)
    def _():
        wait_tile(nxt, 1 - cur)

    def weighted_sum(rows):
        cols = cols_ref[0]
        lane = lax.broadcasted_iota(I32, (TM, rows), 1)
        w = jnp.zeros((TM, rows), F32)
        for k in range(TOP_K):
            sk = cols[:, k:k + 1].astype(I32)
            gk = cols[:, TOP_K + k:TOP_K + k + 1]
            w = jnp.where(lane == sk, gk, w)
        y = jnp.dot(w.astype(BF16), ys_scr[cur, :rows, :], preferred_element_type=F32)
        out = x1_ref[...] + _per_group(y, lambda a, gf: a * gf, gf_ref[...])

        @pl.when(i < ntp)
        def _():
            op_ref[...] = out

        @pl.when(i >= ntp)
        def _():
            os_ref[...] = out

    @pl.when(tot_ref[i] <= CAP_SHORT)
    def _():
        weighted_sum(CAP_SHORT)

    @pl.when(tot_ref[i] > CAP_SHORT)
    def _():
        weighted_sum(CAP)


def _combine(yb, cols, x1, gate_f, off, cnt, base, tot, ntp, nb, tps):
    nt = tot.shape[0]
    grid_spec = pltpu.PrefetchScalarGridSpec(
        num_scalar_prefetch=4,
        grid=(nt,),
        in_specs=[pl.BlockSpec(memory_space=pl.ANY),
                  pl.BlockSpec((1, TM, 128), lambda i, *_: (i, 0, 0)),
                  pl.BlockSpec((TM, D_MODEL), lambda i, *_: (i, 0)),
                  pl.BlockSpec((GROUPS_PER_TILE, D_MODEL), _mod_row(ntp, tps, nb))],
        out_specs=[pl.BlockSpec((TM, D_MODEL), lambda i, *_: (jnp.minimum(i, ntp - 1), 0)),
                   pl.BlockSpec((TM, D_MODEL), lambda i, *_: (0, 0))],
        scratch_shapes=[pltpu.VMEM((2, CAP, D_MODEL), BF16), pltpu.SemaphoreType.DMA((2,))],
    )
    return pl.pallas_call(
        functools.partial(_combine_kernel, nt, ntp),
        grid_spec=grid_spec,
        out_shape=[jax.ShapeDtypeStruct((ntp * TM, D_MODEL), F32),
                   jax.ShapeDtypeStruct((TM, D_MODEL), F32)],
        compiler_params=_cparams(("arbitrary",), VMEM_LIMIT),
        name="combine",
    )(off, cnt, base, tot, yb, cols, x1, gate_f)


def _rotary_tables(seq, dec_batch, dec_seq):
    half = HEAD_DIM_RET // 2
    inv = ROPE_BASE ** (-np.arange(half, dtype=np.float64) / half)
    pos = np.concatenate([np.arange(seq), np.tile(PAST_LEN + np.arange(dec_seq), 2 * dec_batch)])
    ang = pos.astype(np.float64)[:, None] * inv[None, :]
    cos = np.concatenate([np.cos(ang), np.cos(ang)], axis=1)
    sin = np.concatenate([-np.sin(ang), np.sin(ang)], axis=1)
    return jnp.asarray(cos, F32), jnp.asarray(sin, F32)


def _rel_bias_reversed(rel_bias):
    heads = rel_bias.shape[0]
    ext = jnp.concatenate([rel_bias[:, 1:], jnp.broadcast_to(rel_bias[:, -1:], (heads, 2 * MAX_REL))], axis=1)
    return ext[:, ::-1].astype(F32)


def _group_mods(m, nb, ndb):
    assert ndb == GROUPS_PER_TILE
    mp = jnp.broadcast_to(m[:nb, None], (nb, GROUPS_PER_TILE) + m.shape[1:])
    allm = jnp.concatenate([mp.reshape((nb * GROUPS_PER_TILE,) + m.shape[1:]), m[nb:]], axis=0)
    return jnp.transpose(allm, (1, 0, 2))


def _routing_tables(cnt, n_blocks):
    nt = cnt.shape[0]
    off = jnp.cumsum(cnt, axis=1) - cnt
    rows_e = jnp.sum(cnt, axis=0)
    nblk_e = (rows_e + BM - 1) // BM
    blk_end = jnp.cumsum(nblk_e)
    start_e = (blk_end - nblk_e) * BM
    base = start_e[None, :] + jnp.cumsum(cnt, axis=0) - cnt
    n_act = blk_end[-1]
    j = jnp.minimum(jnp.arange(n_blocks), n_act - 1)
    blk_e = jnp.minimum(jnp.sum(blk_end[None, :] <= j[:, None], axis=1), N_EXPERTS - 1)
    later = jnp.where(blk_e[None, :] > blk_e[:, None], blk_e[None, :], N_EXPERTS)
    blk_nx = jnp.min(later, axis=1)
    blk_nx = jnp.where(blk_nx == N_EXPERTS, blk_e, blk_nx)
    tail0 = start_e + rows_e
    tailn = nblk_e * BM - rows_e
    mine = blk_e[:, None] == jnp.arange(N_EXPERTS)[None, :]
    blk_rows = jnp.sum(jnp.where(mine, (rows_e + start_e)[None, :], 0), axis=1) - j * BM
    blk_half = blk_rows <= BM // 2
    i32 = lambda a: a.astype(I32)
    return (i32(off.reshape(nt * N_EXPERTS)), i32(cnt.reshape(nt * N_EXPERTS)),
            i32(base.reshape(nt * N_EXPERTS)), i32(jnp.sum(cnt, axis=1)), i32(tail0), i32(tailn),
            i32(blk_e), i32(j), i32(blk_nx), i32(blk_half), i32(n_act.reshape(1)))


def kernel(x_prompt, x_sample, c_prompt, c_sample, cache_att_k, cache_att_v, state_ret, w_ada, b_ada,
           g_norm_mix, g_norm_ffn, w_in, g_q, g_k, rel_bias, g_ret_out, w_out, w_router, b_router,
           w_up, b_up, w_down, b_down):
    nb, seq, d = x_prompt.shape
    ndb, dseq, _ = x_sample.shape
    assert d == D_MODEL and ndb * dseq == TM and dseq == CHUNK
    assert seq % TM == 0 and seq >= ATT_WINDOW and cache_att_k.shape[2] == ATT_WINDOW
    assert w_ada.shape[0] == 1
    rp = nb * seq
    ntp = rp // TM
    nt = ntp + 1
    tps = seq // TM

    xp = x_prompt.reshape(rp, d)
    xs = x_sample.reshape(TM, d)

    m = _ada(jnp.concatenate([c_prompt, c_sample], axis=0), w_ada[0], b_ada[0])
    mods = _group_mods(m.reshape(nb + ndb, N_ADA, d), nb, ndb)
    shift_m, scale_m, gate_m, shift_f, scale_f, gate_f = [mods[a] for a in range(N_ADA)]

    cos_t, sin_t = _rotary_tables(seq, ndb, dseq)
    bd = jnp.asarray(np.kron(np.eye(N_HEADS_ATT // 2), np.ones((HEAD_DIM_ATT, HEAD_DIM_ATT))), BF16)
    tile8 = lambda g: jnp.tile(g.astype(F32), N_HEADS_ATT).reshape(1, GROUP_W)
    (qa, ka_t, va, qb, kb, vb, gb, kp_tail, vp_tail, ks_new, vs_new) = _inproj(
        xp, xs, shift_m, scale_m, g_norm_mix[0].reshape(1, d), w_in[0].astype(BF16), bd,
        tile8(g_q[0]) * (HEAD_DIM_ATT ** -0.5 * LOG2_E), tile8(g_k[0]), cos_t, sin_t, nb, tps)

    rev = _rel_bias_reversed(rel_bias[0])
    g_ro = g_ret_out[0].astype(F32).reshape(1, GROUP_W)
    zero_state = jnp.zeros((nb, N_HEADS_RET, HEAD_DIM_RET, HEAD_DIM_RET), F32)
    att_p, ret_p, state_p = _mix_prompt(qa, ka_t, va, rev, qb, kb, vb, gb, zero_state, g_ro, nb, seq)
    att_s = _attn_sample(qa, ks_new, vs_new,
                         cache_att_k[0].reshape(ndb, ATT_WINDOW, GROUP_W).astype(BF16),
                         cache_att_v[0].reshape(ndb, ATT_WINDOW, GROUP_W).astype(BF16), rev, rp)

    ret_s, state_s = _ret(qb, kb, vb, gb, state_ret[0].astype(F32), g_ro, CHUNK, rp, ndb, 1, "ret_sample")

    upper = jnp.asarray(np.triu(np.ones((TM, TM)), 1), BF16)
    lower = jnp.asarray(np.tril(np.ones((N_EXPERTS, N_EXPERTS)), -1), BF16)
    x1, h2, slot, cols, cnt = _outproj(
        att_p, att_s, ret_p, ret_s, xp, xs, gate_m, shift_f, scale_f, g_norm_ffn[0].reshape(1, d),
        w_out[0].astype(BF16), w_router[0].T.astype(BF16), b_router[0].astype(F32).reshape(N_EXPERTS, 1),
        upper, lower, nb, tps)

    n_blocks = (TOP_K * (rp + TM) + nt * N_EXPERTS * SEG_ALIGN) // BM + 1 + N_EXPERTS
    (off, cntf, base, tot, tail0, tailn, blk_e, blk_i, blk_nx, blk_half,
     n_act) = _routing_tables(cnt[:nt, :, 0], n_blocks)
    xb = _dispatch(h2, slot, off, cntf, base, tot, tail0, tailn, n_act, n_blocks)
    yb = _experts(xb, blk_e, blk_i, blk_nx, blk_half, n_act, w_up[0], b_up[0], w_down[0], b_down[0])
    out_p, out_s = _combine(yb, cols, x1, gate_f, off, cntf, base, tot, ntp, nb, tps)

    heads = (N_HEADS_ATT, HEAD_DIM_ATT)
    return (out_p.reshape(nb, seq, d), out_s.reshape(ndb, dseq, d),
            kp_tail.reshape(1, nb, ATT_WINDOW, *heads), vp_tail.reshape(1, nb, ATT_WINDOW, *heads),
            state_p[None],
            ks_new.reshape(1, ndb, dseq, *heads), vs_new.reshape(1, ndb, dseq, *heads),
            state_s[None])
```

```python
import functools

import numpy as np
import jax
import jax.numpy as jnp
from jax import lax
from jax.experimental import pallas as pl
from jax.experimental.pallas import tpu as pltpu

F32 = jnp.float32
BF16 = jnp.bfloat16
I32 = jnp.int32
U32 = jnp.uint32

D_MODEL = 1024
GROUP_W = 512
N_SLOTS = 7
N_HEADS_ATT = 8
HEAD_DIM_ATT = 64
N_HEADS_RET = 4
HEAD_DIM_RET = 128
CHUNK = 64
ATT_WINDOW = 512
MAX_REL = 256
PAST_LEN = 2048
RET_DECAY_OFFSET = 5.0
ROPE_BASE = 10000.0
N_EXPERTS = 32
TOP_K = 4
D_FF = 1024
SWIGLU_LIMIT = 7.0
SWIGLU_ALPHA = 1.702
N_ADA = 6
NORM_EPS = 1e-6
NEG_INF = -1e30
LOG2_E = 1.4426950408889634

TM = 512
GROUPS_PER_TILE = TM // CHUNK
ATT_QB = 256
RET_CB = 256
LANES = 128
PACK_ROWS = D_MODEL // (2 * LANES)
SUBLANES_32 = 8
SEG_ALIGN = SUBLANES_32 // PACK_ROWS
MT = 256
CAP_USED = TOP_K * MT + N_EXPERTS * SEG_ALIGN
CAP = -(-CAP_USED // LANES) * LANES
BM = 512
BLOCK_QUARTERS = 4
VMEM_LIMIT = 56 * 1024 * 1024


def _cparams(sem, vmem=None):
    return pltpu.CompilerParams(dimension_semantics=sem, vmem_limit_bytes=vmem)


def _ada_kernel(c_ref, w_ref, b_ref, o_ref):
    c = c_ref[...]
    s = c * jax.nn.sigmoid(c)
    o_ref[...] = jnp.dot(s.astype(BF16), w_ref[...].astype(BF16),
                         preferred_element_type=F32) + b_ref[...]


def _ada(c_all, w_ada, b_ada):
    n, d = c_all.shape
    cols = w_ada.shape[1]
    tn = 1536
    return pl.pallas_call(
        _ada_kernel,
        grid=(cols // tn,),
        in_specs=[pl.BlockSpec((n, d), lambda j: (0, 0)),
                  pl.BlockSpec((d, tn), lambda j: (0, j)),
                  pl.BlockSpec((1, tn), lambda j: (0, j))],
        out_specs=pl.BlockSpec((n, tn), lambda j: (0, j)),
        out_shape=jax.ShapeDtypeStruct((n, cols), F32),
        compiler_params=_cparams(("arbitrary",), VMEM_LIMIT),
        name="ada",
    )(c_all, w_ada, b_ada.reshape(1, cols))


def _rms_rows(x, g):
    ms = jnp.mean(x * x, axis=-1, keepdims=True)
    return x * lax.rsqrt(ms + NORM_EPS) * g


def _mod_row(ntp, tps, nb):
    return lambda i, *_: (jnp.where(i < ntp, i // tps, nb), 0)


def _per_group(x, fn, *mods):
    x3 = x.reshape(x.shape[0] // CHUNK, CHUNK, x.shape[-1])
    y3 = fn(x3, *[m[:, None, :] for m in mods])
    return y3.reshape(x.shape)


def _inproj_kernel(npp, xp_ref, xs_ref, sh_ref, sc_ref, gn_ref, w_ref, bd_ref, gq_ref, gk_ref,
                   cos_ref, sin_ref,
                   qa_ref, ka_ref, va_ref, qb_ref, kb_ref, vb_ref, gb_ref,
                   kpt_ref, vpt_ref, kst_ref, vst_ref):
    is_p = pl.program_id(0) < npp
    subs = range(2)
    rows = [slice(sub * TM, (sub + 1) * TM) for sub in subs]

    hb = []
    for sub in subs:
        x = jnp.where(is_p, xp_ref[rows[sub], :], xs_ref[...])
        y = _rms_rows(x, gn_ref[...])
        h = _per_group(y, lambda a, sh, sc: a * (1.0 + sc) + sh, sh_ref[...], sc_ref[...])
        hb.append(h.astype(BF16))

    def proj(s):
        return [jnp.dot(hb[sub], w_ref[:, s * GROUP_W:(s + 1) * GROUP_W], preferred_element_type=F32)
                for sub in subs]

    def head_rms(z, g):
        zz = (z * z).astype(BF16)
        half = GROUP_W // 2
        ss = jnp.concatenate(
            [jnp.dot(zz[:, :half], bd_ref[...], preferred_element_type=F32),
             jnp.dot(zz[:, half:], bd_ref[...], preferred_element_type=F32)], axis=1)
        return z * lax.rsqrt(ss * (1.0 / HEAD_DIM_ATT) + NORM_EPS) * g

    def rot(z, sub):
        cos = cos_ref[rows[sub], :]
        sin = sin_ref[rows[sub], :]
        outs = []
        for hh in range(N_HEADS_RET):
            zh = z[:, hh * HEAD_DIM_RET:(hh + 1) * HEAD_DIM_RET]
            outs.append(zh * cos + pltpu.roll(zh, HEAD_DIM_RET // 2, axis=1) * sin)
        return jnp.concatenate(outs, axis=1)

    for sub, z in zip(subs, proj(0)):
        qa_ref[rows[sub], :] = head_rms(z, gq_ref[...]).astype(BF16)
    ka = [head_rms(z, gk_ref[...]) for z in proj(1)]
    for sub in subs:
        ka_ref[:, rows[sub]] = ka[sub].T.astype(BF16)
    va = proj(2)
    for sub in subs:
        va_ref[rows[sub], :] = va[sub].astype(BF16)

    @pl.when(is_p)
    def _():
        kpt_ref[...] = ka[1]
        vpt_ref[...] = va[1]

    @pl.when(jnp.logical_not(is_p))
    def _():
        kst_ref[...] = ka[0]
        vst_ref[...] = va[0]

    for sub, z in zip(subs, proj(3)):
        qb_ref[rows[sub], :] = rot(z, sub).astype(BF16)
    for sub, z in zip(subs, proj(4)):
        kb_ref[rows[sub], :] = (rot(z, sub) * (HEAD_DIM_RET ** -0.5)).astype(BF16)
    for sub, z in zip(subs, proj(5)):
        vb_ref[rows[sub], :] = z.astype(BF16)
    for sub, z in zip(subs, proj(6)):
        gb_ref[rows[sub], :] = z.astype(BF16)


def _inproj(xp, xs, shift, scale, g_norm, w_in_b, bd, gq8, gk8, cos_t, sin_t, nb, tps):
    rp = xp.shape[0]
    ntp = rp // TM
    assert ntp % 2 == 0 and tps % 2 == 0
    npp = ntp // 2
    r = (ntp + 2) * TM
    row = lambda p: (p, 0)
    full = lambda p: (0, 0)
    tab = lambda p: (jnp.where(p < npp, p % (tps // 2), tps // 2), 0)
    mod = pl.BlockSpec((GROUPS_PER_TILE, D_MODEL), _mod_row(npp, tps // 2, nb))
    tail_spec = pl.BlockSpec((TM, GROUP_W), lambda p: (jnp.minimum(p // (tps // 2), nb - 1), 0))
    act = jax.ShapeDtypeStruct((r, GROUP_W), BF16)
    return pl.pallas_call(
        functools.partial(_inproj_kernel, npp),
        grid=(npp + 1,),
        in_specs=[pl.BlockSpec((2 * TM, D_MODEL), lambda p: (jnp.minimum(p, npp - 1), 0)),
                  pl.BlockSpec((TM, D_MODEL), full),
                  mod, mod,
                  pl.BlockSpec((1, D_MODEL), full),
                  pl.BlockSpec((D_MODEL, N_SLOTS * GROUP_W), full),
                  pl.BlockSpec((GROUP_W // 2, GROUP_W // 2), full),
                  pl.BlockSpec((1, GROUP_W), full),
                  pl.BlockSpec((1, GROUP_W), full),
                  pl.BlockSpec((2 * TM, HEAD_DIM_RET), tab),
                  pl.BlockSpec((2 * TM, HEAD_DIM_RET), tab)],
        out_specs=[pl.BlockSpec((2 * TM, GROUP_W), row), pl.BlockSpec((GROUP_W, 2 * TM), lambda p: (0, p))]
        + [pl.BlockSpec((2 * TM, GROUP_W), row)] * 5 + [
            tail_spec, tail_spec,
            pl.BlockSpec((TM, GROUP_W), full),
            pl.BlockSpec((TM, GROUP_W), full)],
        out_shape=[act, jax.ShapeDtypeStruct((GROUP_W, r), BF16)] + [act] * 5
        + [jax.ShapeDtypeStruct((nb * TM, GROUP_W), F32)] * 2
        + [jax.ShapeDtypeStruct((TM, GROUP_W), F32)] * 2,
        compiler_params=_cparams(("arbitrary",), VMEM_LIMIT),
        name="inproj",
    )(xp, xs, shift, scale, g_norm, w_in_b, bd, gq8, gk8, cos_t, sin_t)


def _attn_heads(q, k, v, bias_ref, first_valid_col=None):
    qb_rows, kb_rows = q.shape[0], v.shape[0]
    assert qb_rows == 4 * CHUNK
    half_rows, span = qb_rows // 2, kb_rows - 2 * CHUNK
    parts = [(0, 0), (half_rows, 2 * CHUNK)]

    def softmax_part(s_full, hh, half, r0, c0):
        rs = half * qb_rows + r0
        s = s_full[rs:rs + half_rows, c0:c0 + span] + bias_ref[hh, r0:r0 + half_rows, c0:c0 + span]
        if first_valid_col is not None:
            col = lax.broadcasted_iota(I32, (half_rows, span), 1) + c0
            s = jnp.where(col >= first_valid_col, s, NEG_INF)
        m = jnp.max(s, axis=-1, keepdims=True)
        e = jnp.exp2(s - m)
        l = jnp.sum(e, axis=-1, keepdims=True)
        pad = [jnp.zeros((half_rows, c0), BF16)] if c0 else []
        pad_r = [jnp.zeros((half_rows, kb_rows - span - c0), BF16)] if kb_rows - span - c0 else []
        return jnp.concatenate(pad + [e.astype(BF16)] + pad_r, axis=1), l

    pair_w = 2 * HEAD_DIM_ATT
    low = lax.broadcasted_iota(I32, (1, pair_w), 1) < HEAD_DIM_ATT
    outs = []
    for pp in range(N_HEADS_ATT // 2):
        ps = slice(pp * pair_w, (pp + 1) * pair_w)
        q2, v2 = q[:, ps], v[:, ps]
        zero = jnp.zeros_like(q2)
        qs = jnp.concatenate([jnp.where(low, q2, zero), jnp.where(low, zero, q2)], axis=0)
        s = jnp.dot(qs, k[ps, :], preferred_element_type=F32)
        es, ls = zip(*[softmax_part(s, 2 * pp + half, half, r0, c0)
                       for half in range(2) for r0, c0 in parts])
        o = jnp.dot(jnp.concatenate(es, axis=0), v2, preferred_element_type=F32) / jnp.concatenate(ls, axis=0)
        outs.append(jnp.where(low, o[:qb_rows], o[qb_rows:]))
    return jnp.concatenate(outs, axis=1)


def _fill_band_bias(rev_ref, bias_scr):
    _, qb_rows, kb_rows = bias_scr.shape
    width = rev_ref.shape[1]
    q = lax.broadcasted_iota(I32, (qb_rows, kb_rows), 0)
    k = lax.broadcasted_iota(I32, (qb_rows, kb_rows), 1)
    qc = q >> 6
    kc = (k - ATT_WINDOW) >> 6
    band = (kc >= qc - ATT_WINDOW // CHUNK) & (kc <= qc)
    for hh in range(N_HEADS_ATT):
        rows = jnp.broadcast_to(rev_ref[hh:hh + 1, :], (qb_rows, width))
        toep = pltpu.roll(rows, width - MAX_REL, 1, stride=1, stride_axis=0)
        bias_scr[hh] = jnp.where(band, toep[:, :kb_rows] * LOG2_E, NEG_INF)


def _attn_sample_kernel(q_ref, kn_ref, vn_ref, kc_ref, vc_ref, rev_ref, o_ref, bias_scr):
    @pl.when(pl.program_id(0) == 0)
    def _():
        _fill_band_bias(rev_ref, bias_scr)

    q = q_ref[...]
    outs = []
    for hh in range(N_HEADS_ATT):
        hs = slice(hh * HEAD_DIM_ATT, (hh + 1) * HEAD_DIM_ATT)
        k = jnp.concatenate([kc_ref[0, :, hs], kn_ref[:, hs].astype(BF16)], axis=0)
        v = jnp.concatenate([vc_ref[0, :, hs], vn_ref[:, hs].astype(BF16)], axis=0)
        s = lax.dot_general(q[:, hs], k, (((1,), (1,)), ((), ())), preferred_element_type=F32)
        s = s + bias_scr[hh]
        m = jnp.max(s, axis=-1, keepdims=True)
        e = jnp.exp2(s - m)
        l = jnp.sum(e, axis=-1, keepdims=True)
        outs.append(jnp.dot(e.astype(BF16), v, preferred_element_type=F32) / l)
    o_ref[...] = jnp.concatenate(outs, axis=1).astype(BF16)


def _attn_sample(qa, ks_new, vs_new, kc, vc, rev, rp):
    ndb = kc.shape[0]
    base = rp // CHUNK
    spec = pl.BlockSpec((CHUNK, GROUP_W), lambda b: (base + b, 0))
    new = pl.BlockSpec((CHUNK, GROUP_W), lambda b: (b, 0))
    cspec = pl.BlockSpec((1, ATT_WINDOW, GROUP_W), lambda b: (b, 0, 0))
    return pl.pallas_call(
        _attn_sample_kernel,
        grid=(ndb,),
        in_specs=[spec, new, new, cspec, cspec,
                  pl.BlockSpec(rev.shape, lambda b: (0, 0))],
        out_specs=pl.BlockSpec((CHUNK, GROUP_W), lambda b: (b, 0)),
        out_shape=jax.ShapeDtypeStruct((ndb * CHUNK, GROUP_W), BF16),
        scratch_shapes=[pltpu.VMEM((N_HEADS_ATT, CHUNK, ATT_WINDOW + CHUNK), F32)],
        compiler_params=_cparams(("arbitrary",), VMEM_LIMIT),
        name="attn_sample",
    )(qa, ks_new, vs_new, kc, vc, rev)


def _ret_chunk(state_decay, q_ref, k_ref, v_ref, g_ref, dm_ref, xi_ref, zeta_ref, gro_ref, o_ref, s_scr):
    outs = []
    for hh in range(N_HEADS_RET):
        hs = slice(hh * HEAD_DIM_RET, (hh + 1) * HEAD_DIM_RET)
        q = q_ref[:, hs]
        k = k_ref[:, hs]
        v = v_ref[:, hs]
        st = s_scr[hh]
        sc = lax.dot_general(q, k, (((1,), (1,)), ((), ())), preferred_element_type=F32) * dm_ref[hh]
        inner = jnp.dot(sc.astype(BF16), v, preferred_element_type=F32)
        cross = jnp.dot(q, st.astype(BF16), preferred_element_type=F32) * xi_ref[:, hs]
        o = inner + cross
        kz = k.astype(F32) * zeta_ref[:, hs]
        s_scr[hh] = state_decay[hh] * st + jnp.dot(kz.T.astype(BF16), v, preferred_element_type=F32)
        mu = jnp.mean(o, axis=-1, keepdims=True)
        oc = o - mu
        var = jnp.mean(oc * oc, axis=-1, keepdims=True)
        outs.append(oc * lax.rsqrt(var + NORM_EPS))
    y = jnp.concatenate(outs, axis=1) * gro_ref[...]
    g = g_ref[...].astype(F32)
    o_ref[...] = (g * jax.nn.sigmoid(g) * y).astype(BF16)


def _ret_kernel(state_decay, q_ref, k_ref, v_ref, g_ref, s0_ref, dm_ref, xi_ref, zeta_ref,
                gro_ref, o_ref, sn_ref, s_scr):
    j = pl.program_id(1)

    @pl.when(j == 0)
    def _():
        s_scr[...] = s0_ref[0]

    _ret_chunk(state_decay, q_ref, k_ref, v_ref, g_ref, dm_ref, xi_ref, zeta_ref, gro_ref, o_ref, s_scr)

    @pl.when(j == pl.num_programs(1) - 1)
    def _():
        sn_ref[0] = s_scr[...]


def _mix_prompt_kernel(state_decay, q_ref, k0_ref, k1_ref, k2_ref, v0_ref, v1_ref, v2_ref, rev_ref,
                       rq_ref, rk_ref, rv_ref, rg_ref, s0_ref, dm_ref, xi_ref, zeta_ref, gro_ref,
                       att_ref, ret_ref, sn_ref, bias_scr, s_scr):
    j = pl.program_id(1)

    @pl.when((pl.program_id(0) == 0) & (j == 0))
    def _():
        _fill_band_bias(rev_ref, bias_scr)

    @pl.when(j == 0)
    def _():
        s_scr[...] = s0_ref[0]

    k = jnp.concatenate([k0_ref[...], k1_ref[...], k2_ref[...]], axis=1)
    v = jnp.concatenate([v0_ref[...], v1_ref[...], v2_ref[...]], axis=0)

    def block(first_valid_col):
        att_ref[...] = _attn_heads(q_ref[...], k, v, bias_scr, first_valid_col).astype(BF16)
        _ret_chunk(state_decay, rq_ref, rk_ref, rv_ref, rg_ref, dm_ref, xi_ref, zeta_ref, gro_ref,
                   ret_ref, s_scr)

    @pl.when(j >= 2)
    def _():
        block(None)

    @pl.when(j < 2)
    def _():
        block((2 - j) * ATT_QB)

    @pl.when(j == pl.num_programs(1) - 1)
    def _():
        sn_ref[0] = s_scr[...]


def _mix_prompt(qa, ka_t, va, rev, qb, kb, vb, gb, s0, g_ro, nb, seq):
    assert ATT_QB == RET_CB
    r = nb * seq
    nq = seq // ATT_QB
    dm, xi, zeta, state_decay = _ret_consts(RET_CB)
    blk = lambda back: (lambda b, j: (b * nq + jnp.maximum(j - back, 0), 0))
    spec = lambda back: pl.BlockSpec((ATT_QB, GROUP_W), blk(back))
    tspec = lambda back: pl.BlockSpec((GROUP_W, ATT_QB), lambda b, j: (0, b * nq + jnp.maximum(j - back, 0)))
    sspec = pl.BlockSpec((1, N_HEADS_RET, HEAD_DIM_RET, HEAD_DIM_RET), lambda b, j: (b, 0, 0, 0))
    full2 = lambda b, j: (0, 0)
    out = jax.ShapeDtypeStruct((r, GROUP_W), BF16)
    return pl.pallas_call(
        functools.partial(_mix_prompt_kernel, state_decay),
        grid=(nb, nq),
        in_specs=[spec(0), tspec(2), tspec(1), tspec(0), spec(2), spec(1), spec(0),
                  pl.BlockSpec(rev.shape, full2),
                  spec(0), spec(0), spec(0), spec(0), sspec,
                  pl.BlockSpec(dm.shape, lambda b, j: (0, 0, 0)),
                  pl.BlockSpec(xi.shape, full2), pl.BlockSpec(zeta.shape, full2),
                  pl.BlockSpec((1, GROUP_W), full2)],
        out_specs=[spec(0), spec(0), sspec],
        out_shape=[out, out, jax.ShapeDtypeStruct(s0.shape, F32)],
        scratch_shapes=[pltpu.VMEM((N_HEADS_ATT, ATT_QB, ATT_WINDOW + ATT_QB), F32),
                        pltpu.VMEM((N_HEADS_RET, HEAD_DIM_RET, HEAD_DIM_RET), F32)],
        compiler_params=_cparams(("arbitrary", "arbitrary"), VMEM_LIMIT),
        name="mix_prompt",
    )(qa, ka_t, ka_t, ka_t, va, va, va, rev, qb, kb, vb, gb, s0, dm, xi, zeta, g_ro)


def _ret_consts(cb):
    log_g = np.log1p(-np.exp2(-RET_DECAY_OFFSET - np.arange(N_HEADS_RET, dtype=np.float64)))
    n = np.arange(cb, dtype=np.float64)
    diff = n[:, None] - n[None, :]
    dm = np.where(diff[None] >= 0, np.exp(np.maximum(diff, 0.0)[None] * log_g[:, None, None]), 0.0)
    xi = np.exp((n + 1.0)[:, None] * log_g[None, :])
    zeta = np.exp((cb - 1.0 - n)[:, None] * log_g[None, :])
    rep = lambda a: np.repeat(a, HEAD_DIM_RET, axis=1)
    state_decay = tuple(float(v) for v in np.exp(cb * log_g))
    return (jnp.asarray(dm, F32), jnp.asarray(rep(xi), F32), jnp.asarray(rep(zeta), F32), state_decay)


def _ret(qb, kb, vb, gb, s0, g_ro, cb, row0, nb, nc, name):
    dm, xi, zeta, state_decay = _ret_consts(cb)
    base = row0 // cb
    spec = pl.BlockSpec((cb, GROUP_W), lambda b, j: (base + b * nc + j, 0))
    sspec = pl.BlockSpec((1, N_HEADS_RET, HEAD_DIM_RET, HEAD_DIM_RET), lambda b, j: (b, 0, 0, 0))
    full2 = lambda b, j: (0, 0)
    return pl.pallas_call(
        functools.partial(_ret_kernel, state_decay),
        grid=(nb, nc),
        in_specs=[spec, spec, spec, spec, sspec,
                  pl.BlockSpec(dm.shape, lambda b, j: (0, 0, 0)),
                  pl.BlockSpec(xi.shape, full2), pl.BlockSpec(zeta.shape, full2),
                  pl.BlockSpec((1, GROUP_W), full2)],
        out_specs=[pl.BlockSpec((cb, GROUP_W), lambda b, j: (b * nc + j, 0)), sspec],
        out_shape=[jax.ShapeDtypeStruct((nb * nc * cb, GROUP_W), BF16),
                   jax.ShapeDtypeStruct(s0.shape, F32)],
        scratch_shapes=[pltpu.VMEM((N_HEADS_RET, HEAD_DIM_RET, HEAD_DIM_RET), F32)],
        compiler_params=_cparams(("arbitrary", "arbitrary"), VMEM_LIMIT),
        name=name,
    )(qb, kb, vb, gb, s0, dm, xi, zeta, g_ro)


def _outproj_kernel(npp, attp_ref, atts_ref, retp_ref, rets_ref, xp_ref, xs_ref, gm_ref, shf_ref, scf_ref,
                    gn_ref, wo_ref, wr_ref, br_ref, upper_ref, lower_ref,
                    x1_ref, h2_ref, slot_ref, cols_ref, cnt_ref):
    is_p = pl.program_id(0) < npp
    subs = range(2)
    rows = [slice(sub * TM, (sub + 1) * TM) for sub in subs]

    def pick(p_ref, s_ref, sub):
        return jnp.where(is_p, p_ref[rows[sub], :], s_ref[...])

    mix = [jnp.dot(pick(attp_ref, atts_ref, sub), wo_ref[:GROUP_W, :], preferred_element_type=F32)
           + jnp.dot(pick(retp_ref, rets_ref, sub), wo_ref[GROUP_W:, :], preferred_element_type=F32)
           for sub in subs]
    h2b = []
    for sub in subs:
        x1 = _per_group(mix[sub], lambda a, gm: a * gm, gm_ref[...]) + pick(xp_ref, xs_ref, sub)
        x1_ref[rows[sub], :] = x1
        y = _rms_rows(x1, gn_ref[...])
        h2 = _per_group(y, lambda a, sh, sc: a * (1.0 + sc) + sh, shf_ref[...], scf_ref[...])
        h2b.append(h2.astype(BF16))
        h2_ref[rows[sub], :] = h2b[sub]

    work = [lax.dot_general(wr_ref[...], h2b[sub], (((1,), (1,)), ((), ())),
                            preferred_element_type=F32) + br_ref[...] for sub in subs]
    eidx = lax.broadcasted_iota(I32, work[0].shape, 0).astype(F32)
    sel = [[] for _ in subs]
    top = [[] for _ in subs]
    for _ in range(TOP_K):
        for sub in subs:
            m = jnp.max(work[sub], axis=0, keepdims=True)
            idx = jnp.min(jnp.where(work[sub] == m, eidx, float(N_EXPERTS)), axis=0, keepdims=True)
            hit = eidx == idx
            sel[sub].append(hit)
            top[sub].append(m)
            work[sub] = jnp.where(hit, -jnp.inf, work[sub])

    for sub in subs:
        ex = [jnp.exp(t - top[sub][0]) for t in top[sub]]
        den = ex[0] + ex[1] + ex[2] + ex[3]
        gates = [e / den for e in ex]
        for part in range(TM // MT):
            lanes = slice(part * MT, (part + 1) * MT)
            tile = (TM // MT) * sub + part
            hits = [h[:, lanes] for h in sel[sub]]
            multi_f = jnp.where(hits[0] | hits[1] | hits[2] | hits[3], 1.0, 0.0)
            rank = jnp.dot(multi_f.astype(BF16), upper_ref[...], preferred_element_type=F32)
            cnt = jnp.sum(multi_f, axis=1, keepdims=True)
            cnt_pad = jnp.maximum(jnp.floor((cnt + (SEG_ALIGN - 1.0)) * (1.0 / SEG_ALIGN)), 1.0) * SEG_ALIGN
            cnt_pad_b = jnp.broadcast_to(cnt_pad, (N_EXPERTS, 128))
            seg_off = jnp.dot(lower_ref[...], cnt_pad_b.astype(BF16), preferred_element_type=F32)[:, :1]
            pos = seg_off + rank
            slot_rows = jnp.concatenate(
                [jnp.sum(jnp.where(h, pos, 0.0), axis=0, keepdims=True) for h in hits], axis=0)
            gate_rows = jnp.concatenate([g[:, lanes] for g in gates], axis=0)
            slot_ref[tile] = slot_rows.astype(I32)
            cnt_ref[tile] = cnt_pad_b.astype(I32)
            both = jnp.concatenate([slot_rows, gate_rows, jnp.zeros((128 - 2 * TOP_K, MT), F32)], axis=0)
            cols_ref[tile] = both.T


def _outproj(att_p, att_s, ret_p, ret_s, xp, xs, gate_m, shift_f, scale_f, g_norm, w_out_b, wr_t, br,
             upper, lower, nb, tps):
    rp = xp.shape[0]
    ntp = rp // TM
    assert ntp % 2 == 0 and tps % 2 == 0
    npp = ntp // 2
    r = (ntp + 2) * TM
    per_step = 2 * TM // MT
    ntm = (npp + 1) * per_step
    row = lambda p: (p, 0)
    row3 = lambda p: (p, 0, 0)
    full = lambda p: (0, 0)
    prow = lambda p: (jnp.minimum(p, npp - 1), 0)
    mod = pl.BlockSpec((GROUPS_PER_TILE, D_MODEL), _mod_row(npp, tps // 2, nb))
    return pl.pallas_call(
        functools.partial(_outproj_kernel, npp),
        grid=(npp + 1,),
        in_specs=[pl.BlockSpec((2 * TM, GROUP_W), prow), pl.BlockSpec((TM, GROUP_W), full),
                  pl.BlockSpec((2 * TM, GROUP_W), prow), pl.BlockSpec((TM, GROUP_W), full),
                  pl.BlockSpec((2 * TM, D_MODEL), prow),
                  pl.BlockSpec((TM, D_MODEL), full),
                  mod, mod, mod,
                  pl.BlockSpec((1, D_MODEL), full),
                  pl.BlockSpec((D_MODEL, D_MODEL), full),
                  pl.BlockSpec((N_EXPERTS, D_MODEL), full),
                  pl.BlockSpec((N_EXPERTS, 1), full),
                  pl.BlockSpec((MT, MT), full),
                  pl.BlockSpec((N_EXPERTS, N_EXPERTS), full)],
        out_specs=[pl.BlockSpec((2 * TM, D_MODEL), row), pl.BlockSpec((2 * TM, D_MODEL), row),
                   pl.BlockSpec((per_step, TOP_K, MT), row3),
                   pl.BlockSpec((per_step, MT, 128), row3), pl.BlockSpec((per_step, N_EXPERTS, 128), row3)],
        out_shape=[jax.ShapeDtypeStruct((r, D_MODEL), F32), jax.ShapeDtypeStruct((r, D_MODEL), BF16),
                   jax.ShapeDtypeStruct((ntm, TOP_K, MT), I32),
                   jax.ShapeDtypeStruct((ntm, MT, 128), F32), jax.ShapeDtypeStruct((ntm, N_EXPERTS, 128), I32)],
        compiler_params=_cparams(("arbitrary",), VMEM_LIMIT),
        name="outproj",
    )(att_p, att_s, ret_p, ret_s, xp, xs, gate_m, shift_f, scale_f, g_norm, w_out_b, wr_t, br, upper, lower)


def _store_packed(ref, lead, row0, x):
    n = x.shape[0]
    for j in range(PACK_ROWS):
        c = 2 * LANES * j
        lo = lax.bitcast_convert_type(x[:, c:c + LANES].astype(BF16).astype(F32), U32) >> 16
        hi = lax.bitcast_convert_type(x[:, c + LANES:c + 2 * LANES].astype(BF16).astype(F32), U32)
        ref[lead + (pl.ds(PACK_ROWS * row0 + j, n, stride=PACK_ROWS), slice(None))] = hi | lo


def _load_packed(ref, lead, row0, n):
    parts = []
    for j in range(PACK_ROWS):
        u = ref[lead + (pl.ds(PACK_ROWS * row0 + j, n, stride=PACK_ROWS), slice(None))]
        parts.append(lax.bitcast_convert_type(u << 16, F32))
        parts.append(lax.bitcast_convert_type(u & jnp.uint32(0xFFFF0000), F32))
    return jnp.concatenate(parts, axis=1).astype(BF16)


def _packed_rows(ref, lead, tok0, ntok):
    rows = pl.ds(pl.multiple_of(tok0 * PACK_ROWS, SUBLANES_32), ntok * PACK_ROWS)
    return ref.at[lead + (rows, slice(None))]


def _rows_copy(n, src_rows, dst_rows, sem):
    size = pl.multiple_of(n, SEG_ALIGN)
    return pltpu.make_async_copy(src_rows(size), dst_rows(size), sem)


def _start_segments(t, cnt_ref, off_ref, base_ref, local_rows, sorted_rows, sem, to_sorted):
    for e in range(N_EXPERTS):
        n = cnt_ref[t * N_EXPERTS + e]
        off = pl.multiple_of(off_ref[t * N_EXPERTS + e], SEG_ALIGN)
        base = pl.multiple_of(base_ref[t * N_EXPERTS + e], SEG_ALIGN)
        local = lambda z, off=off: local_rows(off, z)
        remote = lambda z, base=base: sorted_rows(base, z)
        (_rows_copy(n, local, remote, sem) if to_sorted else _rows_copy(n, remote, local, sem)).start()


def _dispatch_kernel(nt, n_blocks, off_ref, cnt_ref, base_ref, tot_ref, tail0_ref, tailn_ref, na_ref,
                     h2_ref, slot_ref, slotn_ref, xb_ref, xs_scr, hot_scr, zero_scr, sems, tail_sem):
    i = pl.program_id(0)
    cur = i % 2
    sorted_rows = lambda r, z: _packed_rows(xb_ref, (), r, z)

    def start_tile(t, buf):
        _start_segments(t, cnt_ref, off_ref, base_ref, lambda r, z: _packed_rows(xs_scr, (buf,), r, z),
                        sorted_rows, sems.at[buf], True)

    def wait_tile(t, buf):
        _rows_copy(tot_ref[t], lambda z: _packed_rows(xs_scr, (buf,), 0, z), lambda z: sorted_rows(0, z),
                   sems.at[buf]).wait()

    def tail_copies(wait):
        def body(e, c):
            base = pl.multiple_of(tail0_ref[e], SEG_ALIGN)

            @pl.when(tailn_ref[e] > 0)
            def _():
                cp = _rows_copy(tailn_ref[e], lambda z: _packed_rows(zero_scr, (), 0, z),
                                lambda z: sorted_rows(base, z), tail_sem)
                cp.wait() if wait else cp.start()
            return c
        lax.fori_loop(0, N_EXPERTS, body, 0)

        def unused(j, c):
            cp = pltpu.make_async_copy(zero_scr, sorted_rows(pl.multiple_of(j * BM, BM), BM), tail_sem)
            cp.wait() if wait else cp.start()
            return c
        lax.fori_loop(na_ref[0], n_blocks, unused, 0)

    @pl.when(i >= 2)
    def _():
        wait_tile(i - 2, cur)

    def onehot(slot):
        srow = lax.broadcasted_iota(I32, (CAP_USED, MT), 0)
        hit = (srow == slot[0:1]) | (srow == slot[1:2]) | (srow == slot[2:3]) | (srow == slot[3:4])
        return jnp.where(hit, 1.0, 0.0).astype(BF16)

    @pl.when(i == 0)
    def _():
        hot_scr[0] = onehot(slot_ref[0])
        zero_scr[...] = jnp.zeros_like(zero_scr)
        tail_copies(False)

    _store_packed(xs_scr, (cur,), 0, jnp.dot(hot_scr[cur], h2_ref[...], preferred_element_type=F32))
    hot_scr[1 - cur] = onehot(slotn_ref[0])
    start_tile(i, cur)

    @pl.when(i == nt - 1)
    def _():
        if nt >= 2:
            wait_tile(i - 1, 1 - cur)
        wait_tile(i, cur)
        tail_copies(True)


def _dispatch(h2, slot, off, cnt, base, tot, tail0, tailn, n_act, n_blocks):
    nt = tot.shape[0]
    n_rows = n_blocks * BM
    grid_spec = pltpu.PrefetchScalarGridSpec(
        num_scalar_prefetch=7,
        grid=(nt,),
        in_specs=[pl.BlockSpec((MT, D_MODEL), lambda i, *_: (i, 0)),
                  pl.BlockSpec((1, TOP_K, MT), lambda i, *_: (i, 0, 0)),
                  pl.BlockSpec((1, TOP_K, MT), lambda i, *_: (jnp.minimum(i + 1, nt - 1), 0, 0))],
        out_specs=pl.BlockSpec(memory_space=pl.ANY),
        scratch_shapes=[pltpu.VMEM((2, CAP_USED * PACK_ROWS, LANES), U32),
                        pltpu.VMEM((2, CAP_USED, MT), BF16),
                        pltpu.VMEM((BM * PACK_ROWS, LANES), U32),
                        pltpu.SemaphoreType.DMA((2,)),
                        pltpu.SemaphoreType.DMA(())],
    )
    return pl.pallas_call(
        functools.partial(_dispatch_kernel, nt, n_blocks),
        grid_spec=grid_spec,
        out_shape=jax.ShapeDtypeStruct((n_rows * PACK_ROWS, LANES), U32),
        compiler_params=_cparams(("arbitrary",), VMEM_LIMIT),
        name="dispatch",
    )(off, cnt, base, tot, tail0, tailn, n_act, h2, slot, slot)


def _experts_kernel(be_ref, bi_ref, nx_ref, nq_ref, na_ref, x_ref, wu_hbm, bu_ref, wd_hbm, bd_ref, y_ref,
                    wu_stage, wd_stage, wu_scr, wd_scr, sems):
    j = pl.program_id(0)

    def weight_copies(e):
        return (pltpu.make_async_copy(wu_hbm.at[e], wu_stage, sems.at[0]),
                pltpu.make_async_copy(wd_hbm.at[e], wd_stage, sems.at[1]))

    @pl.when(j < na_ref[0])
    def _():
        e = be_ref[j]
        prev = be_ref[jnp.maximum(j - 1, 0)]

        @pl.when(j == 0)
        def _():
            for cp in weight_copies(e):
                cp.start()

        @pl.when((j == 0) | (e != prev))
        def _():
            for cp in weight_copies(e):
                cp.wait()
            wu_scr[...] = wu_stage[...].astype(BF16)
            wd_scr[...] = wd_stage[...].astype(BF16)

            @pl.when(nx_ref[j] != e)
            def _():
                for cp in weight_copies(nx_ref[j]):
                    cp.start()

        def ffn(n):
            x = _load_packed(x_ref, (), 0, n)
            u = jnp.dot(x, wu_scr[...], preferred_element_type=F32) + bu_ref[0]
            glu = jnp.minimum(u[:, :D_FF], SWIGLU_LIMIT)
            lin = jnp.clip(u[:, D_FF:], -SWIGLU_LIMIT, SWIGLU_LIMIT)
            act = glu * jax.nn.sigmoid(SWIGLU_ALPHA * glu) * (lin + 1.0)
            y = jnp.dot(act.astype(BF16), wd_scr[...], preferred_element_type=F32) + bd_ref[0]
            _store_packed(y_ref, (), 0, y)

        for quarters in range(1, BLOCK_QUARTERS + 1):
            @pl.when(nq_ref[j] == quarters)
            def _(quarters=quarters):
                n = quarters * (BM // BLOCK_QUARTERS)
                ffn(n)
                if n < BM:
                    y_ref[PACK_ROWS * n:, :] = jnp.zeros((PACK_ROWS * (BM - n), LANES), U32)


def _experts(xb, blk_e, blk_i, blk_nx, blk_nq, n_act, w_up, b_up, w_down, b_down):
    n_rows = xb.shape[0] // PACK_ROWS
    nblk = n_rows // BM
    grid_spec = pltpu.PrefetchScalarGridSpec(
        num_scalar_prefetch=5,
        grid=(nblk,),
        in_specs=[pl.BlockSpec((BM * PACK_ROWS, LANES), lambda j, be, bi, *_: (bi[j], 0)),
                  pl.BlockSpec(memory_space=pl.ANY),
                  pl.BlockSpec((1, 1, 2 * D_FF), lambda j, be, *_: (be[j], 0, 0)),
                  pl.BlockSpec(memory_space=pl.ANY),
                  pl.BlockSpec((1, 1, D_MODEL), lambda j, be, *_: (be[j], 0, 0))],
        out_specs=pl.BlockSpec((BM * PACK_ROWS, LANES), lambda j, be, bi, *_: (bi[j], 0)),
        scratch_shapes=[pltpu.VMEM((D_MODEL, 2 * D_FF), F32), pltpu.VMEM((D_FF, D_MODEL), F32),
                        pltpu.VMEM((D_MODEL, 2 * D_FF), BF16), pltpu.VMEM((D_FF, D_MODEL), BF16),
                        pltpu.SemaphoreType.DMA((2,))],
    )
    return pl.pallas_call(
        _experts_kernel,
        grid_spec=grid_spec,
        out_shape=jax.ShapeDtypeStruct(xb.shape, xb.dtype),
        input_output_aliases={5: 0},
        compiler_params=_cparams(("arbitrary",), VMEM_LIMIT),
        name="experts",
    )(blk_e, blk_i, blk_nx, blk_nq, n_act, xb, w_up, b_up.reshape(N_EXPERTS, 1, 2 * D_FF), w_down,
      b_down.reshape(N_EXPERTS, 1, D_MODEL))


def _combine_kernel(nt, ntp, off_ref, cnt_ref, base_ref, tot_ref, yb_ref, cols_ref, x1_ref, gf_ref,
                    op_ref, os_ref, ys_scr, sems):
    i = pl.program_id(0)
    cur = i % 2

    sorted_rows = lambda r, z: _packed_rows(yb_ref, (), r, z)

    def start_tile(t, buf):
        _start_segments(t, cnt_ref, off_ref, base_ref, lambda r, z: _packed_rows(ys_scr, (buf,), r, z),
                        sorted_rows, sems.at[buf], False)

    @pl.when(i == 0)
    def _():
        ys_scr[...] = jnp.zeros_like(ys_scr)
        start_tile(0, 0)

    def wait_tile(t, buf):
        _rows_copy(tot_ref[t], lambda z: sorted_rows(0, z), lambda z: _packed_rows(ys_scr, (buf,), 0, z),
                   sems.at[buf]).wait()

    nxt = jnp.minimum(i + 1, nt - 1)
    wait_tile(i, cur)
    start_tile(nxt, 1 - cur)

    @pl.when(i == nt - 1)
    def _():
        wait_tile(nxt, 1 - cur)

    cols = cols_ref[0]
    lane = lax.broadcasted_iota(I32, (MT, CAP), 1)
    w = jnp.zeros((MT, CAP), F32)
    for k in range(TOP_K):
        sk = cols[:, k:k + 1].astype(I32)
        gk = cols[:, TOP_K + k:TOP_K + k + 1]
        w = jnp.where(lane == sk, gk, w)
    y = jnp.dot(w.astype(BF16), _load_packed(ys_scr, (cur,), 0, CAP), preferred_element_type=F32)
    groups = MT // CHUNK
    gf = gf_ref[...]
    gf = jnp.where(i > ntp, gf[groups:2 * groups], gf[:groups])
    out = x1_ref[...] + _per_group(y, lambda a, g: a * g, gf)

    @pl.when(i < ntp)
    def _():
        op_ref[...] = out

    @pl.when(i >= ntp)
    def _():
        os_ref[...] = out


def _combine(yb, cols, x1, gate_f, off, cnt, base, tot, ntp, nb, tps):
    assert TM == 2 * MT
    nt = tot.shape[0]
    grid_spec = pltpu.PrefetchScalarGridSpec(
        num_scalar_prefetch=4,
        grid=(nt,),
        in_specs=[pl.BlockSpec(memory_space=pl.ANY),
                  pl.BlockSpec((1, MT, 128), lambda i, *_: (i, 0, 0)),
                  pl.BlockSpec((MT, D_MODEL), lambda i, *_: (i, 0)),
                  pl.BlockSpec((GROUPS_PER_TILE, D_MODEL), _mod_row(ntp, tps, nb))],
        out_specs=[pl.BlockSpec((MT, D_MODEL), lambda i, *_: (jnp.minimum(i, ntp - 1), 0)),
                   pl.BlockSpec((MT, D_MODEL), lambda i, *_: (jnp.clip(i - ntp, 0, TM // MT - 1), 0))],
        scratch_shapes=[pltpu.VMEM((2, CAP * PACK_ROWS, LANES), U32), pltpu.SemaphoreType.DMA((2,))],
    )
    return pl.pallas_call(
        functools.partial(_combine_kernel, nt, ntp),
        grid_spec=grid_spec,
        out_shape=[jax.ShapeDtypeStruct((ntp * MT, D_MODEL), F32),
                   jax.ShapeDtypeStruct((TM, D_MODEL), F32)],
        compiler_params=_cparams(("arbitrary",), VMEM_LIMIT),
        name="combine",
    )(off, cnt, base, tot, yb, cols, x1, gate_f)


def _rotary_tables(seq, dec_batch, dec_seq):
    half = HEAD_DIM_RET // 2
    inv = ROPE_BASE ** (-np.arange(half, dtype=np.float64) / half)
    pos = np.concatenate([np.arange(seq), np.tile(PAST_LEN + np.arange(dec_seq), 2 * dec_batch)])
    ang = pos.astype(np.float64)[:, None] * inv[None, :]
    cos = np.concatenate([np.cos(ang), np.cos(ang)], axis=1)
    sin = np.concatenate([-np.sin(ang), np.sin(ang)], axis=1)
    return jnp.asarray(cos, F32), jnp.asarray(sin, F32)


def _rel_bias_reversed(rel_bias):
    heads = rel_bias.shape[0]
    ext = jnp.concatenate([rel_bias[:, 1:], jnp.broadcast_to(rel_bias[:, -1:], (heads, 2 * MAX_REL))], axis=1)
    return ext[:, ::-1].astype(F32)


def _group_mods(m, nb, ndb):
    assert ndb == GROUPS_PER_TILE
    mp = jnp.broadcast_to(m[:nb, None], (nb, GROUPS_PER_TILE) + m.shape[1:])
    allm = jnp.concatenate([mp.reshape((nb * GROUPS_PER_TILE,) + m.shape[1:]), m[nb:]], axis=0)
    return jnp.transpose(allm, (1, 0, 2))


def _routing_tables(cnt, n_blocks):
    nt = cnt.shape[0]
    off = jnp.cumsum(cnt, axis=1) - cnt
    rows_e = jnp.sum(cnt, axis=0)
    nblk_e = (rows_e + BM - 1) // BM
    blk_end = jnp.cumsum(nblk_e)
    start_e = (blk_end - nblk_e) * BM
    base = start_e[None, :] + jnp.cumsum(cnt, axis=0) - cnt
    n_act = blk_end[-1]
    j = jnp.minimum(jnp.arange(n_blocks), n_act - 1)
    blk_e = jnp.minimum(jnp.sum(blk_end[None, :] <= j[:, None], axis=1), N_EXPERTS - 1)
    later = jnp.where(blk_e[None, :] > blk_e[:, None], blk_e[None, :], N_EXPERTS)
    blk_nx = jnp.min(later, axis=1)
    blk_nx = jnp.where(blk_nx == N_EXPERTS, blk_e, blk_nx)
    tail0 = start_e + rows_e
    tailn = nblk_e * BM - rows_e
    mine = blk_e[:, None] == jnp.arange(N_EXPERTS)[None, :]
    blk_rows = jnp.sum(jnp.where(mine, (rows_e + start_e)[None, :], 0), axis=1) - j * BM
    quarter = BM // BLOCK_QUARTERS
    blk_nq = jnp.clip((blk_rows + quarter - 1) // quarter, 1, BLOCK_QUARTERS)
    i32 = lambda a: a.astype(I32)
    return (i32(off.reshape(nt * N_EXPERTS)), i32(cnt.reshape(nt * N_EXPERTS)),
            i32(base.reshape(nt * N_EXPERTS)), i32(jnp.sum(cnt, axis=1)), i32(tail0), i32(tailn),
            i32(blk_e), i32(j), i32(blk_nx), i32(blk_nq), i32(n_act.reshape(1)))


def kernel(x_prompt, x_sample, c_prompt, c_sample, cache_att_k, cache_att_v, state_ret, w_ada, b_ada,
           g_norm_mix, g_norm_ffn, w_in, g_q, g_k, rel_bias, g_ret_out, w_out, w_router, b_router,
           w_up, b_up, w_down, b_down):
    nb, seq, d = x_prompt.shape
    ndb, dseq, _ = x_sample.shape
    assert d == D_MODEL and ndb * dseq == TM and dseq == CHUNK
    assert seq % TM == 0 and seq >= ATT_WINDOW and cache_att_k.shape[2] == ATT_WINDOW
    assert w_ada.shape[0] == 1
    rp = nb * seq
    ntp = rp // TM
    nt = ntp + 1
    tps = seq // TM

    xp = x_prompt.reshape(rp, d)
    xs = x_sample.reshape(TM, d)

    m = _ada(jnp.concatenate([c_prompt, c_sample], axis=0), w_ada[0], b_ada[0])
    mods = _group_mods(m.reshape(nb + ndb, N_ADA, d), nb, ndb)
    shift_m, scale_m, gate_m, shift_f, scale_f, gate_f = [mods[a] for a in range(N_ADA)]

    cos_t, sin_t = _rotary_tables(seq, ndb, dseq)
    bd = jnp.asarray(np.kron(np.eye(N_HEADS_ATT // 2), np.ones((HEAD_DIM_ATT, HEAD_DIM_ATT))), BF16)
    tile8 = lambda g: jnp.tile(g.astype(F32), N_HEADS_ATT).reshape(1, GROUP_W)
    (qa, ka_t, va, qb, kb, vb, gb, kp_tail, vp_tail, ks_new, vs_new) = _inproj(
        xp, xs, shift_m, scale_m, g_norm_mix[0].reshape(1, d), w_in[0].astype(BF16), bd,
        tile8(g_q[0]) * (HEAD_DIM_ATT ** -0.5 * LOG2_E), tile8(g_k[0]), cos_t, sin_t, nb, tps)

    rev = _rel_bias_reversed(rel_bias[0])
    g_ro = g_ret_out[0].astype(F32).reshape(1, GROUP_W)
    zero_state = jnp.zeros((nb, N_HEADS_RET, HEAD_DIM_RET, HEAD_DIM_RET), F32)
    att_p, ret_p, state_p = _mix_prompt(qa, ka_t, va, rev, qb, kb, vb, gb, zero_state, g_ro, nb, seq)
    att_s = _attn_sample(qa, ks_new, vs_new,
                         cache_att_k[0].reshape(ndb, ATT_WINDOW, GROUP_W).astype(BF16),
                         cache_att_v[0].reshape(ndb, ATT_WINDOW, GROUP_W).astype(BF16), rev, rp)

    ret_s, state_s = _ret(qb, kb, vb, gb, state_ret[0].astype(F32), g_ro, CHUNK, rp, ndb, 1, "ret_sample")

    upper = jnp.asarray(np.triu(np.ones((MT, MT)), 1), BF16)
    lower = jnp.asarray(np.tril(np.ones((N_EXPERTS, N_EXPERTS)), -1), BF16)
    x1, h2, slot, cols, cnt = _outproj(
        att_p, att_s, ret_p, ret_s, xp, xs, gate_m, shift_f, scale_f, g_norm_ffn[0].reshape(1, d),
        w_out[0].astype(BF16), w_router[0].T.astype(BF16), b_router[0].astype(F32).reshape(N_EXPERTS, 1),
        upper, lower, nb, tps)

    ntm = nt * (TM // MT)
    n_blocks = (TOP_K * (rp + TM) + ntm * N_EXPERTS * SEG_ALIGN) // BM + 1 + N_EXPERTS
    (off, cntf, base, tot, tail0, tailn, blk_e, blk_i, blk_nx, blk_nq,
     n_act) = _routing_tables(cnt[:ntm, :, 0], n_blocks)
    xb = _dispatch(h2, slot, off, cntf, base, tot, tail0, tailn, n_act, n_blocks)
    yb = _experts(xb, blk_e, blk_i, blk_nx, blk_nq, n_act, w_up[0], b_up[0], w_down[0], b_down[0])
    out_p, out_s = _combine(yb, cols, x1, gate_f, off, cntf, base, tot, rp // MT, nb, seq // MT)

    heads = (N_HEADS_ATT, HEAD_DIM_ATT)
    return (out_p.reshape(nb, seq, d), out_s.reshape(ndb, dseq, d),
            kp_tail.reshape(1, nb, ATT_WINDOW, *heads), vp_tail.reshape(1, nb, ATT_WINDOW, *heads),
            state_p[None],
            ks_new.reshape(1, ndb, dseq, *heads), vs_new.reshape(1, ndb, dseq, *heads),
            state_s[None])
```

```python
import functools

import numpy as np
import jax
import jax.numpy as jnp
from jax import lax
from jax.experimental import pallas as pl
from jax.experimental.pallas import tpu as pltpu

F32 = jnp.float32
BF16 = jnp.bfloat16
I32 = jnp.int32
U32 = jnp.uint32

D_MODEL = 1024
GROUP_W = 512
N_SLOTS = 7
N_HEADS_ATT = 8
HEAD_DIM_ATT = 64
N_HEADS_RET = 4
HEAD_DIM_RET = 128
CHUNK = 64
ATT_WINDOW = 512
MAX_REL = 256
PAST_LEN = 2048
RET_DECAY_OFFSET = 5.0
ROPE_BASE = 10000.0
N_EXPERTS = 32
TOP_K = 4
D_FF = 1024
SWIGLU_LIMIT = 7.0
SWIGLU_ALPHA = 1.702
N_ADA = 6
NORM_EPS = 1e-6
NEG_INF = -1e30
LOG2_E = 1.4426950408889634

TM = 512
GROUPS_PER_TILE = TM // CHUNK
ATT_QB = 256
RET_CB = 256
LANES = 128
PACK_ROWS = D_MODEL // (2 * LANES)
SUBLANES_32 = 8
SEG_ALIGN = SUBLANES_32 // PACK_ROWS
MT = 256
PARTS = TM // MT
CAP_USED = TOP_K * MT + N_EXPERTS * SEG_ALIGN
CAP = -(-CAP_USED // LANES) * LANES
BM = 512
BLOCK_QUARTERS = 4
VMEM_LIMIT = 56 * 1024 * 1024


def _cparams(sem, vmem=None):
    return pltpu.CompilerParams(dimension_semantics=sem, vmem_limit_bytes=vmem)


def _ada_kernel(c_ref, w_ref, b_ref, o_ref):
    c = c_ref[...]
    s = c * jax.nn.sigmoid(c)
    o_ref[...] = jnp.dot(s.astype(BF16), w_ref[...].astype(BF16),
                         preferred_element_type=F32) + b_ref[...]


def _ada(c_all, w_ada, b_ada):
    n, d = c_all.shape
    cols = w_ada.shape[1]
    tn = 1536
    return pl.pallas_call(
        _ada_kernel,
        grid=(cols // tn,),
        in_specs=[pl.BlockSpec((n, d), lambda j: (0, 0)),
                  pl.BlockSpec((d, tn), lambda j: (0, j)),
                  pl.BlockSpec((1, tn), lambda j: (0, j))],
        out_specs=pl.BlockSpec((n, tn), lambda j: (0, j)),
        out_shape=jax.ShapeDtypeStruct((n, cols), F32),
        compiler_params=_cparams(("arbitrary",), VMEM_LIMIT),
        name="ada",
    )(c_all, w_ada, b_ada.reshape(1, cols))


def _rms_rows(x, g):
    ms = jnp.mean(x * x, axis=-1, keepdims=True)
    return x * lax.rsqrt(ms + NORM_EPS) * g


def _mod_row(ntp, tps, nb):
    return lambda i, *_: (jnp.where(i < ntp, i // tps, nb), 0)


def _per_group(x, fn, *mods):
    x3 = x.reshape(x.shape[0] // CHUNK, CHUNK, x.shape[-1])
    y3 = fn(x3, *[m[:, None, :] for m in mods])
    return y3.reshape(x.shape)


def _inproj_kernel(npp, xp_ref, xs_ref, sh_ref, sc_ref, gn_ref, w_ref, bd_ref, gq_ref, gk_ref,
                   cos_ref, sin_ref,
                   qa_ref, ka_ref, va_ref, qb_ref, kb_ref, vb_ref, gb_ref,
                   kpt_ref, vpt_ref, kst_ref, vst_ref):
    is_p = pl.program_id(0) < npp
    subs = range(2)
    rows = [slice(sub * TM, (sub + 1) * TM) for sub in subs]

    hb = []
    for sub in subs:
        x = jnp.where(is_p, xp_ref[rows[sub], :], xs_ref[...])
        y = _rms_rows(x, gn_ref[...])
        h = _per_group(y, lambda a, sh, sc: a * (1.0 + sc) + sh, sh_ref[...], sc_ref[...])
        hb.append(h.astype(BF16))

    def proj(s):
        return [jnp.dot(hb[sub], w_ref[:, s * GROUP_W:(s + 1) * GROUP_W], preferred_element_type=F32)
                for sub in subs]

    def head_rms(z, g):
        zz = (z * z).astype(BF16)
        half = GROUP_W // 2
        ss = jnp.concatenate(
            [jnp.dot(zz[:, :half], bd_ref[...], preferred_element_type=F32),
             jnp.dot(zz[:, half:], bd_ref[...], preferred_element_type=F32)], axis=1)
        return z * lax.rsqrt(ss * (1.0 / HEAD_DIM_ATT) + NORM_EPS) * g

    def rot(z, sub):
        cos = cos_ref[rows[sub], :]
        sin = sin_ref[rows[sub], :]
        outs = []
        for hh in range(N_HEADS_RET):
            zh = z[:, hh * HEAD_DIM_RET:(hh + 1) * HEAD_DIM_RET]
            outs.append(zh * cos + pltpu.roll(zh, HEAD_DIM_RET // 2, axis=1) * sin)
        return jnp.concatenate(outs, axis=1)

    for sub, z in zip(subs, proj(0)):
        qa_ref[rows[sub], :] = head_rms(z, gq_ref[...]).astype(BF16)
    ka = [head_rms(z, gk_ref[...]) for z in proj(1)]
    for sub in subs:
        ka_ref[:, rows[sub]] = ka[sub].T.astype(BF16)
    va = proj(2)
    for sub in subs:
        va_ref[rows[sub], :] = va[sub].astype(BF16)

    @pl.when(is_p)
    def _():
        kpt_ref[...] = ka[1]
        vpt_ref[...] = va[1]

    @pl.when(jnp.logical_not(is_p))
    def _():
        kst_ref[...] = ka[0]
        vst_ref[...] = va[0]

    for sub, z in zip(subs, proj(3)):
        qb_ref[rows[sub], :] = rot(z, sub).astype(BF16)
    for sub, z in zip(subs, proj(4)):
        kb_ref[rows[sub], :] = (rot(z, sub) * (HEAD_DIM_RET ** -0.5)).astype(BF16)
    for sub, z in zip(subs, proj(5)):
        vb_ref[rows[sub], :] = z.astype(BF16)
    for sub, z in zip(subs, proj(6)):
        gb_ref[rows[sub], :] = z.astype(BF16)


def _inproj(xp, xs, shift, scale, g_norm, w_in_b, bd, gq8, gk8, cos_t, sin_t, nb, tps):
    rp = xp.shape[0]
    ntp = rp // TM
    assert ntp % 2 == 0 and tps % 2 == 0
    npp = ntp // 2
    r = (ntp + 2) * TM
    row = lambda p: (p, 0)
    full = lambda p: (0, 0)
    tab = lambda p: (jnp.where(p < npp, p % (tps // 2), tps // 2), 0)
    mod = pl.BlockSpec((GROUPS_PER_TILE, D_MODEL), _mod_row(npp, tps // 2, nb))
    tail_spec = pl.BlockSpec((TM, GROUP_W), lambda p: (jnp.minimum(p // (tps // 2), nb - 1), 0))
    act = jax.ShapeDtypeStruct((r, GROUP_W), BF16)
    return pl.pallas_call(
        functools.partial(_inproj_kernel, npp),
        grid=(npp + 1,),
        in_specs=[pl.BlockSpec((2 * TM, D_MODEL), lambda p: (jnp.minimum(p, npp - 1), 0)),
                  pl.BlockSpec((TM, D_MODEL), full),
                  mod, mod,
                  pl.BlockSpec((1, D_MODEL), full),
                  pl.BlockSpec((D_MODEL, N_SLOTS * GROUP_W), full),
                  pl.BlockSpec((GROUP_W // 2, GROUP_W // 2), full),
                  pl.BlockSpec((1, GROUP_W), full),
                  pl.BlockSpec((1, GROUP_W), full),
                  pl.BlockSpec((2 * TM, HEAD_DIM_RET), tab),
                  pl.BlockSpec((2 * TM, HEAD_DIM_RET), tab)],
        out_specs=[pl.BlockSpec((2 * TM, GROUP_W), row), pl.BlockSpec((GROUP_W, 2 * TM), lambda p: (0, p))]
        + [pl.BlockSpec((2 * TM, GROUP_W), row)] * 5 + [
            tail_spec, tail_spec,
            pl.BlockSpec((TM, GROUP_W), full),
            pl.BlockSpec((TM, GROUP_W), full)],
        out_shape=[act, jax.ShapeDtypeStruct((GROUP_W, r), BF16)] + [act] * 5
        + [jax.ShapeDtypeStruct((nb * TM, GROUP_W), F32)] * 2
        + [jax.ShapeDtypeStruct((TM, GROUP_W), F32)] * 2,
        compiler_params=_cparams(("arbitrary",), VMEM_LIMIT),
        name="inproj",
    )(xp, xs, shift, scale, g_norm, w_in_b, bd, gq8, gk8, cos_t, sin_t)


def _attn_heads(q, k, v, bias_ref, first_valid_col=None):
    qb_rows, kb_rows = q.shape[0], v.shape[0]
    assert qb_rows == 4 * CHUNK
    half_rows, span = qb_rows // 2, kb_rows - 2 * CHUNK
    parts = [(0, 0), (half_rows, 2 * CHUNK)]

    def softmax_part(s_full, hh, half, r0, c0):
        rs = half * qb_rows + r0
        s = s_full[rs:rs + half_rows, c0:c0 + span] + bias_ref[hh, r0:r0 + half_rows, c0:c0 + span]
        if first_valid_col is not None:
            col = lax.broadcasted_iota(I32, (half_rows, span), 1) + c0
            s = jnp.where(col >= first_valid_col, s, NEG_INF)
        m = jnp.max(s, axis=-1, keepdims=True)
        e = jnp.exp2(s - m)
        l = jnp.sum(e, axis=-1, keepdims=True)
        pad = [jnp.zeros((half_rows, c0), BF16)] if c0 else []
        pad_r = [jnp.zeros((half_rows, kb_rows - span - c0), BF16)] if kb_rows - span - c0 else []
        return jnp.concatenate(pad + [e.astype(BF16)] + pad_r, axis=1), l

    pair_w = 2 * HEAD_DIM_ATT
    low = lax.broadcasted_iota(I32, (1, pair_w), 1) < HEAD_DIM_ATT
    outs = []
    for pp in range(N_HEADS_ATT // 2):
        ps = slice(pp * pair_w, (pp + 1) * pair_w)
        q2, v2 = q[:, ps], v[:, ps]
        zero = jnp.zeros_like(q2)
        qs = jnp.concatenate([jnp.where(low, q2, zero), jnp.where(low, zero, q2)], axis=0)
        s = jnp.dot(qs, k[ps, :], preferred_element_type=F32)
        es, ls = zip(*[softmax_part(s, 2 * pp + half, half, r0, c0)
                       for half in range(2) for r0, c0 in parts])
        o = jnp.dot(jnp.concatenate(es, axis=0), v2, preferred_element_type=F32) / jnp.concatenate(ls, axis=0)
        outs.append(jnp.where(low, o[:qb_rows], o[qb_rows:]))
    return jnp.concatenate(outs, axis=1)


def _fill_band_bias(rev_ref, bias_scr):
    _, qb_rows, kb_rows = bias_scr.shape
    width = rev_ref.shape[1]
    q = lax.broadcasted_iota(I32, (qb_rows, kb_rows), 0)
    k = lax.broadcasted_iota(I32, (qb_rows, kb_rows), 1)
    qc = q >> 6
    kc = (k - ATT_WINDOW) >> 6
    band = (kc >= qc - ATT_WINDOW // CHUNK) & (kc <= qc)
    for hh in range(N_HEADS_ATT):
        rows = jnp.broadcast_to(rev_ref[hh:hh + 1, :], (qb_rows, width))
        toep = pltpu.roll(rows, width - MAX_REL, 1, stride=1, stride_axis=0)
        bias_scr[hh] = jnp.where(band, toep[:, :kb_rows] * LOG2_E, NEG_INF)


def _attn_sample_kernel(q_ref, kn_ref, vn_ref, kc_ref, vc_ref, rev_ref, o_ref, bias_scr):
    @pl.when(pl.program_id(0) == 0)
    def _():
        _fill_band_bias(rev_ref, bias_scr)

    q = q_ref[...]
    outs = []
    for hh in range(N_HEADS_ATT):
        hs = slice(hh * HEAD_DIM_ATT, (hh + 1) * HEAD_DIM_ATT)
        k = jnp.concatenate([kc_ref[0, :, hs], kn_ref[:, hs].astype(BF16)], axis=0)
        v = jnp.concatenate([vc_ref[0, :, hs], vn_ref[:, hs].astype(BF16)], axis=0)
        s = lax.dot_general(q[:, hs], k, (((1,), (1,)), ((), ())), preferred_element_type=F32)
        s = s + bias_scr[hh]
        m = jnp.max(s, axis=-1, keepdims=True)
        e = jnp.exp2(s - m)
        l = jnp.sum(e, axis=-1, keepdims=True)
        outs.append(jnp.dot(e.astype(BF16), v, preferred_element_type=F32) / l)
    o_ref[...] = jnp.concatenate(outs, axis=1).astype(BF16)


def _attn_sample(qa, ks_new, vs_new, kc, vc, rev, rp):
    ndb = kc.shape[0]
    base = rp // CHUNK
    spec = pl.BlockSpec((CHUNK, GROUP_W), lambda b: (base + b, 0))
    new = pl.BlockSpec((CHUNK, GROUP_W), lambda b: (b, 0))
    cspec = pl.BlockSpec((1, ATT_WINDOW, GROUP_W), lambda b: (b, 0, 0))
    return pl.pallas_call(
        _attn_sample_kernel,
        grid=(ndb,),
        in_specs=[spec, new, new, cspec, cspec,
                  pl.BlockSpec(rev.shape, lambda b: (0, 0))],
        out_specs=pl.BlockSpec((CHUNK, GROUP_W), lambda b: (b, 0)),
        out_shape=jax.ShapeDtypeStruct((ndb * CHUNK, GROUP_W), BF16),
        scratch_shapes=[pltpu.VMEM((N_HEADS_ATT, CHUNK, ATT_WINDOW + CHUNK), F32)],
        compiler_params=_cparams(("arbitrary",), VMEM_LIMIT),
        name="attn_sample",
    )(qa, ks_new, vs_new, kc, vc, rev)


def _ret_chunk(state_decay, q_ref, k_ref, v_ref, g_ref, dm_ref, xi_ref, zeta_ref, gro_ref, o_ref, s_scr):
    outs = []
    for hh in range(N_HEADS_RET):
        hs = slice(hh * HEAD_DIM_RET, (hh + 1) * HEAD_DIM_RET)
        q = q_ref[:, hs]
        k = k_ref[:, hs]
        v = v_ref[:, hs]
        st = s_scr[hh]
        sc = lax.dot_general(q, k, (((1,), (1,)), ((), ())), preferred_element_type=F32) * dm_ref[hh]
        inner = jnp.dot(sc.astype(BF16), v, preferred_element_type=F32)
        cross = jnp.dot(q, st.astype(BF16), preferred_element_type=F32) * xi_ref[:, hs]
        o = inner + cross
        kz = k.astype(F32) * zeta_ref[:, hs]
        s_scr[hh] = state_decay[hh] * st + jnp.dot(kz.T.astype(BF16), v, preferred_element_type=F32)
        mu = jnp.mean(o, axis=-1, keepdims=True)
        oc = o - mu
        var = jnp.mean(oc * oc, axis=-1, keepdims=True)
        outs.append(oc * lax.rsqrt(var + NORM_EPS))
    y = jnp.concatenate(outs, axis=1) * gro_ref[...]
    g = g_ref[...].astype(F32)
    o_ref[...] = (g * jax.nn.sigmoid(g) * y).astype(BF16)


def _ret_kernel(state_decay, q_ref, k_ref, v_ref, g_ref, s0_ref, dm_ref, xi_ref, zeta_ref,
                gro_ref, o_ref, sn_ref, s_scr):
    j = pl.program_id(1)

    @pl.when(j == 0)
    def _():
        s_scr[...] = s0_ref[0]

    _ret_chunk(state_decay, q_ref, k_ref, v_ref, g_ref, dm_ref, xi_ref, zeta_ref, gro_ref, o_ref, s_scr)

    @pl.when(j == pl.num_programs(1) - 1)
    def _():
        sn_ref[0] = s_scr[...]


def _mix_prompt_kernel(state_decay, q_ref, k0_ref, k1_ref, k2_ref, v0_ref, v1_ref, v2_ref, rev_ref,
                       rq_ref, rk_ref, rv_ref, rg_ref, s0_ref, dm_ref, xi_ref, zeta_ref, gro_ref,
                       att_ref, ret_ref, sn_ref, bias_scr, s_scr):
    j = pl.program_id(1)

    @pl.when((pl.program_id(0) == 0) & (j == 0))
    def _():
        _fill_band_bias(rev_ref, bias_scr)

    @pl.when(j == 0)
    def _():
        s_scr[...] = s0_ref[0]

    k = jnp.concatenate([k0_ref[...], k1_ref[...], k2_ref[...]], axis=1)
    v = jnp.concatenate([v0_ref[...], v1_ref[...], v2_ref[...]], axis=0)

    def block(first_valid_col):
        att_ref[...] = _attn_heads(q_ref[...], k, v, bias_scr, first_valid_col).astype(BF16)
        _ret_chunk(state_decay, rq_ref, rk_ref, rv_ref, rg_ref, dm_ref, xi_ref, zeta_ref, gro_ref,
                   ret_ref, s_scr)

    @pl.when(j >= 2)
    def _():
        block(None)

    @pl.when(j < 2)
    def _():
        block((2 - j) * ATT_QB)

    @pl.when(j == pl.num_programs(1) - 1)
    def _():
        sn_ref[0] = s_scr[...]


def _mix_prompt(qa, ka_t, va, rev, qb, kb, vb, gb, s0, g_ro, nb, seq):
    assert ATT_QB == RET_CB
    r = nb * seq
    nq = seq // ATT_QB
    dm, xi, zeta, state_decay = _ret_consts(RET_CB)
    blk = lambda back: (lambda b, j: (b * nq + jnp.maximum(j - back, 0), 0))
    spec = lambda back: pl.BlockSpec((ATT_QB, GROUP_W), blk(back))
    tspec = lambda back: pl.BlockSpec((GROUP_W, ATT_QB), lambda b, j: (0, b * nq + jnp.maximum(j - back, 0)))
    sspec = pl.BlockSpec((1, N_HEADS_RET, HEAD_DIM_RET, HEAD_DIM_RET), lambda b, j: (b, 0, 0, 0))
    full2 = lambda b, j: (0, 0)
    out = jax.ShapeDtypeStruct((r, GROUP_W), BF16)
    return pl.pallas_call(
        functools.partial(_mix_prompt_kernel, state_decay),
        grid=(nb, nq),
        in_specs=[spec(0), tspec(2), tspec(1), tspec(0), spec(2), spec(1), spec(0),
                  pl.BlockSpec(rev.shape, full2),
                  spec(0), spec(0), spec(0), spec(0), sspec,
                  pl.BlockSpec(dm.shape, lambda b, j: (0, 0, 0)),
                  pl.BlockSpec(xi.shape, full2), pl.BlockSpec(zeta.shape, full2),
                  pl.BlockSpec((1, GROUP_W), full2)],
        out_specs=[spec(0), spec(0), sspec],
        out_shape=[out, out, jax.ShapeDtypeStruct(s0.shape, F32)],
        scratch_shapes=[pltpu.VMEM((N_HEADS_ATT, ATT_QB, ATT_WINDOW + ATT_QB), F32),
                        pltpu.VMEM((N_HEADS_RET, HEAD_DIM_RET, HEAD_DIM_RET), F32)],
        compiler_params=_cparams(("arbitrary", "arbitrary"), VMEM_LIMIT),
        name="mix_prompt",
    )(qa, ka_t, ka_t, ka_t, va, va, va, rev, qb, kb, vb, gb, s0, dm, xi, zeta, g_ro)


def _ret_consts(cb):
    log_g = np.log1p(-np.exp2(-RET_DECAY_OFFSET - np.arange(N_HEADS_RET, dtype=np.float64)))
    n = np.arange(cb, dtype=np.float64)
    diff = n[:, None] - n[None, :]
    dm = np.where(diff[None] >= 0, np.exp(np.maximum(diff, 0.0)[None] * log_g[:, None, None]), 0.0)
    xi = np.exp((n + 1.0)[:, None] * log_g[None, :])
    zeta = np.exp((cb - 1.0 - n)[:, None] * log_g[None, :])
    rep = lambda a: np.repeat(a, HEAD_DIM_RET, axis=1)
    state_decay = tuple(float(v) for v in np.exp(cb * log_g))
    return (jnp.asarray(dm, F32), jnp.asarray(rep(xi), F32), jnp.asarray(rep(zeta), F32), state_decay)


def _ret(qb, kb, vb, gb, s0, g_ro, cb, row0, nb, nc, name):
    dm, xi, zeta, state_decay = _ret_consts(cb)
    base = row0 // cb
    spec = pl.BlockSpec((cb, GROUP_W), lambda b, j: (base + b * nc + j, 0))
    sspec = pl.BlockSpec((1, N_HEADS_RET, HEAD_DIM_RET, HEAD_DIM_RET), lambda b, j: (b, 0, 0, 0))
    full2 = lambda b, j: (0, 0)
    return pl.pallas_call(
        functools.partial(_ret_kernel, state_decay),
        grid=(nb, nc),
        in_specs=[spec, spec, spec, spec, sspec,
                  pl.BlockSpec(dm.shape, lambda b, j: (0, 0, 0)),
                  pl.BlockSpec(xi.shape, full2), pl.BlockSpec(zeta.shape, full2),
                  pl.BlockSpec((1, GROUP_W), full2)],
        out_specs=[pl.BlockSpec((cb, GROUP_W), lambda b, j: (b * nc + j, 0)), sspec],
        out_shape=[jax.ShapeDtypeStruct((nb * nc * cb, GROUP_W), BF16),
                   jax.ShapeDtypeStruct(s0.shape, F32)],
        scratch_shapes=[pltpu.VMEM((N_HEADS_RET, HEAD_DIM_RET, HEAD_DIM_RET), F32)],
        compiler_params=_cparams(("arbitrary", "arbitrary"), VMEM_LIMIT),
        name=name,
    )(qb, kb, vb, gb, s0, dm, xi, zeta, g_ro)


def _outproj_kernel(npp, attp_ref, atts_ref, retp_ref, rets_ref, xp_ref, xs_ref, gm_ref, shf_ref, scf_ref,
                    gn_ref, wo_ref, wr_ref, br_ref, upper_ref, lower_ref,
                    x1_ref, h2_ref, slot_ref, cols_ref, cnt_ref):
    is_p = pl.program_id(0) < npp
    subs = range(2)
    rows = [slice(sub * TM, (sub + 1) * TM) for sub in subs]

    def pick(p_ref, s_ref, sub):
        return jnp.where(is_p, p_ref[rows[sub], :], s_ref[...])

    mix = [jnp.dot(pick(attp_ref, atts_ref, sub), wo_ref[:GROUP_W, :], preferred_element_type=F32)
           + jnp.dot(pick(retp_ref, rets_ref, sub), wo_ref[GROUP_W:, :], preferred_element_type=F32)
           for sub in subs]
    h2b = []
    for sub in subs:
        x1 = _per_group(mix[sub], lambda a, gm: a * gm, gm_ref[...]) + pick(xp_ref, xs_ref, sub)
        x1_ref[rows[sub], :] = x1
        y = _rms_rows(x1, gn_ref[...])
        h2 = _per_group(y, lambda a, sh, sc: a * (1.0 + sc) + sh, shf_ref[...], scf_ref[...])
        h2b.append(h2.astype(BF16))
        h2_ref[rows[sub], :] = h2b[sub]

    work = [lax.dot_general(wr_ref[...], h2b[sub], (((1,), (1,)), ((), ())),
                            preferred_element_type=F32) + br_ref[...] for sub in subs]
    eidx = lax.broadcasted_iota(I32, work[0].shape, 0).astype(F32)
    sel = [[] for _ in subs]
    top = [[] for _ in subs]
    for _ in range(TOP_K):
        for sub in subs:
            m = jnp.max(work[sub], axis=0, keepdims=True)
            idx = jnp.min(jnp.where(work[sub] == m, eidx, float(N_EXPERTS)), axis=0, keepdims=True)
            hit = eidx == idx
            sel[sub].append(hit)
            top[sub].append(m)
            work[sub] = jnp.where(hit, -jnp.inf, work[sub])

    for sub in subs:
        ex = [jnp.exp(t - top[sub][0]) for t in top[sub]]
        den = ex[0] + ex[1] + ex[2] + ex[3]
        gates = [e / den for e in ex]
        for part in range(TM // MT):
            lanes = slice(part * MT, (part + 1) * MT)
            tile = (TM // MT) * sub + part
            hits = [h[:, lanes] for h in sel[sub]]
            multi_f = jnp.where(hits[0] | hits[1] | hits[2] | hits[3], 1.0, 0.0)
            rank = jnp.dot(multi_f.astype(BF16), upper_ref[...], preferred_element_type=F32)
            cnt = jnp.sum(multi_f, axis=1, keepdims=True)
            cnt_pad = jnp.maximum(jnp.floor((cnt + (SEG_ALIGN - 1.0)) * (1.0 / SEG_ALIGN)), 1.0) * SEG_ALIGN
            cnt_pad_b = jnp.broadcast_to(cnt_pad, (N_EXPERTS, 128))
            seg_off = jnp.dot(lower_ref[...], cnt_pad_b.astype(BF16), preferred_element_type=F32)[:, :1]
            pos = seg_off + rank
            slot_rows = jnp.concatenate(
                [jnp.sum(jnp.where(h, pos, 0.0), axis=0, keepdims=True) for h in hits], axis=0)
            gate_rows = jnp.concatenate([g[:, lanes] for g in gates], axis=0)
            slot_ref[tile] = slot_rows.astype(I32)
            cnt_ref[tile] = cnt_pad_b.astype(I32)
            both = jnp.concatenate([slot_rows, gate_rows, jnp.zeros((128 - 2 * TOP_K, MT), F32)], axis=0)
            cols_ref[tile] = both.T


def _outproj(att_p, att_s, ret_p, ret_s, xp, xs, gate_m, shift_f, scale_f, g_norm, w_out_b, wr_t, br,
             upper, lower, nb, tps):
    rp = xp.shape[0]
    ntp = rp // TM
    assert ntp % 2 == 0 and tps % 2 == 0
    npp = ntp // 2
    r = (ntp + 2) * TM
    per_step = 2 * TM // MT
    ntm = (npp + 1) * per_step
    row = lambda p: (p, 0)
    row3 = lambda p: (p, 0, 0)
    full = lambda p: (0, 0)
    prow = lambda p: (jnp.minimum(p, npp - 1), 0)
    mod = pl.BlockSpec((GROUPS_PER_TILE, D_MODEL), _mod_row(npp, tps // 2, nb))
    return pl.pallas_call(
        functools.partial(_outproj_kernel, npp),
        grid=(npp + 1,),
        in_specs=[pl.BlockSpec((2 * TM, GROUP_W), prow), pl.BlockSpec((TM, GROUP_W), full),
                  pl.BlockSpec((2 * TM, GROUP_W), prow), pl.BlockSpec((TM, GROUP_W), full),
                  pl.BlockSpec((2 * TM, D_MODEL), prow),
                  pl.BlockSpec((TM, D_MODEL), full),
                  mod, mod, mod,
                  pl.BlockSpec((1, D_MODEL), full),
                  pl.BlockSpec((D_MODEL, D_MODEL), full),
                  pl.BlockSpec((N_EXPERTS, D_MODEL), full),
                  pl.BlockSpec((N_EXPERTS, 1), full),
                  pl.BlockSpec((MT, MT), full),
                  pl.BlockSpec((N_EXPERTS, N_EXPERTS), full)],
        out_specs=[pl.BlockSpec((2 * TM, D_MODEL), row), pl.BlockSpec((2 * TM, D_MODEL), row),
                   pl.BlockSpec((per_step, TOP_K, MT), row3),
                   pl.BlockSpec((per_step, MT, 128), row3), pl.BlockSpec((per_step, N_EXPERTS, 128), row3)],
        out_shape=[jax.ShapeDtypeStruct((r, D_MODEL), F32), jax.ShapeDtypeStruct((r, D_MODEL), BF16),
                   jax.ShapeDtypeStruct((ntm, TOP_K, MT), I32),
                   jax.ShapeDtypeStruct((ntm, MT, 128), F32), jax.ShapeDtypeStruct((ntm, N_EXPERTS, 128), I32)],
        compiler_params=_cparams(("arbitrary",), VMEM_LIMIT),
        name="outproj",
    )(att_p, att_s, ret_p, ret_s, xp, xs, gate_m, shift_f, scale_f, g_norm, w_out_b, wr_t, br, upper, lower)


def _store_packed(ref, lead, row0, x):
    n = x.shape[0]
    for j in range(PACK_ROWS):
        c = 2 * LANES * j
        lo = lax.bitcast_convert_type(x[:, c:c + LANES].astype(BF16).astype(F32), U32) >> 16
        hi = lax.bitcast_convert_type(x[:, c + LANES:c + 2 * LANES].astype(BF16).astype(F32), U32)
        ref[lead + (pl.ds(PACK_ROWS * row0 + j, n, stride=PACK_ROWS), slice(None))] = hi | lo


def _load_packed(ref, lead, row0, n):
    parts = []
    for j in range(PACK_ROWS):
        u = ref[lead + (pl.ds(PACK_ROWS * row0 + j, n, stride=PACK_ROWS), slice(None))]
        parts.append(lax.bitcast_convert_type(u << 16, F32))
        parts.append(lax.bitcast_convert_type(u & jnp.uint32(0xFFFF0000), F32))
    return jnp.concatenate(parts, axis=1).astype(BF16)


def _packed_rows(ref, lead, tok0, ntok):
    rows = pl.ds(pl.multiple_of(tok0 * PACK_ROWS, SUBLANES_32), ntok * PACK_ROWS)
    return ref.at[lead + (rows, slice(None))]


def _rows_copy(n, src_rows, dst_rows, sem):
    size = pl.multiple_of(n, SEG_ALIGN)
    return pltpu.make_async_copy(src_rows(size), dst_rows(size), sem)


def _start_segments(t, cnt_ref, off_ref, base_ref, local_rows, sorted_rows, sem, to_sorted):
    for e in range(N_EXPERTS):
        n = cnt_ref[t * N_EXPERTS + e]
        off = pl.multiple_of(off_ref[t * N_EXPERTS + e], SEG_ALIGN)
        base = pl.multiple_of(base_ref[t * N_EXPERTS + e], SEG_ALIGN)
        local = lambda z, off=off: local_rows(off, z)
        remote = lambda z, base=base: sorted_rows(base, z)
        (_rows_copy(n, local, remote, sem) if to_sorted else _rows_copy(n, remote, local, sem)).start()


def _dispatch_kernel(nt, n_blocks, off_ref, cnt_ref, base_ref, tot_ref, tail0_ref, tailn_ref, na_ref,
                     h2_ref, slot_ref, slotn_ref, xb_ref, xs_scr, hot_scr, zero_scr, sems, tail_sem):
    i = pl.program_id(0)
    cur = i % 2
    sorted_rows = lambda r, z: _packed_rows(xb_ref, (), r, z)

    def start_tile(t, buf):
        for part in range(PARTS):
            _start_segments(PARTS * t + part, cnt_ref, off_ref, base_ref,
                            lambda r, z, part=part: _packed_rows(xs_scr, (buf, part), r, z),
                            sorted_rows, sems.at[buf], True)

    def wait_tile(t, buf):
        for part in range(PARTS):
            _rows_copy(tot_ref[PARTS * t + part], lambda z, part=part: _packed_rows(xs_scr, (buf, part), 0, z),
                       lambda z: sorted_rows(0, z), sems.at[buf]).wait()

    def tail_copies(wait):
        def body(e, c):
            base = pl.multiple_of(tail0_ref[e], SEG_ALIGN)

            @pl.when(tailn_ref[e] > 0)
            def _():
                cp = _rows_copy(tailn_ref[e], lambda z: _packed_rows(zero_scr, (), 0, z),
                                lambda z: sorted_rows(base, z), tail_sem)
                cp.wait() if wait else cp.start()
            return c
        lax.fori_loop(0, N_EXPERTS, body, 0)

        def unused(j, c):
            cp = pltpu.make_async_copy(zero_scr, sorted_rows(pl.multiple_of(j * BM, BM), BM), tail_sem)
            cp.wait() if wait else cp.start()
            return c
        lax.fori_loop(na_ref[0], n_blocks, unused, 0)

    @pl.when(i >= 2)
    def _():
        wait_tile(i - 2, cur)

    def onehot(slot):
        srow = lax.broadcasted_iota(I32, (CAP_USED, MT), 0)
        hit = (srow == slot[0:1]) | (srow == slot[1:2]) | (srow == slot[2:3]) | (srow == slot[3:4])
        return jnp.where(hit, 1.0, 0.0).astype(BF16)

    @pl.when(i == 0)
    def _():
        for part in range(PARTS):
            hot_scr[0, part] = onehot(slot_ref[part])
        zero_scr[...] = jnp.zeros_like(zero_scr)
        tail_copies(False)

    for part in range(PARTS):
        rows = h2_ref[part * MT:(part + 1) * MT, :]
        _store_packed(xs_scr, (cur, part), 0, jnp.dot(hot_scr[cur, part], rows, preferred_element_type=F32))
    for part in range(PARTS):
        hot_scr[1 - cur, part] = onehot(slotn_ref[part])
    start_tile(i, cur)

    @pl.when(i == nt - 1)
    def _():
        if nt >= 2:
            wait_tile(i - 1, 1 - cur)
        wait_tile(i, cur)
        tail_copies(True)


def _dispatch(h2, slot, off, cnt, base, tot, tail0, tailn, n_act, n_blocks):
    nt = tot.shape[0] // PARTS
    n_rows = n_blocks * BM
    grid_spec = pltpu.PrefetchScalarGridSpec(
        num_scalar_prefetch=7,
        grid=(nt,),
        in_specs=[pl.BlockSpec((TM, D_MODEL), lambda i, *_: (i, 0)),
                  pl.BlockSpec((PARTS, TOP_K, MT), lambda i, *_: (i, 0, 0)),
                  pl.BlockSpec((PARTS, TOP_K, MT), lambda i, *_: (jnp.minimum(i + 1, nt - 1), 0, 0))],
        out_specs=pl.BlockSpec(memory_space=pl.ANY),
        scratch_shapes=[pltpu.VMEM((2, PARTS, CAP_USED * PACK_ROWS, LANES), U32),
                        pltpu.VMEM((2, PARTS, CAP_USED, MT), BF16),
                        pltpu.VMEM((BM * PACK_ROWS, LANES), U32),
                        pltpu.SemaphoreType.DMA((2,)),
                        pltpu.SemaphoreType.DMA(())],
    )
    return pl.pallas_call(
        functools.partial(_dispatch_kernel, nt, n_blocks),
        grid_spec=grid_spec,
        out_shape=jax.ShapeDtypeStruct((n_rows * PACK_ROWS, LANES), U32),
        compiler_params=_cparams(("arbitrary",), VMEM_LIMIT),
        name="dispatch",
    )(off, cnt, base, tot, tail0, tailn, n_act, h2, slot, slot)


def _experts_kernel(be_ref, bi_ref, nx_ref, nq_ref, na_ref, x_ref, wu_hbm, bu_ref, wd_hbm, bd_ref, y_ref,
                    wu_stage, wd_stage, wu_scr, wd_scr, sems):
    j = pl.program_id(0)

    def weight_copies(e):
        return (pltpu.make_async_copy(wu_hbm.at[e], wu_stage, sems.at[0]),
                pltpu.make_async_copy(wd_hbm.at[e], wd_stage, sems.at[1]))

    @pl.when(j < na_ref[0])
    def _():
        e = be_ref[j]
        prev = be_ref[jnp.maximum(j - 1, 0)]

        @pl.when(j == 0)
        def _():
            for cp in weight_copies(e):
                cp.start()

        @pl.when((j == 0) | (e != prev))
        def _():
            for cp in weight_copies(e):
                cp.wait()
            wu_scr[...] = wu_stage[...].astype(BF16)
            wd_scr[...] = wd_stage[...].astype(BF16)

            @pl.when(nx_ref[j] != e)
            def _():
                for cp in weight_copies(nx_ref[j]):
                    cp.start()

        def ffn(n):
            x = _load_packed(x_ref, (), 0, n)
            u = jnp.dot(x, wu_scr[...], preferred_element_type=F32) + bu_ref[0]
            glu = jnp.minimum(u[:, :D_FF], SWIGLU_LIMIT)
            lin = jnp.clip(u[:, D_FF:], -SWIGLU_LIMIT, SWIGLU_LIMIT)
            act = glu * jax.nn.sigmoid(SWIGLU_ALPHA * glu) * (lin + 1.0)
            y = jnp.dot(act.astype(BF16), wd_scr[...], preferred_element_type=F32) + bd_ref[0]
            _store_packed(y_ref, (), 0, y)

        for quarters in range(1, BLOCK_QUARTERS + 1):
            @pl.when(nq_ref[j] == quarters)
            def _(quarters=quarters):
                n = quarters * (BM // BLOCK_QUARTERS)
                ffn(n)
                if n < BM:
                    y_ref[PACK_ROWS * n:, :] = jnp.zeros((PACK_ROWS * (BM - n), LANES), U32)


def _experts(xb, blk_e, blk_i, blk_nx, blk_nq, n_act, w_up, b_up, w_down, b_down):
    n_rows = xb.shape[0] // PACK_ROWS
    nblk = n_rows // BM
    grid_spec = pltpu.PrefetchScalarGridSpec(
        num_scalar_prefetch=5,
        grid=(nblk,),
        in_specs=[pl.BlockSpec((BM * PACK_ROWS, LANES), lambda j, be, bi, *_: (bi[j], 0)),
                  pl.BlockSpec(memory_space=pl.ANY),
                  pl.BlockSpec((1, 1, 2 * D_FF), lambda j, be, *_: (be[j], 0, 0)),
                  pl.BlockSpec(memory_space=pl.ANY),
                  pl.BlockSpec((1, 1, D_MODEL), lambda j, be, *_: (be[j], 0, 0))],
        out_specs=pl.BlockSpec((BM * PACK_ROWS, LANES), lambda j, be, bi, *_: (bi[j], 0)),
        scratch_shapes=[pltpu.VMEM((D_MODEL, 2 * D_FF), F32), pltpu.VMEM((D_FF, D_MODEL), F32),
                        pltpu.VMEM((D_MODEL, 2 * D_FF), BF16), pltpu.VMEM((D_FF, D_MODEL), BF16),
                        pltpu.SemaphoreType.DMA((2,))],
    )
    return pl.pallas_call(
        _experts_kernel,
        grid_spec=grid_spec,
        out_shape=jax.ShapeDtypeStruct(xb.shape, xb.dtype),
        input_output_aliases={5: 0},
        compiler_params=_cparams(("arbitrary",), VMEM_LIMIT),
        name="experts",
    )(blk_e, blk_i, blk_nx, blk_nq, n_act, xb, w_up, b_up.reshape(N_EXPERTS, 1, 2 * D_FF), w_down,
      b_down.reshape(N_EXPERTS, 1, D_MODEL))


def _combine_kernel(nt, ntp, off_ref, cnt_ref, base_ref, tot_ref, yb_ref, cols_ref, x1_ref, gf_ref,
                    op_ref, os_ref, ys_scr, sems):
    i = pl.program_id(0)
    cur = i % 2

    sorted_rows = lambda r, z: _packed_rows(yb_ref, (), r, z)

    def start_tile(t, buf):
        for part in range(PARTS):
            _start_segments(PARTS * t + part, cnt_ref, off_ref, base_ref,
                            lambda r, z, part=part: _packed_rows(ys_scr, (buf, part), r, z),
                            sorted_rows, sems.at[buf], False)

    @pl.when(i == 0)
    def _():
        ys_scr[...] = jnp.zeros_like(ys_scr)
        start_tile(0, 0)

    def wait_tile(t, buf):
        for part in range(PARTS):
            _rows_copy(tot_ref[PARTS * t + part], lambda z: sorted_rows(0, z),
                       lambda z, part=part: _packed_rows(ys_scr, (buf, part), 0, z), sems.at[buf]).wait()

    nxt = jnp.minimum(i + 1, nt - 1)
    wait_tile(i, cur)
    start_tile(nxt, 1 - cur)

    @pl.when(i == nt - 1)
    def _():
        wait_tile(nxt, 1 - cur)

    lane = lax.broadcasted_iota(I32, (MT, CAP), 1)
    ys = []
    for part in range(PARTS):
        cols = cols_ref[part]
        w = jnp.zeros((MT, CAP), F32)
        for k in range(TOP_K):
            sk = cols[:, k:k + 1].astype(I32)
            gk = cols[:, TOP_K + k:TOP_K + k + 1]
            w = jnp.where(lane == sk, gk, w)
        ys.append(jnp.dot(w.astype(BF16), _load_packed(ys_scr, (cur, part), 0, CAP),
                          preferred_element_type=F32))
    out = x1_ref[...] + _per_group(jnp.concatenate(ys, axis=0), lambda a, g: a * g, gf_ref[...])

    @pl.when(i < ntp)
    def _():
        op_ref[...] = out

    @pl.when(i >= ntp)
    def _():
        os_ref[...] = out


def _combine(yb, cols, x1, gate_f, off, cnt, base, tot, ntp, nb, tps):
    nt = tot.shape[0] // PARTS
    grid_spec = pltpu.PrefetchScalarGridSpec(
        num_scalar_prefetch=4,
        grid=(nt,),
        in_specs=[pl.BlockSpec(memory_space=pl.ANY),
                  pl.BlockSpec((PARTS, MT, 128), lambda i, *_: (i, 0, 0)),
                  pl.BlockSpec((TM, D_MODEL), lambda i, *_: (i, 0)),
                  pl.BlockSpec((GROUPS_PER_TILE, D_MODEL), _mod_row(ntp, tps, nb))],
        out_specs=[pl.BlockSpec((TM, D_MODEL), lambda i, *_: (jnp.minimum(i, ntp - 1), 0)),
                   pl.BlockSpec((TM, D_MODEL), lambda i, *_: (0, 0))],
        scratch_shapes=[pltpu.VMEM((2, PARTS, CAP * PACK_ROWS, LANES), U32),
                        pltpu.SemaphoreType.DMA((2,))],
    )
    return pl.pallas_call(
        functools.partial(_combine_kernel, nt, ntp),
        grid_spec=grid_spec,
        out_shape=[jax.ShapeDtypeStruct((ntp * TM, D_MODEL), F32),
                   jax.ShapeDtypeStruct((TM, D_MODEL), F32)],
        compiler_params=_cparams(("arbitrary",), VMEM_LIMIT),
        name="combine",
    )(off, cnt, base, tot, yb, cols, x1, gate_f)


def _rotary_tables(seq, dec_batch, dec_seq):
    half = HEAD_DIM_RET // 2
    inv = ROPE_BASE ** (-np.arange(half, dtype=np.float64) / half)
    pos = np.concatenate([np.arange(seq), np.tile(PAST_LEN + np.arange(dec_seq), 2 * dec_batch)])
    ang = pos.astype(np.float64)[:, None] * inv[None, :]
    cos = np.concatenate([np.cos(ang), np.cos(ang)], axis=1)
    sin = np.concatenate([-np.sin(ang), np.sin(ang)], axis=1)
    return jnp.asarray(cos, F32), jnp.asarray(sin, F32)


def _rel_bias_reversed(rel_bias):
    heads = rel_bias.shape[0]
    ext = jnp.concatenate([rel_bias[:, 1:], jnp.broadcast_to(rel_bias[:, -1:], (heads, 2 * MAX_REL))], axis=1)
    return ext[:, ::-1].astype(F32)


def _group_mods(m, nb, ndb):
    assert ndb == GROUPS_PER_TILE
    mp = jnp.broadcast_to(m[:nb, None], (nb, GROUPS_PER_TILE) + m.shape[1:])
    allm = jnp.concatenate([mp.reshape((nb * GROUPS_PER_TILE,) + m.shape[1:]), m[nb:]], axis=0)
    return jnp.transpose(allm, (1, 0, 2))


def _routing_tables(cnt, n_blocks):
    nt = cnt.shape[0]
    off = jnp.cumsum(cnt, axis=1) - cnt
    rows_e = jnp.sum(cnt, axis=0)
    nblk_e = (rows_e + BM - 1) // BM
    blk_end = jnp.cumsum(nblk_e)
    start_e = (blk_end - nblk_e) * BM
    base = start_e[None, :] + jnp.cumsum(cnt, axis=0) - cnt
    n_act = blk_end[-1]
    j = jnp.minimum(jnp.arange(n_blocks), n_act - 1)
    blk_e = jnp.minimum(jnp.sum(blk_end[None, :] <= j[:, None], axis=1), N_EXPERTS - 1)
    later = jnp.where(blk_e[None, :] > blk_e[:, None], blk_e[None, :], N_EXPERTS)
    blk_nx = jnp.min(later, axis=1)
    blk_nx = jnp.where(blk_nx == N_EXPERTS, blk_e, blk_nx)
    tail0 = start_e + rows_e
    tailn = nblk_e * BM - rows_e
    mine = blk_e[:, None] == jnp.arange(N_EXPERTS)[None, :]
    blk_rows = jnp.sum(jnp.where(mine, (rows_e + start_e)[None, :], 0), axis=1) - j * BM
    quarter = BM // BLOCK_QUARTERS
    blk_nq = jnp.clip((blk_rows + quarter - 1) // quarter, 1, BLOCK_QUARTERS)
    i32 = lambda a: a.astype(I32)
    return (i32(off.reshape(nt * N_EXPERTS)), i32(cnt.reshape(nt * N_EXPERTS)),
            i32(base.reshape(nt * N_EXPERTS)), i32(jnp.sum(cnt, axis=1)), i32(tail0), i32(tailn),
            i32(blk_e), i32(j), i32(blk_nx), i32(blk_nq), i32(n_act.reshape(1)))


def kernel(x_prompt, x_sample, c_prompt, c_sample, cache_att_k, cache_att_v, state_ret, w_ada, b_ada,
           g_norm_mix, g_norm_ffn, w_in, g_q, g_k, rel_bias, g_ret_out, w_out, w_router, b_router,
           w_up, b_up, w_down, b_down):
    nb, seq, d = x_prompt.shape
    ndb, dseq, _ = x_sample.shape
    assert d == D_MODEL and ndb * dseq == TM and dseq == CHUNK
    assert seq % TM == 0 and seq >= ATT_WINDOW and cache_att_k.shape[2] == ATT_WINDOW
    assert w_ada.shape[0] == 1
    rp = nb * seq
    ntp = rp // TM
    nt = ntp + 1
    tps = seq // TM

    xp = x_prompt.reshape(rp, d)
    xs = x_sample.reshape(TM, d)

    m = _ada(jnp.concatenate([c_prompt, c_sample], axis=0), w_ada[0], b_ada[0])
    mods = _group_mods(m.reshape(nb + ndb, N_ADA, d), nb, ndb)
    shift_m, scale_m, gate_m, shift_f, scale_f, gate_f = [mods[a] for a in range(N_ADA)]

    cos_t, sin_t = _rotary_tables(seq, ndb, dseq)
    bd = jnp.asarray(np.kron(np.eye(N_HEADS_ATT // 2), np.ones((HEAD_DIM_ATT, HEAD_DIM_ATT))), BF16)
    tile8 = lambda g: jnp.tile(g.astype(F32), N_HEADS_ATT).reshape(1, GROUP_W)
    (qa, ka_t, va, qb, kb, vb, gb, kp_tail, vp_tail, ks_new, vs_new) = _inproj(
        xp, xs, shift_m, scale_m, g_norm_mix[0].reshape(1, d), w_in[0].astype(BF16), bd,
        tile8(g_q[0]) * (HEAD_DIM_ATT ** -0.5 * LOG2_E), tile8(g_k[0]), cos_t, sin_t, nb, tps)

    rev = _rel_bias_reversed(rel_bias[0])
    g_ro = g_ret_out[0].astype(F32).reshape(1, GROUP_W)
    zero_state = jnp.zeros((nb, N_HEADS_RET, HEAD_DIM_RET, HEAD_DIM_RET), F32)
    att_p, ret_p, state_p = _mix_prompt(qa, ka_t, va, rev, qb, kb, vb, gb, zero_state, g_ro, nb, seq)
    att_s = _attn_sample(qa, ks_new, vs_new,
                         cache_att_k[0].reshape(ndb, ATT_WINDOW, GROUP_W).astype(BF16),
                         cache_att_v[0].reshape(ndb, ATT_WINDOW, GROUP_W).astype(BF16), rev, rp)

    ret_s, state_s = _ret(qb, kb, vb, gb, state_ret[0].astype(F32), g_ro, CHUNK, rp, ndb, 1, "ret_sample")

    upper = jnp.asarray(np.triu(np.ones((MT, MT)), 1), BF16)
    lower = jnp.asarray(np.tril(np.ones((N_EXPERTS, N_EXPERTS)), -1), BF16)
    x1, h2, slot, cols, cnt = _outproj(
        att_p, att_s, ret_p, ret_s, xp, xs, gate_m, shift_f, scale_f, g_norm_ffn[0].reshape(1, d),
        w_out[0].astype(BF16), w_router[0].T.astype(BF16), b_router[0].astype(F32).reshape(N_EXPERTS, 1),
        upper, lower, nb, tps)

    ntm = nt * (TM // MT)
    n_blocks = (TOP_K * (rp + TM) + ntm * N_EXPERTS * SEG_ALIGN) // BM + 1 + N_EXPERTS
    (off, cntf, base, tot, tail0, tailn, blk_e, blk_i, blk_nx, blk_nq,
     n_act) = _routing_tables(cnt[:ntm, :, 0], n_blocks)
    xb = _dispatch(h2, slot, off, cntf, base, tot, tail0, tailn, n_act, n_blocks)
    yb = _experts(xb, blk_e, blk_i, blk_nx, blk_nq, n_act, w_up[0], b_up[0], w_down[0], b_down[0])
    out_p, out_s = _combine(yb, cols, x1, gate_f, off, cntf, base, tot, ntp, nb, tps)

    heads = (N_HEADS_ATT, HEAD_DIM_ATT)
    return (out_p.reshape(nb, seq, d), out_s.reshape(ndb, dseq, d),
            kp_tail.reshape(1, nb, ATT_WINDOW, *heads), vp_tail.reshape(1, nb, ATT_WINDOW, *heads),
            state_p[None],
            ks_new.reshape(1, ndb, dseq, *heads), vs_new.reshape(1, ndb, dseq, *heads),
            state_s[None])
```

```python
import functools

import numpy as np
import jax
import jax.numpy as jnp
from jax import lax
from jax.experimental import pallas as pl
from jax.experimental.pallas import tpu as pltpu

F32 = jnp.float32
BF16 = jnp.bfloat16
I32 = jnp.int32
U32 = jnp.uint32

D_MODEL = 1024
GROUP_W = 512
N_SLOTS = 7
N_HEADS_ATT = 8
HEAD_DIM_ATT = 64
N_HEADS_RET = 4
HEAD_DIM_RET = 128
CHUNK = 64
ATT_WINDOW = 512
MAX_REL = 256
PAST_LEN = 2048
RET_DECAY_OFFSET = 5.0
ROPE_BASE = 10000.0
N_EXPERTS = 32
TOP_K = 4
D_FF = 1024
SWIGLU_LIMIT = 7.0
SWIGLU_ALPHA = 1.702
N_ADA = 6
NORM_EPS = 1e-6
NEG_INF = -1e30
LOG2_E = 1.4426950408889634

TM = 512
GROUPS_PER_TILE = TM // CHUNK
ATT_QB = 256
RET_CB = 256
LANES = 128
PACK_ROWS = D_MODEL // (2 * LANES)
SUBLANES_32 = 8
SEG_ALIGN = SUBLANES_32 // PACK_ROWS
MT = 256
PARTS = TM // MT
CAP_USED = TOP_K * MT + N_EXPERTS * SEG_ALIGN
CAP = -(-CAP_USED // LANES) * LANES
X_RING = 3
BM = 512
BLOCK_QUARTERS = 4
VMEM_LIMIT = 56 * 1024 * 1024


def _cparams(sem, vmem=None):
    return pltpu.CompilerParams(dimension_semantics=sem, vmem_limit_bytes=vmem)


def _ada_kernel(c_ref, w_ref, b_ref, o_ref):
    c = c_ref[...]
    s = c * jax.nn.sigmoid(c)
    o_ref[...] = jnp.dot(s.astype(BF16), w_ref[...].astype(BF16),
                         preferred_element_type=F32) + b_ref[...]


def _ada(c_all, w_ada, b_ada):
    n, d = c_all.shape
    cols = w_ada.shape[1]
    tn = 1536
    return pl.pallas_call(
        _ada_kernel,
        grid=(cols // tn,),
        in_specs=[pl.BlockSpec((n, d), lambda j: (0, 0)),
                  pl.BlockSpec((d, tn), lambda j: (0, j)),
                  pl.BlockSpec((1, tn), lambda j: (0, j))],
        out_specs=pl.BlockSpec((n, tn), lambda j: (0, j)),
        out_shape=jax.ShapeDtypeStruct((n, cols), F32),
        compiler_params=_cparams(("arbitrary",), VMEM_LIMIT),
        name="ada",
    )(c_all, w_ada, b_ada.reshape(1, cols))


def _rms_rows(x, g):
    ms = jnp.mean(x * x, axis=-1, keepdims=True)
    return x * lax.rsqrt(ms + NORM_EPS) * g


def _mod_row(ntp, tps, nb):
    return lambda i, *_: (jnp.where(i < ntp, i // tps, nb), 0)


def _per_group(x, fn, *mods):
    x3 = x.reshape(x.shape[0] // CHUNK, CHUNK, x.shape[-1])
    y3 = fn(x3, *[m[:, None, :] for m in mods])
    return y3.reshape(x.shape)


def _inproj_kernel(npp, xp_ref, xs_ref, sh_ref, sc_ref, gn_ref, w_ref, bd_ref, gq_ref, gk_ref,
                   cos_ref, sin_ref,
                   qa_ref, ka_ref, va_ref, qb_ref, kb_ref, vb_ref, gb_ref,
                   kpt_ref, vpt_ref, kst_ref, vst_ref):
    is_p = pl.program_id(0) < npp
    subs = range(2)
    rows = [slice(sub * TM, (sub + 1) * TM) for sub in subs]

    hb = []
    for sub in subs:
        x = jnp.where(is_p, xp_ref[rows[sub], :], xs_ref[...])
        y = _rms_rows(x, gn_ref[...])
        h = _per_group(y, lambda a, sh, sc: a * (1.0 + sc) + sh, sh_ref[...], sc_ref[...])
        hb.append(h.astype(BF16))

    def proj(s):
        return [jnp.dot(hb[sub], w_ref[:, s * GROUP_W:(s + 1) * GROUP_W], preferred_element_type=F32)
                for sub in subs]

    def head_rms(z, g):
        zz = (z * z).astype(BF16)
        half = GROUP_W // 2
        ss = jnp.concatenate(
            [jnp.dot(zz[:, :half], bd_ref[...], preferred_element_type=F32),
             jnp.dot(zz[:, half:], bd_ref[...], preferred_element_type=F32)], axis=1)
        return z * lax.rsqrt(ss * (1.0 / HEAD_DIM_ATT) + NORM_EPS) * g

    def rot(z, sub):
        cos = cos_ref[rows[sub], :]
        sin = sin_ref[rows[sub], :]
        outs = []
        for hh in range(N_HEADS_RET):
            zh = z[:, hh * HEAD_DIM_RET:(hh + 1) * HEAD_DIM_RET]
            outs.append(zh * cos + pltpu.roll(zh, HEAD_DIM_RET // 2, axis=1) * sin)
        return jnp.concatenate(outs, axis=1)

    for sub, z in zip(subs, proj(0)):
        qa_ref[rows[sub], :] = head_rms(z, gq_ref[...]).astype(BF16)
    ka = [head_rms(z, gk_ref[...]) for z in proj(1)]
    for sub in subs:
        ka_ref[:, rows[sub]] = ka[sub].T.astype(BF16)
    va = proj(2)
    for sub in subs:
        va_ref[rows[sub], :] = va[sub].astype(BF16)

    @pl.when(is_p)
    def _():
        kpt_ref[...] = ka[1]
        vpt_ref[...] = va[1]

    @pl.when(jnp.logical_not(is_p))
    def _():
        kst_ref[...] = ka[0]
        vst_ref[...] = va[0]

    for sub, z in zip(subs, proj(3)):
        qb_ref[rows[sub], :] = rot(z, sub).astype(BF16)
    for sub, z in zip(subs, proj(4)):
        kb_ref[rows[sub], :] = (rot(z, sub) * (HEAD_DIM_RET ** -0.5)).astype(BF16)
    for sub, z in zip(subs, proj(5)):
        vb_ref[rows[sub], :] = z.astype(BF16)
    for sub, z in zip(subs, proj(6)):
        gb_ref[rows[sub], :] = z.astype(BF16)


def _inproj(xp, xs, shift, scale, g_norm, w_in_b, bd, gq8, gk8, cos_t, sin_t, nb, tps):
    rp = xp.shape[0]
    ntp = rp // TM
    assert ntp % 2 == 0 and tps % 2 == 0
    npp = ntp // 2
    r = (ntp + 2) * TM
    row = lambda p: (p, 0)
    full = lambda p: (0, 0)
    tab = lambda p: (jnp.where(p < npp, p % (tps // 2), tps // 2), 0)
    mod = pl.BlockSpec((GROUPS_PER_TILE, D_MODEL), _mod_row(npp, tps // 2, nb))
    tail_spec = pl.BlockSpec((TM, GROUP_W), lambda p: (jnp.minimum(p // (tps // 2), nb - 1), 0))
    act = jax.ShapeDtypeStruct((r, GROUP_W), BF16)
    return pl.pallas_call(
        functools.partial(_inproj_kernel, npp),
        grid=(npp + 1,),
        in_specs=[pl.BlockSpec((2 * TM, D_MODEL), lambda p: (jnp.minimum(p, npp - 1), 0)),
                  pl.BlockSpec((TM, D_MODEL), full),
                  mod, mod,
                  pl.BlockSpec((1, D_MODEL), full),
                  pl.BlockSpec((D_MODEL, N_SLOTS * GROUP_W), full),
                  pl.BlockSpec((GROUP_W // 2, GROUP_W // 2), full),
                  pl.BlockSpec((1, GROUP_W), full),
                  pl.BlockSpec((1, GROUP_W), full),
                  pl.BlockSpec((2 * TM, HEAD_DIM_RET), tab),
                  pl.BlockSpec((2 * TM, HEAD_DIM_RET), tab)],
        out_specs=[pl.BlockSpec((2 * TM, GROUP_W), row), pl.BlockSpec((GROUP_W, 2 * TM), lambda p: (0, p))]
        + [pl.BlockSpec((2 * TM, GROUP_W), row)] * 5 + [
            tail_spec, tail_spec,
            pl.BlockSpec((TM, GROUP_W), full),
            pl.BlockSpec((TM, GROUP_W), full)],
        out_shape=[act, jax.ShapeDtypeStruct((GROUP_W, r), BF16)] + [act] * 5
        + [jax.ShapeDtypeStruct((nb * TM, GROUP_W), F32)] * 2
        + [jax.ShapeDtypeStruct((TM, GROUP_W), F32)] * 2,
        compiler_params=_cparams(("arbitrary",), VMEM_LIMIT),
        name="inproj",
    )(xp, xs, shift, scale, g_norm, w_in_b, bd, gq8, gk8, cos_t, sin_t)


def _attn_heads(q, k, v, bias_ref, first_valid_col=None):
    qb_rows, kb_rows = q.shape[0], v.shape[0]
    assert qb_rows == 4 * CHUNK
    half_rows, span = qb_rows // 2, kb_rows - 2 * CHUNK
    parts = [(0, 0), (half_rows, 2 * CHUNK)]

    def softmax_part(s_full, hh, half, r0, c0):
        rs = half * qb_rows + r0
        s = s_full[rs:rs + half_rows, c0:c0 + span] + bias_ref[hh, r0:r0 + half_rows, c0:c0 + span]
        if first_valid_col is not None:
            col = lax.broadcasted_iota(I32, (half_rows, span), 1) + c0
            s = jnp.where(col >= first_valid_col, s, NEG_INF)
        m = jnp.max(s, axis=-1, keepdims=True)
        e = jnp.exp2(s - m)
        l = jnp.sum(e, axis=-1, keepdims=True)
        pad = [jnp.zeros((half_rows, c0), BF16)] if c0 else []
        pad_r = [jnp.zeros((half_rows, kb_rows - span - c0), BF16)] if kb_rows - span - c0 else []
        return jnp.concatenate(pad + [e.astype(BF16)] + pad_r, axis=1), l

    pair_w = 2 * HEAD_DIM_ATT
    low = lax.broadcasted_iota(I32, (1, pair_w), 1) < HEAD_DIM_ATT
    outs = []
    for pp in range(N_HEADS_ATT // 2):
        ps = slice(pp * pair_w, (pp + 1) * pair_w)
        q2, v2 = q[:, ps], v[:, ps]
        zero = jnp.zeros_like(q2)
        qs = jnp.concatenate([jnp.where(low, q2, zero), jnp.where(low, zero, q2)], axis=0)
        s = jnp.dot(qs, k[ps, :], preferred_element_type=F32)
        es, ls = zip(*[softmax_part(s, 2 * pp + half, half, r0, c0)
                       for half in range(2) for r0, c0 in parts])
        o = jnp.dot(jnp.concatenate(es, axis=0), v2, preferred_element_type=F32) / jnp.concatenate(ls, axis=0)
        outs.append(jnp.where(low, o[:qb_rows], o[qb_rows:]))
    return jnp.concatenate(outs, axis=1)


def _fill_band_bias(rev_ref, bias_scr):
    _, qb_rows, kb_rows = bias_scr.shape
    width = rev_ref.shape[1]
    q = lax.broadcasted_iota(I32, (qb_rows, kb_rows), 0)
    k = lax.broadcasted_iota(I32, (qb_rows, kb_rows), 1)
    qc = q >> 6
    kc = (k - ATT_WINDOW) >> 6
    band = (kc >= qc - ATT_WINDOW // CHUNK) & (kc <= qc)
    for hh in range(N_HEADS_ATT):
        rows = jnp.broadcast_to(rev_ref[hh:hh + 1, :], (qb_rows, width))
        toep = pltpu.roll(rows, width - MAX_REL, 1, stride=1, stride_axis=0)
        bias_scr[hh] = jnp.where(band, toep[:, :kb_rows] * LOG2_E, NEG_INF)


def _attn_sample_kernel(q_ref, kn_ref, vn_ref, kc_ref, vc_ref, rev_ref, o_ref, bias_scr):
    @pl.when(pl.program_id(0) == 0)
    def _():
        _fill_band_bias(rev_ref, bias_scr)

    q = q_ref[...]
    outs = []
    for hh in range(N_HEADS_ATT):
        hs = slice(hh * HEAD_DIM_ATT, (hh + 1) * HEAD_DIM_ATT)
        k = jnp.concatenate([kc_ref[0, :, hs], kn_ref[:, hs].astype(BF16)], axis=0)
        v = jnp.concatenate([vc_ref[0, :, hs], vn_ref[:, hs].astype(BF16)], axis=0)
        s = lax.dot_general(q[:, hs], k, (((1,), (1,)), ((), ())), preferred_element_type=F32)
        s = s + bias_scr[hh]
        m = jnp.max(s, axis=-1, keepdims=True)
        e = jnp.exp2(s - m)
        l = jnp.sum(e, axis=-1, keepdims=True)
        outs.append(jnp.dot(e.astype(BF16), v, preferred_element_type=F32) / l)
    o_ref[...] = jnp.concatenate(outs, axis=1).astype(BF16)


def _attn_sample(qa, ks_new, vs_new, kc, vc, rev, rp):
    ndb = kc.shape[0]
    base = rp // CHUNK
    spec = pl.BlockSpec((CHUNK, GROUP_W), lambda b: (base + b, 0))
    new = pl.BlockSpec((CHUNK, GROUP_W), lambda b: (b, 0))
    cspec = pl.BlockSpec((1, ATT_WINDOW, GROUP_W), lambda b: (b, 0, 0))
    return pl.pallas_call(
        _attn_sample_kernel,
        grid=(ndb,),
        in_specs=[spec, new, new, cspec, cspec,
                  pl.BlockSpec(rev.shape, lambda b: (0, 0))],
        out_specs=pl.BlockSpec((CHUNK, GROUP_W), lambda b: (b, 0)),
        out_shape=jax.ShapeDtypeStruct((ndb * CHUNK, GROUP_W), BF16),
        scratch_shapes=[pltpu.VMEM((N_HEADS_ATT, CHUNK, ATT_WINDOW + CHUNK), F32)],
        compiler_params=_cparams(("arbitrary",), VMEM_LIMIT),
        name="attn_sample",
    )(qa, ks_new, vs_new, kc, vc, rev)


def _ret_chunk(state_decay, q_ref, k_ref, v_ref, g_ref, dm_ref, xi_ref, zeta_ref, gro_ref, o_ref, s_scr):
    outs = []
    for hh in range(N_HEADS_RET):
        hs = slice(hh * HEAD_DIM_RET, (hh + 1) * HEAD_DIM_RET)
        q = q_ref[:, hs]
        k = k_ref[:, hs]
        v = v_ref[:, hs]
        st = s_scr[hh]
        sc = lax.dot_general(q, k, (((1,), (1,)), ((), ())), preferred_element_type=F32) * dm_ref[hh]
        inner = jnp.dot(sc.astype(BF16), v, preferred_element_type=F32)
        cross = jnp.dot(q, st.astype(BF16), preferred_element_type=F32) * xi_ref[:, hs]
        o = inner + cross
        kz = k.astype(F32) * zeta_ref[:, hs]
        s_scr[hh] = state_decay[hh] * st + jnp.dot(kz.T.astype(BF16), v, preferred_element_type=F32)
        mu = jnp.mean(o, axis=-1, keepdims=True)
        oc = o - mu
        var = jnp.mean(oc * oc, axis=-1, keepdims=True)
        outs.append(oc * lax.rsqrt(var + NORM_EPS))
    y = jnp.concatenate(outs, axis=1) * gro_ref[...]
    g = g_ref[...].astype(F32)
    o_ref[...] = (g * jax.nn.sigmoid(g) * y).astype(BF16)


def _ret_kernel(state_decay, q_ref, k_ref, v_ref, g_ref, s0_ref, dm_ref, xi_ref, zeta_ref,
                gro_ref, o_ref, sn_ref, s_scr):
    j = pl.program_id(1)

    @pl.when(j == 0)
    def _():
        s_scr[...] = s0_ref[0]

    _ret_chunk(state_decay, q_ref, k_ref, v_ref, g_ref, dm_ref, xi_ref, zeta_ref, gro_ref, o_ref, s_scr)

    @pl.when(j == pl.num_programs(1) - 1)
    def _():
        sn_ref[0] = s_scr[...]


def _mix_prompt_kernel(state_decay, q_ref, k0_ref, k1_ref, k2_ref, v0_ref, v1_ref, v2_ref, rev_ref,
                       rq_ref, rk_ref, rv_ref, rg_ref, s0_ref, dm_ref, xi_ref, zeta_ref, gro_ref,
                       att_ref, ret_ref, sn_ref, bias_scr, s_scr):
    j = pl.program_id(1)

    @pl.when((pl.program_id(0) == 0) & (j == 0))
    def _():
        _fill_band_bias(rev_ref, bias_scr)

    @pl.when(j == 0)
    def _():
        s_scr[...] = s0_ref[0]

    k = jnp.concatenate([k0_ref[...], k1_ref[...], k2_ref[...]], axis=1)
    v = jnp.concatenate([v0_ref[...], v1_ref[...], v2_ref[...]], axis=0)

    def block(first_valid_col):
        att_ref[...] = _attn_heads(q_ref[...], k, v, bias_scr, first_valid_col).astype(BF16)
        _ret_chunk(state_decay, rq_ref, rk_ref, rv_ref, rg_ref, dm_ref, xi_ref, zeta_ref, gro_ref,
                   ret_ref, s_scr)

    @pl.when(j >= 2)
    def _():
        block(None)

    @pl.when(j < 2)
    def _():
        block((2 - j) * ATT_QB)

    @pl.when(j == pl.num_programs(1) - 1)
    def _():
        sn_ref[0] = s_scr[...]


def _mix_prompt(qa, ka_t, va, rev, qb, kb, vb, gb, s0, g_ro, nb, seq):
    assert ATT_QB == RET_CB
    r = nb * seq
    nq = seq // ATT_QB
    dm, xi, zeta, state_decay = _ret_consts(RET_CB)
    blk = lambda back: (lambda b, j: (b * nq + jnp.maximum(j - back, 0), 0))
    spec = lambda back: pl.BlockSpec((ATT_QB, GROUP_W), blk(back))
    tspec = lambda back: pl.BlockSpec((GROUP_W, ATT_QB), lambda b, j: (0, b * nq + jnp.maximum(j - back, 0)))
    sspec = pl.BlockSpec((1, N_HEADS_RET, HEAD_DIM_RET, HEAD_DIM_RET), lambda b, j: (b, 0, 0, 0))
    full2 = lambda b, j: (0, 0)
    out = jax.ShapeDtypeStruct((r, GROUP_W), BF16)
    return pl.pallas_call(
        functools.partial(_mix_prompt_kernel, state_decay),
        grid=(nb, nq),
        in_specs=[spec(0), tspec(2), tspec(1), tspec(0), spec(2), spec(1), spec(0),
                  pl.BlockSpec(rev.shape, full2),
                  spec(0), spec(0), spec(0), spec(0), sspec,
                  pl.BlockSpec(dm.shape, lambda b, j: (0, 0, 0)),
                  pl.BlockSpec(xi.shape, full2), pl.BlockSpec(zeta.shape, full2),
                  pl.BlockSpec((1, GROUP_W), full2)],
        out_specs=[spec(0), spec(0), sspec],
        out_shape=[out, out, jax.ShapeDtypeStruct(s0.shape, F32)],
        scratch_shapes=[pltpu.VMEM((N_HEADS_ATT, ATT_QB, ATT_WINDOW + ATT_QB), F32),
                        pltpu.VMEM((N_HEADS_RET, HEAD_DIM_RET, HEAD_DIM_RET), F32)],
        compiler_params=_cparams(("arbitrary", "arbitrary"), VMEM_LIMIT),
        name="mix_prompt",
    )(qa, ka_t, ka_t, ka_t, va, va, va, rev, qb, kb, vb, gb, s0, dm, xi, zeta, g_ro)


def _ret_consts(cb):
    log_g = np.log1p(-np.exp2(-RET_DECAY_OFFSET - np.arange(N_HEADS_RET, dtype=np.float64)))
    n = np.arange(cb, dtype=np.float64)
    diff = n[:, None] - n[None, :]
    dm = np.where(diff[None] >= 0, np.exp(np.maximum(diff, 0.0)[None] * log_g[:, None, None]), 0.0)
    xi = np.exp((n + 1.0)[:, None] * log_g[None, :])
    zeta = np.exp((cb - 1.0 - n)[:, None] * log_g[None, :])
    rep = lambda a: np.repeat(a, HEAD_DIM_RET, axis=1)
    state_decay = tuple(float(v) for v in np.exp(cb * log_g))
    return (jnp.asarray(dm, F32), jnp.asarray(rep(xi), F32), jnp.asarray(rep(zeta), F32), state_decay)


def _ret(qb, kb, vb, gb, s0, g_ro, cb, row0, nb, nc, name):
    dm, xi, zeta, state_decay = _ret_consts(cb)
    base = row0 // cb
    spec = pl.BlockSpec((cb, GROUP_W), lambda b, j: (base + b * nc + j, 0))
    sspec = pl.BlockSpec((1, N_HEADS_RET, HEAD_DIM_RET, HEAD_DIM_RET), lambda b, j: (b, 0, 0, 0))
    full2 = lambda b, j: (0, 0)
    return pl.pallas_call(
        functools.partial(_ret_kernel, state_decay),
        grid=(nb, nc),
        in_specs=[spec, spec, spec, spec, sspec,
                  pl.BlockSpec(dm.shape, lambda b, j: (0, 0, 0)),
                  pl.BlockSpec(xi.shape, full2), pl.BlockSpec(zeta.shape, full2),
                  pl.BlockSpec((1, GROUP_W), full2)],
        out_specs=[pl.BlockSpec((cb, GROUP_W), lambda b, j: (b * nc + j, 0)), sspec],
        out_shape=[jax.ShapeDtypeStruct((nb * nc * cb, GROUP_W), BF16),
                   jax.ShapeDtypeStruct(s0.shape, F32)],
        scratch_shapes=[pltpu.VMEM((N_HEADS_RET, HEAD_DIM_RET, HEAD_DIM_RET), F32)],
        compiler_params=_cparams(("arbitrary", "arbitrary"), VMEM_LIMIT),
        name=name,
    )(qb, kb, vb, gb, s0, dm, xi, zeta, g_ro)


def _outproj_kernel(npp, attp_ref, atts_ref, retp_ref, rets_ref, xp_hbm, xs_ref, gm_ref, shf_ref, scf_ref,
                    gn_ref, wo_ref, wr_ref, br_ref, upper_ref, lower_ref,
                    x1_ref, h2_ref, slot_ref, cols_ref, cnt_ref, x_ring, x_sems):
    p = pl.program_id(0)
    is_p = p < npp
    subs = range(2)
    rows = [slice(sub * TM, (sub + 1) * TM) for sub in subs]

    def x_copy(step):
        slot = step % X_RING
        src = xp_hbm.at[pl.ds(pl.multiple_of(step * 2 * TM, 2 * TM), 2 * TM), :]
        return pltpu.make_async_copy(src, x_ring.at[slot], x_sems.at[slot])

    @pl.when(p == 0)
    def _():
        for step in range(min(X_RING - 1, npp)):
            x_copy(step).start()

    @pl.when(p + X_RING - 1 < npp)
    def _():
        x_copy(p + X_RING - 1).start()

    @pl.when(is_p)
    def _():
        x_copy(p).wait()

    xp_ref = x_ring.at[p % X_RING]

    def pick(p_ref, s_ref, sub):
        return jnp.where(is_p, p_ref[rows[sub], :], s_ref[...])

    mix = [jnp.dot(pick(attp_ref, atts_ref, sub), wo_ref[:GROUP_W, :], preferred_element_type=F32)
           + jnp.dot(pick(retp_ref, rets_ref, sub), wo_ref[GROUP_W:, :], preferred_element_type=F32)
           for sub in subs]
    h2b = []
    for sub in subs:
        x1 = _per_group(mix[sub], lambda a, gm: a * gm, gm_ref[...]) + pick(xp_ref, xs_ref, sub)
        x1_ref[rows[sub], :] = x1
        y = _rms_rows(x1, gn_ref[...])
        h2 = _per_group(y, lambda a, sh, sc: a * (1.0 + sc) + sh, shf_ref[...], scf_ref[...])
        h2b.append(h2.astype(BF16))
        h2_ref[rows[sub], :] = h2b[sub]

    work = [lax.dot_general(wr_ref[...], h2b[sub], (((1,), (1,)), ((), ())),
                            preferred_element_type=F32) + br_ref[...] for sub in subs]
    eidx = lax.broadcasted_iota(I32, work[0].shape, 0).astype(F32)
    sel = [[] for _ in subs]
    top = [[] for _ in subs]
    for _ in range(TOP_K):
        for sub in subs:
            m = jnp.max(work[sub], axis=0, keepdims=True)
            idx = jnp.min(jnp.where(work[sub] == m, eidx, float(N_EXPERTS)), axis=0, keepdims=True)
            hit = eidx == idx
            sel[sub].append(hit)
            top[sub].append(m)
            work[sub] = jnp.where(hit, -jnp.inf, work[sub])

    for sub in subs:
        ex = [jnp.exp(t - top[sub][0]) for t in top[sub]]
        den = ex[0] + ex[1] + ex[2] + ex[3]
        gates = [e / den for e in ex]
        for part in range(TM // MT):
            lanes = slice(part * MT, (part + 1) * MT)
            tile = (TM // MT) * sub + part
            hits = [h[:, lanes] for h in sel[sub]]
            multi_f = jnp.where(hits[0] | hits[1] | hits[2] | hits[3], 1.0, 0.0)
            rank = jnp.dot(multi_f.astype(BF16), upper_ref[...], preferred_element_type=F32)
            cnt = jnp.sum(multi_f, axis=1, keepdims=True)
            cnt_pad = jnp.maximum(jnp.floor((cnt + (SEG_ALIGN - 1.0)) * (1.0 / SEG_ALIGN)), 1.0) * SEG_ALIGN
            cnt_pad_b = jnp.broadcast_to(cnt_pad, (N_EXPERTS, 128))
            seg_off = jnp.dot(lower_ref[...], cnt_pad_b.astype(BF16), preferred_element_type=F32)[:, :1]
            pos = seg_off + rank
            slot_rows = jnp.concatenate(
                [jnp.sum(jnp.where(h, pos, 0.0), axis=0, keepdims=True) for h in hits], axis=0)
            gate_rows = jnp.concatenate([g[:, lanes] for g in gates], axis=0)
            slot_ref[tile] = slot_rows.astype(I32)
            cnt_ref[tile] = cnt_pad_b.astype(I32)
            both = jnp.concatenate([slot_rows, gate_rows, jnp.zeros((128 - 2 * TOP_K, MT), F32)], axis=0)
            cols_ref[tile] = both.T


def _outproj(att_p, att_s, ret_p, ret_s, xp, xs, gate_m, shift_f, scale_f, g_norm, w_out_b, wr_t, br,
             upper, lower, nb, tps):
    rp = xp.shape[0]
    ntp = rp // TM
    assert ntp % 2 == 0 and tps % 2 == 0
    npp = ntp // 2
    r = (ntp + 2) * TM
    per_step = 2 * TM // MT
    ntm = (npp + 1) * per_step
    row = lambda p: (p, 0)
    row3 = lambda p: (p, 0, 0)
    full = lambda p: (0, 0)
    prow = lambda p: (jnp.minimum(p, npp - 1), 0)
    mod = pl.BlockSpec((GROUPS_PER_TILE, D_MODEL), _mod_row(npp, tps // 2, nb))
    return pl.pallas_call(
        functools.partial(_outproj_kernel, npp),
        grid=(npp + 1,),
        in_specs=[pl.BlockSpec((2 * TM, GROUP_W), prow), pl.BlockSpec((TM, GROUP_W), full),
                  pl.BlockSpec((2 * TM, GROUP_W), prow), pl.BlockSpec((TM, GROUP_W), full),
                  pl.BlockSpec(memory_space=pl.ANY),
                  pl.BlockSpec((TM, D_MODEL), full),
                  mod, mod, mod,
                  pl.BlockSpec((1, D_MODEL), full),
                  pl.BlockSpec((D_MODEL, D_MODEL), full),
                  pl.BlockSpec((N_EXPERTS, D_MODEL), full),
                  pl.BlockSpec((N_EXPERTS, 1), full),
                  pl.BlockSpec((MT, MT), full),
                  pl.BlockSpec((N_EXPERTS, N_EXPERTS), full)],
        out_specs=[pl.BlockSpec((2 * TM, D_MODEL), row), pl.BlockSpec((2 * TM, D_MODEL), row),
                   pl.BlockSpec((per_step, TOP_K, MT), row3),
                   pl.BlockSpec((per_step, MT, 128), row3), pl.BlockSpec((per_step, N_EXPERTS, 128), row3)],
        out_shape=[jax.ShapeDtypeStruct((r, D_MODEL), F32), jax.ShapeDtypeStruct((r, D_MODEL), BF16),
                   jax.ShapeDtypeStruct((ntm, TOP_K, MT), I32),
                   jax.ShapeDtypeStruct((ntm, MT, 128), F32), jax.ShapeDtypeStruct((ntm, N_EXPERTS, 128), I32)],
        scratch_shapes=[pltpu.VMEM((X_RING, 2 * TM, D_MODEL), F32), pltpu.SemaphoreType.DMA((X_RING,))],
        compiler_params=_cparams(("arbitrary",), VMEM_LIMIT),
        name="outproj",
    )(att_p, att_s, ret_p, ret_s, xp, xs, gate_m, shift_f, scale_f, g_norm, w_out_b, wr_t, br, upper, lower)


def _store_packed(ref, lead, row0, x):
    n = x.shape[0]
    for j in range(PACK_ROWS):
        c = 2 * LANES * j
        lo = lax.bitcast_convert_type(x[:, c:c + LANES].astype(BF16).astype(F32), U32) >> 16
        hi = lax.bitcast_convert_type(x[:, c + LANES:c + 2 * LANES].astype(BF16).astype(F32), U32)
        ref[lead + (pl.ds(PACK_ROWS * row0 + j, n, stride=PACK_ROWS), slice(None))] = hi | lo


def _load_packed(ref, lead, row0, n):
    parts = []
    for j in range(PACK_ROWS):
        u = ref[lead + (pl.ds(PACK_ROWS * row0 + j, n, stride=PACK_ROWS), slice(None))]
        parts.append(lax.bitcast_convert_type(u << 16, F32))
        parts.append(lax.bitcast_convert_type(u & jnp.uint32(0xFFFF0000), F32))
    return jnp.concatenate(parts, axis=1).astype(BF16)


def _packed_rows(ref, lead, tok0, ntok):
    rows = pl.ds(pl.multiple_of(tok0 * PACK_ROWS, SUBLANES_32), ntok * PACK_ROWS)
    return ref.at[lead + (rows, slice(None))]


def _rows_copy(n, src_rows, dst_rows, sem):
    size = pl.multiple_of(n, SEG_ALIGN)
    return pltpu.make_async_copy(src_rows(size), dst_rows(size), sem)


def _start_segments(t, cnt_ref, off_ref, base_ref, local_rows, sorted_rows, sem, to_sorted):
    for e in range(N_EXPERTS):
        n = cnt_ref[t * N_EXPERTS + e]
        off = pl.multiple_of(off_ref[t * N_EXPERTS + e], SEG_ALIGN)
        base = pl.multiple_of(base_ref[t * N_EXPERTS + e], SEG_ALIGN)
        local = lambda z, off=off: local_rows(off, z)
        remote = lambda z, base=base: sorted_rows(base, z)
        (_rows_copy(n, local, remote, sem) if to_sorted else _rows_copy(n, remote, local, sem)).start()


def _dispatch_kernel(nt, n_blocks, off_ref, cnt_ref, base_ref, tot_ref, tail0_ref, tailn_ref, na_ref,
                     h2_ref, slot_ref, slotn_ref, xb_ref, xs_scr, hot_scr, zero_scr, sems, tail_sem):
    i = pl.program_id(0)
    cur = i % 2
    sorted_rows = lambda r, z: _packed_rows(xb_ref, (), r, z)

    def start_tile(t, buf):
        for part in range(PARTS):
            _start_segments(PARTS * t + part, cnt_ref, off_ref, base_ref,
                            lambda r, z, part=part: _packed_rows(xs_scr, (buf, part), r, z),
                            sorted_rows, sems.at[buf], True)

    def wait_tile(t, buf):
        for part in range(PARTS):
            _rows_copy(tot_ref[PARTS * t + part], lambda z, part=part: _packed_rows(xs_scr, (buf, part), 0, z),
                       lambda z: sorted_rows(0, z), sems.at[buf]).wait()

    def tail_copies(wait):
        def body(e, c):
            base = pl.multiple_of(tail0_ref[e], SEG_ALIGN)

            @pl.when(tailn_ref[e] > 0)
            def _():
                cp = _rows_copy(tailn_ref[e], lambda z: _packed_rows(zero_scr, (), 0, z),
                                lambda z: sorted_rows(base, z), tail_sem)
                cp.wait() if wait else cp.start()
            return c
        lax.fori_loop(0, N_EXPERTS, body, 0)

        def unused(j, c):
            cp = pltpu.make_async_copy(zero_scr, sorted_rows(pl.multiple_of(j * BM, BM), BM), tail_sem)
            cp.wait() if wait else cp.start()
            return c
        lax.fori_loop(na_ref[0], n_blocks, unused, 0)

    @pl.when(i >= 2)
    def _():
        wait_tile(i - 2, cur)

    def onehot(slot):
        srow = lax.broadcasted_iota(I32, (CAP_USED, MT), 0)
        hit = (srow == slot[0:1]) | (srow == slot[1:2]) | (srow == slot[2:3]) | (srow == slot[3:4])
        return jnp.where(hit, 1.0, 0.0).astype(BF16)

    @pl.when(i == 0)
    def _():
        for part in range(PARTS):
            hot_scr[0, part] = onehot(slot_ref[part])
        zero_scr[...] = jnp.zeros_like(zero_scr)
        tail_copies(False)

    for part in range(PARTS):
        rows = h2_ref[part * MT:(part + 1) * MT, :]
        _store_packed(xs_scr, (cur, part), 0, jnp.dot(hot_scr[cur, part], rows, preferred_element_type=F32))
    for part in range(PARTS):
        hot_scr[1 - cur, part] = onehot(slotn_ref[part])
    start_tile(i, cur)

    @pl.when(i == nt - 1)
    def _():
        if nt >= 2:
            wait_tile(i - 1, 1 - cur)
        wait_tile(i, cur)
        tail_copies(True)


def _dispatch(h2, slot, off, cnt, base, tot, tail0, tailn, n_act, n_blocks):
    nt = tot.shape[0] // PARTS
    n_rows = n_blocks * BM
    grid_spec = pltpu.PrefetchScalarGridSpec(
        num_scalar_prefetch=7,
        grid=(nt,),
        in_specs=[pl.BlockSpec((TM, D_MODEL), lambda i, *_: (i, 0)),
                  pl.BlockSpec((PARTS, TOP_K, MT), lambda i, *_: (i, 0, 0)),
                  pl.BlockSpec((PARTS, TOP_K, MT), lambda i, *_: (jnp.minimum(i + 1, nt - 1), 0, 0))],
        out_specs=pl.BlockSpec(memory_space=pl.ANY),
        scratch_shapes=[pltpu.VMEM((2, PARTS, CAP_USED * PACK_ROWS, LANES), U32),
                        pltpu.VMEM((2, PARTS, CAP_USED, MT), BF16),
                        pltpu.VMEM((BM * PACK_ROWS, LANES), U32),
                        pltpu.SemaphoreType.DMA((2,)),
                        pltpu.SemaphoreType.DMA(())],
    )
    return pl.pallas_call(
        functools.partial(_dispatch_kernel, nt, n_blocks),
        grid_spec=grid_spec,
        out_shape=jax.ShapeDtypeStruct((n_rows * PACK_ROWS, LANES), U32),
        compiler_params=_cparams(("arbitrary",), VMEM_LIMIT),
        name="dispatch",
    )(off, cnt, base, tot, tail0, tailn, n_act, h2, slot, slot)


def _experts_kernel(be_ref, bi_ref, nx_ref, nq_ref, na_ref, x_ref, wu_hbm, bu_ref, wd_hbm, bd_ref, y_ref,
                    wu_stage, wd_stage, wu_scr, wd_scr, sems):
    j = pl.program_id(0)

    def weight_copies(e):
        return (pltpu.make_async_copy(wu_hbm.at[e], wu_stage, sems.at[0]),
                pltpu.make_async_copy(wd_hbm.at[e], wd_stage, sems.at[1]))

    @pl.when(j < na_ref[0])
    def _():
        e = be_ref[j]
        prev = be_ref[jnp.maximum(j - 1, 0)]

        @pl.when(j == 0)
        def _():
            for cp in weight_copies(e):
                cp.start()

        @pl.when((j == 0) | (e != prev))
        def _():
            for cp in weight_copies(e):
                cp.wait()
            wu_scr[...] = wu_stage[...].astype(BF16)
            wd_scr[...] = wd_stage[...].astype(BF16)

            @pl.when(nx_ref[j] != e)
            def _():
                for cp in weight_copies(nx_ref[j]):
                    cp.start()

        def ffn(n):
            x = _load_packed(x_ref, (), 0, n)
            u = jnp.dot(x, wu_scr[...], preferred_element_type=F32) + bu_ref[0]
            glu = jnp.minimum(u[:, :D_FF], SWIGLU_LIMIT)
            lin = jnp.clip(u[:, D_FF:], -SWIGLU_LIMIT, SWIGLU_LIMIT)
            act = glu * jax.nn.sigmoid(SWIGLU_ALPHA * glu) * (lin + 1.0)
            y = jnp.dot(act.astype(BF16), wd_scr[...], preferred_element_type=F32) + bd_ref[0]
            _store_packed(y_ref, (), 0, y)

        for quarters in range(1, BLOCK_QUARTERS + 1):
            @pl.when(nq_ref[j] == quarters)
            def _(quarters=quarters):
                n = quarters * (BM // BLOCK_QUARTERS)
                ffn(n)
                if n < BM:
                    y_ref[PACK_ROWS * n:, :] = jnp.zeros((PACK_ROWS * (BM - n), LANES), U32)


def _experts(xb, blk_e, blk_i, blk_nx, blk_nq, n_act, w_up, b_up, w_down, b_down):
    n_rows = xb.shape[0] // PACK_ROWS
    nblk = n_rows // BM
    grid_spec = pltpu.PrefetchScalarGridSpec(
        num_scalar_prefetch=5,
        grid=(nblk,),
        in_specs=[pl.BlockSpec((BM * PACK_ROWS, LANES), lambda j, be, bi, *_: (bi[j], 0)),
                  pl.BlockSpec(memory_space=pl.ANY),
                  pl.BlockSpec((1, 1, 2 * D_FF), lambda j, be, *_: (be[j], 0, 0)),
                  pl.BlockSpec(memory_space=pl.ANY),
                  pl.BlockSpec((1, 1, D_MODEL), lambda j, be, *_: (be[j], 0, 0))],
        out_specs=pl.BlockSpec((BM * PACK_ROWS, LANES), lambda j, be, bi, *_: (bi[j], 0)),
        scratch_shapes=[pltpu.VMEM((D_MODEL, 2 * D_FF), F32), pltpu.VMEM((D_FF, D_MODEL), F32),
                        pltpu.VMEM((D_MODEL, 2 * D_FF), BF16), pltpu.VMEM((D_FF, D_MODEL), BF16),
                        pltpu.SemaphoreType.DMA((2,))],
    )
    return pl.pallas_call(
        _experts_kernel,
        grid_spec=grid_spec,
        out_shape=jax.ShapeDtypeStruct(xb.shape, xb.dtype),
        input_output_aliases={5: 0},
        compiler_params=_cparams(("arbitrary",), VMEM_LIMIT),
        name="experts",
    )(blk_e, blk_i, blk_nx, blk_nq, n_act, xb, w_up, b_up.reshape(N_EXPERTS, 1, 2 * D_FF), w_down,
      b_down.reshape(N_EXPERTS, 1, D_MODEL))


def _combine_kernel(nt, ntp, off_ref, cnt_ref, base_ref, tot_ref, yb_ref, cols_ref, x1_ref, gf_ref,
                    op_ref, os_ref, ys_scr, sems):
    i = pl.program_id(0)
    cur = i % 2

    sorted_rows = lambda r, z: _packed_rows(yb_ref, (), r, z)

    def start_tile(t, buf):
        for part in range(PARTS):
            _start_segments(PARTS * t + part, cnt_ref, off_ref, base_ref,
                            lambda r, z, part=part: _packed_rows(ys_scr, (buf, part), r, z),
                            sorted_rows, sems.at[buf], False)

    @pl.when(i == 0)
    def _():
        ys_scr[...] = jnp.zeros_like(ys_scr)
        start_tile(0, 0)

    def wait_tile(t, buf):
        for part in range(PARTS):
            _rows_copy(tot_ref[PARTS * t + part], lambda z: sorted_rows(0, z),
                       lambda z, part=part: _packed_rows(ys_scr, (buf, part), 0, z), sems.at[buf]).wait()

    nxt = jnp.minimum(i + 1, nt - 1)
    wait_tile(i, cur)
    start_tile(nxt, 1 - cur)

    @pl.when(i == nt - 1)
    def _():
        wait_tile(nxt, 1 - cur)

    lane = lax.broadcasted_iota(I32, (MT, CAP), 1)
    ys = []
    for part in range(PARTS):
        cols = cols_ref[part]
        w = jnp.zeros((MT, CAP), F32)
        for k in range(TOP_K):
            sk = cols[:, k:k + 1].astype(I32)
            gk = cols[:, TOP_K + k:TOP_K + k + 1]
            w = jnp.where(lane == sk, gk, w)
        ys.append(jnp.dot(w.astype(BF16), _load_packed(ys_scr, (cur, part), 0, CAP),
                          preferred_element_type=F32))
    out = x1_ref[...] + _per_group(jnp.concatenate(ys, axis=0), lambda a, g: a * g, gf_ref[...])

    @pl.when(i < ntp)
    def _():
        op_ref[...] = out

    @pl.when(i >= ntp)
    def _():
        os_ref[...] = out


def _combine(yb, cols, x1, gate_f, off, cnt, base, tot, ntp, nb, tps):
    nt = tot.shape[0] // PARTS
    grid_spec = pltpu.PrefetchScalarGridSpec(
        num_scalar_prefetch=4,
        grid=(nt,),
        in_specs=[pl.BlockSpec(memory_space=pl.ANY),
                  pl.BlockSpec((PARTS, MT, 128), lambda i, *_: (i, 0, 0)),
                  pl.BlockSpec((TM, D_MODEL), lambda i, *_: (i, 0)),
                  pl.BlockSpec((GROUPS_PER_TILE, D_MODEL), _mod_row(ntp, tps, nb))],
        out_specs=[pl.BlockSpec((TM, D_MODEL), lambda i, *_: (jnp.minimum(i, ntp - 1), 0)),
                   pl.BlockSpec((TM, D_MODEL), lambda i, *_: (0, 0))],
        scratch_shapes=[pltpu.VMEM((2, PARTS, CAP * PACK_ROWS, LANES), U32),
                        pltpu.SemaphoreType.DMA((2,))],
    )
    return pl.pallas_call(
        functools.partial(_combine_kernel, nt, ntp),
        grid_spec=grid_spec,
        out_shape=[jax.ShapeDtypeStruct((ntp * TM, D_MODEL), F32),
                   jax.ShapeDtypeStruct((TM, D_MODEL), F32)],
        compiler_params=_cparams(("arbitrary",), VMEM_LIMIT),
        name="combine",
    )(off, cnt, base, tot, yb, cols, x1, gate_f)


def _rotary_tables(seq, dec_batch, dec_seq):
    half = HEAD_DIM_RET // 2
    inv = ROPE_BASE ** (-np.arange(half, dtype=np.float64) / half)
    pos = np.concatenate([np.arange(seq), np.tile(PAST_LEN + np.arange(dec_seq), 2 * dec_batch)])
    ang = pos.astype(np.float64)[:, None] * inv[None, :]
    cos = np.concatenate([np.cos(ang), np.cos(ang)], axis=1)
    sin = np.concatenate([-np.sin(ang), np.sin(ang)], axis=1)
    return jnp.asarray(cos, F32), jnp.asarray(sin, F32)


def _rel_bias_reversed(rel_bias):
    heads = rel_bias.shape[0]
    ext = jnp.concatenate([rel_bias[:, 1:], jnp.broadcast_to(rel_bias[:, -1:], (heads, 2 * MAX_REL))], axis=1)
    return ext[:, ::-1].astype(F32)


def _group_mods(m, nb, ndb):
    assert ndb == GROUPS_PER_TILE
    mp = jnp.broadcast_to(m[:nb, None], (nb, GROUPS_PER_TILE) + m.shape[1:])
    allm = jnp.concatenate([mp.reshape((nb * GROUPS_PER_TILE,) + m.shape[1:]), m[nb:]], axis=0)
    return jnp.transpose(allm, (1, 0, 2))


def _routing_tables(cnt, n_blocks):
    nt = cnt.shape[0]
    off = jnp.cumsum(cnt, axis=1) - cnt
    rows_e = jnp.sum(cnt, axis=0)
    nblk_e = (rows_e + BM - 1) // BM
    blk_end = jnp.cumsum(nblk_e)
    start_e = (blk_end - nblk_e) * BM
    base = start_e[None, :] + jnp.cumsum(cnt, axis=0) - cnt
    n_act = blk_end[-1]
    j = jnp.minimum(jnp.arange(n_blocks), n_act - 1)
    blk_e = jnp.minimum(jnp.sum(blk_end[None, :] <= j[:, None], axis=1), N_EXPERTS - 1)
    later = jnp.where(blk_e[None, :] > blk_e[:, None], blk_e[None, :], N_EXPERTS)
    blk_nx = jnp.min(later, axis=1)
    blk_nx = jnp.where(blk_nx == N_EXPERTS, blk_e, blk_nx)
    tail0 = start_e + rows_e
    tailn = nblk_e * BM - rows_e
    mine = blk_e[:, None] == jnp.arange(N_EXPERTS)[None, :]
    blk_rows = jnp.sum(jnp.where(mine, (rows_e + start_e)[None, :], 0), axis=1) - j * BM
    quarter = BM // BLOCK_QUARTERS
    blk_nq = jnp.clip((blk_rows + quarter - 1) // quarter, 1, BLOCK_QUARTERS)
    i32 = lambda a: a.astype(I32)
    return (i32(off.reshape(nt * N_EXPERTS)), i32(cnt.reshape(nt * N_EXPERTS)),
            i32(base.reshape(nt * N_EXPERTS)), i32(jnp.sum(cnt, axis=1)), i32(tail0), i32(tailn),
            i32(blk_e), i32(j), i32(blk_nx), i32(blk_nq), i32(n_act.reshape(1)))


def kernel(x_prompt, x_sample, c_prompt, c_sample, cache_att_k, cache_att_v, state_ret, w_ada, b_ada,
           g_norm_mix, g_norm_ffn, w_in, g_q, g_k, rel_bias, g_ret_out, w_out, w_router, b_router,
           w_up, b_up, w_down, b_down):
    nb, seq, d = x_prompt.shape
    ndb, dseq, _ = x_sample.shape
    assert d == D_MODEL and ndb * dseq == TM and dseq == CHUNK
    assert seq % TM == 0 and seq >= ATT_WINDOW and cache_att_k.shape[2] == ATT_WINDOW
    assert w_ada.shape[0] == 1
    rp = nb * seq
    ntp = rp // TM
    nt = ntp + 1
    tps = seq // TM

    xp = x_prompt.reshape(rp, d)
    xs = x_sample.reshape(TM, d)

    m = _ada(jnp.concatenate([c_prompt, c_sample], axis=0), w_ada[0], b_ada[0])
    mods = _group_mods(m.reshape(nb + ndb, N_ADA, d), nb, ndb)
    shift_m, scale_m, gate_m, shift_f, scale_f, gate_f = [mods[a] for a in range(N_ADA)]

    cos_t, sin_t = _rotary_tables(seq, ndb, dseq)
    bd = jnp.asarray(np.kron(np.eye(N_HEADS_ATT // 2), np.ones((HEAD_DIM_ATT, HEAD_DIM_ATT))), BF16)
    tile8 = lambda g: jnp.tile(g.astype(F32), N_HEADS_ATT).reshape(1, GROUP_W)
    (qa, ka_t, va, qb, kb, vb, gb, kp_tail, vp_tail, ks_new, vs_new) = _inproj(
        xp, xs, shift_m, scale_m, g_norm_mix[0].reshape(1, d), w_in[0].astype(BF16), bd,
        tile8(g_q[0]) * (HEAD_DIM_ATT ** -0.5 * LOG2_E), tile8(g_k[0]), cos_t, sin_t, nb, tps)

    rev = _rel_bias_reversed(rel_bias[0])
    g_ro = g_ret_out[0].astype(F32).reshape(1, GROUP_W)
    zero_state = jnp.zeros((nb, N_HEADS_RET, HEAD_DIM_RET, HEAD_DIM_RET), F32)
    att_p, ret_p, state_p = _mix_prompt(qa, ka_t, va, rev, qb, kb, vb, gb, zero_state, g_ro, nb, seq)
    att_s = _attn_sample(qa, ks_new, vs_new,
                         cache_att_k[0].reshape(ndb, ATT_WINDOW, GROUP_W).astype(BF16),
                         cache_att_v[0].reshape(ndb, ATT_WINDOW, GROUP_W).astype(BF16), rev, rp)

    ret_s, state_s = _ret(qb, kb, vb, gb, state_ret[0].astype(F32), g_ro, CHUNK, rp, ndb, 1, "ret_sample")

    upper = jnp.asarray(np.triu(np.ones((MT, MT)), 1), BF16)
    lower = jnp.asarray(np.tril(np.ones((N_EXPERTS, N_EXPERTS)), -1), BF16)
    x1, h2, slot, cols, cnt = _outproj(
        att_p, att_s, ret_p, ret_s, xp, xs, gate_m, shift_f, scale_f, g_norm_ffn[0].reshape(1, d),
        w_out[0].astype(BF16), w_router[0].T.astype(BF16), b_router[0].astype(F32).reshape(N_EXPERTS, 1),
        upper, lower, nb, tps)

    ntm = nt * (TM // MT)
    n_blocks = (TOP_K * (rp + TM) + ntm * N_EXPERTS * SEG_ALIGN) // BM + 1 + N_EXPERTS
    (off, cntf, base, tot, tail0, tailn, blk_e, blk_i, blk_nx, blk_nq,
     n_act) = _routing_tables(cnt[:ntm, :, 0], n_blocks)
    xb = _dispatch(h2, slot, off, cntf, base, tot, tail0, tailn, n_act, n_blocks)
    yb = _experts(xb, blk_e, blk_i, blk_nx, blk_nq, n_act, w_up[0], b_up[0], w_down[0], b_down[0])
    out_p, out_s = _combine(yb, cols, x1, gate_f, off, cntf, base, tot, ntp, nb, tps)

    heads = (N_HEADS_ATT, HEAD_DIM_ATT)
    return (out_p.reshape(nb, seq, d), out_s.reshape(ndb, dseq, d),
            kp_tail.reshape(1, nb, ATT_WINDOW, *heads), vp_tail.reshape(1, nb, ATT_WINDOW, *heads),
            state_p[None],
            ks_new.reshape(1, ndb, dseq, *heads), vs_new.reshape(1, ndb, dseq, *heads),
            state_s[None])
```

```python
import functools

import numpy as np
import jax
import jax.numpy as jnp
from jax import lax
from jax.experimental import pallas as pl
from jax.experimental.pallas import tpu as pltpu

F32 = jnp.float32
BF16 = jnp.bfloat16
I32 = jnp.int32
U32 = jnp.uint32

D_MODEL = 1024
GROUP_W = 512
N_SLOTS = 7
N_HEADS_ATT = 8
HEAD_DIM_ATT = 64
N_HEADS_RET = 4
HEAD_DIM_RET = 128
CHUNK = 64
ATT_WINDOW = 512
MAX_REL = 256
PAST_LEN = 2048
RET_DECAY_OFFSET = 5.0
ROPE_BASE = 10000.0
N_EXPERTS = 32
TOP_K = 4
D_FF = 1024
SWIGLU_LIMIT = 7.0
SWIGLU_ALPHA = 1.702
N_ADA = 6
NORM_EPS = 1e-6
NEG_INF = -1e30
LOG2_E = 1.4426950408889634

TM = 512
GROUPS_PER_TILE = TM // CHUNK
ATT_QB = 256
RET_CB = 256
LANES = 128
PACK_ROWS = D_MODEL // (2 * LANES)
SUBLANES_32 = 8
SEG_ALIGN = SUBLANES_32 // PACK_ROWS
MT = 256
PARTS = TM // MT
CAP_USED = TOP_K * MT + N_EXPERTS * SEG_ALIGN
CAP = -(-CAP_USED // LANES) * LANES
X_RING = 3
BM = 512
BLOCK_QUARTERS = 4
VMEM_LIMIT = 56 * 1024 * 1024


def _cparams(sem, vmem=None):
    return pltpu.CompilerParams(dimension_semantics=sem, vmem_limit_bytes=vmem)


def _ada_kernel(c_ref, w_ref, b_ref, o_ref):
    c = c_ref[...]
    s = c * jax.nn.sigmoid(c)
    o_ref[...] = jnp.dot(s.astype(BF16), w_ref[...].astype(BF16),
                         preferred_element_type=F32) + b_ref[...]


def _ada(c_all, w_ada, b_ada):
    n, d = c_all.shape
    cols = w_ada.shape[1]
    tn = 1536
    return pl.pallas_call(
        _ada_kernel,
        grid=(cols // tn,),
        in_specs=[pl.BlockSpec((n, d), lambda j: (0, 0)),
                  pl.BlockSpec((d, tn), lambda j: (0, j)),
                  pl.BlockSpec((1, tn), lambda j: (0, j))],
        out_specs=pl.BlockSpec((n, tn), lambda j: (0, j)),
        out_shape=jax.ShapeDtypeStruct((n, cols), F32),
        compiler_params=_cparams(("arbitrary",), VMEM_LIMIT),
        name="ada",
    )(c_all, w_ada, b_ada.reshape(1, cols))


def _rms_rows(x, g):
    ms = jnp.mean(x * x, axis=-1, keepdims=True)
    return x * lax.rsqrt(ms + NORM_EPS) * g


def _mod_row(ntp, tps, nb):
    return lambda i, *_: (jnp.where(i < ntp, i // tps, nb), 0)


def _per_group(x, fn, *mods):
    x3 = x.reshape(x.shape[0] // CHUNK, CHUNK, x.shape[-1])
    y3 = fn(x3, *[m[:, None, :] for m in mods])
    return y3.reshape(x.shape)


def _inproj_kernel(npp, xp_ref, xs_ref, sh_ref, sc_ref, gn_ref, w_ref, bd_ref, gq_ref, gk_ref,
                   cos_ref, sin_ref,
                   qa_ref, ka_ref, va_ref, qb_ref, kb_ref, vb_ref, gb_ref,
                   kpt_ref, vpt_ref, kst_ref, vst_ref):
    is_p = pl.program_id(0) < npp
    subs = range(2)
    rows = [slice(sub * TM, (sub + 1) * TM) for sub in subs]

    hb = []
    for sub in subs:
        x = jnp.where(is_p, xp_ref[rows[sub], :], xs_ref[...])
        y = _rms_rows(x, gn_ref[...])
        h = _per_group(y, lambda a, sh, sc: a * (1.0 + sc) + sh, sh_ref[...], sc_ref[...])
        hb.append(h.astype(BF16))

    def proj(s):
        return [jnp.dot(hb[sub], w_ref[:, s * GROUP_W:(s + 1) * GROUP_W], preferred_element_type=F32)
                for sub in subs]

    def head_rms(z, g):
        zz = (z * z).astype(BF16)
        half = GROUP_W // 2
        ss = jnp.concatenate(
            [jnp.dot(zz[:, :half], bd_ref[...], preferred_element_type=F32),
             jnp.dot(zz[:, half:], bd_ref[...], preferred_element_type=F32)], axis=1)
        return z * lax.rsqrt(ss * (1.0 / HEAD_DIM_ATT) + NORM_EPS) * g

    def rot(z, sub):
        cos = cos_ref[rows[sub], :]
        sin = sin_ref[rows[sub], :]
        outs = []
        for hh in range(N_HEADS_RET):
            zh = z[:, hh * HEAD_DIM_RET:(hh + 1) * HEAD_DIM_RET]
            outs.append(zh * cos + pltpu.roll(zh, HEAD_DIM_RET // 2, axis=1) * sin)
        return jnp.concatenate(outs, axis=1)

    for sub, z in zip(subs, proj(0)):
        qa_ref[rows[sub], :] = head_rms(z, gq_ref[...]).astype(BF16)
    ka = [head_rms(z, gk_ref[...]) for z in proj(1)]
    for sub in subs:
        ka_ref[:, rows[sub]] = ka[sub].T.astype(BF16)
    va = proj(2)
    for sub in subs:
        va_ref[rows[sub], :] = va[sub].astype(BF16)

    @pl.when(is_p)
    def _():
        kpt_ref[...] = ka[1]
        vpt_ref[...] = va[1]

    @pl.when(jnp.logical_not(is_p))
    def _():
        kst_ref[...] = ka[0]
        vst_ref[...] = va[0]

    for sub, z in zip(subs, proj(3)):
        qb_ref[rows[sub], :] = rot(z, sub).astype(BF16)
    for sub, z in zip(subs, proj(4)):
        kb_ref[rows[sub], :] = (rot(z, sub) * (HEAD_DIM_RET ** -0.5)).astype(BF16)
    for sub, z in zip(subs, proj(5)):
        vb_ref[rows[sub], :] = z.astype(BF16)
    for sub, z in zip(subs, proj(6)):
        gb_ref[rows[sub], :] = z.astype(BF16)


def _inproj(xp, xs, shift, scale, g_norm, w_in_b, bd, gq8, gk8, cos_t, sin_t, nb, tps):
    rp = xp.shape[0]
    ntp = rp // TM
    assert ntp % 2 == 0 and tps % 2 == 0
    npp = ntp // 2
    r = (ntp + 2) * TM
    row = lambda p: (p, 0)
    full = lambda p: (0, 0)
    tab = lambda p: (jnp.where(p < npp, p % (tps // 2), tps // 2), 0)
    mod = pl.BlockSpec((GROUPS_PER_TILE, D_MODEL), _mod_row(npp, tps // 2, nb))
    tail_spec = pl.BlockSpec((TM, GROUP_W), lambda p: (jnp.minimum(p // (tps // 2), nb - 1), 0))
    act = jax.ShapeDtypeStruct((r, GROUP_W), BF16)
    return pl.pallas_call(
        functools.partial(_inproj_kernel, npp),
        grid=(npp + 1,),
        in_specs=[pl.BlockSpec((2 * TM, D_MODEL), lambda p: (jnp.minimum(p, npp - 1), 0)),
                  pl.BlockSpec((TM, D_MODEL), full),
                  mod, mod,
                  pl.BlockSpec((1, D_MODEL), full),
                  pl.BlockSpec((D_MODEL, N_SLOTS * GROUP_W), full),
                  pl.BlockSpec((GROUP_W // 2, GROUP_W // 2), full),
                  pl.BlockSpec((1, GROUP_W), full),
                  pl.BlockSpec((1, GROUP_W), full),
                  pl.BlockSpec((2 * TM, HEAD_DIM_RET), tab),
                  pl.BlockSpec((2 * TM, HEAD_DIM_RET), tab)],
        out_specs=[pl.BlockSpec((2 * TM, GROUP_W), row), pl.BlockSpec((GROUP_W, 2 * TM), lambda p: (0, p))]
        + [pl.BlockSpec((2 * TM, GROUP_W), row)] * 5 + [
            tail_spec, tail_spec,
            pl.BlockSpec((TM, GROUP_W), full),
            pl.BlockSpec((TM, GROUP_W), full)],
        out_shape=[act, jax.ShapeDtypeStruct((GROUP_W, r), BF16)] + [act] * 5
        + [jax.ShapeDtypeStruct((nb * TM, GROUP_W), F32)] * 2
        + [jax.ShapeDtypeStruct((TM, GROUP_W), F32)] * 2,
        compiler_params=_cparams(("arbitrary",), VMEM_LIMIT),
        name="inproj",
    )(xp, xs, shift, scale, g_norm, w_in_b, bd, gq8, gk8, cos_t, sin_t)


def _attn_heads(q, k, v, bias_ref, first_valid_col=None):
    qb_rows, kb_rows = q.shape[0], v.shape[0]
    assert qb_rows == 4 * CHUNK
    half_rows, span = qb_rows // 2, kb_rows - 2 * CHUNK
    parts = [(0, 0), (half_rows, 2 * CHUNK)]

    def softmax_part(s_full, hh, half, r0, c0):
        rs = half * qb_rows + r0
        s = s_full[rs:rs + half_rows, c0:c0 + span] + bias_ref[hh, r0:r0 + half_rows, c0:c0 + span]
        if first_valid_col is not None:
            col = lax.broadcasted_iota(I32, (half_rows, span), 1) + c0
            s = jnp.where(col >= first_valid_col, s, NEG_INF)
        m = jnp.max(s, axis=-1, keepdims=True)
        e = jnp.exp2(s - m)
        l = jnp.sum(e, axis=-1, keepdims=True)
        pad = [jnp.zeros((half_rows, c0), BF16)] if c0 else []
        pad_r = [jnp.zeros((half_rows, kb_rows - span - c0), BF16)] if kb_rows - span - c0 else []
        return jnp.concatenate(pad + [e.astype(BF16)] + pad_r, axis=1), l

    pair_w = 2 * HEAD_DIM_ATT
    low = lax.broadcasted_iota(I32, (1, pair_w), 1) < HEAD_DIM_ATT
    outs = []
    for pp in range(N_HEADS_ATT // 2):
        ps = slice(pp * pair_w, (pp + 1) * pair_w)
        q2, v2 = q[:, ps], v[:, ps]
        zero = jnp.zeros_like(q2)
        qs = jnp.concatenate([jnp.where(low, q2, zero), jnp.where(low, zero, q2)], axis=0)
        s = jnp.dot(qs, k[ps, :], preferred_element_type=F32)
        es, ls = zip(*[softmax_part(s, 2 * pp + half, half, r0, c0)
                       for half in range(2) for r0, c0 in parts])
        o = jnp.dot(jnp.concatenate(es, axis=0), v2, preferred_element_type=F32) / jnp.concatenate(ls, axis=0)
        outs.append(jnp.where(low, o[:qb_rows], o[qb_rows:]))
    return jnp.concatenate(outs, axis=1)


def _fill_band_bias(rev_ref, bias_scr):
    _, qb_rows, kb_rows = bias_scr.shape
    width = rev_ref.shape[1]
    q = lax.broadcasted_iota(I32, (qb_rows, kb_rows), 0)
    k = lax.broadcasted_iota(I32, (qb_rows, kb_rows), 1)
    qc = q >> 6
    kc = (k - ATT_WINDOW) >> 6
    band = (kc >= qc - ATT_WINDOW // CHUNK) & (kc <= qc)
    for hh in range(N_HEADS_ATT):
        rows = jnp.broadcast_to(rev_ref[hh:hh + 1, :], (qb_rows, width))
        toep = pltpu.roll(rows, width - MAX_REL, 1, stride=1, stride_axis=0)
        bias_scr[hh] = jnp.where(band, toep[:, :kb_rows] * LOG2_E, NEG_INF)


def _attn_sample_kernel(q_ref, kn_ref, vn_ref, kc_ref, vc_ref, rev_ref, o_ref, bias_scr):
    @pl.when(pl.program_id(0) == 0)
    def _():
        _fill_band_bias(rev_ref, bias_scr)

    q = q_ref[...]
    outs = []
    for hh in range(N_HEADS_ATT):
        hs = slice(hh * HEAD_DIM_ATT, (hh + 1) * HEAD_DIM_ATT)
        k = jnp.concatenate([kc_ref[0, :, hs], kn_ref[:, hs].astype(BF16)], axis=0)
        v = jnp.concatenate([vc_ref[0, :, hs], vn_ref[:, hs].astype(BF16)], axis=0)
        s = lax.dot_general(q[:, hs], k, (((1,), (1,)), ((), ())), preferred_element_type=F32)
        s = s + bias_scr[hh]
        m = jnp.max(s, axis=-1, keepdims=True)
        e = jnp.exp2(s - m)
        l = jnp.sum(e, axis=-1, keepdims=True)
        outs.append(jnp.dot(e.astype(BF16), v, preferred_element_type=F32) / l)
    o_ref[...] = jnp.concatenate(outs, axis=1).astype(BF16)


def _attn_sample(qa, ks_new, vs_new, kc, vc, rev, rp):
    ndb = kc.shape[0]
    base = rp // CHUNK
    spec = pl.BlockSpec((CHUNK, GROUP_W), lambda b: (base + b, 0))
    new = pl.BlockSpec((CHUNK, GROUP_W), lambda b: (b, 0))
    cspec = pl.BlockSpec((1, ATT_WINDOW, GROUP_W), lambda b: (b, 0, 0))
    return pl.pallas_call(
        _attn_sample_kernel,
        grid=(ndb,),
        in_specs=[spec, new, new, cspec, cspec,
                  pl.BlockSpec(rev.shape, lambda b: (0, 0))],
        out_specs=pl.BlockSpec((CHUNK, GROUP_W), lambda b: (b, 0)),
        out_shape=jax.ShapeDtypeStruct((ndb * CHUNK, GROUP_W), BF16),
        scratch_shapes=[pltpu.VMEM((N_HEADS_ATT, CHUNK, ATT_WINDOW + CHUNK), F32)],
        compiler_params=_cparams(("arbitrary",), VMEM_LIMIT),
        name="attn_sample",
    )(qa, ks_new, vs_new, kc, vc, rev)


def _ret_chunk(state_decay, q_ref, k_ref, v_ref, g_ref, dm_ref, xi_ref, zeta_ref, gro_ref, o_ref, s_scr):
    outs = []
    for hh in range(N_HEADS_RET):
        hs = slice(hh * HEAD_DIM_RET, (hh + 1) * HEAD_DIM_RET)
        q = q_ref[:, hs]
        k = k_ref[:, hs]
        v = v_ref[:, hs]
        st = s_scr[hh]
        sc = lax.dot_general(q, k, (((1,), (1,)), ((), ())), preferred_element_type=F32) * dm_ref[hh]
        inner = jnp.dot(sc.astype(BF16), v, preferred_element_type=F32)
        cross = jnp.dot(q, st.astype(BF16), preferred_element_type=F32) * xi_ref[:, hs]
        o = inner + cross
        kz = k.astype(F32) * zeta_ref[:, hs]
        s_scr[hh] = state_decay[hh] * st + jnp.dot(kz.T.astype(BF16), v, preferred_element_type=F32)
        mu = jnp.mean(o, axis=-1, keepdims=True)
        oc = o - mu
        var = jnp.mean(oc * oc, axis=-1, keepdims=True)
        outs.append(oc * lax.rsqrt(var + NORM_EPS))
    y = jnp.concatenate(outs, axis=1) * gro_ref[...]
    g = g_ref[...].astype(F32)
    o_ref[...] = (g * jax.nn.sigmoid(g) * y).astype(BF16)


def _ret_kernel(state_decay, q_ref, k_ref, v_ref, g_ref, s0_ref, dm_ref, xi_ref, zeta_ref,
                gro_ref, o_ref, sn_ref, s_scr):
    j = pl.program_id(1)

    @pl.when(j == 0)
    def _():
        s_scr[...] = s0_ref[0]

    _ret_chunk(state_decay, q_ref, k_ref, v_ref, g_ref, dm_ref, xi_ref, zeta_ref, gro_ref, o_ref, s_scr)

    @pl.when(j == pl.num_programs(1) - 1)
    def _():
        sn_ref[0] = s_scr[...]


def _mix_prompt_kernel(state_decay, q_ref, k0_ref, k1_ref, k2_ref, v0_ref, v1_ref, v2_ref, rev_ref,
                       rq_ref, rk_ref, rv_ref, rg_ref, s0_ref, dm_ref, xi_ref, zeta_ref, gro_ref,
                       att_ref, ret_ref, sn_ref, bias_scr, s_scr):
    j = pl.program_id(1)

    @pl.when((pl.program_id(0) == 0) & (j == 0))
    def _():
        _fill_band_bias(rev_ref, bias_scr)

    @pl.when(j == 0)
    def _():
        s_scr[...] = s0_ref[0]

    k = jnp.concatenate([k0_ref[...], k1_ref[...], k2_ref[...]], axis=1)
    v = jnp.concatenate([v0_ref[...], v1_ref[...], v2_ref[...]], axis=0)

    def block(first_valid_col):
        att_ref[...] = _attn_heads(q_ref[...], k, v, bias_scr, first_valid_col).astype(BF16)
        _ret_chunk(state_decay, rq_ref, rk_ref, rv_ref, rg_ref, dm_ref, xi_ref, zeta_ref, gro_ref,
                   ret_ref, s_scr)

    @pl.when(j >= 2)
    def _():
        block(None)

    @pl.when(j < 2)
    def _():
        block((2 - j) * ATT_QB)

    @pl.when(j == pl.num_programs(1) - 1)
    def _():
        sn_ref[0] = s_scr[...]


def _mix_prompt(qa, ka_t, va, rev, qb, kb, vb, gb, s0, g_ro, nb, seq):
    assert ATT_QB == RET_CB
    r = nb * seq
    nq = seq // ATT_QB
    dm, xi, zeta, state_decay = _ret_consts(RET_CB)
    blk = lambda back: (lambda b, j: (b * nq + jnp.maximum(j - back, 0), 0))
    spec = lambda back: pl.BlockSpec((ATT_QB, GROUP_W), blk(back))
    tspec = lambda back: pl.BlockSpec((GROUP_W, ATT_QB), lambda b, j: (0, b * nq + jnp.maximum(j - back, 0)))
    sspec = pl.BlockSpec((1, N_HEADS_RET, HEAD_DIM_RET, HEAD_DIM_RET), lambda b, j: (b, 0, 0, 0))
    full2 = lambda b, j: (0, 0)
    out = jax.ShapeDtypeStruct((r, GROUP_W), BF16)
    return pl.pallas_call(
        functools.partial(_mix_prompt_kernel, state_decay),
        grid=(nb, nq),
        in_specs=[spec(0), tspec(2), tspec(1), tspec(0), spec(2), spec(1), spec(0),
                  pl.BlockSpec(rev.shape, full2),
                  spec(0), spec(0), spec(0), spec(0), sspec,
                  pl.BlockSpec(dm.shape, lambda b, j: (0, 0, 0)),
                  pl.BlockSpec(xi.shape, full2), pl.BlockSpec(zeta.shape, full2),
                  pl.BlockSpec((1, GROUP_W), full2)],
        out_specs=[spec(0), spec(0), sspec],
        out_shape=[out, out, jax.ShapeDtypeStruct(s0.shape, F32)],
        scratch_shapes=[pltpu.VMEM((N_HEADS_ATT, ATT_QB, ATT_WINDOW + ATT_QB), F32),
                        pltpu.VMEM((N_HEADS_RET, HEAD_DIM_RET, HEAD_DIM_RET), F32)],
        compiler_params=_cparams(("arbitrary", "arbitrary"), VMEM_LIMIT),
        name="mix_prompt",
    )(qa, ka_t, ka_t, ka_t, va, va, va, rev, qb, kb, vb, gb, s0, dm, xi, zeta, g_ro)


def _ret_consts(cb):
    log_g = np.log1p(-np.exp2(-RET_DECAY_OFFSET - np.arange(N_HEADS_RET, dtype=np.float64)))
    n = np.arange(cb, dtype=np.float64)
    diff = n[:, None] - n[None, :]
    dm = np.where(diff[None] >= 0, np.exp(np.maximum(diff, 0.0)[None] * log_g[:, None, None]), 0.0)
    xi = np.exp((n + 1.0)[:, None] * log_g[None, :])
    zeta = np.exp((cb - 1.0 - n)[:, None] * log_g[None, :])
    rep = lambda a: np.repeat(a, HEAD_DIM_RET, axis=1)
    state_decay = tuple(float(v) for v in np.exp(cb * log_g))
    return (jnp.asarray(dm, F32), jnp.asarray(rep(xi), F32), jnp.asarray(rep(zeta), F32), state_decay)


def _ret(qb, kb, vb, gb, s0, g_ro, cb, row0, nb, nc, name):
    dm, xi, zeta, state_decay = _ret_consts(cb)
    base = row0 // cb
    spec = pl.BlockSpec((cb, GROUP_W), lambda b, j: (base + b * nc + j, 0))
    sspec = pl.BlockSpec((1, N_HEADS_RET, HEAD_DIM_RET, HEAD_DIM_RET), lambda b, j: (b, 0, 0, 0))
    full2 = lambda b, j: (0, 0)
    return pl.pallas_call(
        functools.partial(_ret_kernel, state_decay),
        grid=(nb, nc),
        in_specs=[spec, spec, spec, spec, sspec,
                  pl.BlockSpec(dm.shape, lambda b, j: (0, 0, 0)),
                  pl.BlockSpec(xi.shape, full2), pl.BlockSpec(zeta.shape, full2),
                  pl.BlockSpec((1, GROUP_W), full2)],
        out_specs=[pl.BlockSpec((cb, GROUP_W), lambda b, j: (b * nc + j, 0)), sspec],
        out_shape=[jax.ShapeDtypeStruct((nb * nc * cb, GROUP_W), BF16),
                   jax.ShapeDtypeStruct(s0.shape, F32)],
        scratch_shapes=[pltpu.VMEM((N_HEADS_RET, HEAD_DIM_RET, HEAD_DIM_RET), F32)],
        compiler_params=_cparams(("arbitrary", "arbitrary"), VMEM_LIMIT),
        name=name,
    )(qb, kb, vb, gb, s0, dm, xi, zeta, g_ro)


def _outproj_kernel(npp, attp_ref, atts_ref, retp_ref, rets_ref, xp_hbm, xs_ref, gm_ref, shf_ref, scf_ref,
                    gn_ref, wo_ref, wr_ref, br_ref, upper_ref, lower_ref,
                    x1_ref, h2_ref, slot_ref, cols_ref, cnt_ref, x_ring, x_sems):
    p = pl.program_id(0)
    is_p = p < npp
    subs = range(2)
    rows = [slice(sub * TM, (sub + 1) * TM) for sub in subs]

    def x_copy(step):
        slot = step % X_RING
        src = xp_hbm.at[pl.ds(pl.multiple_of(step * 2 * TM, 2 * TM), 2 * TM), :]
        return pltpu.make_async_copy(src, x_ring.at[slot], x_sems.at[slot])

    @pl.when(p == 0)
    def _():
        for step in range(min(X_RING - 1, npp)):
            x_copy(step).start()

    @pl.when(p + X_RING - 1 < npp)
    def _():
        x_copy(p + X_RING - 1).start()

    @pl.when(is_p)
    def _():
        x_copy(p).wait()

    xp_ref = x_ring.at[p % X_RING]

    def pick(p_ref, s_ref, sub):
        return jnp.where(is_p, p_ref[rows[sub], :], s_ref[...])

    mix = [jnp.dot(pick(attp_ref, atts_ref, sub), wo_ref[:GROUP_W, :], preferred_element_type=F32)
           + jnp.dot(pick(retp_ref, rets_ref, sub), wo_ref[GROUP_W:, :], preferred_element_type=F32)
           for sub in subs]
    h2b = []
    for sub in subs:
        x1 = _per_group(mix[sub], lambda a, gm: a * gm, gm_ref[...]) + pick(xp_ref, xs_ref, sub)
        x1_ref[rows[sub], :] = x1
        y = _rms_rows(x1, gn_ref[...])
        h2 = _per_group(y, lambda a, sh, sc: a * (1.0 + sc) + sh, shf_ref[...], scf_ref[...])
        h2b.append(h2.astype(BF16))
        h2_ref[rows[sub], :] = h2b[sub]

    work = [lax.dot_general(wr_ref[...], h2b[sub], (((1,), (1,)), ((), ())),
                            preferred_element_type=F32) + br_ref[...] for sub in subs]
    eidx = lax.broadcasted_iota(I32, work[0].shape, 0).astype(F32)
    sel = [[] for _ in subs]
    top = [[] for _ in subs]
    for _ in range(TOP_K):
        for sub in subs:
            m = jnp.max(work[sub], axis=0, keepdims=True)
            idx = jnp.min(jnp.where(work[sub] == m, eidx, float(N_EXPERTS)), axis=0, keepdims=True)
            hit = eidx == idx
            sel[sub].append(hit)
            top[sub].append(m)
            work[sub] = jnp.where(hit, -jnp.inf, work[sub])

    for sub in subs:
        ex = [jnp.exp(t - top[sub][0]) for t in top[sub]]
        den = ex[0] + ex[1] + ex[2] + ex[3]
        gates = [e / den for e in ex]
        for part in range(TM // MT):
            lanes = slice(part * MT, (part + 1) * MT)
            tile = (TM // MT) * sub + part
            hits = [h[:, lanes] for h in sel[sub]]
            multi_f = jnp.where(hits[0] | hits[1] | hits[2] | hits[3], 1.0, 0.0)
            rank = jnp.dot(multi_f.astype(BF16), upper_ref[...], preferred_element_type=F32)
            cnt = jnp.sum(multi_f, axis=1, keepdims=True)
            cnt_pad = jnp.maximum(jnp.floor((cnt + (SEG_ALIGN - 1.0)) * (1.0 / SEG_ALIGN)), 1.0) * SEG_ALIGN
            cnt_pad_b = jnp.broadcast_to(cnt_pad, (N_EXPERTS, 128))
            seg_off = jnp.dot(lower_ref[...], cnt_pad_b.astype(BF16), preferred_element_type=F32)[:, :1]
            pos = seg_off + rank
            slot_rows = jnp.concatenate(
                [jnp.sum(jnp.where(h, pos, 0.0), axis=0, keepdims=True) for h in hits], axis=0)
            gate_rows = jnp.concatenate([g[:, lanes] for g in gates], axis=0)
            slot_ref[tile] = slot_rows.astype(I32)
            cnt_ref[tile] = cnt_pad_b.astype(I32)
            both = jnp.concatenate([slot_rows, gate_rows, jnp.zeros((128 - 2 * TOP_K, MT), F32)], axis=0)
            cols_ref[tile] = both.T


def _outproj(att_p, att_s, ret_p, ret_s, xp, xs, gate_m, shift_f, scale_f, g_norm, w_out_b, wr_t, br,
             upper, lower, nb, tps):
    rp = xp.shape[0]
    ntp = rp // TM
    assert ntp % 2 == 0 and tps % 2 == 0
    npp = ntp // 2
    r = (ntp + 2) * TM
    per_step = 2 * TM // MT
    ntm = (npp + 1) * per_step
    row = lambda p: (p, 0)
    row3 = lambda p: (p, 0, 0)
    full = lambda p: (0, 0)
    prow = lambda p: (jnp.minimum(p, npp - 1), 0)
    mod = pl.BlockSpec((GROUPS_PER_TILE, D_MODEL), _mod_row(npp, tps // 2, nb))
    return pl.pallas_call(
        functools.partial(_outproj_kernel, npp),
        grid=(npp + 1,),
        in_specs=[pl.BlockSpec((2 * TM, GROUP_W), prow), pl.BlockSpec((TM, GROUP_W), full),
                  pl.BlockSpec((2 * TM, GROUP_W), prow), pl.BlockSpec((TM, GROUP_W), full),
                  pl.BlockSpec(memory_space=pl.ANY),
                  pl.BlockSpec((TM, D_MODEL), full),
                  mod, mod, mod,
                  pl.BlockSpec((1, D_MODEL), full),
                  pl.BlockSpec((D_MODEL, D_MODEL), full),
                  pl.BlockSpec((N_EXPERTS, D_MODEL), full),
                  pl.BlockSpec((N_EXPERTS, 1), full),
                  pl.BlockSpec((MT, MT), full),
                  pl.BlockSpec((N_EXPERTS, N_EXPERTS), full)],
        out_specs=[pl.BlockSpec((2 * TM, D_MODEL), row), pl.BlockSpec((2 * TM, D_MODEL), row),
                   pl.BlockSpec((per_step, TOP_K, MT), row3),
                   pl.BlockSpec((per_step, MT, 128), row3), pl.BlockSpec((per_step, N_EXPERTS, 128), row3)],
        out_shape=[jax.ShapeDtypeStruct((r, D_MODEL), F32), jax.ShapeDtypeStruct((r, D_MODEL), BF16),
                   jax.ShapeDtypeStruct((ntm, TOP_K, MT), I32),
                   jax.ShapeDtypeStruct((ntm, MT, 128), F32), jax.ShapeDtypeStruct((ntm, N_EXPERTS, 128), I32)],
        scratch_shapes=[pltpu.VMEM((X_RING, 2 * TM, D_MODEL), F32), pltpu.SemaphoreType.DMA((X_RING,))],
        compiler_params=_cparams(("arbitrary",), VMEM_LIMIT),
        name="outproj",
    )(att_p, att_s, ret_p, ret_s, xp, xs, gate_m, shift_f, scale_f, g_norm, w_out_b, wr_t, br, upper, lower)


def _store_packed(ref, lead, row0, x):
    n = x.shape[0]
    for j in range(PACK_ROWS):
        c = 2 * LANES * j
        lo = lax.bitcast_convert_type(x[:, c:c + LANES].astype(BF16).astype(F32), U32) >> 16
        hi = lax.bitcast_convert_type(x[:, c + LANES:c + 2 * LANES].astype(BF16).astype(F32), U32)
        ref[lead + (pl.ds(PACK_ROWS * row0 + j, n, stride=PACK_ROWS), slice(None))] = hi | lo


def _load_packed(ref, lead, row0, n):
    parts = []
    for j in range(PACK_ROWS):
        u = ref[lead + (pl.ds(PACK_ROWS * row0 + j, n, stride=PACK_ROWS), slice(None))]
        parts.append(lax.bitcast_convert_type(u << 16, F32))
        parts.append(lax.bitcast_convert_type(u & jnp.uint32(0xFFFF0000), F32))
    return jnp.concatenate(parts, axis=1).astype(BF16)


def _packed_rows(ref, lead, tok0, ntok):
    rows = pl.ds(pl.multiple_of(tok0 * PACK_ROWS, SUBLANES_32), ntok * PACK_ROWS)
    return ref.at[lead + (rows, slice(None))]


def _rows_copy(n, src_rows, dst_rows, sem):
    size = pl.multiple_of(n, SEG_ALIGN)
    return pltpu.make_async_copy(src_rows(size), dst_rows(size), sem)


def _start_segments(t, cnt_ref, off_ref, base_ref, local_rows, sorted_rows, sem, to_sorted):
    for e in range(N_EXPERTS):
        n = cnt_ref[t * N_EXPERTS + e]
        off = pl.multiple_of(off_ref[t * N_EXPERTS + e], SEG_ALIGN)
        base = pl.multiple_of(base_ref[t * N_EXPERTS + e], SEG_ALIGN)
        local = lambda z, off=off: local_rows(off, z)
        remote = lambda z, base=base: sorted_rows(base, z)
        (_rows_copy(n, local, remote, sem) if to_sorted else _rows_copy(n, remote, local, sem)).start()


def _dispatch_kernel(nt, n_blocks, off_ref, cnt_ref, base_ref, tot_ref, tail0_ref, tailn_ref, na_ref,
                     h2_ref, slot_ref, slotn_ref, xb_ref, xs_scr, hot_scr, zero_scr, sems, tail_sem):
    i = pl.program_id(0)
    cur = i % 2
    sorted_rows = lambda r, z: _packed_rows(xb_ref, (), r, z)

    def start_tile(t, buf):
        for part in range(PARTS):
            _start_segments(PARTS * t + part, cnt_ref, off_ref, base_ref,
                            lambda r, z, part=part: _packed_rows(xs_scr, (buf, part), r, z),
                            sorted_rows, sems.at[buf], True)

    def wait_tile(t, buf):
        for part in range(PARTS):
            _rows_copy(tot_ref[PARTS * t + part], lambda z, part=part: _packed_rows(xs_scr, (buf, part), 0, z),
                       lambda z: sorted_rows(0, z), sems.at[buf]).wait()

    def tail_copies(wait):
        def body(e, c):
            base = pl.multiple_of(tail0_ref[e], SEG_ALIGN)

            @pl.when(tailn_ref[e] > 0)
            def _():
                cp = _rows_copy(tailn_ref[e], lambda z: _packed_rows(zero_scr, (), 0, z),
                                lambda z: sorted_rows(base, z), tail_sem)
                cp.wait() if wait else cp.start()
            return c
        lax.fori_loop(0, N_EXPERTS, body, 0)

        def unused(j, c):
            cp = pltpu.make_async_copy(zero_scr, sorted_rows(pl.multiple_of(j * BM, BM), BM), tail_sem)
            cp.wait() if wait else cp.start()
            return c
        lax.fori_loop(na_ref[0], n_blocks, unused, 0)

    @pl.when(i >= 2)
    def _():
        wait_tile(i - 2, cur)

    def onehot(slot):
        srow = lax.broadcasted_iota(I32, (CAP_USED, MT), 0)
        hit = (srow == slot[0:1]) | (srow == slot[1:2]) | (srow == slot[2:3]) | (srow == slot[3:4])
        return jnp.where(hit, 1.0, 0.0).astype(BF16)

    @pl.when(i == 0)
    def _():
        for part in range(PARTS):
            hot_scr[0, part] = onehot(slot_ref[part])
        zero_scr[...] = jnp.zeros_like(zero_scr)
        tail_copies(False)

    for part in range(PARTS):
        rows = h2_ref[part * MT:(part + 1) * MT, :]
        _store_packed(xs_scr, (cur, part), 0, jnp.dot(hot_scr[cur, part], rows, preferred_element_type=F32))
    for part in range(PARTS):
        hot_scr[1 - cur, part] = onehot(slotn_ref[part])
    start_tile(i, cur)

    @pl.when(i == nt - 1)
    def _():
        if nt >= 2:
            wait_tile(i - 1, 1 - cur)
        wait_tile(i, cur)
        tail_copies(True)


def _dispatch(h2, slot, off, cnt, base, tot, tail0, tailn, n_act, n_blocks):
    nt = tot.shape[0] // PARTS
    n_rows = n_blocks * BM
    grid_spec = pltpu.PrefetchScalarGridSpec(
        num_scalar_prefetch=7,
        grid=(nt,),
        in_specs=[pl.BlockSpec((TM, D_MODEL), lambda i, *_: (i, 0)),
                  pl.BlockSpec((PARTS, TOP_K, MT), lambda i, *_: (i, 0, 0)),
                  pl.BlockSpec((PARTS, TOP_K, MT), lambda i, *_: (jnp.minimum(i + 1, nt - 1), 0, 0))],
        out_specs=pl.BlockSpec(memory_space=pl.ANY),
        scratch_shapes=[pltpu.VMEM((2, PARTS, CAP_USED * PACK_ROWS, LANES), U32),
                        pltpu.VMEM((2, PARTS, CAP_USED, MT), BF16),
                        pltpu.VMEM((BM * PACK_ROWS, LANES), U32),
                        pltpu.SemaphoreType.DMA((2,)),
                        pltpu.SemaphoreType.DMA(())],
    )
    return pl.pallas_call(
        functools.partial(_dispatch_kernel, nt, n_blocks),
        grid_spec=grid_spec,
        out_shape=jax.ShapeDtypeStruct((n_rows * PACK_ROWS, LANES), U32),
        compiler_params=_cparams(("arbitrary",), VMEM_LIMIT),
        name="dispatch",
    )(off, cnt, base, tot, tail0, tailn, n_act, h2, slot, slot)


def _experts_kernel(be_ref, bi_ref, nx_ref, nq_ref, na_ref, x_ref, wu_hbm, bu_ref, wd_hbm, bd_ref, y_ref,
                    wu_stage, wd_stage, wu_scr, wd_scr, sems):
    j = pl.program_id(0)

    def weight_copies(e):
        return (pltpu.make_async_copy(wu_hbm.at[e], wu_stage, sems.at[0]),
                pltpu.make_async_copy(wd_hbm.at[e], wd_stage, sems.at[1]))

    @pl.when(j < na_ref[0])
    def _():
        e = be_ref[j]
        prev = be_ref[jnp.maximum(j - 1, 0)]

        @pl.when(j == 0)
        def _():
            for cp in weight_copies(e):
                cp.start()

        @pl.when((j == 0) | (e != prev))
        def _():
            for cp in weight_copies(e):
                cp.wait()
            wu_scr[...] = wu_stage[...].astype(BF16)
            wd_scr[...] = wd_stage[...].astype(BF16)

            @pl.when(nx_ref[j] != e)
            def _():
                for cp in weight_copies(nx_ref[j]):
                    cp.start()

        def ffn(n):
            x = _load_packed(x_ref, (), 0, n)
            u = jnp.dot(x, wu_scr[...], preferred_element_type=F32) + bu_ref[0]
            glu = jnp.minimum(u[:, :D_FF], SWIGLU_LIMIT)
            lin = jnp.clip(u[:, D_FF:], -SWIGLU_LIMIT, SWIGLU_LIMIT)
            act = glu * jax.nn.sigmoid(SWIGLU_ALPHA * glu) * (lin + 1.0)
            y = jnp.dot(act.astype(BF16), wd_scr[...], preferred_element_type=F32) + bd_ref[0]
            _store_packed(y_ref, (), 0, y)

        for quarters in range(1, BLOCK_QUARTERS + 1):
            @pl.when(nq_ref[j] == quarters)
            def _(quarters=quarters):
                n = quarters * (BM // BLOCK_QUARTERS)
                ffn(n)
                if n < BM:
                    y_ref[PACK_ROWS * n:, :] = jnp.zeros((PACK_ROWS * (BM - n), LANES), U32)


def _experts(xb, blk_e, blk_i, blk_nx, blk_nq, n_act, w_up, b_up, w_down, b_down):
    n_rows = xb.shape[0] // PACK_ROWS
    nblk = n_rows // BM
    grid_spec = pltpu.PrefetchScalarGridSpec(
        num_scalar_prefetch=5,
        grid=(nblk,),
        in_specs=[pl.BlockSpec((BM * PACK_ROWS, LANES), lambda j, be, bi, *_: (bi[j], 0)),
                  pl.BlockSpec(memory_space=pl.ANY),
                  pl.BlockSpec((1, 1, 2 * D_FF), lambda j, be, *_: (be[j], 0, 0)),
                  pl.BlockSpec(memory_space=pl.ANY),
                  pl.BlockSpec((1, 1, D_MODEL), lambda j, be, *_: (be[j], 0, 0))],
        out_specs=pl.BlockSpec((BM * PACK_ROWS, LANES), lambda j, be, bi, *_: (bi[j], 0)),
        scratch_shapes=[pltpu.VMEM((D_MODEL, 2 * D_FF), F32), pltpu.VMEM((D_FF, D_MODEL), F32),
                        pltpu.VMEM((D_MODEL, 2 * D_FF), BF16), pltpu.VMEM((D_FF, D_MODEL), BF16),
                        pltpu.SemaphoreType.DMA((2,))],
    )
    return pl.pallas_call(
        _experts_kernel,
        grid_spec=grid_spec,
        out_shape=jax.ShapeDtypeStruct(xb.shape, xb.dtype),
        input_output_aliases={5: 0},
        compiler_params=_cparams(("arbitrary",), VMEM_LIMIT),
        name="experts",
    )(blk_e, blk_i, blk_nx, blk_nq, n_act, xb, w_up, b_up.reshape(N_EXPERTS, 1, 2 * D_FF), w_down,
      b_down.reshape(N_EXPERTS, 1, D_MODEL))


def _combine_kernel(nt, ntp, off_ref, cnt_ref, base_ref, tot_ref, yb_ref, cols_ref, x1_hbm, gf_ref,
                    op_ref, os_ref, ys_scr, sems, x_ring, x_sems):
    i = pl.program_id(0)
    cur = i % 2

    def x_copy(step):
        slot = step % X_RING
        src = x1_hbm.at[pl.ds(pl.multiple_of(step * TM, TM), TM), :]
        return pltpu.make_async_copy(src, x_ring.at[slot], x_sems.at[slot])

    @pl.when(i == 0)
    def _():
        for step in range(min(X_RING - 1, nt)):
            x_copy(step).start()

    @pl.when(i + X_RING - 1 < nt)
    def _():
        x_copy(i + X_RING - 1).start()

    sorted_rows = lambda r, z: _packed_rows(yb_ref, (), r, z)

    def start_tile(t, buf):
        for part in range(PARTS):
            _start_segments(PARTS * t + part, cnt_ref, off_ref, base_ref,
                            lambda r, z, part=part: _packed_rows(ys_scr, (buf, part), r, z),
                            sorted_rows, sems.at[buf], False)

    @pl.when(i == 0)
    def _():
        ys_scr[...] = jnp.zeros_like(ys_scr)
        start_tile(0, 0)

    def wait_tile(t, buf):
        for part in range(PARTS):
            _rows_copy(tot_ref[PARTS * t + part], lambda z: sorted_rows(0, z),
                       lambda z, part=part: _packed_rows(ys_scr, (buf, part), 0, z), sems.at[buf]).wait()

    nxt = jnp.minimum(i + 1, nt - 1)
    wait_tile(i, cur)
    start_tile(nxt, 1 - cur)

    @pl.when(i == nt - 1)
    def _():
        wait_tile(nxt, 1 - cur)

    lane = lax.broadcasted_iota(I32, (MT, CAP), 1)
    ys = []
    for part in range(PARTS):
        cols = cols_ref[part]
        w = jnp.zeros((MT, CAP), F32)
        for k in range(TOP_K):
            sk = cols[:, k:k + 1].astype(I32)
            gk = cols[:, TOP_K + k:TOP_K + k + 1]
            w = jnp.where(lane == sk, gk, w)
        ys.append(jnp.dot(w.astype(BF16), _load_packed(ys_scr, (cur, part), 0, CAP),
                          preferred_element_type=F32))
    x_copy(i).wait()
    out = x_ring[i % X_RING] + _per_group(jnp.concatenate(ys, axis=0), lambda a, g: a * g, gf_ref[...])

    @pl.when(i < ntp)
    def _():
        op_ref[...] = out

    @pl.when(i >= ntp)
    def _():
        os_ref[...] = out


def _combine(yb, cols, x1, gate_f, off, cnt, base, tot, ntp, nb, tps):
    nt = tot.shape[0] // PARTS
    grid_spec = pltpu.PrefetchScalarGridSpec(
        num_scalar_prefetch=4,
        grid=(nt,),
        in_specs=[pl.BlockSpec(memory_space=pl.ANY),
                  pl.BlockSpec((PARTS, MT, 128), lambda i, *_: (i, 0, 0)),
                  pl.BlockSpec(memory_space=pl.ANY),
                  pl.BlockSpec((GROUPS_PER_TILE, D_MODEL), _mod_row(ntp, tps, nb))],
        out_specs=[pl.BlockSpec((TM, D_MODEL), lambda i, *_: (jnp.minimum(i, ntp - 1), 0)),
                   pl.BlockSpec((TM, D_MODEL), lambda i, *_: (0, 0))],
        scratch_shapes=[pltpu.VMEM((2, PARTS, CAP * PACK_ROWS, LANES), U32),
                        pltpu.SemaphoreType.DMA((2,)),
                        pltpu.VMEM((X_RING, TM, D_MODEL), F32), pltpu.SemaphoreType.DMA((X_RING,))],
    )
    return pl.pallas_call(
        functools.partial(_combine_kernel, nt, ntp),
        grid_spec=grid_spec,
        out_shape=[jax.ShapeDtypeStruct((ntp * TM, D_MODEL), F32),
                   jax.ShapeDtypeStruct((TM, D_MODEL), F32)],
        compiler_params=_cparams(("arbitrary",), VMEM_LIMIT),
        name="combine",
    )(off, cnt, base, tot, yb, cols, x1, gate_f)


def _rotary_tables(seq, dec_batch, dec_seq):
    half = HEAD_DIM_RET // 2
    inv = ROPE_BASE ** (-np.arange(half, dtype=np.float64) / half)
    pos = np.concatenate([np.arange(seq), np.tile(PAST_LEN + np.arange(dec_seq), 2 * dec_batch)])
    ang = pos.astype(np.float64)[:, None] * inv[None, :]
    cos = np.concatenate([np.cos(ang), np.cos(ang)], axis=1)
    sin = np.concatenate([-np.sin(ang), np.sin(ang)], axis=1)
    return jnp.asarray(cos, F32), jnp.asarray(sin, F32)


def _rel_bias_reversed(rel_bias):
    heads = rel_bias.shape[0]
    ext = jnp.concatenate([rel_bias[:, 1:], jnp.broadcast_to(rel_bias[:, -1:], (heads, 2 * MAX_REL))], axis=1)
    return ext[:, ::-1].astype(F32)


def _group_mods(m, nb, ndb):
    assert ndb == GROUPS_PER_TILE
    mp = jnp.broadcast_to(m[:nb, None], (nb, GROUPS_PER_TILE) + m.shape[1:])
    allm = jnp.concatenate([mp.reshape((nb * GROUPS_PER_TILE,) + m.shape[1:]), m[nb:]], axis=0)
    return jnp.transpose(allm, (1, 0, 2))


def _routing_tables(cnt, n_blocks):
    nt = cnt.shape[0]
    off = jnp.cumsum(cnt, axis=1) - cnt
    rows_e = jnp.sum(cnt, axis=0)
    nblk_e = (rows_e + BM - 1) // BM
    blk_end = jnp.cumsum(nblk_e)
    start_e = (blk_end - nblk_e) * BM
    base = start_e[None, :] + jnp.cumsum(cnt, axis=0) - cnt
    n_act = blk_end[-1]
    j = jnp.minimum(jnp.arange(n_blocks), n_act - 1)
    blk_e = jnp.minimum(jnp.sum(blk_end[None, :] <= j[:, None], axis=1), N_EXPERTS - 1)
    later = jnp.where(blk_e[None, :] > blk_e[:, None], blk_e[None, :], N_EXPERTS)
    blk_nx = jnp.min(later, axis=1)
    blk_nx = jnp.where(blk_nx == N_EXPERTS, blk_e, blk_nx)
    tail0 = start_e + rows_e
    tailn = nblk_e * BM - rows_e
    mine = blk_e[:, None] == jnp.arange(N_EXPERTS)[None, :]
    blk_rows = jnp.sum(jnp.where(mine, (rows_e + start_e)[None, :], 0), axis=1) - j * BM
    quarter = BM // BLOCK_QUARTERS
    blk_nq = jnp.clip((blk_rows + quarter - 1) // quarter, 1, BLOCK_QUARTERS)
    i32 = lambda a: a.astype(I32)
    return (i32(off.reshape(nt * N_EXPERTS)), i32(cnt.reshape(nt * N_EXPERTS)),
            i32(base.reshape(nt * N_EXPERTS)), i32(jnp.sum(cnt, axis=1)), i32(tail0), i32(tailn),
            i32(blk_e), i32(j), i32(blk_nx), i32(blk_nq), i32(n_act.reshape(1)))


def kernel(x_prompt, x_sample, c_prompt, c_sample, cache_att_k, cache_att_v, state_ret, w_ada, b_ada,
           g_norm_mix, g_norm_ffn, w_in, g_q, g_k, rel_bias, g_ret_out, w_out, w_router, b_router,
           w_up, b_up, w_down, b_down):
    nb, seq, d = x_prompt.shape
    ndb, dseq, _ = x_sample.shape
    assert d == D_MODEL and ndb * dseq == TM and dseq == CHUNK
    assert seq % TM == 0 and seq >= ATT_WINDOW and cache_att_k.shape[2] == ATT_WINDOW
    assert w_ada.shape[0] == 1
    rp = nb * seq
    ntp = rp // TM
    nt = ntp + 1
    tps = seq // TM

    xp = x_prompt.reshape(rp, d)
    xs = x_sample.reshape(TM, d)

    m = _ada(jnp.concatenate([c_prompt, c_sample], axis=0), w_ada[0], b_ada[0])
    mods = _group_mods(m.reshape(nb + ndb, N_ADA, d), nb, ndb)
    shift_m, scale_m, gate_m, shift_f, scale_f, gate_f = [mods[a] for a in range(N_ADA)]

    cos_t, sin_t = _rotary_tables(seq, ndb, dseq)
    bd = jnp.asarray(np.kron(np.eye(N_HEADS_ATT // 2), np.ones((HEAD_DIM_ATT, HEAD_DIM_ATT))), BF16)
    tile8 = lambda g: jnp.tile(g.astype(F32), N_HEADS_ATT).reshape(1, GROUP_W)
    (qa, ka_t, va, qb, kb, vb, gb, kp_tail, vp_tail, ks_new, vs_new) = _inproj(
        xp, xs, shift_m, scale_m, g_norm_mix[0].reshape(1, d), w_in[0].astype(BF16), bd,
        tile8(g_q[0]) * (HEAD_DIM_ATT ** -0.5 * LOG2_E), tile8(g_k[0]), cos_t, sin_t, nb, tps)

    rev = _rel_bias_reversed(rel_bias[0])
    g_ro = g_ret_out[0].astype(F32).reshape(1, GROUP_W)
    zero_state = jnp.zeros((nb, N_HEADS_RET, HEAD_DIM_RET, HEAD_DIM_RET), F32)
    att_p, ret_p, state_p = _mix_prompt(qa, ka_t, va, rev, qb, kb, vb, gb, zero_state, g_ro, nb, seq)
    att_s = _attn_sample(qa, ks_new, vs_new,
                         cache_att_k[0].reshape(ndb, ATT_WINDOW, GROUP_W).astype(BF16),
                         cache_att_v[0].reshape(ndb, ATT_WINDOW, GROUP_W).astype(BF16), rev, rp)

    ret_s, state_s = _ret(qb, kb, vb, gb, state_ret[0].astype(F32), g_ro, CHUNK, rp, ndb, 1, "ret_sample")

    upper = jnp.asarray(np.triu(np.ones((MT, MT)), 1), BF16)
    lower = jnp.asarray(np.tril(np.ones((N_EXPERTS, N_EXPERTS)), -1), BF16)
    x1, h2, slot, cols, cnt = _outproj(
        att_p, att_s, ret_p, ret_s, xp, xs, gate_m, shift_f, scale_f, g_norm_ffn[0].reshape(1, d),
        w_out[0].astype(BF16), w_router[0].T.astype(BF16), b_router[0].astype(F32).reshape(N_EXPERTS, 1),
        upper, lower, nb, tps)

    ntm = nt * (TM // MT)
    n_blocks = (TOP_K * (rp + TM) + ntm * N_EXPERTS * SEG_ALIGN) // BM + 1 + N_EXPERTS
    (off, cntf, base, tot, tail0, tailn, blk_e, blk_i, blk_nx, blk_nq,
     n_act) = _routing_tables(cnt[:ntm, :, 0], n_blocks)
    xb = _dispatch(h2, slot, off, cntf, base, tot, tail0, tailn, n_act, n_blocks)
    yb = _experts(xb, blk_e, blk_i, blk_nx, blk_nq, n_act, w_up[0], b_up[0], w_down[0], b_down[0])
    out_p, out_s = _combine(yb, cols, x1, gate_f, off, cntf, base, tot, ntp, nb, tps)

    heads = (N_HEADS_ATT, HEAD_DIM_ATT)
    return (out_p.reshape(nb, seq, d), out_s.reshape(ndb, dseq, d),
            kp_tail.reshape(1, nb, ATT_WINDOW, *heads), vp_tail.reshape(1, nb, ATT_WINDOW, *heads),
            state_p[None],
            ks_new.reshape(1, ndb, dseq, *heads), vs_new.reshape(1, ndb, dseq, *heads),
            state_s[None])
```

```python
import functools

import numpy as np
import jax
import jax.numpy as jnp
from jax import lax
from jax.experimental import pallas as pl
from jax.experimental.pallas import tpu as pltpu

F32 = jnp.float32
BF16 = jnp.bfloat16
I32 = jnp.int32
U32 = jnp.uint32

D_MODEL = 1024
GROUP_W = 512
N_SLOTS = 7
N_HEADS_ATT = 8
HEAD_DIM_ATT = 64
N_HEADS_RET = 4
HEAD_DIM_RET = 128
CHUNK = 64
ATT_WINDOW = 512
MAX_REL = 256
PAST_LEN = 2048
RET_DECAY_OFFSET = 5.0
ROPE_BASE = 10000.0
N_EXPERTS = 32
TOP_K = 4
D_FF = 1024
SWIGLU_LIMIT = 7.0
SWIGLU_ALPHA = 1.702
N_ADA = 6
NORM_EPS = 1e-6
NEG_INF = -1e30
LOG2_E = 1.4426950408889634

TM = 512
GROUPS_PER_TILE = TM // CHUNK
ATT_QB = 256
RET_CB = 256
LANES = 128
PACK_ROWS = D_MODEL // (2 * LANES)
SUBLANES_32 = 8
SEG_ALIGN = SUBLANES_32 // PACK_ROWS
MT = 256
PARTS = TM // MT
CAP_USED = TOP_K * MT + N_EXPERTS * SEG_ALIGN
CAP = -(-CAP_USED // LANES) * LANES
X_RING = 3
BM = 512
BLOCK_QUARTERS = 4
VMEM_LIMIT = 56 * 1024 * 1024


def _cparams(sem, vmem=None):
    return pltpu.CompilerParams(dimension_semantics=sem, vmem_limit_bytes=vmem)


def _ada_kernel(c_ref, w_ref, b_ref, o_ref):
    c = c_ref[...]
    s = c * jax.nn.sigmoid(c)
    o_ref[...] = jnp.dot(s.astype(BF16), w_ref[...].astype(BF16),
                         preferred_element_type=F32) + b_ref[...]


def _ada(c_all, w_ada, b_ada):
    n, d = c_all.shape
    cols = w_ada.shape[1]
    tn = 1536
    return pl.pallas_call(
        _ada_kernel,
        grid=(cols // tn,),
        in_specs=[pl.BlockSpec((n, d), lambda j: (0, 0)),
                  pl.BlockSpec((d, tn), lambda j: (0, j)),
                  pl.BlockSpec((1, tn), lambda j: (0, j))],
        out_specs=pl.BlockSpec((n, tn), lambda j: (0, j)),
        out_shape=jax.ShapeDtypeStruct((n, cols), F32),
        compiler_params=_cparams(("arbitrary",), VMEM_LIMIT),
        name="ada",
    )(c_all, w_ada, b_ada.reshape(1, cols))


def _rms_rows(x, g):
    ms = jnp.mean(x * x, axis=-1, keepdims=True)
    return x * lax.rsqrt(ms + NORM_EPS) * g


def _mod_row(ntp, tps, nb):
    return lambda i, *_: (jnp.where(i < ntp, i // tps, nb), 0)


def _per_group(x, fn, *mods):
    x3 = x.reshape(x.shape[0] // CHUNK, CHUNK, x.shape[-1])
    y3 = fn(x3, *[m[:, None, :] for m in mods])
    return y3.reshape(x.shape)


def _inproj_kernel(npp, xp_ref, xs_ref, sh_ref, sc_ref, gn_ref, w_ref, bd_ref, gq_ref, gk_ref,
                   cos_ref, sin_ref,
                   qa_ref, ka_ref, va_ref, qb_ref, kb_ref, vb_ref, gb_ref,
                   kpt_ref, vpt_ref, kst_ref, vst_ref):
    is_p = pl.program_id(0) < npp
    subs = range(2)
    rows = [slice(sub * TM, (sub + 1) * TM) for sub in subs]

    hb = []
    for sub in subs:
        x = jnp.where(is_p, xp_ref[rows[sub], :], xs_ref[...])
        y = _rms_rows(x, gn_ref[...])
        h = _per_group(y, lambda a, sh, sc: a * (1.0 + sc) + sh, sh_ref[...], sc_ref[...])
        hb.append(h.astype(BF16))

    def proj(s):
        return [jnp.dot(hb[sub], w_ref[:, s * GROUP_W:(s + 1) * GROUP_W], preferred_element_type=F32)
                for sub in subs]

    def head_rms(z, g):
        zz = (z * z).astype(BF16)
        half = GROUP_W // 2
        ss = jnp.concatenate(
            [jnp.dot(zz[:, :half], bd_ref[...], preferred_element_type=F32),
             jnp.dot(zz[:, half:], bd_ref[...], preferred_element_type=F32)], axis=1)
        return z * lax.rsqrt(ss * (1.0 / HEAD_DIM_ATT) + NORM_EPS) * g

    def rot(z, sub):
        cos = cos_ref[rows[sub], :]
        sin = sin_ref[rows[sub], :]
        outs = []
        for hh in range(N_HEADS_RET):
            zh = z[:, hh * HEAD_DIM_RET:(hh + 1) * HEAD_DIM_RET]
            outs.append(zh * cos + pltpu.roll(zh, HEAD_DIM_RET // 2, axis=1) * sin)
        return jnp.concatenate(outs, axis=1)

    for sub, z in zip(subs, proj(0)):
        qa_ref[rows[sub], :] = head_rms(z, gq_ref[...]).astype(BF16)
    ka = [head_rms(z, gk_ref[...]) for z in proj(1)]
    for sub in subs:
        ka_ref[:, rows[sub]] = ka[sub].T.astype(BF16)
    va = proj(2)
    for sub in subs:
        va_ref[rows[sub], :] = va[sub].astype(BF16)

    @pl.when(is_p)
    def _():
        kpt_ref[...] = ka[1]
        vpt_ref[...] = va[1]

    @pl.when(jnp.logical_not(is_p))
    def _():
        kst_ref[...] = ka[0]
        vst_ref[...] = va[0]

    for sub, z in zip(subs, proj(3)):
        qb_ref[rows[sub], :] = rot(z, sub).astype(BF16)
    for sub, z in zip(subs, proj(4)):
        kb_ref[rows[sub], :] = (rot(z, sub) * (HEAD_DIM_RET ** -0.5)).astype(BF16)
    for sub, z in zip(subs, proj(5)):
        vb_ref[rows[sub], :] = z.astype(BF16)
    for sub, z in zip(subs, proj(6)):
        gb_ref[rows[sub], :] = z.astype(BF16)


def _inproj(xp, xs, shift, scale, g_norm, w_in_b, bd, gq8, gk8, cos_t, sin_t, nb, tps):
    rp = xp.shape[0]
    ntp = rp // TM
    assert ntp % 2 == 0 and tps % 2 == 0
    npp = ntp // 2
    r = (ntp + 2) * TM
    row = lambda p: (p, 0)
    full = lambda p: (0, 0)
    tab = lambda p: (jnp.where(p < npp, p % (tps // 2), tps // 2), 0)
    mod = pl.BlockSpec((GROUPS_PER_TILE, D_MODEL), _mod_row(npp, tps // 2, nb))
    tail_spec = pl.BlockSpec((TM, GROUP_W), lambda p: (jnp.minimum(p // (tps // 2), nb - 1), 0))
    act = jax.ShapeDtypeStruct((r, GROUP_W), BF16)
    return pl.pallas_call(
        functools.partial(_inproj_kernel, npp),
        grid=(npp + 1,),
        in_specs=[pl.BlockSpec((2 * TM, D_MODEL), lambda p: (jnp.minimum(p, npp - 1), 0)),
                  pl.BlockSpec((TM, D_MODEL), full),
                  mod, mod,
                  pl.BlockSpec((1, D_MODEL), full),
                  pl.BlockSpec((D_MODEL, N_SLOTS * GROUP_W), full),
                  pl.BlockSpec((GROUP_W // 2, GROUP_W // 2), full),
                  pl.BlockSpec((1, GROUP_W), full),
                  pl.BlockSpec((1, GROUP_W), full),
                  pl.BlockSpec((2 * TM, HEAD_DIM_RET), tab),
                  pl.BlockSpec((2 * TM, HEAD_DIM_RET), tab)],
        out_specs=[pl.BlockSpec((2 * TM, GROUP_W), row), pl.BlockSpec((GROUP_W, 2 * TM), lambda p: (0, p))]
        + [pl.BlockSpec((2 * TM, GROUP_W), row)] * 5 + [
            tail_spec, tail_spec,
            pl.BlockSpec((TM, GROUP_W), full),
            pl.BlockSpec((TM, GROUP_W), full)],
        out_shape=[act, jax.ShapeDtypeStruct((GROUP_W, r), BF16)] + [act] * 5
        + [jax.ShapeDtypeStruct((nb * TM, GROUP_W), F32)] * 2
        + [jax.ShapeDtypeStruct((TM, GROUP_W), F32)] * 2,
        compiler_params=_cparams(("arbitrary",), VMEM_LIMIT),
        name="inproj",
    )(xp, xs, shift, scale, g_norm, w_in_b, bd, gq8, gk8, cos_t, sin_t)


def _attn_heads(q, k, v, bias_ref, first_valid_col=None):
    qb_rows, kb_rows = q.shape[0], v.shape[0]
    assert qb_rows == 4 * CHUNK
    half_rows, span = qb_rows // 2, kb_rows - 2 * CHUNK
    parts = [(0, 0), (half_rows, 2 * CHUNK)]

    def softmax_part(s_full, hh, half, r0, c0):
        rs = half * qb_rows + r0
        s = s_full[rs:rs + half_rows, c0:c0 + span] + bias_ref[hh, r0:r0 + half_rows, c0:c0 + span]
        if first_valid_col is not None:
            col = lax.broadcasted_iota(I32, (half_rows, span), 1) + c0
            s = jnp.where(col >= first_valid_col, s, NEG_INF)
        m = jnp.max(s, axis=-1, keepdims=True)
        e = jnp.exp2(s - m)
        l = jnp.sum(e, axis=-1, keepdims=True)
        pad = [jnp.zeros((half_rows, c0), BF16)] if c0 else []
        pad_r = [jnp.zeros((half_rows, kb_rows - span - c0), BF16)] if kb_rows - span - c0 else []
        return jnp.concatenate(pad + [e.astype(BF16)] + pad_r, axis=1), l

    pair_w = 2 * HEAD_DIM_ATT
    low = lax.broadcasted_iota(I32, (1, pair_w), 1) < HEAD_DIM_ATT
    outs = []
    for pp in range(N_HEADS_ATT // 2):
        ps = slice(pp * pair_w, (pp + 1) * pair_w)
        q2, v2 = q[:, ps], v[:, ps]
        zero = jnp.zeros_like(q2)
        qs = jnp.concatenate([jnp.where(low, q2, zero), jnp.where(low, zero, q2)], axis=0)
        s = jnp.dot(qs, k[ps, :], preferred_element_type=F32)
        es, ls = zip(*[softmax_part(s, 2 * pp + half, half, r0, c0)
                       for half in range(2) for r0, c0 in parts])
        o = jnp.dot(jnp.concatenate(es, axis=0), v2, preferred_element_type=F32) / jnp.concatenate(ls, axis=0)
        outs.append(jnp.where(low, o[:qb_rows], o[qb_rows:]))
    return jnp.concatenate(outs, axis=1)


def _fill_band_bias(rev_ref, bias_scr):
    _, qb_rows, kb_rows = bias_scr.shape
    width = rev_ref.shape[1]
    q = lax.broadcasted_iota(I32, (qb_rows, kb_rows), 0)
    k = lax.broadcasted_iota(I32, (qb_rows, kb_rows), 1)
    qc = q >> 6
    kc = (k - ATT_WINDOW) >> 6
    band = (kc >= qc - ATT_WINDOW // CHUNK) & (kc <= qc)
    for hh in range(N_HEADS_ATT):
        rows = jnp.broadcast_to(rev_ref[hh:hh + 1, :], (qb_rows, width))
        toep = pltpu.roll(rows, width - MAX_REL, 1, stride=1, stride_axis=0)
        bias_scr[hh] = jnp.where(band, toep[:, :kb_rows] * LOG2_E, NEG_INF)


def _attn_sample_kernel(q_ref, kn_ref, vn_ref, kc_ref, vc_ref, rev_ref, o_ref, bias_scr):
    @pl.when(pl.program_id(0) == 0)
    def _():
        _fill_band_bias(rev_ref, bias_scr)

    q = q_ref[...]
    outs = []
    for hh in range(N_HEADS_ATT):
        hs = slice(hh * HEAD_DIM_ATT, (hh + 1) * HEAD_DIM_ATT)
        k = jnp.concatenate([kc_ref[0, :, hs], kn_ref[:, hs].astype(BF16)], axis=0)
        v = jnp.concatenate([vc_ref[0, :, hs], vn_ref[:, hs].astype(BF16)], axis=0)
        s = lax.dot_general(q[:, hs], k, (((1,), (1,)), ((), ())), preferred_element_type=F32)
        s = s + bias_scr[hh]
        m = jnp.max(s, axis=-1, keepdims=True)
        e = jnp.exp2(s - m)
        l = jnp.sum(e, axis=-1, keepdims=True)
        outs.append(jnp.dot(e.astype(BF16), v, preferred_element_type=F32) / l)
    o_ref[...] = jnp.concatenate(outs, axis=1).astype(BF16)


def _attn_sample(qa, ks_new, vs_new, kc, vc, rev, rp):
    ndb = kc.shape[0]
    base = rp // CHUNK
    spec = pl.BlockSpec((CHUNK, GROUP_W), lambda b: (base + b, 0))
    new = pl.BlockSpec((CHUNK, GROUP_W), lambda b: (b, 0))
    cspec = pl.BlockSpec((1, ATT_WINDOW, GROUP_W), lambda b: (b, 0, 0))
    return pl.pallas_call(
        _attn_sample_kernel,
        grid=(ndb,),
        in_specs=[spec, new, new, cspec, cspec,
                  pl.BlockSpec(rev.shape, lambda b: (0, 0))],
        out_specs=pl.BlockSpec((CHUNK, GROUP_W), lambda b: (b, 0)),
        out_shape=jax.ShapeDtypeStruct((ndb * CHUNK, GROUP_W), BF16),
        scratch_shapes=[pltpu.VMEM((N_HEADS_ATT, CHUNK, ATT_WINDOW + CHUNK), F32)],
        compiler_params=_cparams(("arbitrary",), VMEM_LIMIT),
        name="attn_sample",
    )(qa, ks_new, vs_new, kc, vc, rev)


def _ret_chunk(state_decay, q_ref, k_ref, v_ref, g_ref, dm_ref, xi_ref, zeta_ref, gro_ref, o_ref, s_scr):
    outs = []
    for hh in range(N_HEADS_RET):
        hs = slice(hh * HEAD_DIM_RET, (hh + 1) * HEAD_DIM_RET)
        q = q_ref[:, hs]
        k = k_ref[:, hs]
        v = v_ref[:, hs]
        st = s_scr[hh]
        sc = lax.dot_general(q, k, (((1,), (1,)), ((), ())), preferred_element_type=F32) * dm_ref[hh]
        inner = jnp.dot(sc.astype(BF16), v, preferred_element_type=F32)
        cross = jnp.dot(q, st.astype(BF16), preferred_element_type=F32) * xi_ref[:, hs]
        o = inner + cross
        kz = k.astype(F32) * zeta_ref[:, hs]
        s_scr[hh] = state_decay[hh] * st + jnp.dot(kz.T.astype(BF16), v, preferred_element_type=F32)
        mu = jnp.mean(o, axis=-1, keepdims=True)
        oc = o - mu
        var = jnp.mean(oc * oc, axis=-1, keepdims=True)
        outs.append(oc * lax.rsqrt(var + NORM_EPS))
    y = jnp.concatenate(outs, axis=1) * gro_ref[...]
    g = g_ref[...].astype(F32)
    o_ref[...] = (g * jax.nn.sigmoid(g) * y).astype(BF16)


def _ret_kernel(state_decay, q_ref, k_ref, v_ref, g_ref, s0_ref, dm_ref, xi_ref, zeta_ref,
                gro_ref, o_ref, sn_ref, s_scr):
    j = pl.program_id(1)

    @pl.when(j == 0)
    def _():
        s_scr[...] = s0_ref[0]

    _ret_chunk(state_decay, q_ref, k_ref, v_ref, g_ref, dm_ref, xi_ref, zeta_ref, gro_ref, o_ref, s_scr)

    @pl.when(j == pl.num_programs(1) - 1)
    def _():
        sn_ref[0] = s_scr[...]


def _mix_prompt_kernel(state_decay, q_ref, k0_ref, k1_ref, k2_ref, v0_ref, v1_ref, v2_ref, rev_ref,
                       rq_ref, rk_ref, rv_ref, rg_ref, s0_ref, dm_ref, xi_ref, zeta_ref, gro_ref,
                       att_ref, ret_ref, sn_ref, bias_scr, s_scr):
    j = pl.program_id(1)

    @pl.when((pl.program_id(0) == 0) & (j == 0))
    def _():
        _fill_band_bias(rev_ref, bias_scr)

    @pl.when(j == 0)
    def _():
        s_scr[...] = s0_ref[0]

    k = jnp.concatenate([k0_ref[...], k1_ref[...], k2_ref[...]], axis=1)
    v = jnp.concatenate([v0_ref[...], v1_ref[...], v2_ref[...]], axis=0)

    def block(first_valid_col):
        att_ref[...] = _attn_heads(q_ref[...], k, v, bias_scr, first_valid_col).astype(BF16)
        _ret_chunk(state_decay, rq_ref, rk_ref, rv_ref, rg_ref, dm_ref, xi_ref, zeta_ref, gro_ref,
                   ret_ref, s_scr)

    @pl.when(j >= 2)
    def _():
        block(None)

    @pl.when(j < 2)
    def _():
        block((2 - j) * ATT_QB)

    @pl.when(j == pl.num_programs(1) - 1)
    def _():
        sn_ref[0] = s_scr[...]


def _mix_prompt(qa, ka_t, va, rev, qb, kb, vb, gb, s0, g_ro, nb, seq):
    assert ATT_QB == RET_CB
    r = nb * seq
    nq = seq // ATT_QB
    dm, xi, zeta, state_decay = _ret_consts(RET_CB)
    blk = lambda back: (lambda b, j: (b * nq + jnp.maximum(j - back, 0), 0))
    spec = lambda back: pl.BlockSpec((ATT_QB, GROUP_W), blk(back))
    tspec = lambda back: pl.BlockSpec((GROUP_W, ATT_QB), lambda b, j: (0, b * nq + jnp.maximum(j - back, 0)))
    sspec = pl.BlockSpec((1, N_HEADS_RET, HEAD_DIM_RET, HEAD_DIM_RET), lambda b, j: (b, 0, 0, 0))
    full2 = lambda b, j: (0, 0)
    out = jax.ShapeDtypeStruct((r, GROUP_W), BF16)
    return pl.pallas_call(
        functools.partial(_mix_prompt_kernel, state_decay),
        grid=(nb, nq),
        in_specs=[spec(0), tspec(2), tspec(1), tspec(0), spec(2), spec(1), spec(0),
                  pl.BlockSpec(rev.shape, full2),
                  spec(0), spec(0), spec(0), spec(0), sspec,
                  pl.BlockSpec(dm.shape, lambda b, j: (0, 0, 0)),
                  pl.BlockSpec(xi.shape, full2), pl.BlockSpec(zeta.shape, full2),
                  pl.BlockSpec((1, GROUP_W), full2)],
        out_specs=[spec(0), spec(0), sspec],
        out_shape=[out, out, jax.ShapeDtypeStruct(s0.shape, F32)],
        scratch_shapes=[pltpu.VMEM((N_HEADS_ATT, ATT_QB, ATT_WINDOW + ATT_QB), F32),
                        pltpu.VMEM((N_HEADS_RET, HEAD_DIM_RET, HEAD_DIM_RET), F32)],
        compiler_params=_cparams(("arbitrary", "arbitrary"), VMEM_LIMIT),
        name="mix_prompt",
    )(qa, ka_t, ka_t, ka_t, va, va, va, rev, qb, kb, vb, gb, s0, dm, xi, zeta, g_ro)


def _ret_consts(cb):
    log_g = np.log1p(-np.exp2(-RET_DECAY_OFFSET - np.arange(N_HEADS_RET, dtype=np.float64)))
    n = np.arange(cb, dtype=np.float64)
    diff = n[:, None] - n[None, :]
    dm = np.where(diff[None] >= 0, np.exp(np.maximum(diff, 0.0)[None] * log_g[:, None, None]), 0.0)
    xi = np.exp((n + 1.0)[:, None] * log_g[None, :])
    zeta = np.exp((cb - 1.0 - n)[:, None] * log_g[None, :])
    rep = lambda a: np.repeat(a, HEAD_DIM_RET, axis=1)
    state_decay = tuple(float(v) for v in np.exp(cb * log_g))
    return (jnp.asarray(dm, F32), jnp.asarray(rep(xi), F32), jnp.asarray(rep(zeta), F32), state_decay)


def _ret(qb, kb, vb, gb, s0, g_ro, cb, row0, nb, nc, name):
    dm, xi, zeta, state_decay = _ret_consts(cb)
    base = row0 // cb
    spec = pl.BlockSpec((cb, GROUP_W), lambda b, j: (base + b * nc + j, 0))
    sspec = pl.BlockSpec((1, N_HEADS_RET, HEAD_DIM_RET, HEAD_DIM_RET), lambda b, j: (b, 0, 0, 0))
    full2 = lambda b, j: (0, 0)
    return pl.pallas_call(
        functools.partial(_ret_kernel, state_decay),
        grid=(nb, nc),
        in_specs=[spec, spec, spec, spec, sspec,
                  pl.BlockSpec(dm.shape, lambda b, j: (0, 0, 0)),
                  pl.BlockSpec(xi.shape, full2), pl.BlockSpec(zeta.shape, full2),
                  pl.BlockSpec((1, GROUP_W), full2)],
        out_specs=[pl.BlockSpec((cb, GROUP_W), lambda b, j: (b * nc + j, 0)), sspec],
        out_shape=[jax.ShapeDtypeStruct((nb * nc * cb, GROUP_W), BF16),
                   jax.ShapeDtypeStruct(s0.shape, F32)],
        scratch_shapes=[pltpu.VMEM((N_HEADS_RET, HEAD_DIM_RET, HEAD_DIM_RET), F32)],
        compiler_params=_cparams(("arbitrary", "arbitrary"), VMEM_LIMIT),
        name=name,
    )(qb, kb, vb, gb, s0, dm, xi, zeta, g_ro)


def _outproj_kernel(npp, attp_hbm, atts_ref, retp_hbm, rets_ref, xp_hbm, xs_ref, gm_ref, shf_ref, scf_ref,
                    gn_ref, wo_ref, wr_ref, br_ref, upper_ref, lower_ref,
                    x1_ref, h2_ref, slot_ref, cols_ref, cnt_ref, x_ring, x_sems, a_ring, a_sems, r_ring, r_sems):
    p = pl.program_id(0)
    is_p = p < npp
    subs = range(2)
    rows = [slice(sub * TM, (sub + 1) * TM) for sub in subs]

    def x_copies(step):
        slot = step % X_RING
        rows_in = pl.ds(pl.multiple_of(step * 2 * TM, 2 * TM), 2 * TM)
        return [pltpu.make_async_copy(hbm.at[rows_in, :], ring.at[slot], sem.at[slot])
                for hbm, ring, sem in ((xp_hbm, x_ring, x_sems), (attp_hbm, a_ring, a_sems),
                                       (retp_hbm, r_ring, r_sems))]

    @pl.when(p == 0)
    def _():
        for step in range(min(X_RING - 1, npp)):
            for cp in x_copies(step):
                cp.start()

    @pl.when(p + X_RING - 1 < npp)
    def _():
        for cp in x_copies(p + X_RING - 1):
            cp.start()

    @pl.when(is_p)
    def _():
        for cp in x_copies(p):
            cp.wait()

    xp_ref, attp_ref, retp_ref = (ring.at[p % X_RING] for ring in (x_ring, a_ring, r_ring))

    def pick(p_ref, s_ref, sub):
        return jnp.where(is_p, p_ref[rows[sub], :], s_ref[...])

    mix = [jnp.dot(pick(attp_ref, atts_ref, sub), wo_ref[:GROUP_W, :], preferred_element_type=F32)
           + jnp.dot(pick(retp_ref, rets_ref, sub), wo_ref[GROUP_W:, :], preferred_element_type=F32)
           for sub in subs]
    h2b = []
    for sub in subs:
        x1 = _per_group(mix[sub], lambda a, gm: a * gm, gm_ref[...]) + pick(xp_ref, xs_ref, sub)
        x1_ref[rows[sub], :] = x1
        y = _rms_rows(x1, gn_ref[...])
        h2 = _per_group(y, lambda a, sh, sc: a * (1.0 + sc) + sh, shf_ref[...], scf_ref[...])
        h2b.append(h2.astype(BF16))
        h2_ref[rows[sub], :] = h2b[sub]

    work = [lax.dot_general(wr_ref[...], h2b[sub], (((1,), (1,)), ((), ())),
                            preferred_element_type=F32) + br_ref[...] for sub in subs]
    eidx = lax.broadcasted_iota(I32, work[0].shape, 0).astype(F32)
    sel = [[] for _ in subs]
    top = [[] for _ in subs]
    for _ in range(TOP_K):
        for sub in subs:
            m = jnp.max(work[sub], axis=0, keepdims=True)
            idx = jnp.min(jnp.where(work[sub] == m, eidx, float(N_EXPERTS)), axis=0, keepdims=True)
            hit = eidx == idx
            sel[sub].append(hit)
            top[sub].append(m)
            work[sub] = jnp.where(hit, -jnp.inf, work[sub])

    for sub in subs:
        ex = [jnp.exp(t - top[sub][0]) for t in top[sub]]
        den = ex[0] + ex[1] + ex[2] + ex[3]
        gates = [e / den for e in ex]
        for part in range(TM // MT):
            lanes = slice(part * MT, (part + 1) * MT)
            tile = (TM // MT) * sub + part
            hits = [h[:, lanes] for h in sel[sub]]
            multi_f = jnp.where(hits[0] | hits[1] | hits[2] | hits[3], 1.0, 0.0)
            rank = jnp.dot(multi_f.astype(BF16), upper_ref[...], preferred_element_type=F32)
            cnt = jnp.sum(multi_f, axis=1, keepdims=True)
            cnt_pad = jnp.maximum(jnp.floor((cnt + (SEG_ALIGN - 1.0)) * (1.0 / SEG_ALIGN)), 1.0) * SEG_ALIGN
            cnt_pad_b = jnp.broadcast_to(cnt_pad, (N_EXPERTS, 128))
            seg_off = jnp.dot(lower_ref[...], cnt_pad_b.astype(BF16), preferred_element_type=F32)[:, :1]
            pos = seg_off + rank
            slot_rows = jnp.concatenate(
                [jnp.sum(jnp.where(h, pos, 0.0), axis=0, keepdims=True) for h in hits], axis=0)
            gate_rows = jnp.concatenate([g[:, lanes] for g in gates], axis=0)
            slot_ref[tile] = slot_rows.astype(I32)
            cnt_ref[tile] = cnt_pad_b.astype(I32)
            both = jnp.concatenate([slot_rows, gate_rows, jnp.zeros((128 - 2 * TOP_K, MT), F32)], axis=0)
            cols_ref[tile] = both.T


def _outproj(att_p, att_s, ret_p, ret_s, xp, xs, gate_m, shift_f, scale_f, g_norm, w_out_b, wr_t, br,
             upper, lower, nb, tps):
    rp = xp.shape[0]
    ntp = rp // TM
    assert ntp % 2 == 0 and tps % 2 == 0
    npp = ntp // 2
    r = (ntp + 2) * TM
    per_step = 2 * TM // MT
    ntm = (npp + 1) * per_step
    row = lambda p: (p, 0)
    row3 = lambda p: (p, 0, 0)
    full = lambda p: (0, 0)
    prow = lambda p: (jnp.minimum(p, npp - 1), 0)
    mod = pl.BlockSpec((GROUPS_PER_TILE, D_MODEL), _mod_row(npp, tps // 2, nb))
    return pl.pallas_call(
        functools.partial(_outproj_kernel, npp),
        grid=(npp + 1,),
        in_specs=[pl.BlockSpec(memory_space=pl.ANY), pl.BlockSpec((TM, GROUP_W), full),
                  pl.BlockSpec(memory_space=pl.ANY), pl.BlockSpec((TM, GROUP_W), full),
                  pl.BlockSpec(memory_space=pl.ANY),
                  pl.BlockSpec((TM, D_MODEL), full),
                  mod, mod, mod,
                  pl.BlockSpec((1, D_MODEL), full),
                  pl.BlockSpec((D_MODEL, D_MODEL), full),
                  pl.BlockSpec((N_EXPERTS, D_MODEL), full),
                  pl.BlockSpec((N_EXPERTS, 1), full),
                  pl.BlockSpec((MT, MT), full),
                  pl.BlockSpec((N_EXPERTS, N_EXPERTS), full)],
        out_specs=[pl.BlockSpec((2 * TM, D_MODEL), row), pl.BlockSpec((2 * TM, D_MODEL), row),
                   pl.BlockSpec((per_step, TOP_K, MT), row3),
                   pl.BlockSpec((per_step, MT, 128), row3), pl.BlockSpec((per_step, N_EXPERTS, 128), row3)],
        out_shape=[jax.ShapeDtypeStruct((r, D_MODEL), F32), jax.ShapeDtypeStruct((r, D_MODEL), BF16),
                   jax.ShapeDtypeStruct((ntm, TOP_K, MT), I32),
                   jax.ShapeDtypeStruct((ntm, MT, 128), F32), jax.ShapeDtypeStruct((ntm, N_EXPERTS, 128), I32)],
        scratch_shapes=[pltpu.VMEM((X_RING, 2 * TM, D_MODEL), F32), pltpu.SemaphoreType.DMA((X_RING,)),
                        pltpu.VMEM((X_RING, 2 * TM, GROUP_W), BF16), pltpu.SemaphoreType.DMA((X_RING,)),
                        pltpu.VMEM((X_RING, 2 * TM, GROUP_W), BF16), pltpu.SemaphoreType.DMA((X_RING,))],
        compiler_params=_cparams(("arbitrary",), VMEM_LIMIT),
        name="outproj",
    )(att_p, att_s, ret_p, ret_s, xp, xs, gate_m, shift_f, scale_f, g_norm, w_out_b, wr_t, br, upper, lower)


def _store_packed(ref, lead, row0, x):
    n = x.shape[0]
    for j in range(PACK_ROWS):
        c = 2 * LANES * j
        lo = lax.bitcast_convert_type(x[:, c:c + LANES].astype(BF16).astype(F32), U32) >> 16
        hi = lax.bitcast_convert_type(x[:, c + LANES:c + 2 * LANES].astype(BF16).astype(F32), U32)
        ref[lead + (pl.ds(PACK_ROWS * row0 + j, n, stride=PACK_ROWS), slice(None))] = hi | lo


def _load_packed(ref, lead, row0, n):
    parts = []
    for j in range(PACK_ROWS):
        u = ref[lead + (pl.ds(PACK_ROWS * row0 + j, n, stride=PACK_ROWS), slice(None))]
        parts.append(lax.bitcast_convert_type(u << 16, F32))
        parts.append(lax.bitcast_convert_type(u & jnp.uint32(0xFFFF0000), F32))
    return jnp.concatenate(parts, axis=1).astype(BF16)


def _packed_rows(ref, lead, tok0, ntok):
    rows = pl.ds(pl.multiple_of(tok0 * PACK_ROWS, SUBLANES_32), ntok * PACK_ROWS)
    return ref.at[lead + (rows, slice(None))]


def _rows_copy(n, src_rows, dst_rows, sem):
    size = pl.multiple_of(n, SEG_ALIGN)
    return pltpu.make_async_copy(src_rows(size), dst_rows(size), sem)


def _start_segments(t, cnt_ref, off_ref, base_ref, local_rows, sorted_rows, sem, to_sorted):
    for e in range(N_EXPERTS):
        n = cnt_ref[t * N_EXPERTS + e]
        off = pl.multiple_of(off_ref[t * N_EXPERTS + e], SEG_ALIGN)
        base = pl.multiple_of(base_ref[t * N_EXPERTS + e], SEG_ALIGN)
        local = lambda z, off=off: local_rows(off, z)
        remote = lambda z, base=base: sorted_rows(base, z)
        (_rows_copy(n, local, remote, sem) if to_sorted else _rows_copy(n, remote, local, sem)).start()


def _dispatch_kernel(nt, n_blocks, off_ref, cnt_ref, base_ref, tot_ref, tail0_ref, tailn_ref, na_ref,
                     h2_ref, slot_ref, slotn_ref, xb_ref, xs_scr, hot_scr, zero_scr, sems, tail_sem):
    i = pl.program_id(0)
    cur = i % 2
    sorted_rows = lambda r, z: _packed_rows(xb_ref, (), r, z)

    def start_tile(t, buf):
        for part in range(PARTS):
            _start_segments(PARTS * t + part, cnt_ref, off_ref, base_ref,
                            lambda r, z, part=part: _packed_rows(xs_scr, (buf, part), r, z),
                            sorted_rows, sems.at[buf], True)

    def wait_tile(t, buf):
        for part in range(PARTS):
            _rows_copy(tot_ref[PARTS * t + part], lambda z, part=part: _packed_rows(xs_scr, (buf, part), 0, z),
                       lambda z: sorted_rows(0, z), sems.at[buf]).wait()

    def tail_copies(wait):
        def body(e, c):
            base = pl.multiple_of(tail0_ref[e], SEG_ALIGN)

            @pl.when(tailn_ref[e] > 0)
            def _():
                cp = _rows_copy(tailn_ref[e], lambda z: _packed_rows(zero_scr, (), 0, z),
                                lambda z: sorted_rows(base, z), tail_sem)
                cp.wait() if wait else cp.start()
            return c
        lax.fori_loop(0, N_EXPERTS, body, 0)

        def unused(j, c):
            cp = pltpu.make_async_copy(zero_scr, sorted_rows(pl.multiple_of(j * BM, BM), BM), tail_sem)
            cp.wait() if wait else cp.start()
            return c
        lax.fori_loop(na_ref[0], n_blocks, unused, 0)

    @pl.when(i >= 2)
    def _():
        wait_tile(i - 2, cur)

    def onehot(slot):
        srow = lax.broadcasted_iota(I32, (CAP_USED, MT), 0)
        hit = (srow == slot[0:1]) | (srow == slot[1:2]) | (srow == slot[2:3]) | (srow == slot[3:4])
        return jnp.where(hit, 1.0, 0.0).astype(BF16)

    @pl.when(i == 0)
    def _():
        for part in range(PARTS):
            hot_scr[0, part] = onehot(slot_ref[part])
        zero_scr[...] = jnp.zeros_like(zero_scr)
        tail_copies(False)

    for part in range(PARTS):
        rows = h2_ref[part * MT:(part + 1) * MT, :]
        _store_packed(xs_scr, (cur, part), 0, jnp.dot(hot_scr[cur, part], rows, preferred_element_type=F32))
    for part in range(PARTS):
        hot_scr[1 - cur, part] = onehot(slotn_ref[part])
    start_tile(i, cur)

    @pl.when(i == nt - 1)
    def _():
        if nt >= 2:
            wait_tile(i - 1, 1 - cur)
        wait_tile(i, cur)
        tail_copies(True)


def _dispatch(h2, slot, off, cnt, base, tot, tail0, tailn, n_act, n_blocks):
    nt = tot.shape[0] // PARTS
    n_rows = n_blocks * BM
    grid_spec = pltpu.PrefetchScalarGridSpec(
        num_scalar_prefetch=7,
        grid=(nt,),
        in_specs=[pl.BlockSpec((TM, D_MODEL), lambda i, *_: (i, 0)),
                  pl.BlockSpec((PARTS, TOP_K, MT), lambda i, *_: (i, 0, 0)),
                  pl.BlockSpec((PARTS, TOP_K, MT), lambda i, *_: (jnp.minimum(i + 1, nt - 1), 0, 0))],
        out_specs=pl.BlockSpec(memory_space=pl.ANY),
        scratch_shapes=[pltpu.VMEM((2, PARTS, CAP_USED * PACK_ROWS, LANES), U32),
                        pltpu.VMEM((2, PARTS, CAP_USED, MT), BF16),
                        pltpu.VMEM((BM * PACK_ROWS, LANES), U32),
                        pltpu.SemaphoreType.DMA((2,)),
                        pltpu.SemaphoreType.DMA(())],
    )
    return pl.pallas_call(
        functools.partial(_dispatch_kernel, nt, n_blocks),
        grid_spec=grid_spec,
        out_shape=jax.ShapeDtypeStruct((n_rows * PACK_ROWS, LANES), U32),
        compiler_params=_cparams(("arbitrary",), VMEM_LIMIT),
        name="dispatch",
    )(off, cnt, base, tot, tail0, tailn, n_act, h2, slot, slot)


def _experts_kernel(be_ref, bi_ref, nx_ref, nq_ref, na_ref, x_ref, wu_hbm, bu_ref, wd_hbm, bd_ref, y_ref,
                    wu_stage, wd_stage, wu_scr, wd_scr, sems):
    j = pl.program_id(0)

    def weight_copies(e):
        return (pltpu.make_async_copy(wu_hbm.at[e], wu_stage, sems.at[0]),
                pltpu.make_async_copy(wd_hbm.at[e], wd_stage, sems.at[1]))

    @pl.when(j < na_ref[0])
    def _():
        e = be_ref[j]
        prev = be_ref[jnp.maximum(j - 1, 0)]

        @pl.when(j == 0)
        def _():
            for cp in weight_copies(e):
                cp.start()

        @pl.when((j == 0) | (e != prev))
        def _():
            for cp in weight_copies(e):
                cp.wait()
            wu_scr[...] = wu_stage[...].astype(BF16)
            wd_scr[...] = wd_stage[...].astype(BF16)

            @pl.when(nx_ref[j] != e)
            def _():
                for cp in weight_copies(nx_ref[j]):
                    cp.start()

        def ffn(n):
            x = _load_packed(x_ref, (), 0, n)
            u = jnp.dot(x, wu_scr[...], preferred_element_type=F32) + bu_ref[0]
            glu = jnp.minimum(u[:, :D_FF], SWIGLU_LIMIT)
            lin = jnp.clip(u[:, D_FF:], -SWIGLU_LIMIT, SWIGLU_LIMIT)
            act = glu * jax.nn.sigmoid(SWIGLU_ALPHA * glu) * (lin + 1.0)
            y = jnp.dot(act.astype(BF16), wd_scr[...], preferred_element_type=F32) + bd_ref[0]
            _store_packed(y_ref, (), 0, y)

        for quarters in range(1, BLOCK_QUARTERS + 1):
            @pl.when(nq_ref[j] == quarters)
            def _(quarters=quarters):
                n = quarters * (BM // BLOCK_QUARTERS)
                ffn(n)
                if n < BM:
                    y_ref[PACK_ROWS * n:, :] = jnp.zeros((PACK_ROWS * (BM - n), LANES), U32)


def _experts(xb, blk_e, blk_i, blk_nx, blk_nq, n_act, w_up, b_up, w_down, b_down):
    n_rows = xb.shape[0] // PACK_ROWS
    nblk = n_rows // BM
    grid_spec = pltpu.PrefetchScalarGridSpec(
        num_scalar_prefetch=5,
        grid=(nblk,),
        in_specs=[pl.BlockSpec((BM * PACK_ROWS, LANES), lambda j, be, bi, *_: (bi[j], 0)),
                  pl.BlockSpec(memory_space=pl.ANY),
                  pl.BlockSpec((1, 1, 2 * D_FF), lambda j, be, *_: (be[j], 0, 0)),
                  pl.BlockSpec(memory_space=pl.ANY),
                  pl.BlockSpec((1, 1, D_MODEL), lambda j, be, *_: (be[j], 0, 0))],
        out_specs=pl.BlockSpec((BM * PACK_ROWS, LANES), lambda j, be, bi, *_: (bi[j], 0)),
        scratch_shapes=[pltpu.VMEM((D_MODEL, 2 * D_FF), F32), pltpu.VMEM((D_FF, D_MODEL), F32),
                        pltpu.VMEM((D_MODEL, 2 * D_FF), BF16), pltpu.VMEM((D_FF, D_MODEL), BF16),
                        pltpu.SemaphoreType.DMA((2,))],
    )
    return pl.pallas_call(
        _experts_kernel,
        grid_spec=grid_spec,
        out_shape=jax.ShapeDtypeStruct(xb.shape, xb.dtype),
        input_output_aliases={5: 0},
        compiler_params=_cparams(("arbitrary",), VMEM_LIMIT),
        name="experts",
    )(blk_e, blk_i, blk_nx, blk_nq, n_act, xb, w_up, b_up.reshape(N_EXPERTS, 1, 2 * D_FF), w_down,
      b_down.reshape(N_EXPERTS, 1, D_MODEL))


def _combine_kernel(nt, ntp, off_ref, cnt_ref, base_ref, tot_ref, yb_ref, cols_ref, x1_ref, gf_ref,
                    op_ref, os_ref, ys_scr, sems):
    i = pl.program_id(0)
    cur = i % 2

    sorted_rows = lambda r, z: _packed_rows(yb_ref, (), r, z)

    def start_tile(t, buf):
        for part in range(PARTS):
            _start_segments(PARTS * t + part, cnt_ref, off_ref, base_ref,
                            lambda r, z, part=part: _packed_rows(ys_scr, (buf, part), r, z),
                            sorted_rows, sems.at[buf], False)

    @pl.when(i == 0)
    def _():
        ys_scr[...] = jnp.zeros_like(ys_scr)
        start_tile(0, 0)

    def wait_tile(t, buf):
        for part in range(PARTS):
            _rows_copy(tot_ref[PARTS * t + part], lambda z: sorted_rows(0, z),
                       lambda z, part=part: _packed_rows(ys_scr, (buf, part), 0, z), sems.at[buf]).wait()

    nxt = jnp.minimum(i + 1, nt - 1)
    wait_tile(i, cur)
    start_tile(nxt, 1 - cur)

    @pl.when(i == nt - 1)
    def _():
        wait_tile(nxt, 1 - cur)

    lane = lax.broadcasted_iota(I32, (MT, CAP), 1)
    ys = []
    for part in range(PARTS):
        cols = cols_ref[part]
        w = jnp.zeros((MT, CAP), F32)
        for k in range(TOP_K):
            sk = cols[:, k:k + 1].astype(I32)
            gk = cols[:, TOP_K + k:TOP_K + k + 1]
            w = jnp.where(lane == sk, gk, w)
        ys.append(jnp.dot(w.astype(BF16), _load_packed(ys_scr, (cur, part), 0, CAP),
                          preferred_element_type=F32))
    out = x1_ref[...] + _per_group(jnp.concatenate(ys, axis=0), lambda a, g: a * g, gf_ref[...])

    @pl.when(i < ntp)
    def _():
        op_ref[...] = out

    @pl.when(i >= ntp)
    def _():
        os_ref[...] = out


def _combine(yb, cols, x1, gate_f, off, cnt, base, tot, ntp, nb, tps):
    nt = tot.shape[0] // PARTS
    grid_spec = pltpu.PrefetchScalarGridSpec(
        num_scalar_prefetch=4,
        grid=(nt,),
        in_specs=[pl.BlockSpec(memory_space=pl.ANY),
                  pl.BlockSpec((PARTS, MT, 128), lambda i, *_: (i, 0, 0)),
                  pl.BlockSpec((TM, D_MODEL), lambda i, *_: (i, 0)),
                  pl.BlockSpec((GROUPS_PER_TILE, D_MODEL), _mod_row(ntp, tps, nb))],
        out_specs=[pl.BlockSpec((TM, D_MODEL), lambda i, *_: (jnp.minimum(i, ntp - 1), 0)),
                   pl.BlockSpec((TM, D_MODEL), lambda i, *_: (0, 0))],
        scratch_shapes=[pltpu.VMEM((2, PARTS, CAP * PACK_ROWS, LANES), U32),
                        pltpu.SemaphoreType.DMA((2,))],
    )
    return pl.pallas_call(
        functools.partial(_combine_kernel, nt, ntp),
        grid_spec=grid_spec,
        out_shape=[jax.ShapeDtypeStruct((ntp * TM, D_MODEL), F32),
                   jax.ShapeDtypeStruct((TM, D_MODEL), F32)],
        compiler_params=_cparams(("arbitrary",), VMEM_LIMIT),
        name="combine",
    )(off, cnt, base, tot, yb, cols, x1, gate_f)


def _rotary_tables(seq, dec_batch, dec_seq):
    half = HEAD_DIM_RET // 2
    inv = ROPE_BASE ** (-np.arange(half, dtype=np.float64) / half)
    pos = np.concatenate([np.arange(seq), np.tile(PAST_LEN + np.arange(dec_seq), 2 * dec_batch)])
    ang = pos.astype(np.float64)[:, None] * inv[None, :]
    cos = np.concatenate([np.cos(ang), np.cos(ang)], axis=1)
    sin = np.concatenate([-np.sin(ang), np.sin(ang)], axis=1)
    return jnp.asarray(cos, F32), jnp.asarray(sin, F32)


def _rel_bias_reversed(rel_bias):
    heads = rel_bias.shape[0]
    ext = jnp.concatenate([rel_bias[:, 1:], jnp.broadcast_to(rel_bias[:, -1:], (heads, 2 * MAX_REL))], axis=1)
    return ext[:, ::-1].astype(F32)


def _group_mods(m, nb, ndb):
    assert ndb == GROUPS_PER_TILE
    mp = jnp.broadcast_to(m[:nb, None], (nb, GROUPS_PER_TILE) + m.shape[1:])
    allm = jnp.concatenate([mp.reshape((nb * GROUPS_PER_TILE,) + m.shape[1:]), m[nb:]], axis=0)
    return jnp.transpose(allm, (1, 0, 2))


def _routing_tables(cnt, n_blocks):
    nt = cnt.shape[0]
    off = jnp.cumsum(cnt, axis=1) - cnt
    rows_e = jnp.sum(cnt, axis=0)
    nblk_e = (rows_e + BM - 1) // BM
    blk_end = jnp.cumsum(nblk_e)
    start_e = (blk_end - nblk_e) * BM
    base = start_e[None, :] + jnp.cumsum(cnt, axis=0) - cnt
    n_act = blk_end[-1]
    j = jnp.minimum(jnp.arange(n_blocks), n_act - 1)
    blk_e = jnp.minimum(jnp.sum(blk_end[None, :] <= j[:, None], axis=1), N_EXPERTS - 1)
    later = jnp.where(blk_e[None, :] > blk_e[:, None], blk_e[None, :], N_EXPERTS)
    blk_nx = jnp.min(later, axis=1)
    blk_nx = jnp.where(blk_nx == N_EXPERTS, blk_e, blk_nx)
    tail0 = start_e + rows_e
    tailn = nblk_e * BM - rows_e
    mine = blk_e[:, None] == jnp.arange(N_EXPERTS)[None, :]
    blk_rows = jnp.sum(jnp.where(mine, (rows_e + start_e)[None, :], 0), axis=1) - j * BM
    quarter = BM // BLOCK_QUARTERS
    blk_nq = jnp.clip((blk_rows + quarter - 1) // quarter, 1, BLOCK_QUARTERS)
    i32 = lambda a: a.astype(I32)
    return (i32(off.reshape(nt * N_EXPERTS)), i32(cnt.reshape(nt * N_EXPERTS)),
            i32(base.reshape(nt * N_EXPERTS)), i32(jnp.sum(cnt, axis=1)), i32(tail0), i32(tailn),
            i32(blk_e), i32(j), i32(blk_nx), i32(blk_nq), i32(n_act.reshape(1)))


def kernel(x_prompt, x_sample, c_prompt, c_sample, cache_att_k, cache_att_v, state_ret, w_ada, b_ada,
           g_norm_mix, g_norm_ffn, w_in, g_q, g_k, rel_bias, g_ret_out, w_out, w_router, b_router,
           w_up, b_up, w_down, b_down):
    nb, seq, d = x_prompt.shape
    ndb, dseq, _ = x_sample.shape
    assert d == D_MODEL and ndb * dseq == TM and dseq == CHUNK
    assert seq % TM == 0 and seq >= ATT_WINDOW and cache_att_k.shape[2] == ATT_WINDOW
    assert w_ada.shape[0] == 1
    rp = nb * seq
    ntp = rp // TM
    nt = ntp + 1
    tps = seq // TM

    xp = x_prompt.reshape(rp, d)
    xs = x_sample.reshape(TM, d)

    m = _ada(jnp.concatenate([c_prompt, c_sample], axis=0), w_ada[0], b_ada[0])
    mods = _group_mods(m.reshape(nb + ndb, N_ADA, d), nb, ndb)
    shift_m, scale_m, gate_m, shift_f, scale_f, gate_f = [mods[a] for a in range(N_ADA)]

    cos_t, sin_t = _rotary_tables(seq, ndb, dseq)
    bd = jnp.asarray(np.kron(np.eye(N_HEADS_ATT // 2), np.ones((HEAD_DIM_ATT, HEAD_DIM_ATT))), BF16)
    tile8 = lambda g: jnp.tile(g.astype(F32), N_HEADS_ATT).reshape(1, GROUP_W)
    (qa, ka_t, va, qb, kb, vb, gb, kp_tail, vp_tail, ks_new, vs_new) = _inproj(
        xp, xs, shift_m, scale_m, g_norm_mix[0].reshape(1, d), w_in[0].astype(BF16), bd,
        tile8(g_q[0]) * (HEAD_DIM_ATT ** -0.5 * LOG2_E), tile8(g_k[0]), cos_t, sin_t, nb, tps)

    rev = _rel_bias_reversed(rel_bias[0])
    g_ro = g_ret_out[0].astype(F32).reshape(1, GROUP_W)
    zero_state = jnp.zeros((nb, N_HEADS_RET, HEAD_DIM_RET, HEAD_DIM_RET), F32)
    att_p, ret_p, state_p = _mix_prompt(qa, ka_t, va, rev, qb, kb, vb, gb, zero_state, g_ro, nb, seq)
    att_s = _attn_sample(qa, ks_new, vs_new,
                         cache_att_k[0].reshape(ndb, ATT_WINDOW, GROUP_W).astype(BF16),
                         cache_att_v[0].reshape(ndb, ATT_WINDOW, GROUP_W).astype(BF16), rev, rp)

    ret_s, state_s = _ret(qb, kb, vb, gb, state_ret[0].astype(F32), g_ro, CHUNK, rp, ndb, 1, "ret_sample")

    upper = jnp.asarray(np.triu(np.ones((MT, MT)), 1), BF16)
    lower = jnp.asarray(np.tril(np.ones((N_EXPERTS, N_EXPERTS)), -1), BF16)
    x1, h2, slot, cols, cnt = _outproj(
        att_p, att_s, ret_p, ret_s, xp, xs, gate_m, shift_f, scale_f, g_norm_ffn[0].reshape(1, d),
        w_out[0].astype(BF16), w_router[0].T.astype(BF16), b_router[0].astype(F32).reshape(N_EXPERTS, 1),
        upper, lower, nb, tps)

    ntm = nt * (TM // MT)
    n_blocks = (TOP_K * (rp + TM) + ntm * N_EXPERTS * SEG_ALIGN) // BM + 1 + N_EXPERTS
    (off, cntf, base, tot, tail0, tailn, blk_e, blk_i, blk_nx, blk_nq,
     n_act) = _routing_tables(cnt[:ntm, :, 0], n_blocks)
    xb = _dispatch(h2, slot, off, cntf, base, tot, tail0, tailn, n_act, n_blocks)
    yb = _experts(xb, blk_e, blk_i, blk_nx, blk_nq, n_act, w_up[0], b_up[0], w_down[0], b_down[0])
    out_p, out_s = _combine(yb, cols, x1, gate_f, off, cntf, base, tot, ntp, nb, tps)

    heads = (N_HEADS_ATT, HEAD_DIM_ATT)
    return (out_p.reshape(nb, seq, d), out_s.reshape(ndb, dseq, d),
            kp_tail.reshape(1, nb, ATT_WINDOW, *heads), vp_tail.reshape(1, nb, ATT_WINDOW, *heads),
            state_p[None],
            ks_new.reshape(1, ndb, dseq, *heads), vs_new.reshape(1, ndb, dseq, *heads),
            state_s[None])
```
